```python
import jax, jax.numpy as jnp
from jax import lax
import numpy as np

D_MODEL = 1024
BATCH = 16
SEQ = 2048
DEPTH = 4

N_META = 16
N_A_LAYERS = DEPTH // 2
N_B_LAYERS = DEPTH - N_A_LAYERS
CONV_WIDTH = 31
HEAD_DIM = 64
N_HEADS = D_MODEL // HEAD_DIM
N_KV_HEADS = 4
Q_PER_KV = N_HEADS // N_KV_HEADS
WINDOW = 128
BLOCK = 128
D_FF = -(-8 * D_MODEL // (3 * 256)) * 256
NORM_EPS = 1e-6
NEG_INF = -1e30

kernel_name = "yoco_conformer_swa_sink_hybrid"


def rms_norm(x, g):
    xf = x.astype(jnp.float32)
    y = xf * lax.rsqrt(jnp.mean(xf * xf, axis=-1, keepdims=True) + NORM_EPS)
    return (y * g.astype(jnp.float32)).astype(x.dtype)


def layer_norm(x, g, b):
    xf = x.astype(jnp.float32)
    mu = jnp.mean(xf, axis=-1, keepdims=True)
    var = jnp.mean(jnp.square(xf - mu), axis=-1, keepdims=True)
    y = (xf - mu) * lax.rsqrt(var + NORM_EPS)
    return (y * g.astype(jnp.float32) + b.astype(jnp.float32)).astype(x.dtype)


def conformer_conv(u, w_in, b_in, dw, ln_g, ln_b, w_out, b_out):
    d = u.shape[-1]
    a = u @ w_in + b_in
    a = a[..., :d] * jax.nn.sigmoid(a[..., d:])
    a = jnp.pad(a, ((0, 0), (CONV_WIDTH - 1, 0), (0, 0)))
    c = lax.conv_general_dilated(a, dw[:, None, :], window_strides=(1,), padding='VALID',
                                 dimension_numbers=('NWC', 'WIO', 'NWC'),
                                 feature_group_count=d)
    c = jax.nn.silu(layer_norm(c, ln_g, ln_b))
    return c @ w_out + b_out


def swiglu(u, w_gate, w_up, w_down):
    return (jax.nn.silu(u @ w_gate) * (u @ w_up)) @ w_down


def shared_kv(h, g, w_kv, k_norm):
    b, l, _ = h.shape
    kv = rms_norm(h, g) @ w_kv
    k, v = jnp.split(kv, 2, axis=-1)
    k = rms_norm(k.reshape(b, l, N_KV_HEADS, HEAD_DIM), k_norm)
    v = v.reshape(b, l, N_KV_HEADS, HEAD_DIM)
    return k, v


def band_keys(t, nb):
    b = t.shape[0]
    tp = jnp.pad(t, ((0, 0), (BLOCK, 0), (0, 0), (0, 0)))
    tb = tp.reshape(b, nb + 1, BLOCK, N_KV_HEADS, HEAD_DIM)
    band = jnp.concatenate([tb[:, :-1], tb[:, 1:]], axis=2)
    meta = jnp.broadcast_to(t[:, None, :N_META], (b, nb, N_META, N_KV_HEADS, HEAD_DIM))
    return jnp.concatenate([meta, band], axis=2)


def window_mask(nb):
    qpos = jnp.arange(nb)[:, None] * BLOCK + jnp.arange(BLOCK)[None, :]
    band_pos = jnp.arange(nb)[:, None] * BLOCK - BLOCK + jnp.arange(2 * BLOCK)[None, :]
    diff = qpos[:, :, None] - band_pos[:, None, :]
    band_ok = (diff >= 0) & (diff < WINDOW) & (band_pos[:, None, :] >= N_META)
    meta_ok = jnp.arange(N_META)[None, None, :] <= qpos[:, :, None]
    return jnp.concatenate([meta_ok, band_ok], axis=-1)


def sliding_window_attention(u, w_q, q_g, sinks, w_o, k, v):
    b, l, _ = u.shape
    nb = l // BLOCK
    q = (u @ w_q).reshape(b, l, N_KV_HEADS, Q_PER_KV, HEAD_DIM)
    q = rms_norm(q, q_g).reshape(b, nb, BLOCK, N_KV_HEADS, Q_PER_KV, HEAD_DIM)
    kw = band_keys(k, nb)
    vw = band_keys(v, nb)
    s = jnp.einsum('bnqgrd,bnkgd->bgrnqk', q, kw).astype(jnp.float32) * (HEAD_DIM ** -0.5)
    s = jnp.where(window_mask(nb), s, NEG_INF)
    sink = sinks.astype(jnp.float32).reshape(1, N_KV_HEADS, Q_PER_KV, 1, 1, 1)
    m = jnp.maximum(jnp.max(s, axis=-1, keepdims=True), sink)
    p = jnp.exp(s - m)
    denom = jnp.sum(p, axis=-1, keepdims=True) + jnp.exp(sink - m)
    p = (p / denom).astype(vw.dtype)
    o = jnp.einsum('bgrnqk,bnkgd->bnqgrd', p, vw).reshape(b, l, N_HEADS * HEAD_DIM)
    return o @ w_o


def _fwd_setup_inputs(seed: int = 0) -> dict:
    key = jax.random.key(seed)
    ks = jax.random.split(key, 21)
    out_scale = (2 * DEPTH) ** -0.5
    kvd = 2 * N_KV_HEADS * HEAD_DIM
    qd = N_HEADS * HEAD_DIM

    def nrm(k, shape, scale):
        return jax.random.normal(k, shape, jnp.float32) * scale

    def gain(k, shape):
        return 1.0 + nrm(k, shape, 0.02)

    return {
        "x": nrm(ks[0], (BATCH, SEQ, D_MODEL), 1.0),
        "meta_tokens": nrm(ks[1], (N_META, D_MODEL), 1.0),
        "norm_mix": gain(ks[2], (DEPTH, D_MODEL)),
        "norm_ffn": gain(ks[3], (DEPTH, D_MODEL)),
        "conv_w_in": nrm(ks[4], (N_A_LAYERS, D_MODEL, 2 * D_MODEL), D_MODEL ** -0.5),
        "conv_b_in": nrm(ks[5], (N_A_LAYERS, 2 * D_MODEL), 0.02),
        "conv_dw": nrm(ks[6], (N_A_LAYERS, CONV_WIDTH, D_MODEL), CONV_WIDTH ** -0.5),
        "conv_ln_g": gain(ks[7], (N_A_LAYERS, D_MODEL)),
        "conv_ln_b": nrm(ks[8], (N_A_LAYERS, D_MODEL), 0.02),
        "conv_w_out": nrm(ks[9], (N_A_LAYERS, D_MODEL, D_MODEL), D_MODEL ** -0.5 * out_scale),
        "conv_b_out": nrm(ks[10], (N_A_LAYERS, D_MODEL), 0.02),
        "kv_norm": gain(ks[11], (D_MODEL,)),
        "w_kv": nrm(ks[12], (D_MODEL, kvd), D_MODEL ** -0.5),
        "k_norm": gain(ks[13], (HEAD_DIM,)),
        "w_q": nrm(ks[14], (N_B_LAYERS, D_MODEL, qd), D_MODEL ** -0.5),
        "q_norm": gain(ks[15], (N_B_LAYERS, HEAD_DIM)),
        "attn_sinks": nrm(ks[16], (N_B_LAYERS, N_HEADS), 0.5),
        "w_o": nrm(ks[17], (N_B_LAYERS, qd, D_MODEL), qd ** -0.5 * out_scale),
        "ffn_w_gate": nrm(ks[18], (DEPTH, D_MODEL, D_FF), D_MODEL ** -0.5),
        "ffn_w_up": nrm(ks[19], (DEPTH, D_MODEL, D_FF), D_MODEL ** -0.5),
        "ffn_w_down": nrm(ks[20], (DEPTH, D_FF, D_MODEL), D_FF ** -0.5 * out_scale),
    }


def _fwd_reference(x, meta_tokens, norm_mix, norm_ffn, conv_w_in, conv_b_in, conv_dw, conv_ln_g, conv_ln_b,
              conv_w_out, conv_b_out, kv_norm, w_kv, k_norm, w_q, q_norm, attn_sinks, w_o,
              ffn_w_gate, ffn_w_up, ffn_w_down):
    b, seq, d = x.shape
    l = N_META + seq
    lp = -(-l // BLOCK) * BLOCK
    meta = jnp.broadcast_to(meta_tokens.astype(x.dtype)[None], (b, N_META, d))
    h = jnp.concatenate([meta, x, jnp.zeros((b, lp - l, d), x.dtype)], axis=1)
    k_sh = None
    v_sh = None
    for layer in range(DEPTH):
        u = rms_norm(h, norm_mix[layer])
        if layer < N_A_LAYERS:
            i = layer
            h = h + conformer_conv(u, conv_w_in[i], conv_b_in[i], conv_dw[i], conv_ln_g[i],
                                   conv_ln_b[i], conv_w_out[i], conv_b_out[i])
        else:
            if layer == N_A_LAYERS:
                k_sh, v_sh = shared_kv(h, kv_norm, w_kv, k_norm)
            j = layer - N_A_LAYERS
            h = h + sliding_window_attention(u, w_q[j], q_norm[j], attn_sinks[j], w_o[j], k_sh, v_sh)
        h = h + swiglu(rms_norm(h, norm_ffn[layer]), ffn_w_gate[layer], ffn_w_up[layer], ffn_w_down[layer])
    return h[:, N_META:l]


import jax as _jax
import jax.numpy as _jnp

TWIN_FORMAT = 'train_step'
FWD_PARAMS = ['x', 'meta_tokens', 'norm_mix', 'norm_ffn', 'conv_w_in', 'conv_b_in', 'conv_dw', 'conv_ln_g', 'conv_ln_b', 'conv_w_out', 'conv_b_out', 'kv_norm', 'w_kv', 'k_norm', 'w_q', 'q_norm', 'attn_sinks', 'w_o', 'ffn_w_gate', 'ffn_w_up', 'ffn_w_down']
TWIN_WEIGHTS = ['meta_tokens', 'norm_mix', 'norm_ffn', 'conv_w_in', 'conv_b_in', 'conv_dw', 'conv_ln_g', 'conv_ln_b', 'conv_w_out', 'conv_b_out', 'kv_norm', 'w_kv', 'k_norm', 'w_q', 'q_norm', 'attn_sinks', 'w_o', 'ffn_w_gate', 'ffn_w_up', 'ffn_w_down']
TWIN_DIFF_INPUT = 'x'
TWIN_INPUTS = ['x', 'meta_tokens', 'norm_mix', 'norm_ffn', 'conv_w_in', 'conv_b_in', 'conv_dw', 'conv_ln_g', 'conv_ln_b', 'conv_w_out', 'conv_b_out', 'kv_norm', 'w_kv', 'k_norm', 'w_q', 'q_norm', 'attn_sinks', 'w_o', 'ffn_w_gate', 'ffn_w_up', 'ffn_w_down', 'loss_target', 'm_meta_tokens', 'm_norm_mix', 'm_norm_ffn', 'm_conv_w_in', 'm_conv_b_in', 'm_conv_dw', 'm_conv_ln_g', 'm_conv_ln_b', 'm_conv_w_out', 'm_conv_b_out', 'm_kv_norm', 'm_w_kv', 'm_k_norm', 'm_w_q', 'm_q_norm', 'm_attn_sinks', 'm_w_o', 'm_ffn_w_gate', 'm_ffn_w_up', 'm_ffn_w_down', 'v_meta_tokens', 'v_norm_mix', 'v_norm_ffn', 'v_conv_w_in', 'v_conv_b_in', 'v_conv_dw', 'v_conv_ln_g', 'v_conv_ln_b', 'v_conv_w_out', 'v_conv_b_out', 'v_kv_norm', 'v_w_kv', 'v_k_norm', 'v_w_q', 'v_q_norm', 'v_attn_sinks', 'v_w_o', 'v_ffn_w_gate', 'v_ffn_w_up', 'v_ffn_w_down']
TWIN_OUTPUTS = ['loss', 'grad_x', 'grad_meta_tokens', 'grad_norm_mix', 'grad_norm_ffn', 'grad_conv_w_in', 'grad_conv_b_in', 'grad_conv_dw', 'grad_conv_ln_g', 'grad_conv_ln_b', 'grad_conv_w_out', 'grad_conv_b_out', 'grad_kv_norm', 'grad_w_kv', 'grad_k_norm', 'grad_w_q', 'grad_q_norm', 'grad_attn_sinks', 'grad_w_o', 'grad_ffn_w_gate', 'grad_ffn_w_up', 'grad_ffn_w_down', 'delta_meta_tokens', 'delta_norm_mix', 'delta_norm_ffn', 'delta_conv_w_in', 'delta_conv_b_in', 'delta_conv_dw', 'delta_conv_ln_g', 'delta_conv_ln_b', 'delta_conv_w_out', 'delta_conv_b_out', 'delta_kv_norm', 'delta_w_kv', 'delta_k_norm', 'delta_w_q', 'delta_q_norm', 'delta_attn_sinks', 'delta_w_o', 'delta_ffn_w_gate', 'delta_ffn_w_up', 'delta_ffn_w_down', 'new_m_meta_tokens', 'new_m_norm_mix', 'new_m_norm_ffn', 'new_m_conv_w_in', 'new_m_conv_b_in', 'new_m_conv_dw', 'new_m_conv_ln_g', 'new_m_conv_ln_b', 'new_m_conv_w_out', 'new_m_conv_b_out', 'new_m_kv_norm', 'new_m_w_kv', 'new_m_k_norm', 'new_m_w_q', 'new_m_q_norm', 'new_m_attn_sinks', 'new_m_w_o', 'new_m_ffn_w_gate', 'new_m_ffn_w_up', 'new_m_ffn_w_down', 'new_v_meta_tokens', 'new_v_norm_mix', 'new_v_norm_ffn', 'new_v_conv_w_in', 'new_v_conv_b_in', 'new_v_conv_dw', 'new_v_conv_ln_g', 'new_v_conv_ln_b', 'new_v_conv_w_out', 'new_v_conv_b_out', 'new_v_kv_norm', 'new_v_w_kv', 'new_v_k_norm', 'new_v_w_q', 'new_v_q_norm', 'new_v_attn_sinks', 'new_v_w_o', 'new_v_ffn_w_gate', 'new_v_ffn_w_up', 'new_v_ffn_w_down']
TWIN_LEAF_KINDS = {'loss': 'loss', 'grad_x': 'grad_x', 'grad_meta_tokens': 'grad_w', 'grad_norm_mix': 'grad_w', 'grad_norm_ffn': 'grad_w', 'grad_conv_w_in': 'grad_w', 'grad_conv_b_in': 'grad_w', 'grad_conv_dw': 'grad_w', 'grad_conv_ln_g': 'grad_w', 'grad_conv_ln_b': 'grad_w', 'grad_conv_w_out': 'grad_w', 'grad_conv_b_out': 'grad_w', 'grad_kv_norm': 'grad_w', 'grad_w_kv': 'grad_w', 'grad_k_norm': 'grad_w', 'grad_w_q': 'grad_w', 'grad_q_norm': 'grad_w', 'grad_attn_sinks': 'grad_w', 'grad_w_o': 'grad_w', 'grad_ffn_w_gate': 'grad_w', 'grad_ffn_w_up': 'grad_w', 'grad_ffn_w_down': 'grad_w', 'delta_meta_tokens': 'delta_w', 'delta_norm_mix': 'delta_w', 'delta_norm_ffn': 'delta_w', 'delta_conv_w_in': 'delta_w', 'delta_conv_b_in': 'delta_w', 'delta_conv_dw': 'delta_w', 'delta_conv_ln_g': 'delta_w', 'delta_conv_ln_b': 'delta_w', 'delta_conv_w_out': 'delta_w', 'delta_conv_b_out': 'delta_w', 'delta_kv_norm': 'delta_w', 'delta_w_kv': 'delta_w', 'delta_k_norm': 'delta_w', 'delta_w_q': 'delta_w', 'delta_q_norm': 'delta_w', 'delta_attn_sinks': 'delta_w', 'delta_w_o': 'delta_w', 'delta_ffn_w_gate': 'delta_w', 'delta_ffn_w_up': 'delta_w', 'delta_ffn_w_down': 'delta_w', 'new_m_meta_tokens': 'new_m', 'new_m_norm_mix': 'new_m', 'new_m_norm_ffn': 'new_m', 'new_m_conv_w_in': 'new_m', 'new_m_conv_b_in': 'new_m', 'new_m_conv_dw': 'new_m', 'new_m_conv_ln_g': 'new_m', 'new_m_conv_ln_b': 'new_m', 'new_m_conv_w_out': 'new_m', 'new_m_conv_b_out': 'new_m', 'new_m_kv_norm': 'new_m', 'new_m_w_kv': 'new_m', 'new_m_k_norm': 'new_m', 'new_m_w_q': 'new_m', 'new_m_q_norm': 'new_m', 'new_m_attn_sinks': 'new_m', 'new_m_w_o': 'new_m', 'new_m_ffn_w_gate': 'new_m', 'new_m_ffn_w_up': 'new_m', 'new_m_ffn_w_down': 'new_m', 'new_v_meta_tokens': 'new_v', 'new_v_norm_mix': 'new_v', 'new_v_norm_ffn': 'new_v', 'new_v_conv_w_in': 'new_v', 'new_v_conv_b_in': 'new_v', 'new_v_conv_dw': 'new_v', 'new_v_conv_ln_g': 'new_v', 'new_v_conv_ln_b': 'new_v', 'new_v_conv_w_out': 'new_v', 'new_v_conv_b_out': 'new_v', 'new_v_kv_norm': 'new_v', 'new_v_w_kv': 'new_v', 'new_v_k_norm': 'new_v', 'new_v_w_q': 'new_v', 'new_v_q_norm': 'new_v', 'new_v_attn_sinks': 'new_v', 'new_v_w_o': 'new_v', 'new_v_ffn_w_gate': 'new_v', 'new_v_ffn_w_up': 'new_v', 'new_v_ffn_w_down': 'new_v'}


def _forward(args):
    return _fwd_reference(*[args[k] for k in FWD_PARAMS])


def _output_shape():
    out = _jax.eval_shape(lambda: _forward(_fwd_setup_inputs(0)))
    return out.shape, out.dtype

N_MICROBATCH = 1
ADAM_LR = 0.001
ADAM_B1 = 0.9
ADAM_B2 = 0.999
ADAM_EPS = 1e-08
ADAM_WD = 0.01
ADAM_STEP = 10
PER_EXAMPLE_BATCH_AXIS = {'x': 0, 'loss_target': 0}
SHARED_INPUTS = []
_WEIGHT_DTYPES = {'meta_tokens': _jnp.float32, 'norm_mix': _jnp.float32, 'norm_ffn': _jnp.float32, 'conv_w_in': _jnp.float32, 'conv_b_in': _jnp.float32, 'conv_dw': _jnp.float32, 'conv_ln_g': _jnp.float32, 'conv_ln_b': _jnp.float32, 'conv_w_out': _jnp.float32, 'conv_b_out': _jnp.float32, 'kv_norm': _jnp.float32, 'w_kv': _jnp.float32, 'k_norm': _jnp.float32, 'w_q': _jnp.float32, 'q_norm': _jnp.float32, 'attn_sinks': _jnp.float32, 'w_o': _jnp.float32, 'ffn_w_gate': _jnp.float32, 'ffn_w_up': _jnp.float32, 'ffn_w_down': _jnp.float32}
MOMENT_SCALE = {'meta_tokens': 1.102656e-02, 'norm_mix': 5.797920e-02, 'norm_ffn': 3.101444e+00, 'conv_w_in': 5.794845e-02, 'conv_b_in': 5.942917e-01, 'conv_dw': 1.047375e-01, 'conv_ln_g': 1.945838e+00, 'conv_ln_b': 1.523670e+00, 'conv_w_out': 9.381576e-01, 'conv_b_out': 5.225091e+00, 'kv_norm': 3.020282e-01, 'w_kv': 2.874191e-01, 'k_norm': 1.423859e+00, 'w_q': 1.646316e-02, 'q_norm': 7.181515e-01, 'attn_sinks': 2.313615e-02, 'w_o': 3.509426e-01, 'ffn_w_gate': 4.800993e-02, 'ffn_w_up': 5.036767e-02, 'ffn_w_down': 2.336849e-01}


def _to_microbatches(a, axis):
    t = _jnp.moveaxis(a, axis, 0)
    t = t.reshape((N_MICROBATCH, t.shape[0] // N_MICROBATCH) + t.shape[1:])
    return _jnp.moveaxis(t, 1, axis + 1)


def setup_inputs(seed: int = 0) -> dict:
    inp = _fwd_setup_inputs(seed)
    key = _jax.random.fold_in(_jax.random.key(seed), 7919)
    shape, _ = _output_shape()
    out = dict(inp)
    out["loss_target"] = _jax.random.normal(_jax.random.fold_in(key, 0), shape, _jnp.float32)
    for i, name in enumerate(TWIN_WEIGHTS):
        w = inp[name].astype(_jnp.float32)
        if MOMENT_SCALE is None:
            s = _jnp.sqrt(_jnp.mean(_jnp.square(w)) + 1e-30)
        else:
            s = MOMENT_SCALE[name]
        km, kv = _jax.random.split(_jax.random.fold_in(key, i + 1))
        out[name] = w
        out["m_" + name] = s * _jax.random.normal(km, w.shape, _jnp.float32)
        out["v_" + name] = (s * s) * _jax.random.uniform(kv, w.shape, _jnp.float32, 0.5, 1.5)
    if N_MICROBATCH > 1:
        for name, axis in PER_EXAMPLE_BATCH_AXIS.items():
            out[name] = _to_microbatches(out[name], axis)
    return {'x': out['x'], 'meta_tokens': out['meta_tokens'], 'norm_mix': out['norm_mix'], 'norm_ffn': out['norm_ffn'], 'conv_w_in': out['conv_w_in'], 'conv_b_in': out['conv_b_in'], 'conv_dw': out['conv_dw'], 'conv_ln_g': out['conv_ln_g'], 'conv_ln_b': out['conv_ln_b'], 'conv_w_out': out['conv_w_out'], 'conv_b_out': out['conv_b_out'], 'kv_norm': out['kv_norm'], 'w_kv': out['w_kv'], 'k_norm': out['k_norm'], 'w_q': out['w_q'], 'q_norm': out['q_norm'], 'attn_sinks': out['attn_sinks'], 'w_o': out['w_o'], 'ffn_w_gate': out['ffn_w_gate'], 'ffn_w_up': out['ffn_w_up'], 'ffn_w_down': out['ffn_w_down'], 'loss_target': out['loss_target'], 'm_meta_tokens': out['m_meta_tokens'], 'm_norm_mix': out['m_norm_mix'], 'm_norm_ffn': out['m_norm_ffn'], 'm_conv_w_in': out['m_conv_w_in'], 'm_conv_b_in': out['m_conv_b_in'], 'm_conv_dw': out['m_conv_dw'], 'm_conv_ln_g': out['m_conv_ln_g'], 'm_conv_ln_b': out['m_conv_ln_b'], 'm_conv_w_out': out['m_conv_w_out'], 'm_conv_b_out': out['m_conv_b_out'], 'm_kv_norm': out['m_kv_norm'], 'm_w_kv': out['m_w_kv'], 'm_k_norm': out['m_k_norm'], 'm_w_q': out['m_w_q'], 'm_q_norm': out['m_q_norm'], 'm_attn_sinks': out['m_attn_sinks'], 'm_w_o': out['m_w_o'], 'm_ffn_w_gate': out['m_ffn_w_gate'], 'm_ffn_w_up': out['m_ffn_w_up'], 'm_ffn_w_down': out['m_ffn_w_down'], 'v_meta_tokens': out['v_meta_tokens'], 'v_norm_mix': out['v_norm_mix'], 'v_norm_ffn': out['v_norm_ffn'], 'v_conv_w_in': out['v_conv_w_in'], 'v_conv_b_in': out['v_conv_b_in'], 'v_conv_dw': out['v_conv_dw'], 'v_conv_ln_g': out['v_conv_ln_g'], 'v_conv_ln_b': out['v_conv_ln_b'], 'v_conv_w_out': out['v_conv_w_out'], 'v_conv_b_out': out['v_conv_b_out'], 'v_kv_norm': out['v_kv_norm'], 'v_w_kv': out['v_w_kv'], 'v_k_norm': out['v_k_norm'], 'v_w_q': out['v_w_q'], 'v_q_norm': out['v_q_norm'], 'v_attn_sinks': out['v_attn_sinks'], 'v_w_o': out['v_w_o'], 'v_ffn_w_gate': out['v_ffn_w_gate'], 'v_ffn_w_up': out['v_ffn_w_up'], 'v_ffn_w_down': out['v_ffn_w_down']}


def _loss(weights, diff, rest, loss_target):
    with _jax.named_scope("forward"):
        args = {**rest, TWIN_DIFF_INPUT: diff, **{k: w.astype(_WEIGHT_DTYPES[k]) for k, w in weights.items()}}
        y = _forward(args)
    with _jax.named_scope("loss_head"):
        err = _jnp.square(y.astype(_jnp.float32) - loss_target)
        return 0.5 * _jnp.sum(_jnp.mean(err, axis=-1)) if err.ndim else 0.5 * err


def _adamw(w, g, m, v):
    m = ADAM_B1 * m + (1.0 - ADAM_B1) * g
    v = ADAM_B2 * v + (1.0 - ADAM_B2) * _jnp.square(g)
    m_hat = m / (1.0 - ADAM_B1 ** ADAM_STEP)
    v_hat = v / (1.0 - ADAM_B2 ** ADAM_STEP)
    delta = -ADAM_LR * (m_hat / (_jnp.sqrt(v_hat) + ADAM_EPS) + ADAM_WD * w)
    return delta, m, v


def reference(x, meta_tokens, norm_mix, norm_ffn, conv_w_in, conv_b_in, conv_dw, conv_ln_g, conv_ln_b, conv_w_out, conv_b_out, kv_norm, w_kv, k_norm, w_q, q_norm, attn_sinks, w_o, ffn_w_gate, ffn_w_up, ffn_w_down, loss_target, m_meta_tokens, m_norm_mix, m_norm_ffn, m_conv_w_in, m_conv_b_in, m_conv_dw, m_conv_ln_g, m_conv_ln_b, m_conv_w_out, m_conv_b_out, m_kv_norm, m_w_kv, m_k_norm, m_w_q, m_q_norm, m_attn_sinks, m_w_o, m_ffn_w_gate, m_ffn_w_up, m_ffn_w_down, v_meta_tokens, v_norm_mix, v_norm_ffn, v_conv_w_in, v_conv_b_in, v_conv_dw, v_conv_ln_g, v_conv_ln_b, v_conv_w_out, v_conv_b_out, v_kv_norm, v_w_kv, v_k_norm, v_w_q, v_q_norm, v_attn_sinks, v_w_o, v_ffn_w_gate, v_ffn_w_up, v_ffn_w_down):
    given = dict(x=x, meta_tokens=meta_tokens, norm_mix=norm_mix, norm_ffn=norm_ffn, conv_w_in=conv_w_in, conv_b_in=conv_b_in, conv_dw=conv_dw, conv_ln_g=conv_ln_g, conv_ln_b=conv_ln_b, conv_w_out=conv_w_out, conv_b_out=conv_b_out, kv_norm=kv_norm, w_kv=w_kv, k_norm=k_norm, w_q=w_q, q_norm=q_norm, attn_sinks=attn_sinks, w_o=w_o, ffn_w_gate=ffn_w_gate, ffn_w_up=ffn_w_up, ffn_w_down=ffn_w_down, loss_target=loss_target, m_meta_tokens=m_meta_tokens, m_norm_mix=m_norm_mix, m_norm_ffn=m_norm_ffn, m_conv_w_in=m_conv_w_in, m_conv_b_in=m_conv_b_in, m_conv_dw=m_conv_dw, m_conv_ln_g=m_conv_ln_g, m_conv_ln_b=m_conv_ln_b, m_conv_w_out=m_conv_w_out, m_conv_b_out=m_conv_b_out, m_kv_norm=m_kv_norm, m_w_kv=m_w_kv, m_k_norm=m_k_norm, m_w_q=m_w_q, m_q_norm=m_q_norm, m_attn_sinks=m_attn_sinks, m_w_o=m_w_o, m_ffn_w_gate=m_ffn_w_gate, m_ffn_w_up=m_ffn_w_up, m_ffn_w_down=m_ffn_w_down, v_meta_tokens=v_meta_tokens, v_norm_mix=v_norm_mix, v_norm_ffn=v_norm_ffn, v_conv_w_in=v_conv_w_in, v_conv_b_in=v_conv_b_in, v_conv_dw=v_conv_dw, v_conv_ln_g=v_conv_ln_g, v_conv_ln_b=v_conv_ln_b, v_conv_w_out=v_conv_w_out, v_conv_b_out=v_conv_b_out, v_kv_norm=v_kv_norm, v_w_kv=v_w_kv, v_k_norm=v_k_norm, v_w_q=v_w_q, v_q_norm=v_q_norm, v_attn_sinks=v_attn_sinks, v_w_o=v_w_o, v_ffn_w_gate=v_ffn_w_gate, v_ffn_w_up=v_ffn_w_up, v_ffn_w_down=v_ffn_w_down)
    weights = {n: given[n] for n in TWIN_WEIGHTS}
    shared = {n: given[n] for n in SHARED_INPUTS}
    per_example = {n: given[n] for n in ['x']}
    grad_fn = _jax.value_and_grad(_loss, argnums=(0, 1))

    def one_microbatch(ex, loss_target):
        ex = dict(ex)
        diff = ex.pop(TWIN_DIFF_INPUT)
        return grad_fn(weights, diff, {**shared, **ex}, loss_target)

    if N_MICROBATCH == 1:
        loss, (grad_w, grad_x) = one_microbatch(per_example, given["loss_target"])
    else:
        def body(carry, xs):
            loss_sum, grad_sum = carry
            l_k, (gw_k, gx_k) = one_microbatch(xs[0], xs[1])
            with _jax.named_scope("update"):
                return (loss_sum + l_k, _jax.tree.map(_jnp.add, grad_sum, gw_k)), gx_k

        init = (_jnp.zeros((), _jnp.float32), _jax.tree.map(_jnp.zeros_like, weights))
        (loss, grad_w), grad_x = _jax.lax.scan(body, init, (per_example, given["loss_target"]))
    with _jax.named_scope("update"):
        delta_w, new_m, new_v = {}, {}, {}
        for n in TWIN_WEIGHTS:
            delta_w[n], new_m[n], new_v[n] = _adamw(weights[n], grad_w[n], given["m_" + n], given["v_" + n])
    return (loss, grad_x, *[grad_w[n] for n in TWIN_WEIGHTS], *[delta_w[n] for n in TWIN_WEIGHTS],
            *[new_m[n] for n in TWIN_WEIGHTS], *[new_v[n] for n in TWIN_WEIGHTS])
```

```python
import functools

import jax
import jax.numpy as jnp
from jax import lax
from jax.experimental import pallas as pl
from jax.experimental.pallas import tpu as pltpu

f32, bf16 = jnp.float32, jnp.bfloat16

N_META_ROWS = 16
ATT_BLOCK = 128
HEAD_DIM = 64
N_KV = 4
CONV_TAPS = 31
CONV_PAD = 32
EPS = 1e-6
MASKED = -1e30
LR, B1, B2, ADAM_EPS, WD, STEP = 0.001, 0.9, 0.999, 1e-08, 0.01, 10
N_CHIPS = 4
VMEM_LIMIT_BYTES = 56 * 1024 * 1024
MESH = pl.DeviceIdType.MESH
ANY = pl.BlockSpec(memory_space=pl.ANY)
S = jax.ShapeDtypeStruct


def _pcall(body, **kw):
    return pl.pallas_call(body, **kw)


def _params(sem=None):
    return pltpu.CompilerParams(dimension_semantics=sem, vmem_limit_bytes=VMEM_LIMIT_BYTES)


def _tile(n, prefs):
    for p in prefs:
        if n % p == 0:
            return p
    return n


_DN = {"nn": (((1,), (0,)), ((), ())), "nt": (((1,), (1,)), ((), ())), "tn": (((0,), (0,)), ((), ()))}


def _matmul(name, mode, grid, a_ops, b_ops, x_ops, outs, terms, acc_shape, n_acc, epilogue):
    na, nb, nx, no = len(a_ops), len(b_ops), len(x_ops), len(outs)
    nk = grid[2]

    def body(*refs):
        a_refs, b_refs = refs[:na], refs[na:na + nb]
        x_refs = refs[na + nb:na + nb + nx]
        o_refs = refs[na + nb + nx:na + nb + nx + no]
        acc_refs = refs[na + nb + nx + no:]
        k = pl.program_id(2)

        @pl.when(k == 0)
        def _():
            for acc in acc_refs:
                acc[...] = jnp.zeros_like(acc)

        for ai, bi, ci in terms:
            a = a_refs[ai][...].astype(bf16)
            b = b_refs[bi][...].astype(bf16)
            acc_refs[ci][...] += lax.dot_general(a, b, _DN[mode], preferred_element_type=f32)

        @pl.when(k == nk - 1)
        def _():
            res = epilogue([acc[...] for acc in acc_refs], [x[...] for x in x_refs])
            for o_ref, r in zip(o_refs, res):
                o_ref[...] = r.astype(o_ref.dtype)

    ops = list(a_ops) + list(b_ops) + list(x_ops)
    res = _pcall(body, name=name, grid=grid, in_specs=[s for _, s in ops], out_specs=[s for _, s in outs],
                 out_shape=[s for s, _ in outs], scratch_shapes=[pltpu.VMEM(acc_shape, f32)] * n_acc,
                 compiler_params=_params(("parallel", "parallel", "arbitrary")))(*[a for a, _ in ops])
    return res


def _first(accs, xs):
    return (accs[0],)


def _proj(name, a, w, l, tm, bias=None, resid=None, out_dtype=f32):
    T = a.shape[0]
    Q, _, Kq, N = w.shape
    tn = _tile(N, (512, 256, 128))
    x_ops, epi = [], _first
    if bias is not None:
        x_ops = [(bias, pl.BlockSpec((1, tn), lambda i, j, k: (0, j))), (resid, pl.BlockSpec((tm, tn), lambda i, j, k: (i, j)))]
        epi = lambda accs, xs: (accs[0] + xs[0] + xs[1],)
    elif resid is not None:
        x_ops = [(resid, pl.BlockSpec((tm, tn), lambda i, j, k: (i, j)))]
        epi = lambda accs, xs: (accs[0] + xs[0],)
    return _matmul(name, "nn", (T // tm, N // tn, Q),
                   [(a, pl.BlockSpec((tm, Kq), lambda i, j, k: (i, k)))],
                   [(w, pl.BlockSpec((None, None, Kq, tn), lambda i, j, k: (k, l, 0, j)))],
                   x_ops, [(S((T, N), out_dtype), pl.BlockSpec((tm, tn), lambda i, j, k: (i, j)))],
                   [(0, 0, 0)], (tm, tn), 1, epi)[0]


def _proj_dx(name, dy, w, l, tm, out_dtype=f32):
    T, N = dy.shape
    Q, _, Kq, _ = w.shape
    return _matmul(name, "nt", (T // tm, Q, 1),
                   [(dy, pl.BlockSpec((tm, N), lambda i, j, k: (i, 0)))],
                   [(w, pl.BlockSpec((None, None, Kq, N), lambda i, j, k: (j, l, 0, 0)))],
                   [], [(S((T, Q * Kq), out_dtype), pl.BlockSpec((tm, Kq), lambda i, j, k: (i, j)))],
                   [(0, 0, 0)], (tm, Kq), 1, _first)[0]


def _dw_rows(name, a, dy, Q, tk):
    T, N = dy.shape
    tn = _tile(N, (512, 256, 128))
    if a.ndim == 2:
        Kq = a.shape[1] // Q
        a_op = (a, pl.BlockSpec((tk, Kq), lambda i, j, k: (k, i)))
    else:
        Kq = a.shape[2]
        a_op = (a, pl.BlockSpec((None, tk, Kq), lambda i, j, k: (i, k, 0)))
    return _matmul(name, "tn", (Q, N // tn, T // tk), [a_op],
                   [(dy, pl.BlockSpec((tk, tn), lambda i, j, k: (k, j)))],
                   [], [(S((Q, Kq, N), bf16), pl.BlockSpec((None, Kq, tn), lambda i, j, k: (i, 0, j)))],
                   [(0, 0, 0)], (Kq, tn), 1, _first)[0]


def _dw_cols(name, a, dyc, Qc, tk):
    T, K = a.shape
    tkin = _tile(K, (512, 256, 128))
    if dyc.ndim == 3:
        Nq = dyc.shape[2]
        b_op = (dyc, pl.BlockSpec((None, tk, Nq), lambda i, j, k: (j, k, 0)))
    else:
        Nq = dyc.shape[1] // Qc
        b_op = (dyc, pl.BlockSpec((tk, Nq), lambda i, j, k: (k, j)))
    return _matmul(name, "tn", (K // tkin, Qc, T // tk),
                   [(a, pl.BlockSpec((tk, tkin), lambda i, j, k: (k, i)))], [b_op],
                   [], [(S((Qc, K, Nq), bf16), pl.BlockSpec((None, tkin, Nq), lambda i, j, k: (j, i, 0)))],
                   [(0, 0, 0)], (tkin, Nq), 1, _first)[0]


def _glu_fwd(name, u, w, l, b_in, tm):
    T, D = u.shape
    Q, _, _, Cq = w.shape
    H = Q // 2

    def epi(accs, xs):
        av, ag = accs[0] + xs[0], accs[1] + xs[1]
        return av * jax.nn.sigmoid(ag), av, ag

    wspec = lambda off: pl.BlockSpec((None, None, D, Cq), lambda i, j, k: (j + off, l, 0, 0))
    bspec = lambda off: pl.BlockSpec((1, Cq), lambda i, j, k: (0, j + off))
    ospec = pl.BlockSpec((tm, Cq), lambda i, j, k: (i, j))
    return _matmul(name, "nn", (T // tm, H, 1),
                   [(u, pl.BlockSpec((tm, D), lambda i, j, k: (i, 0)))],
                   [(w, wspec(0)), (w, wspec(H))], [(b_in, bspec(0)), (b_in, bspec(H))],
                   [(S((T, H * Cq), f32), ospec), (S((T, H * Cq), bf16), ospec), (S((T, H * Cq), bf16), ospec)],
                   [(0, 0, 0), (0, 1, 1)], (tm, Cq), 2, epi)


def _glu_du(name, dav, dag, w, l, tm):
    T = dav.shape[0]
    Q, _, D, Cq = w.shape
    H = Q // 2
    tn = _tile(D, (512, 256, 128))
    aspec = pl.BlockSpec((tm, Cq), lambda i, j, k: (i, k))
    wspec = lambda off: pl.BlockSpec((None, None, tn, Cq), lambda i, j, k: (k + off, l, j, 0))
    return _matmul(name, "nt", (T // tm, D // tn, H), [(dav, aspec), (dag, aspec)],
                   [(w, wspec(0)), (w, wspec(H))], [],
                   [(S((T, D), f32), pl.BlockSpec((tm, tn), lambda i, j, k: (i, j)))],
                   [(0, 0, 0), (1, 1, 0)], (tm, tn), 1, _first)[0]


def _ffn_up(name, u, wg, wu, l, tm):
    T, D = u.shape
    Q, _, _, Fq = wg.shape

    def epi(accs, xs):
        g, up = accs
        return g, up, g * jax.nn.sigmoid(g) * up

    wspec = pl.BlockSpec((None, None, D, Fq), lambda i, j, k: (j, l, 0, 0))
    ospec = pl.BlockSpec((None, tm, Fq), lambda i, j, k: (j, i, 0))
    osh = S((Q, T, Fq), bf16)
    return _matmul(name, "nn", (T // tm, Q, 1), [(u, pl.BlockSpec((tm, D), lambda i, j, k: (i, 0)))],
                   [(wg, wspec), (wu, wspec)], [], [(osh, ospec)] * 3, [(0, 0, 0), (0, 1, 1)], (tm, Fq), 2, epi)


def _ffn_down(name, hid, wd, l, resid, tm):
    Q, T, Fq = hid.shape
    D = wd.shape[3]
    tn = _tile(D, (512, 256, 128))
    return _matmul(name, "nn", (T // tm, D // tn, Q),
                   [(hid, pl.BlockSpec((None, tm, Fq), lambda i, j, k: (k, i, 0)))],
                   [(wd, pl.BlockSpec((None, None, Fq, tn), lambda i, j, k: (k, l, 0, j)))],
                   [(resid, pl.BlockSpec((tm, tn), lambda i, j, k: (i, j)))],
                   [(S((T, D), f32), pl.BlockSpec((tm, tn), lambda i, j, k: (i, j)))],
                   [(0, 0, 0)], (tm, tn), 1, lambda accs, xs: (accs[0] + xs[0],))[0]


def _ffn_dhid(name, dy, wd, l, gate, up, tm):
    T, D = dy.shape
    Q, _, Fq, _ = wd.shape

    def epi(accs, xs):
        dh, g, up = accs[0], xs[0].astype(f32), xs[1].astype(f32)
        sg = jax.nn.sigmoid(g)
        return dh * up * (sg * (1.0 + g * (1.0 - sg))), dh * (g * sg)

    cspec = pl.BlockSpec((None, tm, Fq), lambda i, j, k: (j, i, 0))
    osh = S((Q, T, Fq), bf16)
    return _matmul(name, "nt", (T // tm, Q, 1), [(dy, pl.BlockSpec((tm, D), lambda i, j, k: (i, 0)))],
                   [(wd, pl.BlockSpec((None, None, Fq, D), lambda i, j, k: (j, l, 0, 0)))],
                   [(gate, cspec), (up, cspec)], [(osh, cspec)] * 2, [(0, 0, 0)], (tm, Fq), 1, epi)


def _ffn_du(name, dgate, dup, wg, wu, l, tm):
    Q, T, Fq = dgate.shape
    D = wg.shape[2]
    tn = _tile(D, (512, 256, 128))
    aspec = pl.BlockSpec((None, tm, Fq), lambda i, j, k: (k, i, 0))
    wspec = pl.BlockSpec((None, None, tn, Fq), lambda i, j, k: (k, l, j, 0))
    return _matmul(name, "nt", (T // tm, D // tn, Q), [(dgate, aspec), (dup, aspec)], [(wg, wspec), (wu, wspec)], [],
                   [(S((T, D), f32), pl.BlockSpec((tm, tn), lambda i, j, k: (i, j)))],
                   [(0, 0, 0), (1, 1, 0)], (tm, tn), 1, _first)[0]


def _rowwise(name, fn, rows, params, out_dtypes, tm):
    nr, npar = len(rows), len(params)
    T = rows[0].shape[0]
    shp = jax.eval_shape(fn, *[S((tm, r.shape[1]), f32) for r in rows], *[S(p.shape, f32) for p in params])

    def body(*refs):
        r = [x[...].astype(f32) for x in refs[:nr]]
        p = [x[...] for x in refs[nr:nr + npar]]
        for o_ref, o in zip(refs[nr + npar:], fn(*r, *p)):
            o_ref[...] = o.astype(o_ref.dtype)

    row_spec = lambda w: pl.BlockSpec((tm, w), lambda i: (i, 0))
    par_spec = lambda p: pl.BlockSpec(p.shape, lambda i: (0, 0))
    return _pcall(body, name=name, grid=(T // tm,),
                  in_specs=[row_spec(r.shape[1]) for r in rows] + [par_spec(p) for p in params],
                  out_specs=[row_spec(s.shape[1]) for s in shp],
                  out_shape=[S((T, s.shape[1]), dt) for s, dt in zip(shp, out_dtypes)],
                  compiler_params=_params(("parallel",)))(*rows, *params)


def _rowwise_vjp(name, fn, rows, params, cots, drow_dtypes, tm):
    nr, npar, nc = len(rows), len(params), len(cots)
    T = rows[0].shape[0]

    def body(*refs):
        r = [x[...].astype(f32) for x in refs[:nr]]
        p = [x[...] for x in refs[nr:nr + npar]]
        c = tuple(x[...].astype(f32) for x in refs[nr + npar:nr + npar + nc])
        o_refs = refs[nr + npar + nc:]
        _, vjp = jax.vjp(fn, *r, *p)
        grads = vjp(c)
        for o_ref, g in zip(o_refs[:nr], grads[:nr]):
            o_ref[...] = g.astype(o_ref.dtype)

        @pl.when(pl.program_id(0) == 0)
        def _():
            for o_ref in o_refs[nr:]:
                o_ref[...] = jnp.zeros_like(o_ref)

        for o_ref, g in zip(o_refs[nr:], grads[nr:]):
            o_ref[...] += g

    row_spec = lambda w: pl.BlockSpec((tm, w), lambda i: (i, 0))
    par_spec = lambda p: pl.BlockSpec(p.shape, lambda i: (0, 0))
    return _pcall(body, name=name, grid=(T // tm,),
                  in_specs=[row_spec(r.shape[1]) for r in rows] + [par_spec(p) for p in params] + [row_spec(c.shape[1]) for c in cots],
                  out_specs=[row_spec(r.shape[1]) for r in rows] + [par_spec(p) for p in params],
                  out_shape=[S(r.shape, dt) for r, dt in zip(rows, drow_dtypes)] + [S(p.shape, f32) for p in params],
                  compiler_params=_params(("arbitrary",)))(*rows, *params, *cots)


def _rms(h, g):
    return h * lax.rsqrt(jnp.mean(h * h, axis=-1, keepdims=True) + EPS) * g


def _rms_fn(h, g):
    return (_rms(h, g),)


def _rms_res_fn(h, g):
    return _rms(h, g), h


def _rms_res_bias_fn(h, g, b0):
    return _rms(h, g), h + b0


def _ln_silu_fn(c, g, b):
    mu = jnp.mean(c, axis=-1, keepdims=True)
    var = jnp.mean(jnp.square(c - mu), axis=-1, keepdims=True)
    y = (c - mu) * lax.rsqrt(var + EPS) * g + b
    return (y * jax.nn.sigmoid(y),)


def _glu_fn(av, ag, bv, bg):
    return ((av + bv) * jax.nn.sigmoid(ag + bg),)


def _loss_head(h, tgt, Lp, n_real, tm):
    T, D = h.shape

    def body(h_ref, t_ref, dy_ref, part_ref):
        i = pl.program_id(0)
        pos = (i * tm + lax.broadcasted_iota(jnp.int32, (tm, 1), 0)) % Lp
        real = (pos >= N_META_ROWS) & (pos < N_META_ROWS + n_real)
        err = jnp.where(real, h_ref[...] - t_ref[...], 0.0)
        dy_ref[...] = err * (1.0 / D)

        @pl.when(i == 0)
        def _():
            part_ref[...] = jnp.zeros_like(part_ref)

        part_ref[...] += jnp.sum(err * err, axis=0, keepdims=True)

    spec = pl.BlockSpec((tm, D), lambda i: (i, 0))
    return _pcall(body, name="loss_head", grid=(T // tm,), in_specs=[spec, spec],
                  out_specs=[spec, pl.BlockSpec((1, D), lambda i: (0, 0))],
                  out_shape=[S((T, D), f32), S((1, D), f32)], compiler_params=_params(("arbitrary",)))(h, tgt)


def _dwconv_fwd(name, p_pad, dw):
    B, Lpp, D = p_pad.shape
    Lp = Lpp - CONV_PAD
    dc = _tile(D, (256, 128))
    off = CONV_PAD - (CONV_TAPS - 1)
    win_rows = ATT_BLOCK + CONV_PAD

    def body(p_ref, w_ref, o_ref):
        def step(r, carry):
            base = pl.multiple_of(r * ATT_BLOCK, ATT_BLOCK)
            win = p_ref[pl.ds(base, win_rows), :]
            acc = jnp.zeros((ATT_BLOCK, dc), f32)
            for k in range(CONV_TAPS):
                acc = acc + win[off + k:off + k + ATT_BLOCK] * w_ref[k:k + 1, :]
            o_ref[pl.ds(base, ATT_BLOCK), :] = acc
            return carry

        lax.fori_loop(0, Lp // ATT_BLOCK, step, 0)

    return _pcall(body, name=name, grid=(B, D // dc),
                  in_specs=[pl.BlockSpec((None, Lpp, dc), lambda b, j: (b, 0, j)), pl.BlockSpec((CONV_PAD, dc), lambda b, j: (0, j))],
                  out_specs=pl.BlockSpec((None, Lp, dc), lambda b, j: (b, 0, j)),
                  out_shape=S((B, Lp, D), f32), compiler_params=_params(("parallel", "parallel")))(p_pad, dw)


def _dwconv_bwd(name, dc_pad, p_pad, dw):
    B, Lpp, D = p_pad.shape
    Lp = Lpp - CONV_PAD
    dcw = _tile(D, (256, 128))
    off = CONV_PAD - (CONV_TAPS - 1)
    win_rows = ATT_BLOCK + CONV_PAD

    def body(dc_ref, p_ref, w_ref, dp_ref, ddw_ref):
        @pl.when(pl.program_id(1) == 0)
        def _():
            ddw_ref[...] = jnp.zeros_like(ddw_ref)

        def step(r, carry):
            base = pl.multiple_of(r * ATT_BLOCK, ATT_BLOCK)
            dwin = dc_ref[pl.ds(base, win_rows), :]
            pwin = p_ref[pl.ds(base, win_rows), :]
            dtile = dwin[0:ATT_BLOCK]
            acc = jnp.zeros((ATT_BLOCK, dcw), f32)
            for k in range(CONV_TAPS):
                s = CONV_TAPS - 1 - k
                acc = acc + dwin[s:s + ATT_BLOCK] * w_ref[k:k + 1, :]
                ddw_ref[k:k + 1, :] += jnp.sum(dtile * pwin[off + k:off + k + ATT_BLOCK], axis=0, keepdims=True)
            dp_ref[pl.ds(base, ATT_BLOCK), :] = acc
            return carry

        lax.fori_loop(0, Lp // ATT_BLOCK, step, 0)

    seq = pl.BlockSpec((None, Lpp, dcw), lambda j, b: (b, 0, j))
    wsp = pl.BlockSpec((CONV_PAD, dcw), lambda j, b: (0, j))
    return _pcall(body, name=name, grid=(D // dcw, B), in_specs=[seq, seq, wsp],
                  out_specs=[pl.BlockSpec((None, Lp, dcw), lambda j, b: (b, 0, j)), wsp],
                  out_shape=[S((B, Lp, D), f32), S((CONV_PAD, D), f32)],
                  compiler_params=_params(("parallel", "arbitrary")))(dc_pad, p_pad, dw)


def _attn_mask(n):
    qpos = n * ATT_BLOCK + lax.broadcasted_iota(jnp.int32, (ATT_BLOCK, 3 * ATT_BLOCK), 0)
    col = lax.broadcasted_iota(jnp.int32, (ATT_BLOCK, 3 * ATT_BLOCK), 1)
    meta_ok = (col < N_META_ROWS) & (col <= qpos)
    band_pos = (n - 1) * ATT_BLOCK + (col - ATT_BLOCK)
    diff = qpos - band_pos
    band_ok = (col >= ATT_BLOCK) & (diff >= 0) & (diff < ATT_BLOCK) & (band_pos >= N_META_ROWS)
    return meta_ok | band_ok


def _attn_head(q, kk, vv, qg, kg, sink_vec, onehot, mask):
    qn = _rms(q, qg)
    kn = _rms(kk, kg)
    s = lax.dot_general(qn.astype(bf16), kn.astype(bf16), _DN["nt"], preferred_element_type=f32) * (HEAD_DIM ** -0.5)
    s = jnp.where(mask, s, MASKED)
    sink = jnp.sum(sink_vec * onehot, axis=-1, keepdims=True)
    m = lax.stop_gradient(jnp.maximum(jnp.max(s, axis=-1, keepdims=True), sink))
    p = jnp.exp(s - m)
    denom = jnp.sum(p, axis=-1, keepdims=True) + jnp.exp(sink - m)
    return jnp.dot((p / denom).astype(bf16), vv.astype(bf16), preferred_element_type=f32)


def _keys_of(ref, n):
    meta = ref[ATT_BLOCK:2 * ATT_BLOCK, :]
    band = ref[pl.ds(pl.multiple_of(n * ATT_BLOCK, ATT_BLOCK), 2 * ATT_BLOCK), :]
    return jnp.concatenate([meta, band], axis=0)


def _attn_specs(R, Lk):
    qspec = pl.BlockSpec((None, None, R, ATT_BLOCK, HEAD_DIM), lambda g, b, n: (b, g, 0, n, 0))
    kspec = pl.BlockSpec((None, None, Lk, HEAD_DIM), lambda g, b, n: (b, g, 0, 0))
    gspec = pl.BlockSpec((1, HEAD_DIM), lambda g, b, n: (0, 0))
    sspec = pl.BlockSpec((None, 1, R), lambda g, b, n: (g, 0, 0))
    return qspec, kspec, gspec, sspec


def _attn_fwd(name, q, kp, vp, qg, kg, sinks):
    B, G, R, Lp, _ = q.shape
    qspec, kspec, gspec, sspec = _attn_specs(R, kp.shape[2])

    def body(q_ref, k_ref, v_ref, qg_ref, kg_ref, s_ref, o_ref):
        n = pl.program_id(2)
        kk, vv, mask = _keys_of(k_ref, n), _keys_of(v_ref, n), _attn_mask(n)
        lane = lax.broadcasted_iota(jnp.int32, (1, R), 1)
        for r in range(R):
            o = _attn_head(q_ref[r], kk, vv, qg_ref[...], kg_ref[...], s_ref[...], (lane == r).astype(f32), mask)
            o_ref[r] = o.astype(o_ref.dtype)

    return _pcall(body, name=name, grid=(G, B, Lp // ATT_BLOCK), in_specs=[qspec, kspec, kspec, gspec, gspec, sspec],
                  out_specs=qspec, out_shape=S(q.shape, bf16),
                  compiler_params=_params(("parallel", "parallel", "parallel")))(q, kp, vp, qg, kg, sinks)


def _attn_bwd(name, q, kp, vp, qg, kg, sinks, do):
    B, G, R, Lp, _ = q.shape
    qspec, kspec, gspec, sspec = _attn_specs(R, kp.shape[2])

    def body(q_ref, k_ref, v_ref, qg_ref, kg_ref, s_ref, do_ref, dq_ref, dk_ref, dv_ref, dqg_ref, dkg_ref, ds_ref):
        g, b, n = pl.program_id(0), pl.program_id(1), pl.program_id(2)

        @pl.when((g == 0) & (b == 0) & (n == 0))
        def _():
            dqg_ref[...] = jnp.zeros_like(dqg_ref)
            dkg_ref[...] = jnp.zeros_like(dkg_ref)

        @pl.when((b == 0) & (n == 0))
        def _():
            ds_ref[...] = jnp.zeros_like(ds_ref)

        @pl.when(n == 0)
        def _():
            dk_ref[...] = jnp.zeros_like(dk_ref)
            dv_ref[...] = jnp.zeros_like(dv_ref)

        kk, vv, mask = _keys_of(k_ref, n), _keys_of(v_ref, n), _attn_mask(n)
        lane = lax.broadcasted_iota(jnp.int32, (1, R), 1)
        dkk = jnp.zeros_like(kk)
        dvv = jnp.zeros_like(vv)
        for r in range(R):
            onehot = (lane == r).astype(f32)
            _, vjp = jax.vjp(lambda q_, k_, v_, a_, b_, s_: _attn_head(q_, k_, v_, a_, b_, s_, onehot, mask),
                             q_ref[r], kk, vv, qg_ref[...], kg_ref[...], s_ref[...])
            dq, dk1, dv1, dqg, dkg, dsk = vjp(do_ref[r].astype(f32))
            dq_ref[r] = dq
            dkk, dvv = dkk + dk1, dvv + dv1
            dqg_ref[...] += dqg
            dkg_ref[...] += dkg
            ds_ref[...] += dsk
        band = pl.ds(pl.multiple_of(n * ATT_BLOCK, ATT_BLOCK), 2 * ATT_BLOCK)
        dk_ref[band, :] += dkk[ATT_BLOCK:]
        dv_ref[band, :] += dvv[ATT_BLOCK:]
        dk_ref[ATT_BLOCK:2 * ATT_BLOCK, :] += dkk[:ATT_BLOCK]
        dv_ref[ATT_BLOCK:2 * ATT_BLOCK, :] += dvv[:ATT_BLOCK]

    return _pcall(body, name=name, grid=(G, B, Lp // ATT_BLOCK),
                  in_specs=[qspec, kspec, kspec, gspec, gspec, sspec, qspec],
                  out_specs=[qspec, kspec, kspec, gspec, gspec, sspec],
                  out_shape=[S(q.shape, f32), S(kp.shape, f32), S(vp.shape, f32), S(qg.shape, f32), S(kg.shape, f32), S(sinks.shape, f32)],
                  compiler_params=_params(("arbitrary", "arbitrary", "arbitrary")))(q, kp, vp, qg, kg, sinks, do)


def _place():
    x, y, c = lax.axis_index("x"), lax.axis_index("y"), lax.axis_index("c")
    chips = [(1 - x, y), (x, 1 - y), (1 - x, 1 - y)]
    return x, y, c, chips, [2 * cx + cy for cx, cy in chips]


def _remote(src, dst, send_sem, recv_sem, to):
    return pltpu.make_async_remote_copy(src_ref=src, dst_ref=dst, send_sem=send_sem, recv_sem=recv_sem,
                                        device_id=to, device_id_type=MESH)


def _gather_weights(shards):
    n = len(shards)

    def body(*refs):
        ins, outs = refs[:n], refs[n:2 * n]
        send, recv, loc = refs[2 * n:]
        x, y, c, chips, qk = _place()
        me_q, sib = 2 * x + y, (x, y, 1 - c)
        local = [pltpu.make_async_copy(ins[p], outs[p].at[me_q], loc.at[p]) for p in range(n)]
        for cp in local:
            cp.start()
        first = [_remote(ins[p].at[c], outs[p].at[me_q, c], send.at[6 * p + k], recv.at[6 * p + k], (*chips[k], c))
                 for p in range(n) for k in range(3)]
        for cp in first:
            cp.start()
        passed = []
        for k in range(3):
            for p in range(n):
                land = outs[p].at[qk[k], c]
                _remote(land, land, send.at[6 * p + k], recv.at[6 * p + k], sib).wait_recv()
                cp = _remote(land, land, send.at[6 * p + 3 + k], recv.at[6 * p + 3 + k], sib)
                cp.start()
                passed.append(cp)
        for k in range(3):
            for p in range(n):
                land = outs[p].at[qk[k], 1 - c]
                _remote(land, land, send.at[6 * p + 3 + k], recv.at[6 * p + 3 + k], sib).wait_recv()
        for cp in first + passed:
            cp.wait_send()
        for cp in local:
            cp.wait()

    return _pcall(body, name="gather_weights", in_specs=[ANY] * n, out_specs=[ANY] * n,
                  out_shape=[S((N_CHIPS,) + s.shape, s.dtype) for s in shards],
                  scratch_shapes=[pltpu.SemaphoreType.DMA((6 * n,)), pltpu.SemaphoreType.DMA((6 * n,)), pltpu.SemaphoreType.DMA((n,))])(*shards)


def _exchange_halves(grads):
    n = len(grads)

    def body(*refs):
        ins, outs = refs[:n], refs[n:2 * n]
        send, recv = refs[2 * n:]
        x, y, c, _, _ = _place()
        sib = (x, y, 1 - c)
        cps = [_remote(ins[p].at[q, 1 - c], outs[p].at[q], send.at[N_CHIPS * p + q], recv.at[N_CHIPS * p + q], sib)
               for p in range(n) for q in range(N_CHIPS)]
        for cp in cps:
            cp.start()
        for cp in cps:
            cp.wait_recv()
        for cp in cps:
            cp.wait_send()

    return _pcall(body, name="exchange_halves", in_specs=[ANY] * n, out_specs=[ANY] * n,
                  out_shape=[S((N_CHIPS,) + g.shape[2:], g.dtype) for g in grads],
                  scratch_shapes=[pltpu.SemaphoreType.DMA((N_CHIPS * n,)), pltpu.SemaphoreType.DMA((N_CHIPS * n,))])(*grads)


def _scatter_to_owners(sums):
    n = len(sums)

    def body(*refs):
        ins, outs = refs[:n], refs[n:2 * n]
        send, recv = refs[2 * n:]
        x, y, c, chips, qk = _place()
        cps = [_remote(ins[p].at[qk[k]], outs[p].at[k], send.at[3 * p + k], recv.at[3 * p + k], (*chips[k], c))
               for p in range(n) for k in range(3)]
        for cp in cps:
            cp.start()
        for cp in cps:
            cp.wait_recv()
        for cp in cps:
            cp.wait_send()

    return _pcall(body, name="scatter_to_owners", in_specs=[ANY] * n, out_specs=[ANY] * n,
                  out_shape=[S((3,) + s.shape[1:], s.dtype) for s in sums],
                  scratch_shapes=[pltpu.SemaphoreType.DMA((3 * n,)), pltpu.SemaphoreType.DMA((3 * n,))])(*sums)


def _share_halves(halves):
    n = len(halves)

    def body(*refs):
        ins, outs = refs[:n], refs[n:2 * n]
        send, recv, loc = refs[2 * n:]
        x, y, c, _, _ = _place()
        sib = (x, y, 1 - c)
        local = [pltpu.make_async_copy(ins[p], outs[p].at[c], loc.at[p]) for p in range(n)]
        cps = [_remote(ins[p], outs[p].at[c], send.at[p], recv.at[p], sib) for p in range(n)]
        for cp in local + cps:
            cp.start()
        for p in range(n):
            _remote(ins[p], outs[p].at[1 - c], send.at[p], recv.at[p], sib).wait_recv()
        for cp in cps:
            cp.wait_send()
        for cp in local:
            cp.wait()

    return _pcall(body, name="share_halves", in_specs=[ANY] * n, out_specs=[ANY] * n,
                  out_shape=[S((2,) + h.shape, h.dtype) for h in halves],
                  scratch_shapes=[pltpu.SemaphoreType.DMA((n,)), pltpu.SemaphoreType.DMA((n,)), pltpu.SemaphoreType.DMA((n,))])(*halves)


def _gather_all_devices(block):
    m_per, ncol = block.shape

    def body(x_ref, out_ref, send_sems, recv_sems, local_sem):
        x, y, c, chips, _ = _place()
        me, sib = (x, y, c), (x, y, 1 - c)

        def rows(px, py, pc):
            return out_ref.at[pl.ds((4 * px + 2 * py + pc) * m_per, m_per), :]

        def copy(k, blk, to, src=None):
            return _remote(rows(*blk) if src is None else src, rows(*blk), send_sems.at[k], recv_sems.at[k], to)

        mine = pltpu.make_async_copy(x_ref, rows(*me), local_sem)
        mine.start()
        first = [copy(0, me, sib, src=x_ref)] + [copy(1 + j, me, (*chip, c), src=x_ref) for j, chip in enumerate(chips)]
        for cp in first:
            cp.start()
        passed = [copy(4 + j, (*chip, c), sib) for j, chip in enumerate(chips)]
        for j, chip in enumerate(chips):
            copy(1 + j, (*chip, c), me).wait_recv()
            passed[j].start()
        copy(0, sib, me).wait_recv()
        for j, chip in enumerate(chips):
            copy(4 + j, (*chip, 1 - c), me).wait_recv()
        for cp in first + passed:
            cp.wait_send()
        mine.wait()

    vm = pl.BlockSpec(memory_space=pltpu.VMEM)
    return _pcall(body, name="gather_small_grads", in_specs=[vm], out_specs=vm,
                  out_shape=S((8 * m_per, ncol), block.dtype),
                  scratch_shapes=[pltpu.SemaphoreType.DMA((7,)), pltpu.SemaphoreType.DMA((7,)), pltpu.SemaphoreType.DMA],
                  compiler_params=pltpu.CompilerParams(vmem_limit_bytes=VMEM_LIMIT_BYTES))(block)


def _sum_pair(name, g, r1, c_idx):
    Q, _, Rr, Cc = g.shape
    tr = _tile(Rr, (512, 256, 128))

    def body(c_ref, g_ref, r_ref, o_ref):
        o_ref[...] = (g_ref[...].astype(f32) + r_ref[...].astype(f32)).astype(o_ref.dtype)

    spec = pl.BlockSpec((None, tr, Cc), lambda q, i, c_ref: (q, i, 0))
    gs = pltpu.PrefetchScalarGridSpec(
        num_scalar_prefetch=1, grid=(Q, Rr // tr),
        in_specs=[pl.BlockSpec((None, None, tr, Cc), lambda q, i, c_ref: (q, c_ref[0], i, 0)), spec], out_specs=spec)
    return _pcall(body, name=name, grid_spec=gs, out_shape=S((Q, Rr, Cc), bf16),
                  compiler_params=_params(("parallel", "parallel")))(c_idx, g, r1)


def _sum_owner(name, s, r2, q_idx):
    Q, Rr, Cc = s.shape
    tr = _tile(Rr, (512, 256, 128))

    def body(q_ref, s_ref, r_ref, o_ref):
        o_ref[...] = ((s_ref[...].astype(f32) + r_ref[0].astype(f32)) + r_ref[1].astype(f32)) + r_ref[2].astype(f32)

    gs = pltpu.PrefetchScalarGridSpec(
        num_scalar_prefetch=1, grid=(Rr // tr,),
        in_specs=[pl.BlockSpec((None, tr, Cc), lambda i, q_ref: (q_ref[0], i, 0)), pl.BlockSpec((3, tr, Cc), lambda i, q_ref: (0, i, 0))],
        out_specs=pl.BlockSpec((tr, Cc), lambda i, q_ref: (i, 0)))
    return _pcall(body, name=name, grid_spec=gs, out_shape=S((Rr, Cc), f32),
                  compiler_params=_params(("parallel",)))(q_idx, s, r2)


def _sum_devices(stack):
    n, M, C = stack.shape

    def body(s_ref, o_ref):
        acc = s_ref[0]
        for d in range(1, n):
            acc = acc + s_ref[d]
        o_ref[...] = acc

    return _pcall(body, name="sum_small_grads", out_shape=S((M, C), f32), compiler_params=_params())(stack)


def _adamw(name, w, g, m, v):
    Rr, Cc = w.shape
    tr = _tile(Rr, (256, 128, 64, 32, 16, 8))

    def body(w_ref, g_ref, m_ref, v_ref, d_ref, mo_ref, vo_ref):
        g_ = g_ref[...]
        m_ = B1 * m_ref[...] + (1.0 - B1) * g_
        v_ = B2 * v_ref[...] + (1.0 - B2) * jnp.square(g_)
        m_hat = m_ / (1.0 - B1 ** STEP)
        v_hat = v_ / (1.0 - B2 ** STEP)
        d_ref[...] = -LR * (m_hat / (jnp.sqrt(v_hat) + ADAM_EPS) + WD * w_ref[...])
        mo_ref[...] = m_
        vo_ref[...] = v_

    spec = pl.BlockSpec((tr, Cc), lambda i: (i, 0))
    return _pcall(body, name=name, grid=(Rr // tr,), in_specs=[spec] * 4, out_specs=[spec] * 3,
                  out_shape=[S((Rr, Cc), f32)] * 3, compiler_params=_params(("parallel",)))(w, g, m, v)


def _pack(arrs, multiple):
    flat = jnp.concatenate([a.reshape(-1) for a in arrs])
    pad = (-flat.shape[0]) % multiple
    return jnp.pad(flat, (0, pad)).reshape(-1, 128)


def _unpack(slab, shapes):
    flat, out, o = slab.reshape(-1), [], 0
    for shp in shapes:
        n = 1
        for d in shp:
            n *= d
        out.append(flat[o:o + n].reshape(shp))
        o += n
    return out


def kernel(x, meta_tokens, norm_mix, norm_ffn, conv_w_in, conv_b_in, conv_dw, conv_ln_g, conv_ln_b, conv_w_out, conv_b_out, kv_norm, w_kv, k_norm, w_q, q_norm, attn_sinks, w_o, ffn_w_gate, ffn_w_up, ffn_w_down, loss_target, m_meta_tokens, m_norm_mix, m_norm_ffn, m_conv_w_in, m_conv_b_in, m_conv_dw, m_conv_ln_g, m_conv_ln_b, m_conv_w_out, m_conv_b_out, m_kv_norm, m_w_kv, m_k_norm, m_w_q, m_q_norm, m_attn_sinks, m_w_o, m_ffn_w_gate, m_ffn_w_up, m_ffn_w_down, v_meta_tokens, v_norm_mix, v_norm_ffn, v_conv_w_in, v_conv_b_in, v_conv_dw, v_conv_ln_g, v_conv_ln_b, v_conv_w_out, v_conv_b_out, v_kv_norm, v_w_kv, v_k_norm, v_w_q, v_q_norm, v_attn_sinks, v_w_o, v_ffn_w_gate, v_ffn_w_up, v_ffn_w_down):
    Q = N_CHIPS
    B, SEQ, D = x.shape
    L = N_META_ROWS + SEQ
    Lp = -(-L // ATT_BLOCK) * ATT_BLOCK
    T = B * Lp
    NA, NB = conv_w_in.shape[0], w_q.shape[0]
    NL = NA + NB
    Dq = D // Q
    G = N_KV
    R = D // (HEAD_DIM * G)
    KVW = w_kv.shape[1]
    assert NA % 2 == 0 and NB % 2 == 0 and NL % 2 == 0 and (D // Q) % 32 == 0
    tm = _tile(T, (1088, 544, 512, 256, 128))
    tk = _tile(T, (544, 512, 256, 128))
    tr = _tile(T, (272, 256, 128))
    my_c = lax.axis_index("c").astype(jnp.int32).reshape(1)
    my_q = (2 * lax.axis_index("x") + lax.axis_index("y")).astype(jnp.int32)

    small_shapes = [meta_tokens.shape, conv_b_in.shape, conv_dw.shape, conv_ln_g.shape, conv_ln_b.shape, conv_b_out.shape]
    small = _pack([meta_tokens, conv_b_in, conv_dw, conv_ln_g, conv_ln_b, conv_b_out], 2048)
    half = lambda a: a.reshape((2, a.shape[0] // 2) + a.shape[1:])
    big = [conv_w_in, conv_w_out, w_kv, w_q, w_o, ffn_w_gate, ffn_w_up, ffn_w_down]
    gathered = _gather_weights([half(a.astype(bf16)) for a in big] + [half(small)])
    full = lambda a, like: a.reshape((Q,) + like.shape)
    Wcin, Wcout, Wkv, Wq, Wo, Wg, Wu, Wd = [full(a, like) for a, like in zip(gathered[:8], big)]
    Wkv = Wkv.reshape(Q, 1, Dq, KVW)
    parts = [_unpack(gathered[8][q], small_shapes) for q in range(Q)]
    meta_f, b_in_f, dw_f, ln_g_f, ln_b_f, b_out_f = [jnp.concatenate([parts[q][i] for q in range(Q)], axis=-1) for i in range(6)]
    dw_pad = jnp.pad(dw_f, ((0, 0), (0, CONV_PAD - CONV_TAPS), (0, 0)))

    h = jnp.concatenate([jnp.broadcast_to(meta_f[None], (B, N_META_ROWS, D)), x, jnp.zeros((B, Lp - L, D), f32)], axis=1).reshape(T, D)
    tgt = jnp.pad(loss_target, ((0, 0), (N_META_ROWS, Lp - L), (0, 0))).reshape(T, D)
    row = lambda a: a.reshape(1, -1)
    to_heads = lambda a, nh: a.reshape(B, Lp, G, nh, HEAD_DIM).transpose(0, 2, 3, 1, 4)
    from_heads = lambda a: a.transpose(0, 3, 1, 2, 4).reshape(T, -1)
    saved = []
    kp = vp = None
    sinks3 = attn_sinks.reshape(NB, G, 1, R)
    for l in range(NL):
        st = {"h_a": h}
        u = _rowwise(f"rms_mix{l}", _rms_fn, [h], [row(norm_mix[l])], [bf16], tr)[0]
        st["u"] = u
        if l < NA:
            p, av, ag = _glu_fwd(f"glu{l}", u, Wcin, l, b_in_f[l:l + 1], tm)
            p_pad = jnp.pad(p.reshape(B, Lp, D), ((0, 0), (CONV_PAD, 0), (0, 0)))
            cv = _dwconv_fwd(f"dwconv{l}", p_pad, dw_pad[l]).reshape(T, D)
            s = _rowwise(f"ln_silu{l}", _ln_silu_fn, [cv], [row(ln_g_f[l]), row(ln_b_f[l])], [bf16], tr)[0]
            h = _proj(f"conv_out{l}", s, Wcout, l, tm, bias=row(b_out_f[l]), resid=h)
            st.update(av=av, ag=ag, p_pad=p_pad, cv=cv, s=s)
        else:
            j = l - NA
            if j == 0:
                kvn = _rowwise("rms_kv", _rms_fn, [h], [row(kv_norm)], [bf16], tr)[0]
                kv = _proj("kv_proj", kvn, Wkv, 0, tm)
                kvh = jnp.pad(kv.reshape(B, Lp, 2, G, HEAD_DIM).transpose(2, 0, 3, 1, 4), ((0, 0), (0, 0), (0, 0), (ATT_BLOCK, 0), (0, 0)))
                kp, vp = kvh[0], kvh[1]
                st["kvn"] = kvn
            qh = to_heads(_proj(f"q_proj{j}", u, Wq, j, tm), R)
            o = from_heads(_attn_fwd(f"attn{j}", qh, kp, vp, row(q_norm[j]), row(k_norm), sinks3[j]).reshape(B, G, R, Lp, HEAD_DIM))
            h = _proj(f"o_proj{j}", o, Wo, j, tm, resid=h)
            st.update(qh=qh, o=o)
        st["h_b"] = h
        u2 = _rowwise(f"rms_ffn{l}", _rms_fn, [h], [row(norm_ffn[l])], [bf16], tr)[0]
        gate, up, hid = _ffn_up(f"ffn_up{l}", u2, Wg, Wu, l, tm)
        h = _ffn_down(f"ffn_down{l}", hid, Wd, l, h, tm)
        st.update(u2=u2, gate=gate, up=up, hid=hid)
        saved.append(st)

    dh, part = _loss_head(h, tgt, Lp, SEQ, tr)
    loss = lax.psum(0.5 / D * jnp.sum(part), ("x", "y", "c"))

    gW = {k: [None] * n for k, n in dict(cin=NA, cout=NA, q=NB, o=NB, g=NL, u=NL, d=NL).items()}
    g_mix, g_ffn = [None] * NL, [None] * NL
    g_bin, g_dw, g_lng, g_lnb, g_bout = ([None] * NA for _ in range(5))
    g_qn, g_sink = [None] * NB, [None] * NB
    g_kn = jnp.zeros((1, HEAD_DIM), f32)
    dkp = dvp = None
    zero_row = jnp.zeros((1, D), f32)
    for l in reversed(range(NL)):
        st = saved[l]
        dgate, dup = _ffn_dhid(f"ffn_dhid{l}", dh, Wd, l, st["gate"], st["up"], tm)
        gW["d"][l] = _dw_rows(f"ffn_dwd{l}", st["hid"], dh, Q, tk)
        gW["g"][l] = _dw_cols(f"ffn_dwg{l}", st["u2"], dgate, Q, tk)
        gW["u"][l] = _dw_cols(f"ffn_dwu{l}", st["u2"], dup, Q, tk)
        du2 = _ffn_du(f"ffn_du{l}", dgate, dup, Wg, Wu, l, tm)
        dh, g_ffn[l] = _rowwise_vjp(f"rms_ffn_bwd{l}", _rms_res_fn, [st["h_b"]], [row(norm_ffn[l])], [du2, dh], [f32], tr)
        if l < NA:
            ds = _proj_dx(f"conv_out_dx{l}", dh, Wcout, l, tm)
            gW["cout"][l] = _dw_rows(f"conv_out_dw{l}", st["s"], dh, Q, tk)
            dcv, g_lng[l], g_lnb[l] = _rowwise_vjp(f"ln_silu_bwd{l}", _ln_silu_fn, [st["cv"]], [row(ln_g_f[l]), row(ln_b_f[l])], [ds], [f32], tr)
            dcv_pad = jnp.pad(dcv.reshape(B, Lp, D), ((0, 0), (0, CONV_PAD), (0, 0)))
            dp, g_dw[l] = _dwconv_bwd(f"dwconv_bwd{l}", dcv_pad, st["p_pad"], dw_pad[l])
            dav, dag, dbv, dbg = _rowwise_vjp(f"glu_bwd{l}", _glu_fn, [st["av"], st["ag"]], [zero_row, zero_row], [dp.reshape(T, D)], [bf16, bf16], tr)
            g_bin[l] = jnp.concatenate([dbv, dbg], axis=1)
            gW["cin"][l] = jnp.concatenate([_dw_cols(f"glu_dwv{l}", st["u"], dav, Q // 2, tk),
                                            _dw_cols(f"glu_dwg{l}", st["u"], dag, Q // 2, tk)], axis=0)
            du = _glu_du(f"glu_du{l}", dav, dag, Wcin, l, tm)
            dh, g_mix[l], g_bout[l] = _rowwise_vjp(f"rms_mix_bwd{l}", _rms_res_bias_fn, [st["h_a"]], [row(norm_mix[l]), zero_row], [du, dh], [f32], tr)
        else:
            j = l - NA
            do = _proj_dx(f"o_proj_dx{j}", dh, Wo, j, tm, out_dtype=bf16)
            gW["o"][j] = _dw_rows(f"o_proj_dw{j}", st["o"], dh, Q, tk)
            dqh, dk1, dv1, g_qn[j], dkn, g_sink[j] = _attn_bwd(f"attn_bwd{j}", st["qh"], kp, vp, row(q_norm[j]), row(k_norm), sinks3[j], to_heads(do, R))
            g_kn = g_kn + dkn
            dkp, dvp = (dk1, dv1) if dkp is None else (dkp + dk1, dvp + dv1)
            dq = from_heads(dqh)
            gW["q"][j] = _dw_rows(f"q_proj_dw{j}", st["u"], dq, Q, tk)
            du = _proj_dx(f"q_proj_dx{j}", dq, Wq, j, tm)
            dh, g_mix[l] = _rowwise_vjp(f"rms_mix_bwd{l}", _rms_res_fn, [st["h_a"]], [row(norm_mix[l])], [du, dh], [f32], tr)
            if j == 0:
                dkv = jnp.stack([dkp, dvp])[:, :, :, ATT_BLOCK:].transpose(1, 3, 0, 2, 4).reshape(T, KVW)
                g_wkv = _dw_rows("kv_proj_dw", st["kvn"], dkv, Q, tk)
                dkvn = _proj_dx("kv_proj_dx", dkv, Wkv, 0, tm)
                dh, g_kvn = _rowwise_vjp("rms_kv_bwd", _rms_res_fn, [st["h_a"]], [row(kv_norm)], [dkvn, dh], [f32], tr)
    dh3 = dh.reshape(B, Lp, D)
    grad_x = dh3[:, N_META_ROWS:L]
    g_meta = jnp.sum(dh3[:, :N_META_ROWS], axis=0)

    stacked = [jnp.stack(gW[k], axis=1) for k in ("cin", "cout")] + [g_wkv] + [jnp.stack(gW[k], axis=1) for k in ("q", "o", "g", "u", "d")]
    as4 = lambda a: a.reshape(Q, 2, -1, a.shape[-1])
    grads4 = [as4(a) for a in stacked]
    from_sib = _exchange_halves(grads4)
    chip_sums = [_sum_pair(f"sum_pair{i}", g, r, my_c) for i, (g, r) in enumerate(zip(grads4, from_sib))]
    from_chips = _scatter_to_owners(chip_sums)
    mine = [_sum_owner(f"sum_owner{i}", s, r, my_q.reshape(1)) for i, (s, r) in enumerate(zip(chip_sums, from_chips))]
    shard_grads = _share_halves(mine)

    names = ["conv_w_in", "conv_w_out", "w_kv", "w_q", "w_o", "ffn_w_gate", "ffn_w_up", "ffn_w_down"]
    ws = dict(zip(names, big))
    ms = dict(zip(names, [m_conv_w_in, m_conv_w_out, m_w_kv, m_w_q, m_w_o, m_ffn_w_gate, m_ffn_w_up, m_ffn_w_down]))
    vs = dict(zip(names, [v_conv_w_in, v_conv_w_out, v_w_kv, v_w_q, v_w_o, v_ffn_w_gate, v_ffn_w_up, v_ffn_w_down]))
    out_g, out_d, out_m, out_v = {}, {}, {}, {}
    for nm, gsh in zip(names, shard_grads):
        w = ws[nm]
        flat = lambda a: a.reshape(-1, w.shape[-1])
        g2 = flat(gsh)
        d2, m2, v2 = _adamw(f"adamw_{nm}", flat(w), g2, flat(ms[nm]), flat(vs[nm]))
        out_g[nm], out_d[nm], out_m[nm], out_v[nm] = (a.reshape(w.shape) for a in (g2, d2, m2, v2))

    small_names = ["norm_mix", "norm_ffn", "kv_norm", "k_norm", "q_norm", "attn_sinks", "meta_tokens", "conv_b_in", "conv_dw", "conv_ln_g", "conv_ln_b", "conv_b_out"]
    small_grads = [jnp.concatenate(g_mix, 0), jnp.concatenate(g_ffn, 0), g_kvn.reshape(-1), g_kn.reshape(-1), jnp.concatenate(g_qn, 0),
                   jnp.stack(g_sink).reshape(NB, G * R), g_meta, jnp.concatenate(g_bin, 0), jnp.stack(g_dw)[:, :CONV_TAPS],
                   jnp.concatenate(g_lng, 0), jnp.concatenate(g_lnb, 0), jnp.concatenate(g_bout, 0)]
    slab = _pack(small_grads, 1024)
    total = _sum_devices(_gather_all_devices(slab).reshape(8, slab.shape[0], 128))
    full_grads = _unpack(total, [g.shape for g in small_grads])
    small_w = dict(zip(small_names, [norm_mix, norm_ffn, kv_norm, k_norm, q_norm, attn_sinks, meta_tokens, conv_b_in, conv_dw, conv_ln_g, conv_ln_b, conv_b_out]))
    small_m = dict(zip(small_names, [m_norm_mix, m_norm_ffn, m_kv_norm, m_k_norm, m_q_norm, m_attn_sinks, m_meta_tokens, m_conv_b_in, m_conv_dw, m_conv_ln_g, m_conv_ln_b, m_conv_b_out]))
    small_v = dict(zip(small_names, [v_norm_mix, v_norm_ffn, v_kv_norm, v_k_norm, v_q_norm, v_attn_sinks, v_meta_tokens, v_conv_b_in, v_conv_dw, v_conv_ln_g, v_conv_ln_b, v_conv_b_out]))
    local_grads = []
    for nm, g in zip(small_names, full_grads):
        w = small_w[nm]
        if g.shape != w.shape:
            wq = w.shape[-1]
            g = lax.dynamic_slice_in_dim(g, my_q * wq, wq, axis=g.ndim - 1)
        local_grads.append(g)
    shapes = [small_w[nm].shape for nm in small_names]
    d_s, m_s, v_s = _adamw("adamw_small", _pack([small_w[nm] for nm in small_names], 1024), _pack(local_grads, 1024),
                           _pack([small_m[nm] for nm in small_names], 1024), _pack([small_v[nm] for nm in small_names], 1024))
    for nm, g, d_, m_, v_ in zip(small_names, local_grads, _unpack(d_s, shapes), _unpack(m_s, shapes), _unpack(v_s, shapes)):
        out_g[nm], out_d[nm], out_m[nm], out_v[nm] = g, d_, m_, v_

    order = ["meta_tokens", "norm_mix", "norm_ffn", "conv_w_in", "conv_b_in", "conv_dw", "conv_ln_g", "conv_ln_b", "conv_w_out", "conv_b_out",
             "kv_norm", "w_kv", "k_norm", "w_q", "q_norm", "attn_sinks", "w_o", "ffn_w_gate", "ffn_w_up", "ffn_w_down"]
    return (loss, grad_x, *[out_g[n] for n in order], *[out_d[n] for n in order], *[out_m[n] for n in order], *[out_v[n] for n in order])
```

```python
import functools

import jax
import jax.numpy as jnp
from jax import lax
from jax.experimental import pallas as pl
from jax.experimental.pallas import tpu as pltpu

f32, bf16 = jnp.float32, jnp.bfloat16

N_META_ROWS = 16
ATT_BLOCK = 128
HEAD_DIM = 64
N_KV = 4
CONV_TAPS = 31
CONV_PAD = 32
EPS = 1e-6
MASKED = -1e30
LR, B1, B2, ADAM_EPS, WD, STEP = 0.001, 0.9, 0.999, 1e-08, 0.01, 10
N_CHIPS = 4
VMEM_LIMIT_BYTES = 56 * 1024 * 1024
MESH = pl.DeviceIdType.MESH
ANY = pl.BlockSpec(memory_space=pl.ANY)
S = jax.ShapeDtypeStruct


def _pcall(body, **kw):
    return pl.pallas_call(body, **kw)


def _params(sem=None):
    return pltpu.CompilerParams(dimension_semantics=sem, vmem_limit_bytes=VMEM_LIMIT_BYTES)


def _tile(n, prefs):
    for p in prefs:
        if n % p == 0:
            return p
    return n


_DN = {"nn": (((1,), (0,)), ((), ())), "nt": (((1,), (1,)), ((), ())), "tn": (((0,), (0,)), ((), ()))}


def _matmul(name, mode, grid, a_ops, b_ops, x_ops, outs, terms, acc_shape, n_acc, epilogue, into=None):
    na, nb, nx, no = len(a_ops), len(b_ops), len(x_ops), len(outs)
    nk = grid[2]
    n_in = na + nb + nx + (0 if into is None else 1)

    def body(*refs):
        a_refs, b_refs = refs[:na], refs[na:na + nb]
        x_refs = refs[na + nb:na + nb + nx]
        o_refs = refs[n_in:n_in + no]
        acc_refs = refs[n_in + no:]
        k = pl.program_id(2)

        @pl.when(k == 0)
        def _():
            for acc in acc_refs:
                acc[...] = jnp.zeros_like(acc)

        for ai, bi, ci in terms:
            a = a_refs[ai][...].astype(bf16)
            b = b_refs[bi][...].astype(bf16)
            acc_refs[ci][...] += lax.dot_general(a, b, _DN[mode], preferred_element_type=f32)

        @pl.when(k == nk - 1)
        def _():
            res = epilogue([acc[...] for acc in acc_refs], [x[...] for x in x_refs])
            for o_ref, r in zip(o_refs, res):
                o_ref[...] = r.astype(o_ref.dtype)

    ops = list(a_ops) + list(b_ops) + list(x_ops)
    aliases = {}
    if into is not None:
        ops.append((into, ANY))
        aliases = {len(ops) - 1: 0}
    res = _pcall(body, name=name, grid=grid, in_specs=[s for _, s in ops], out_specs=[s for _, s in outs],
                 out_shape=[s for s, _ in outs], scratch_shapes=[pltpu.VMEM(acc_shape, f32)] * n_acc,
                 input_output_aliases=aliases,
                 compiler_params=_params(("parallel", "parallel", "arbitrary")))(*[a for a, _ in ops])
    return res


def _first(accs, xs):
    return (accs[0],)


def _proj(name, a, w, l, tm, bias=None, resid=None, out_dtype=f32):
    T = a.shape[0]
    Q, _, Kq, N = w.shape
    tn = _tile(N, (512, 256, 128))
    x_ops, epi = [], _first
    if bias is not None:
        x_ops = [(bias, pl.BlockSpec((1, tn), lambda i, j, k: (0, j))), (resid, pl.BlockSpec((tm, tn), lambda i, j, k: (i, j)))]
        epi = lambda accs, xs: (accs[0] + xs[0] + xs[1],)
    elif resid is not None:
        x_ops = [(resid, pl.BlockSpec((tm, tn), lambda i, j, k: (i, j)))]
        epi = lambda accs, xs: (accs[0] + xs[0],)
    return _matmul(name, "nn", (T // tm, N // tn, Q),
                   [(a, pl.BlockSpec((tm, Kq), lambda i, j, k: (i, k)))],
                   [(w, pl.BlockSpec((None, None, Kq, tn), lambda i, j, k: (k, l, 0, j)))],
                   x_ops, [(S((T, N), out_dtype), pl.BlockSpec((tm, tn), lambda i, j, k: (i, j)))],
                   [(0, 0, 0)], (tm, tn), 1, epi)[0]


def _proj_dx(name, dy, w, l, tm, out_dtype=f32):
    T, N = dy.shape
    Q, _, Kq, _ = w.shape
    return _matmul(name, "nt", (T // tm, Q, 1),
                   [(dy, pl.BlockSpec((tm, N), lambda i, j, k: (i, 0)))],
                   [(w, pl.BlockSpec((None, None, Kq, N), lambda i, j, k: (j, l, 0, 0)))],
                   [], [(S((T, Q * Kq), out_dtype), pl.BlockSpec((tm, Kq), lambda i, j, k: (i, j)))],
                   [(0, 0, 0)], (tm, Kq), 1, _first)[0]


def _slot(nl, l, blocks_per_layer):
    per_half = nl // 2
    return l // per_half, (l % per_half) * blocks_per_layer


def _dw_rows(name, a, dy, Q, tk, nl, l, into):
    T, N = dy.shape
    tn = _tile(N, (512, 256, 128))
    if nl == 1:
        Kq = a.shape[1] // (2 * Q)
        a_op = (a, pl.BlockSpec((tk, Kq), lambda i, j, k: (k, i)))
        osh, ni = S((Q, 2, Kq, N), bf16), 2 * Q
        ospec = pl.BlockSpec((None, None, Kq, tn), lambda i, j, k: (i // 2, i % 2, 0, j))
    else:
        if a.ndim == 2:
            Kq = a.shape[1] // Q
            a_op = (a, pl.BlockSpec((tk, Kq), lambda i, j, k: (k, i)))
        else:
            Kq = a.shape[2]
            a_op = (a, pl.BlockSpec((None, tk, Kq), lambda i, j, k: (i, k, 0)))
        hf, rb = _slot(nl, l, 1)
        osh, ni = S((Q, 2, (nl // 2) * Kq, N), bf16), Q
        ospec = pl.BlockSpec((None, None, Kq, tn), lambda i, j, k: (i, hf, rb, j))
    return _matmul(name, "tn", (ni, N // tn, T // tk), [a_op], [(dy, pl.BlockSpec((tk, tn), lambda i, j, k: (k, j)))],
                   [], [(osh, ospec)], [(0, 0, 0)], (Kq, tn), 1, _first, into=into)[0]


def _dw_cols(name, a, dyc, Qc, tk, nl, l, into, Q, q_off=0):
    T, K = a.shape
    tkin = _tile(K, (512, 256, 128))
    if dyc.ndim == 3:
        Nq = dyc.shape[2]
        b_op = (dyc, pl.BlockSpec((None, tk, Nq), lambda i, j, k: (j, k, 0)))
    else:
        Nq = dyc.shape[1] // Qc
        b_op = (dyc, pl.BlockSpec((tk, Nq), lambda i, j, k: (k, j)))
    hf, rb = _slot(nl, l, K // tkin)
    ospec = pl.BlockSpec((None, None, tkin, Nq), lambda i, j, k: (j + q_off, hf, rb + i, 0))
    return _matmul(name, "tn", (K // tkin, Qc, T // tk),
                   [(a, pl.BlockSpec((tk, tkin), lambda i, j, k: (k, i)))], [b_op], [],
                   [(S((Q, 2, (nl // 2) * K, Nq), bf16), ospec)], [(0, 0, 0)], (tkin, Nq), 1, _first, into=into)[0]


def _glu_fwd(name, u, w, l, b_in, tm):
    T, D = u.shape
    Q, _, _, Cq = w.shape
    H = Q // 2

    def epi(accs, xs):
        av, ag = accs[0] + xs[0], accs[1] + xs[1]
        return av * jax.nn.sigmoid(ag), av, ag

    wspec = lambda off: pl.BlockSpec((None, None, D, Cq), lambda i, j, k: (j + off, l, 0, 0))
    bspec = lambda off: pl.BlockSpec((1, Cq), lambda i, j, k: (0, j + off))
    ospec = pl.BlockSpec((tm, Cq), lambda i, j, k: (i, j))
    return _matmul(name, "nn", (T // tm, H, 1),
                   [(u, pl.BlockSpec((tm, D), lambda i, j, k: (i, 0)))],
                   [(w, wspec(0)), (w, wspec(H))], [(b_in, bspec(0)), (b_in, bspec(H))],
                   [(S((T, H * Cq), f32), ospec), (S((T, H * Cq), bf16), ospec), (S((T, H * Cq), bf16), ospec)],
                   [(0, 0, 0), (0, 1, 1)], (tm, Cq), 2, epi)


def _glu_du(name, dav, dag, w, l, tm):
    T = dav.shape[0]
    Q, _, D, Cq = w.shape
    H = Q // 2
    tn = _tile(D, (512, 256, 128))
    aspec = pl.BlockSpec((tm, Cq), lambda i, j, k: (i, k))
    wspec = lambda off: pl.BlockSpec((None, None, tn, Cq), lambda i, j, k: (k + off, l, j, 0))
    return _matmul(name, "nt", (T // tm, D // tn, H), [(dav, aspec), (dag, aspec)],
                   [(w, wspec(0)), (w, wspec(H))], [],
                   [(S((T, D), f32), pl.BlockSpec((tm, tn), lambda i, j, k: (i, j)))],
                   [(0, 0, 0), (1, 1, 0)], (tm, tn), 1, _first)[0]


def _ffn_up(name, u, wg, wu, l, tm):
    T, D = u.shape
    Q, _, _, Fq = wg.shape

    def epi(accs, xs):
        g, up = accs
        return g, up, g * jax.nn.sigmoid(g) * up

    wspec = pl.BlockSpec((None, None, D, Fq), lambda i, j, k: (j, l, 0, 0))
    ospec = pl.BlockSpec((None, tm, Fq), lambda i, j, k: (j, i, 0))
    osh = S((Q, T, Fq), bf16)
    return _matmul(name, "nn", (T // tm, Q, 1), [(u, pl.BlockSpec((tm, D), lambda i, j, k: (i, 0)))],
                   [(wg, wspec), (wu, wspec)], [], [(osh, ospec)] * 3, [(0, 0, 0), (0, 1, 1)], (tm, Fq), 2, epi)


def _ffn_down(name, hid, wd, l, resid, tm):
    Q, T, Fq = hid.shape
    D = wd.shape[3]
    tn = _tile(D, (512, 256, 128))
    return _matmul(name, "nn", (T // tm, D // tn, Q),
                   [(hid, pl.BlockSpec((None, tm, Fq), lambda i, j, k: (k, i, 0)))],
                   [(wd, pl.BlockSpec((None, None, Fq, tn), lambda i, j, k: (k, l, 0, j)))],
                   [(resid, pl.BlockSpec((tm, tn), lambda i, j, k: (i, j)))],
                   [(S((T, D), f32), pl.BlockSpec((tm, tn), lambda i, j, k: (i, j)))],
                   [(0, 0, 0)], (tm, tn), 1, lambda accs, xs: (accs[0] + xs[0],))[0]


def _ffn_dhid(name, dy, wd, l, gate, up, tm):
    T, D = dy.shape
    Q, _, Fq, _ = wd.shape

    def epi(accs, xs):
        dh, g, up = accs[0], xs[0].astype(f32), xs[1].astype(f32)
        sg = jax.nn.sigmoid(g)
        return dh * up * (sg * (1.0 + g * (1.0 - sg))), dh * (g * sg)

    cspec = pl.BlockSpec((None, tm, Fq), lambda i, j, k: (j, i, 0))
    osh = S((Q, T, Fq), bf16)
    return _matmul(name, "nt", (T // tm, Q, 1), [(dy, pl.BlockSpec((tm, D), lambda i, j, k: (i, 0)))],
                   [(wd, pl.BlockSpec((None, None, Fq, D), lambda i, j, k: (j, l, 0, 0)))],
                   [(gate, cspec), (up, cspec)], [(osh, cspec)] * 2, [(0, 0, 0)], (tm, Fq), 1, epi)


def _ffn_du(name, dgate, dup, wg, wu, l, tm):
    Q, T, Fq = dgate.shape
    D = wg.shape[2]
    tn = _tile(D, (512, 256, 128))
    aspec = pl.BlockSpec((None, tm, Fq), lambda i, j, k: (k, i, 0))
    wspec = pl.BlockSpec((None, None, tn, Fq), lambda i, j, k: (k, l, j, 0))
    return _matmul(name, "nt", (T // tm, D // tn, Q), [(dgate, aspec), (dup, aspec)], [(wg, wspec), (wu, wspec)], [],
                   [(S((T, D), f32), pl.BlockSpec((tm, tn), lambda i, j, k: (i, j)))],
                   [(0, 0, 0), (1, 1, 0)], (tm, tn), 1, _first)[0]


def _rowwise(name, fn, rows, params, out_dtypes, tm):
    nr, npar = len(rows), len(params)
    T = rows[0].shape[0]
    shp = jax.eval_shape(fn, *[S((tm, r.shape[1]), f32) for r in rows], *[S(p.shape, f32) for p in params])

    def body(*refs):
        r = [x[...].astype(f32) for x in refs[:nr]]
        p = [x[...] for x in refs[nr:nr + npar]]
        for o_ref, o in zip(refs[nr + npar:], fn(*r, *p)):
            o_ref[...] = o.astype(o_ref.dtype)

    row_spec = lambda w: pl.BlockSpec((tm, w), lambda i: (i, 0))
    par_spec = lambda p: pl.BlockSpec(p.shape, lambda i: (0, 0))
    return _pcall(body, name=name, grid=(T // tm,),
                  in_specs=[row_spec(r.shape[1]) for r in rows] + [par_spec(p) for p in params],
                  out_specs=[row_spec(s.shape[1]) for s in shp],
                  out_shape=[S((T, s.shape[1]), dt) for s, dt in zip(shp, out_dtypes)],
                  compiler_params=_params(("parallel",)))(*rows, *params)


def _rowwise_vjp(name, fn, rows, params, cots, drow_dtypes, tm):
    nr, npar, nc = len(rows), len(params), len(cots)
    T = rows[0].shape[0]

    def body(*refs):
        r = [x[...].astype(f32) for x in refs[:nr]]
        p = [x[...] for x in refs[nr:nr + npar]]
        c = tuple(x[...].astype(f32) for x in refs[nr + npar:nr + npar + nc])
        o_refs = refs[nr + npar + nc:]
        _, vjp = jax.vjp(fn, *r, *p)
        grads = vjp(c)
        for o_ref, g in zip(o_refs[:nr], grads[:nr]):
            o_ref[...] = g.astype(o_ref.dtype)

        @pl.when(pl.program_id(0) == 0)
        def _():
            for o_ref in o_refs[nr:]:
                o_ref[...] = jnp.zeros_like(o_ref)

        for o_ref, g in zip(o_refs[nr:], grads[nr:]):
            o_ref[...] += g

    row_spec = lambda w: pl.BlockSpec((tm, w), lambda i: (i, 0))
    par_spec = lambda p: pl.BlockSpec(p.shape, lambda i: (0, 0))
    return _pcall(body, name=name, grid=(T // tm,),
                  in_specs=[row_spec(r.shape[1]) for r in rows] + [par_spec(p) for p in params] + [row_spec(c.shape[1]) for c in cots],
                  out_specs=[row_spec(r.shape[1]) for r in rows] + [par_spec(p) for p in params],
                  out_shape=[S(r.shape, dt) for r, dt in zip(rows, drow_dtypes)] + [S(p.shape, f32) for p in params],
                  compiler_params=_params(("arbitrary",)))(*rows, *params, *cots)


def _rms(h, g):
    return h * lax.rsqrt(jnp.mean(h * h, axis=-1, keepdims=True) + EPS) * g


def _rms_fn(h, g):
    return (_rms(h, g),)


def _rms_res_fn(h, g):
    return _rms(h, g), h


def _rms_res_bias_fn(h, g, b0):
    return _rms(h, g), h + b0


def _ln_silu_fn(c, g, b):
    mu = jnp.mean(c, axis=-1, keepdims=True)
    var = jnp.mean(jnp.square(c - mu), axis=-1, keepdims=True)
    y = (c - mu) * lax.rsqrt(var + EPS) * g + b
    return (y * jax.nn.sigmoid(y),)


def _glu_fn(av, ag, bv, bg):
    return ((av + bv) * jax.nn.sigmoid(ag + bg),)


def _loss_head(h, tgt, Lp, n_real, tm):
    T, D = h.shape

    def body(h_ref, t_ref, dy_ref, part_ref):
        i = pl.program_id(0)
        pos = (i * tm + lax.broadcasted_iota(jnp.int32, (tm, 1), 0)) % Lp
        real = (pos >= N_META_ROWS) & (pos < N_META_ROWS + n_real)
        err = jnp.where(real, h_ref[...] - t_ref[...], 0.0)
        dy_ref[...] = err * (1.0 / D)

        @pl.when(i == 0)
        def _():
            part_ref[...] = jnp.zeros_like(part_ref)

        part_ref[...] += jnp.sum(err * err, axis=0, keepdims=True)

    spec = pl.BlockSpec((tm, D), lambda i: (i, 0))
    return _pcall(body, name="loss_head", grid=(T // tm,), in_specs=[spec, spec],
                  out_specs=[spec, pl.BlockSpec((1, D), lambda i: (0, 0))],
                  out_shape=[S((T, D), f32), S((1, D), f32)], compiler_params=_params(("arbitrary",)))(h, tgt)


CONV_OFF = CONV_PAD - (CONV_TAPS - 1)
WIN_ROWS = ATT_BLOCK + CONV_PAD


def _phases(win):
    n = win.shape[0]
    return [win] + [win[b:n - 8 + b] for b in range(1, 8)]


def _tap(phases, o):
    a = (o // 8) * 8
    return phases[o % 8][a:a + ATT_BLOCK]


def _dwconv_fwd(name, p, dw):
    B, Lp, D = p.shape
    dc = _tile(D, (256, 128))

    def body(p_ref, w_ref, o_ref):
        def tile(win, base):
            ph = _phases(win)
            acc = jnp.zeros((ATT_BLOCK, dc), f32)
            for k in range(CONV_TAPS):
                acc = acc + _tap(ph, CONV_OFF + k) * w_ref[k:k + 1, :]
            o_ref[pl.ds(base, ATT_BLOCK), :] = acc

        tile(jnp.concatenate([jnp.zeros((CONV_PAD, dc), f32), p_ref[0:ATT_BLOCK, :]], axis=0), 0)

        def step(r, carry):
            base = pl.multiple_of(r * ATT_BLOCK, ATT_BLOCK)
            tile(p_ref[pl.ds(pl.multiple_of(base - CONV_PAD, CONV_PAD), WIN_ROWS), :], base)
            return carry

        lax.fori_loop(1, Lp // ATT_BLOCK, step, 0)

    seq = pl.BlockSpec((None, Lp, dc), lambda b, j: (b, 0, j))
    return _pcall(body, name=name, grid=(B, D // dc), in_specs=[seq, pl.BlockSpec((CONV_PAD, dc), lambda b, j: (0, j))],
                  out_specs=seq, out_shape=S((B, Lp, D), f32), compiler_params=_params(("parallel", "parallel")))(p, dw)


def _dwconv_bwd(name, dcv, p, dw):
    B, Lp, D = p.shape
    dcw = _tile(D, (256, 128))
    nblk = Lp // ATT_BLOCK
    assert nblk >= 2
    zeros = lambda: jnp.zeros((CONV_PAD, dcw), f32)

    def body(dc_ref, p_ref, w_ref, dp_ref, ddw_ref):
        @pl.when(pl.program_id(1) == 0)
        def _():
            ddw_ref[...] = jnp.zeros_like(ddw_ref)

        def tile(dwin, pwin, base):
            dph, pph = _phases(dwin), _phases(pwin)
            dtile = dwin[0:ATT_BLOCK]
            acc = jnp.zeros((ATT_BLOCK, dcw), f32)
            for k in range(CONV_TAPS):
                acc = acc + _tap(dph, CONV_TAPS - 1 - k) * w_ref[k:k + 1, :]
                ddw_ref[k:k + 1, :] += jnp.sum(dtile * _tap(pph, CONV_OFF + k), axis=0, keepdims=True)
            dp_ref[pl.ds(base, ATT_BLOCK), :] = acc

        tile(dc_ref[0:WIN_ROWS, :], jnp.concatenate([zeros(), p_ref[0:ATT_BLOCK, :]], axis=0), 0)

        def step(r, carry):
            base = pl.multiple_of(r * ATT_BLOCK, ATT_BLOCK)
            tile(dc_ref[pl.ds(base, WIN_ROWS), :], p_ref[pl.ds(pl.multiple_of(base - CONV_PAD, CONV_PAD), WIN_ROWS), :], base)
            return carry

        lax.fori_loop(1, nblk - 1, step, 0)
        last = Lp - ATT_BLOCK
        tile(jnp.concatenate([dc_ref[last:Lp, :], zeros()], axis=0), p_ref[last - CONV_PAD:Lp, :], last)

    seq = pl.BlockSpec((None, Lp, dcw), lambda j, b: (b, 0, j))
    wsp = pl.BlockSpec((CONV_PAD, dcw), lambda j, b: (0, j))
    return _pcall(body, name=name, grid=(D // dcw, B), in_specs=[seq, seq, wsp], out_specs=[seq, wsp],
                  out_shape=[S((B, Lp, D), f32), S((CONV_PAD, D), f32)],
                  compiler_params=_params(("parallel", "arbitrary")))(dcv, p, dw)


def _attn_mask(n, R):
    shape = (R * ATT_BLOCK, 3 * ATT_BLOCK)
    qpos = n * ATT_BLOCK + (lax.broadcasted_iota(jnp.int32, shape, 0) & (ATT_BLOCK - 1))
    col = lax.broadcasted_iota(jnp.int32, shape, 1)
    meta_ok = (col < N_META_ROWS) & (col <= qpos)
    band_pos = (n - 1) * ATT_BLOCK + (col - ATT_BLOCK)
    diff = qpos - band_pos
    band_ok = (col >= ATT_BLOCK) & (diff >= 0) & (diff < ATT_BLOCK) & (band_pos >= N_META_ROWS)
    return meta_ok | band_ok


def _row_head(R):
    row = lax.broadcasted_iota(jnp.int32, (R * ATT_BLOCK, R), 0)
    lo = lax.broadcasted_iota(jnp.int32, (R * ATT_BLOCK, R), 1) * ATT_BLOCK
    return ((row >= lo) & (row < lo + ATT_BLOCK)).astype(f32)


def _attn_core(q, kn, vv, qg, sink_vec, row_head, mask):
    qn = _rms(q, qg)
    s = lax.dot_general(qn.astype(bf16), kn.astype(bf16), _DN["nt"], preferred_element_type=f32) * (HEAD_DIM ** -0.5)
    s = jnp.where(mask, s, MASKED)
    sink = jnp.sum(row_head * sink_vec, axis=-1, keepdims=True)
    m = lax.stop_gradient(jnp.maximum(jnp.max(s, axis=-1, keepdims=True), sink))
    p = jnp.exp(s - m)
    denom = jnp.sum(p, axis=-1, keepdims=True) + jnp.exp(sink - m)
    return jnp.dot((p / denom).astype(bf16), vv.astype(bf16), preferred_element_type=f32)


def _keys_of(ref, n):
    meta = ref[ATT_BLOCK:2 * ATT_BLOCK, :]
    band = ref[pl.ds(pl.multiple_of(n * ATT_BLOCK, ATT_BLOCK), 2 * ATT_BLOCK), :]
    return jnp.concatenate([meta, band], axis=0)


def _attn_specs(R, Lk):
    qspec = pl.BlockSpec((None, None, R, ATT_BLOCK, HEAD_DIM), lambda g, b, n: (b, g, 0, n, 0))
    kspec = pl.BlockSpec((None, None, Lk, HEAD_DIM), lambda g, b, n: (b, g, 0, 0))
    gspec = pl.BlockSpec((1, HEAD_DIM), lambda g, b, n: (0, 0))
    sspec = pl.BlockSpec((None, 1, R), lambda g, b, n: (g, 0, 0))
    return qspec, kspec, gspec, sspec


def _attn_fwd(name, q, knp, vp, qg, sinks):
    B, G, R, Lp, _ = q.shape
    qspec, kspec, gspec, sspec = _attn_specs(R, knp.shape[2])

    def body(q_ref, k_ref, v_ref, qg_ref, s_ref, o_ref):
        n = pl.program_id(2)
        o = _attn_core(q_ref[...].reshape(R * ATT_BLOCK, HEAD_DIM), _keys_of(k_ref, n), _keys_of(v_ref, n),
                       qg_ref[...], s_ref[...], _row_head(R), _attn_mask(n, R))
        o_ref[...] = o.reshape(R, ATT_BLOCK, HEAD_DIM).astype(o_ref.dtype)

    return _pcall(body, name=name, grid=(G, B, Lp // ATT_BLOCK), in_specs=[qspec, kspec, kspec, gspec, sspec],
                  out_specs=qspec, out_shape=S(q.shape, bf16),
                  compiler_params=_params(("parallel", "parallel", "parallel")))(q, knp, vp, qg, sinks)


def _attn_bwd(name, q, knp, vp, qg, sinks, do):
    B, G, R, Lp, _ = q.shape
    qspec, kspec, gspec, sspec = _attn_specs(R, knp.shape[2])

    def body(q_ref, k_ref, v_ref, qg_ref, s_ref, do_ref, dq_ref, dk_ref, dv_ref, dqg_ref, ds_ref):
        g, b, n = pl.program_id(0), pl.program_id(1), pl.program_id(2)

        @pl.when((g == 0) & (b == 0) & (n == 0))
        def _():
            dqg_ref[...] = jnp.zeros_like(dqg_ref)

        @pl.when((b == 0) & (n == 0))
        def _():
            ds_ref[...] = jnp.zeros_like(ds_ref)

        @pl.when(n == 0)
        def _():
            dk_ref[...] = jnp.zeros_like(dk_ref)
            dv_ref[...] = jnp.zeros_like(dv_ref)

        row_head, mask = _row_head(R), _attn_mask(n, R)
        _, vjp = jax.vjp(lambda q_, k_, v_, a_, s_: _attn_core(q_, k_, v_, a_, s_, row_head, mask),
                         q_ref[...].reshape(R * ATT_BLOCK, HEAD_DIM), _keys_of(k_ref, n).astype(f32),
                         _keys_of(v_ref, n).astype(f32), qg_ref[...], s_ref[...])
        dq, dkk, dvv, dqg, dsk = vjp(do_ref[...].reshape(R * ATT_BLOCK, HEAD_DIM).astype(f32))
        dq_ref[...] = dq.reshape(R, ATT_BLOCK, HEAD_DIM)
        dqg_ref[...] += dqg
        ds_ref[...] += dsk
        band = pl.ds(pl.multiple_of(n * ATT_BLOCK, ATT_BLOCK), 2 * ATT_BLOCK)
        dk_ref[band, :] += dkk[ATT_BLOCK:]
        dv_ref[band, :] += dvv[ATT_BLOCK:]
        dk_ref[ATT_BLOCK:2 * ATT_BLOCK, :] += dkk[:ATT_BLOCK]
        dv_ref[ATT_BLOCK:2 * ATT_BLOCK, :] += dvv[:ATT_BLOCK]

    return _pcall(body, name=name, grid=(G, B, Lp // ATT_BLOCK),
                  in_specs=[qspec, kspec, kspec, gspec, sspec, qspec],
                  out_specs=[qspec, kspec, kspec, gspec, sspec],
                  out_shape=[S(q.shape, f32), S(knp.shape, f32), S(vp.shape, f32), S(qg.shape, f32), S(sinks.shape, f32)],
                  compiler_params=_params(("arbitrary", "arbitrary", "arbitrary")))(q, knp, vp, qg, sinks, do)


def _place():
    x, y, c = lax.axis_index("x"), lax.axis_index("y"), lax.axis_index("c")
    chips = [(1 - x, y), (x, 1 - y), (1 - x, 1 - y)]
    return x, y, c, chips, [2 * cx + cy for cx, cy in chips]


def _remote(src, dst, send_sem, recv_sem, to):
    return pltpu.make_async_remote_copy(src_ref=src, dst_ref=dst, send_sem=send_sem, recv_sem=recv_sem,
                                        device_id=to, device_id_type=MESH)


def _into_slot(name, shard, idx, slots, dtype):
    Rr, Cc = shard.shape
    tr = _tile(Rr, (512, 256, 128, 64, 32, 16))

    def body(i_ref, x_ref, o_ref):
        o_ref[...] = x_ref[...].astype(o_ref.dtype)

    gs = pltpu.PrefetchScalarGridSpec(
        num_scalar_prefetch=1, grid=(Rr // tr,),
        in_specs=[pl.BlockSpec((tr, Cc), lambda i, i_ref: (i, 0))],
        out_specs=pl.BlockSpec((None, tr, Cc), lambda i, i_ref: (i_ref[0], i, 0)))
    return _pcall(body, name=name, grid_spec=gs, out_shape=S((slots, Rr, Cc), dtype), compiler_params=_params(("parallel",)))(idx, shard)


def _gather_weights(slabs):
    n = len(slabs)

    def body(*refs):
        bufs = refs[n:2 * n]
        send, recv = refs[2 * n:]
        x, y, c, chips, qk = _place()
        me_q, sib = 2 * x + y, (x, y, 1 - c)
        first = [_remote(bufs[p].at[me_q, c], bufs[p].at[me_q, c], send.at[6 * p + k], recv.at[6 * p + k], (*chips[k], c))
                 for p in range(n) for k in range(3)]
        for cp in first:
            cp.start()
        passed = []
        for k in range(3):
            for p in range(n):
                land = bufs[p].at[qk[k], c]
                _remote(land, land, send.at[6 * p + k], recv.at[6 * p + k], sib).wait_recv()
                cp = _remote(land, land, send.at[6 * p + 3 + k], recv.at[6 * p + 3 + k], sib)
                cp.start()
                passed.append(cp)
        for k in range(3):
            for p in range(n):
                land = bufs[p].at[qk[k], 1 - c]
                _remote(land, land, send.at[6 * p + 3 + k], recv.at[6 * p + 3 + k], sib).wait_recv()
        for cp in first + passed:
            cp.wait_send()

    return _pcall(body, name="gather_weights", in_specs=[ANY] * n, out_specs=[ANY] * n,
                  out_shape=[S(s.shape, s.dtype) for s in slabs], input_output_aliases={p: p for p in range(n)},
                  scratch_shapes=[pltpu.SemaphoreType.DMA((6 * n,)), pltpu.SemaphoreType.DMA((6 * n,))])(*slabs)


def _exchange_halves(grads):
    n = len(grads)

    def body(*refs):
        ins, outs = refs[:n], refs[n:2 * n]
        send, recv = refs[2 * n:]
        x, y, c, _, _ = _place()
        sib = (x, y, 1 - c)
        cps = [_remote(ins[p].at[q, 1 - c], outs[p].at[q], send.at[N_CHIPS * p + q], recv.at[N_CHIPS * p + q], sib)
               for p in range(n) for q in range(N_CHIPS)]
        for cp in cps:
            cp.start()
        for cp in cps:
            cp.wait_recv()
        for cp in cps:
            cp.wait_send()

    return _pcall(body, name="exchange_halves", in_specs=[ANY] * n, out_specs=[ANY] * n,
                  out_shape=[S((N_CHIPS,) + g.shape[2:], g.dtype) for g in grads],
                  scratch_shapes=[pltpu.SemaphoreType.DMA((N_CHIPS * n,)), pltpu.SemaphoreType.DMA((N_CHIPS * n,))])(*grads)


def _scatter_to_owners(sums):
    n = len(sums)

    def body(*refs):
        ins, outs = refs[:n], refs[n:2 * n]
        send, recv = refs[2 * n:]
        x, y, c, chips, qk = _place()
        cps = [_remote(ins[p].at[qk[k]], outs[p].at[k], send.at[3 * p + k], recv.at[3 * p + k], (*chips[k], c))
               for p in range(n) for k in range(3)]
        for cp in cps:
            cp.start()
        for cp in cps:
            cp.wait_recv()
        for cp in cps:
            cp.wait_send()

    return _pcall(body, name="scatter_to_owners", in_specs=[ANY] * n, out_specs=[ANY] * n,
                  out_shape=[S((3,) + s.shape[1:], s.dtype) for s in sums],
                  scratch_shapes=[pltpu.SemaphoreType.DMA((3 * n,)), pltpu.SemaphoreType.DMA((3 * n,))])(*sums)


def _share_halves(slabs):
    n = len(slabs)

    def body(*refs):
        bufs = refs[n:2 * n]
        send, recv = refs[2 * n:]
        x, y, c, _, _ = _place()
        sib = (x, y, 1 - c)
        cps = [_remote(bufs[p].at[c], bufs[p].at[c], send.at[p], recv.at[p], sib) for p in range(n)]
        for cp in cps:
            cp.start()
        for p in range(n):
            _remote(bufs[p].at[1 - c], bufs[p].at[1 - c], send.at[p], recv.at[p], sib).wait_recv()
        for cp in cps:
            cp.wait_send()

    return _pcall(body, name="share_halves", in_specs=[ANY] * n, out_specs=[ANY] * n,
                  out_shape=[S(s.shape, s.dtype) for s in slabs], input_output_aliases={p: p for p in range(n)},
                  scratch_shapes=[pltpu.SemaphoreType.DMA((n,)), pltpu.SemaphoreType.DMA((n,))])(*slabs)


def _gather_all_devices(block):
    m_per, ncol = block.shape

    def body(x_ref, out_ref, send_sems, recv_sems, local_sem):
        x, y, c, chips, _ = _place()
        me, sib = (x, y, c), (x, y, 1 - c)

        def rows(px, py, pc):
            return out_ref.at[pl.ds((4 * px + 2 * py + pc) * m_per, m_per), :]

        def copy(k, blk, to, src=None):
            return _remote(rows(*blk) if src is None else src, rows(*blk), send_sems.at[k], recv_sems.at[k], to)

        mine = pltpu.make_async_copy(x_ref, rows(*me), local_sem)
        mine.start()
        first = [copy(0, me, sib, src=x_ref)] + [copy(1 + j, me, (*chip, c), src=x_ref) for j, chip in enumerate(chips)]
        for cp in first:
            cp.start()
        passed = [copy(4 + j, (*chip, c), sib) for j, chip in enumerate(chips)]
        for j, chip in enumerate(chips):
            copy(1 + j, (*chip, c), me).wait_recv()
            passed[j].start()
        copy(0, sib, me).wait_recv()
        for j, chip in enumerate(chips):
            copy(4 + j, (*chip, 1 - c), me).wait_recv()
        for cp in first + passed:
            cp.wait_send()
        mine.wait()

    vm = pl.BlockSpec(memory_space=pltpu.VMEM)
    return _pcall(body, name="gather_small_grads", in_specs=[vm], out_specs=vm,
                  out_shape=S((8 * m_per, ncol), block.dtype),
                  scratch_shapes=[pltpu.SemaphoreType.DMA((7,)), pltpu.SemaphoreType.DMA((7,)), pltpu.SemaphoreType.DMA],
                  compiler_params=pltpu.CompilerParams(vmem_limit_bytes=VMEM_LIMIT_BYTES))(block)


def _sum_pair(name, g, r1, c_idx):
    Q, _, Rr, Cc = g.shape
    tr = _tile(Rr, (512, 256, 128))

    def body(c_ref, g_ref, r_ref, o_ref):
        o_ref[...] = (g_ref[...].astype(f32) + r_ref[...].astype(f32)).astype(o_ref.dtype)

    spec = pl.BlockSpec((None, tr, Cc), lambda q, i, c_ref: (q, i, 0))
    gs = pltpu.PrefetchScalarGridSpec(
        num_scalar_prefetch=1, grid=(Q, Rr // tr),
        in_specs=[pl.BlockSpec((None, None, tr, Cc), lambda q, i, c_ref: (q, c_ref[0], i, 0)), spec], out_specs=spec)
    return _pcall(body, name=name, grid_spec=gs, out_shape=S((Q, Rr, Cc), bf16),
                  compiler_params=_params(("parallel", "parallel")))(c_idx, g, r1)


def _sum_owner(name, s, r2, qc_idx):
    Q, Rr, Cc = s.shape
    tr = _tile(Rr, (512, 256, 128))

    def body(q_ref, s_ref, r_ref, o_ref):
        o_ref[...] = ((s_ref[...].astype(f32) + r_ref[0].astype(f32)) + r_ref[1].astype(f32)) + r_ref[2].astype(f32)

    gs = pltpu.PrefetchScalarGridSpec(
        num_scalar_prefetch=1, grid=(Rr // tr,),
        in_specs=[pl.BlockSpec((None, tr, Cc), lambda i, q_ref: (q_ref[0], i, 0)), pl.BlockSpec((3, tr, Cc), lambda i, q_ref: (0, i, 0))],
        out_specs=pl.BlockSpec((None, tr, Cc), lambda i, q_ref: (q_ref[1], i, 0)))
    return _pcall(body, name=name, grid_spec=gs, out_shape=S((2, Rr, Cc), f32),
                  compiler_params=_params(("parallel",)))(qc_idx, s, r2)


def _sum_devices(stack):
    n, M, C = stack.shape

    def body(s_ref, o_ref):
        acc = s_ref[0]
        for d in range(1, n):
            acc = acc + s_ref[d]
        o_ref[...] = acc

    return _pcall(body, name="sum_small_grads", out_shape=S((M, C), f32), compiler_params=_params())(stack)


def _adamw(name, w, g, m, v):
    Rr, Cc = w.shape
    tr = _tile(Rr, (256, 128, 64, 32, 16, 8))

    def body(w_ref, g_ref, m_ref, v_ref, d_ref, mo_ref, vo_ref):
        g_ = g_ref[...]
        m_ = B1 * m_ref[...] + (1.0 - B1) * g_
        v_ = B2 * v_ref[...] + (1.0 - B2) * jnp.square(g_)
        m_hat = m_ / (1.0 - B1 ** STEP)
        v_hat = v_ / (1.0 - B2 ** STEP)
        d_ref[...] = -LR * (m_hat / (jnp.sqrt(v_hat) + ADAM_EPS) + WD * w_ref[...])
        mo_ref[...] = m_
        vo_ref[...] = v_

    spec = pl.BlockSpec((tr, Cc), lambda i: (i, 0))
    return _pcall(body, name=name, grid=(Rr // tr,), in_specs=[spec] * 4, out_specs=[spec] * 3,
                  out_shape=[S((Rr, Cc), f32)] * 3, compiler_params=_params(("parallel",)))(w, g, m, v)


def _pack(arrs, multiple):
    flat = jnp.concatenate([a.reshape(-1) for a in arrs])
    pad = (-flat.shape[0]) % multiple
    return jnp.pad(flat, (0, pad)).reshape(-1, 128)


def _unpack(slab, shapes):
    flat, out, o = slab.reshape(-1), [], 0
    for shp in shapes:
        n = 1
        for d in shp:
            n *= d
        out.append(flat[o:o + n].reshape(shp))
        o += n
    return out


def kernel(x, meta_tokens, norm_mix, norm_ffn, conv_w_in, conv_b_in, conv_dw, conv_ln_g, conv_ln_b, conv_w_out, conv_b_out, kv_norm, w_kv, k_norm, w_q, q_norm, attn_sinks, w_o, ffn_w_gate, ffn_w_up, ffn_w_down, loss_target, m_meta_tokens, m_norm_mix, m_norm_ffn, m_conv_w_in, m_conv_b_in, m_conv_dw, m_conv_ln_g, m_conv_ln_b, m_conv_w_out, m_conv_b_out, m_kv_norm, m_w_kv, m_k_norm, m_w_q, m_q_norm, m_attn_sinks, m_w_o, m_ffn_w_gate, m_ffn_w_up, m_ffn_w_down, v_meta_tokens, v_norm_mix, v_norm_ffn, v_conv_w_in, v_conv_b_in, v_conv_dw, v_conv_ln_g, v_conv_ln_b, v_conv_w_out, v_conv_b_out, v_kv_norm, v_w_kv, v_k_norm, v_w_q, v_q_norm, v_attn_sinks, v_w_o, v_ffn_w_gate, v_ffn_w_up, v_ffn_w_down):
    Q = N_CHIPS
    B, SEQ, D = x.shape
    L = N_META_ROWS + SEQ
    Lp = -(-L // ATT_BLOCK) * ATT_BLOCK
    T = B * Lp
    NA, NB = conv_w_in.shape[0], w_q.shape[0]
    NL = NA + NB
    Dq = D // Q
    G = N_KV
    R = D // (HEAD_DIM * G)
    KVW = w_kv.shape[1]
    assert NA % 2 == 0 and NB % 2 == 0 and NL % 2 == 0 and (D // Q) % 32 == 0
    tm = _tile(T, (1088, 544, 512, 256, 128))
    tk = _tile(T, (1088, 544, 512, 256, 128))
    tr = _tile(T, (272, 256, 128))
    my_c = lax.axis_index("c").astype(jnp.int32).reshape(1)
    my_q = (2 * lax.axis_index("x") + lax.axis_index("y")).astype(jnp.int32)
    my_qc = jnp.concatenate([my_q.reshape(1), my_c])

    small_shapes = [meta_tokens.shape, conv_b_in.shape, conv_dw.shape, conv_ln_g.shape, conv_ln_b.shape, conv_b_out.shape]
    small = _pack([meta_tokens, conv_b_in, conv_dw, conv_ln_g, conv_ln_b, conv_b_out], 2048)
    big = [conv_w_in, conv_w_out, w_kv, w_q, w_o, ffn_w_gate, ffn_w_up, ffn_w_down]
    slabs = [_into_slot(f"own_slot{i}", a.reshape(-1, a.shape[-1]), my_qc, Q, bf16 if i < len(big) else f32)
             for i, a in enumerate(big + [small])]
    gathered = _gather_weights([s.reshape(Q, 2, s.shape[1] // 2, s.shape[2]) for s in slabs])
    full = lambda a, like: a.reshape((Q,) + like.shape)
    Wcin, Wcout, Wkv, Wq, Wo, Wg, Wu, Wd = [full(a, like) for a, like in zip(gathered[:8], big)]
    Wkv = Wkv.reshape(Q, 1, Dq, KVW)
    parts = [_unpack(gathered[8][q], small_shapes) for q in range(Q)]
    meta_f, b_in_f, dw_f, ln_g_f, ln_b_f, b_out_f = [jnp.concatenate([parts[q][i] for q in range(Q)], axis=-1) for i in range(6)]
    dw_pad = jnp.pad(dw_f, ((0, 0), (0, CONV_PAD - CONV_TAPS), (0, 0)))

    h = jnp.concatenate([jnp.broadcast_to(meta_f[None], (B, N_META_ROWS, D)), x, jnp.zeros((B, Lp - L, D), f32)], axis=1).reshape(T, D)
    tgt = jnp.pad(loss_target, ((0, 0), (N_META_ROWS, Lp - L), (0, 0))).reshape(T, D)
    row = lambda a: a.reshape(1, -1)
    to_heads = lambda a, nh: a.reshape(B, Lp, G, nh, HEAD_DIM).transpose(0, 2, 3, 1, 4)
    from_heads = lambda a: a.transpose(0, 3, 1, 2, 4).reshape(T, -1)
    saved = []
    kp = vp = None
    sinks3 = attn_sinks.reshape(NB, G, 1, R)
    for l in range(NL):
        st = {"h_a": h}
        u = _rowwise(f"rms_mix{l}", _rms_fn, [h], [row(norm_mix[l])], [bf16], tr)[0]
        st["u"] = u
        if l < NA:
            p, av, ag = _glu_fwd(f"glu{l}", u, Wcin, l, b_in_f[l:l + 1], tm)
            cv = _dwconv_fwd(f"dwconv{l}", p.reshape(B, Lp, D), dw_pad[l]).reshape(T, D)
            s = _rowwise(f"ln_silu{l}", _ln_silu_fn, [cv], [row(ln_g_f[l]), row(ln_b_f[l])], [bf16], tr)[0]
            h = _proj(f"conv_out{l}", s, Wcout, l, tm, bias=row(b_out_f[l]), resid=h)
            st.update(av=av, ag=ag, p=p, cv=cv, s=s)
        else:
            j = l - NA
            if j == 0:
                kvn = _rowwise("rms_kv", _rms_fn, [h], [row(kv_norm)], [bf16], tr)[0]
                kv = _proj("kv_proj", kvn, Wkv, 0, tm)
                kvh = jnp.pad(kv.reshape(B, Lp, 2, G, HEAD_DIM).transpose(2, 0, 3, 1, 4), ((0, 0), (0, 0), (0, 0), (ATT_BLOCK, 0), (0, 0)))
                kp = kvh[0].reshape(-1, HEAD_DIM)
                thd = _tile(kp.shape[0], (1024, 512, 256, 128))
                knp = _rowwise("k_norm", _rms_fn, [kp], [row(k_norm)], [bf16], thd)[0].reshape(kvh.shape[1:])
                vp = kvh[1].astype(bf16)
                st["kvn"] = kvn
            qh = to_heads(_proj(f"q_proj{j}", u, Wq, j, tm), R)
            o = from_heads(_attn_fwd(f"attn{j}", qh, knp, vp, row(q_norm[j]), sinks3[j]))
            h = _proj(f"o_proj{j}", o, Wo, j, tm, resid=h)
            st.update(qh=qh, o=o)
        st["h_b"] = h
        u2 = _rowwise(f"rms_ffn{l}", _rms_fn, [h], [row(norm_ffn[l])], [bf16], tr)[0]
        gate, up, hid = _ffn_up(f"ffn_up{l}", u2, Wg, Wu, l, tm)
        h = _ffn_down(f"ffn_down{l}", hid, Wd, l, h, tm)
        st.update(u2=u2, gate=gate, up=up, hid=hid)
        saved.append(st)

    dh, part = _loss_head(h, tgt, Lp, SEQ, tr)
    loss = lax.psum(0.5 / D * jnp.sum(part), ("x", "y", "c"))

    gW = dict(cin=None, cout=None, q=None, o=None, g=None, u=None, d=None)
    g_mix, g_ffn = [None] * NL, [None] * NL
    g_bin, g_dw, g_lng, g_lnb, g_bout = ([None] * NA for _ in range(5))
    g_qn, g_sink = [None] * NB, [None] * NB
    dknp = dvp = None
    zero_row = jnp.zeros((1, D), f32)
    for l in reversed(range(NL)):
        st = saved[l]
        dgate, dup = _ffn_dhid(f"ffn_dhid{l}", dh, Wd, l, st["gate"], st["up"], tm)
        gW["d"] = _dw_rows(f"ffn_dwd{l}", st["hid"], dh, Q, tk, NL, l, gW["d"])
        gW["g"] = _dw_cols(f"ffn_dwg{l}", st["u2"], dgate, Q, tk, NL, l, gW["g"], Q)
        gW["u"] = _dw_cols(f"ffn_dwu{l}", st["u2"], dup, Q, tk, NL, l, gW["u"], Q)
        du2 = _ffn_du(f"ffn_du{l}", dgate, dup, Wg, Wu, l, tm)
        dh, g_ffn[l] = _rowwise_vjp(f"rms_ffn_bwd{l}", _rms_res_fn, [st["h_b"]], [row(norm_ffn[l])], [du2, dh], [f32], tr)
        if l < NA:
            ds = _proj_dx(f"conv_out_dx{l}", dh, Wcout, l, tm)
            gW["cout"] = _dw_rows(f"conv_out_dw{l}", st["s"], dh, Q, tk, NA, l, gW["cout"])
            dcv, g_lng[l], g_lnb[l] = _rowwise_vjp(f"ln_silu_bwd{l}", _ln_silu_fn, [st["cv"]], [row(ln_g_f[l]), row(ln_b_f[l])], [ds], [f32], tr)
            dp, g_dw[l] = _dwconv_bwd(f"dwconv_bwd{l}", dcv.reshape(B, Lp, D), st["p"].reshape(B, Lp, D), dw_pad[l])
            dav, dag, dbv, dbg = _rowwise_vjp(f"glu_bwd{l}", _glu_fn, [st["av"], st["ag"]], [zero_row, zero_row], [dp.reshape(T, D)], [bf16, bf16], tr)
            g_bin[l] = jnp.concatenate([dbv, dbg], axis=1)
            gW["cin"] = _dw_cols(f"glu_dwv{l}", st["u"], dav, Q // 2, tk, NA, l, gW["cin"], Q)
            gW["cin"] = _dw_cols(f"glu_dwg{l}", st["u"], dag, Q // 2, tk, NA, l, gW["cin"], Q, q_off=Q // 2)
            du = _glu_du(f"glu_du{l}", dav, dag, Wcin, l, tm)
            dh, g_mix[l], g_bout[l] = _rowwise_vjp(f"rms_mix_bwd{l}", _rms_res_bias_fn, [st["h_a"]], [row(norm_mix[l]), zero_row], [du, dh], [f32], tr)
        else:
            j = l - NA
            do = _proj_dx(f"o_proj_dx{j}", dh, Wo, j, tm, out_dtype=bf16)
            gW["o"] = _dw_rows(f"o_proj_dw{j}", st["o"], dh, Q, tk, NB, j, gW["o"])
            dqh, dk1, dv1, g_qn[j], g_sink[j] = _attn_bwd(f"attn_bwd{j}", st["qh"], knp, vp, row(q_norm[j]), sinks3[j], to_heads(do, R))
            dknp, dvp = (dk1, dv1) if dknp is None else (dknp + dk1, dvp + dv1)
            dq = from_heads(dqh)
            gW["q"] = _dw_rows(f"q_proj_dw{j}", st["u"], dq, Q, tk, NB, j, gW["q"])
            du = _proj_dx(f"q_proj_dx{j}", dq, Wq, j, tm)
            dh, g_mix[l] = _rowwise_vjp(f"rms_mix_bwd{l}", _rms_res_fn, [st["h_a"]], [row(norm_mix[l])], [du, dh], [f32], tr)
            if j == 0:
                dkp, g_kn = _rowwise_vjp("k_norm_bwd", _rms_fn, [kp], [row(k_norm)], [dknp.reshape(-1, HEAD_DIM)], [f32], thd)
                dkv = jnp.stack([dkp.reshape(dvp.shape), dvp])[:, :, :, ATT_BLOCK:].transpose(1, 3, 0, 2, 4).reshape(T, KVW)
                g_wkv = _dw_rows("kv_proj_dw", st["kvn"], dkv, Q, tk, 1, 0, None)
                dkvn = _proj_dx("kv_proj_dx", dkv, Wkv, 0, tm)
                dh, g_kvn = _rowwise_vjp("rms_kv_bwd", _rms_res_fn, [st["h_a"]], [row(kv_norm)], [dkvn, dh], [f32], tr)
    dh3 = dh.reshape(B, Lp, D)
    grad_x = dh3[:, N_META_ROWS:L]
    g_meta = jnp.sum(dh3[:, :N_META_ROWS], axis=0)

    grads4 = [gW["cin"], gW["cout"], g_wkv, gW["q"], gW["o"], gW["g"], gW["u"], gW["d"]]
    from_sib = _exchange_halves(grads4)
    chip_sums = [_sum_pair(f"sum_pair{i}", g, r, my_c) for i, (g, r) in enumerate(zip(grads4, from_sib))]
    from_chips = _scatter_to_owners(chip_sums)
    shard_grads = _share_halves([_sum_owner(f"sum_owner{i}", s, r, my_qc) for i, (s, r) in enumerate(zip(chip_sums, from_chips))])

    names = ["conv_w_in", "conv_w_out", "w_kv", "w_q", "w_o", "ffn_w_gate", "ffn_w_up", "ffn_w_down"]
    ws = dict(zip(names, big))
    ms = dict(zip(names, [m_conv_w_in, m_conv_w_out, m_w_kv, m_w_q, m_w_o, m_ffn_w_gate, m_ffn_w_up, m_ffn_w_down]))
    vs = dict(zip(names, [v_conv_w_in, v_conv_w_out, v_w_kv, v_w_q, v_w_o, v_ffn_w_gate, v_ffn_w_up, v_ffn_w_down]))
    out_g, out_d, out_m, out_v = {}, {}, {}, {}
    for nm, gsh in zip(names, shard_grads):
        w = ws[nm]
        flat = lambda a: a.reshape(-1, w.shape[-1])
        g2 = flat(gsh)
        d2, m2, v2 = _adamw(f"adamw_{nm}", flat(w), g2, flat(ms[nm]), flat(vs[nm]))
        out_g[nm], out_d[nm], out_m[nm], out_v[nm] = (a.reshape(w.shape) for a in (g2, d2, m2, v2))

    small_names = ["norm_mix", "norm_ffn", "kv_norm", "k_norm", "q_norm", "attn_sinks", "meta_tokens", "conv_b_in", "conv_dw", "conv_ln_g", "conv_ln_b", "conv_b_out"]
    small_grads = [jnp.concatenate(g_mix, 0), jnp.concatenate(g_ffn, 0), g_kvn.reshape(-1), g_kn.reshape(-1), jnp.concatenate(g_qn, 0),
                   jnp.stack(g_sink).reshape(NB, G * R), g_meta, jnp.concatenate(g_bin, 0), jnp.stack(g_dw)[:, :CONV_TAPS],
                   jnp.concatenate(g_lng, 0), jnp.concatenate(g_lnb, 0), jnp.concatenate(g_bout, 0)]
    slab = _pack(small_grads, 1024)
    total = _sum_devices(_gather_all_devices(slab).reshape(8, slab.shape[0], 128))
    full_grads = _unpack(total, [g.shape for g in small_grads])
    small_w = dict(zip(small_names, [norm_mix, norm_ffn, kv_norm, k_norm, q_norm, attn_sinks, meta_tokens, conv_b_in, conv_dw, conv_ln_g, conv_ln_b, conv_b_out]))
    small_m = dict(zip(small_names, [m_norm_mix, m_norm_ffn, m_kv_norm, m_k_norm, m_q_norm, m_attn_sinks, m_meta_tokens, m_conv_b_in, m_conv_dw, m_conv_ln_g, m_conv_ln_b, m_conv_b_out]))
    small_v = dict(zip(small_names, [v_norm_mix, v_norm_ffn, v_kv_norm, v_k_norm, v_q_norm, v_attn_sinks, v_meta_tokens, v_conv_b_in, v_conv_dw, v_conv_ln_g, v_conv_ln_b, v_conv_b_out]))
    local_grads = []
    for nm, g in zip(small_names, full_grads):
        w = small_w[nm]
        if g.shape != w.shape:
            wq = w.shape[-1]
            g = lax.dynamic_slice_in_dim(g, my_q * wq, wq, axis=g.ndim - 1)
        local_grads.append(g)
    shapes = [small_w[nm].shape for nm in small_names]
    d_s, m_s, v_s = _adamw("adamw_small", _pack([small_w[nm] for nm in small_names], 1024), _pack(local_grads, 1024),
                           _pack([small_m[nm] for nm in small_names], 1024), _pack([small_v[nm] for nm in small_names], 1024))
    for nm, g, d_, m_, v_ in zip(small_names, local_grads, _unpack(d_s, shapes), _unpack(m_s, shapes), _unpack(v_s, shapes)):
        out_g[nm], out_d[nm], out_m[nm], out_v[nm] = g, d_, m_, v_

    order = ["meta_tokens", "norm_mix", "norm_ffn", "conv_w_in", "conv_b_in", "conv_dw", "conv_ln_g", "conv_ln_b", "conv_w_out", "conv_b_out",
             "kv_norm", "w_kv", "k_norm", "w_q", "q_norm", "attn_sinks", "w_o", "ffn_w_gate", "ffn_w_up", "ffn_w_down"]
    return (loss, grad_x, *[out_g[n] for n in order], *[out_d[n] for n in order], *[out_m[n] for n in order], *[out_v[n] for n in order])
```

```python
import functools

import jax
import jax.numpy as jnp
from jax import lax
from jax.experimental import pallas as pl
from jax.experimental.pallas import tpu as pltpu

f32, bf16 = jnp.float32, jnp.bfloat16

N_META_ROWS = 16
ATT_BLOCK = 128
HEAD_DIM = 64
N_KV = 4
CONV_TAPS = 31
CONV_PAD = 32
EPS = 1e-6
MASKED = -1e30
LR, B1, B2, ADAM_EPS, WD, STEP = 0.001, 0.9, 0.999, 1e-08, 0.01, 10
N_CHIPS = 4
VMEM_LIMIT_BYTES = 56 * 1024 * 1024
MESH = pl.DeviceIdType.MESH
ANY = pl.BlockSpec(memory_space=pl.ANY)
S = jax.ShapeDtypeStruct


def _pcall(body, **kw):
    return pl.pallas_call(body, **kw)


def _params(sem=None):
    return pltpu.CompilerParams(dimension_semantics=sem, vmem_limit_bytes=VMEM_LIMIT_BYTES)


def _tile(n, prefs):
    for p in prefs:
        if n % p == 0:
            return p
    return n


_DN = {"nn": (((1,), (0,)), ((), ())), "nt": (((1,), (1,)), ((), ())), "tn": (((0,), (0,)), ((), ()))}


def _matmul(name, mode, grid, a_ops, b_ops, x_ops, outs, terms, acc_shape, n_acc, epilogue, into=None):
    na, nb, nx, no = len(a_ops), len(b_ops), len(x_ops), len(outs)
    nk = grid[2]
    n_in = na + nb + nx + (0 if into is None else 1)

    def flat2d(v):
        return v.reshape(-1, v.shape[-1]) if v.ndim == 3 else v

    def dots(a_refs, b_refs):
        parts = [None] * n_acc
        for ai, bi, ci in terms:
            d = lax.dot_general(flat2d(a_refs[ai][...]).astype(bf16), flat2d(b_refs[bi][...]).astype(bf16), _DN[mode],
                                preferred_element_type=f32)
            parts[ci] = d if parts[ci] is None else parts[ci] + d
        return parts

    def finish(accs, x_refs, o_refs):
        res = epilogue(accs, [x[...] for x in x_refs])
        for o_ref, r in zip(o_refs, res):
            o_ref[...] = r.reshape(o_ref.shape).astype(o_ref.dtype)

    def body(*refs):
        a_refs, b_refs = refs[:na], refs[na:na + nb]
        x_refs = refs[na + nb:na + nb + nx]
        o_refs = refs[n_in:n_in + no]
        acc_refs = refs[n_in + no:]
        if nk == 1:
            finish(dots(a_refs, b_refs), x_refs, o_refs)
            return
        k = pl.program_id(2)

        @pl.when(k == 0)
        def _():
            for acc in acc_refs:
                acc[...] = jnp.zeros_like(acc)

        for acc, d in zip(acc_refs, dots(a_refs, b_refs)):
            acc[...] += d

        @pl.when(k == nk - 1)
        def _():
            finish([acc[...] for acc in acc_refs], x_refs, o_refs)

    ops = list(a_ops) + list(b_ops) + list(x_ops)
    aliases = {}
    if into is not None:
        ops.append((into, ANY))
        aliases = {len(ops) - 1: 0}
    res = _pcall(body, name=name, grid=grid, in_specs=[s for _, s in ops], out_specs=[s for _, s in outs],
                 out_shape=[s for s, _ in outs], scratch_shapes=[pltpu.VMEM(acc_shape, f32)] * (n_acc if nk > 1 else 0),
                 input_output_aliases=aliases,
                 compiler_params=_params(("parallel", "parallel", "arbitrary")))(*[a for a, _ in ops])
    return res


def _first(accs, xs):
    return (accs[0],)


def _proj(name, a, w, l, tm, bias=None, resid=None, out_dtype=f32):
    T = a.shape[0]
    Q, _, Kq, N = w.shape
    tn = _tile(N, (512, 256, 128))
    x_ops, epi = [], _first
    if bias is not None:
        x_ops = [(bias, pl.BlockSpec((1, tn), lambda i, j, k: (0, j))), (resid, pl.BlockSpec((tm, tn), lambda i, j, k: (i, j)))]
        epi = lambda accs, xs: (accs[0] + xs[0] + xs[1],)
    elif resid is not None:
        x_ops = [(resid, pl.BlockSpec((tm, tn), lambda i, j, k: (i, j)))]
        epi = lambda accs, xs: (accs[0] + xs[0],)
    return _matmul(name, "nn", (T // tm, N // tn, 1),
                   [(a, pl.BlockSpec((tm, Q * Kq), lambda i, j, k: (i, 0)))],
                   [(w, pl.BlockSpec((Q, None, Kq, tn), lambda i, j, k: (0, l, 0, j)))],
                   x_ops, [(S((T, N), out_dtype), pl.BlockSpec((tm, tn), lambda i, j, k: (i, j)))],
                   [(0, 0, 0)], (tm, tn), 1, epi)[0]


def _proj_dx(name, dy, w, l, tm, out_dtype=f32):
    T, N = dy.shape
    Q, _, Kq, _ = w.shape
    return _matmul(name, "nt", (T // tm, 1, 1),
                   [(dy, pl.BlockSpec((tm, N), lambda i, j, k: (i, 0)))],
                   [(w, pl.BlockSpec((Q, None, Kq, N), lambda i, j, k: (0, l, 0, 0)))],
                   [], [(S((T, Q * Kq), out_dtype), pl.BlockSpec((tm, Q * Kq), lambda i, j, k: (i, 0)))],
                   [(0, 0, 0)], (tm, Q * Kq), 1, _first)[0]


def _slot(nl, l, blocks_per_layer):
    per_half = nl // 2
    return l // per_half, (l % per_half) * blocks_per_layer


def _dw_rows(name, a, dy, Q, tk, nl, l, into):
    T, N = dy.shape
    tn = _tile(N, (512, 256, 128))
    if a.ndim == 2:
        K = a.shape[1]
        a_op, ni, acc_rows = (a, pl.BlockSpec((tk, K), lambda i, j, k: (k, 0))), 1, K
        if nl == 1:
            osh = S((Q, 2, K // (2 * Q), N), bf16)
            ospec = pl.BlockSpec((Q, 2, K // (2 * Q), tn), lambda i, j, k: (0, 0, 0, j))
        else:
            hf, rb = _slot(nl, l, 1)
            osh = S((Q, 2, (nl // 2) * (K // Q), N), bf16)
            ospec = pl.BlockSpec((Q, None, K // Q, tn), lambda i, j, k: (0, hf, rb, j))
    else:
        Kq = a.shape[2]
        a_op, ni, acc_rows = (a, pl.BlockSpec((None, tk, Kq), lambda i, j, k: (i, k, 0))), Q, Kq
        hf, rb = _slot(nl, l, 1)
        osh = S((Q, 2, (nl // 2) * Kq, N), bf16)
        ospec = pl.BlockSpec((None, None, Kq, tn), lambda i, j, k: (i, hf, rb, j))
    return _matmul(name, "tn", (ni, N // tn, T // tk), [a_op], [(dy, pl.BlockSpec((tk, tn), lambda i, j, k: (k, j)))],
                   [], [(osh, ospec)], [(0, 0, 0)], (acc_rows, tn), 1, _first, into=into)[0]


def _dw_cols(name, a, dyc, Qc, tk, nl, l, into, Q, q_off=0):
    T, K = a.shape
    tkin = _tile(K, (512, 256, 128))
    if dyc.ndim == 3:
        Nq = dyc.shape[2]
        b_op = (dyc, pl.BlockSpec((None, tk, Nq), lambda i, j, k: (j, k, 0)))
    else:
        Nq = dyc.shape[1] // Qc
        b_op = (dyc, pl.BlockSpec((tk, Nq), lambda i, j, k: (k, j)))
    hf, rb = _slot(nl, l, K // tkin)
    ospec = pl.BlockSpec((None, None, tkin, Nq), lambda i, j, k: (j + q_off, hf, rb + i, 0))
    return _matmul(name, "tn", (K // tkin, Qc, T // tk),
                   [(a, pl.BlockSpec((tk, tkin), lambda i, j, k: (k, i)))], [b_op], [],
                   [(S((Q, 2, (nl // 2) * K, Nq), bf16), ospec)], [(0, 0, 0)], (tkin, Nq), 1, _first, into=into)[0]


def _glu_fwd(name, u, w, l, b_in, tm):
    T, D = u.shape
    Q, _, _, Cq = w.shape
    H = Q // 2

    def epi(accs, xs):
        av, ag = accs[0] + xs[0], accs[1] + xs[1]
        return av * jax.nn.sigmoid(ag), av, ag

    wspec = lambda off: pl.BlockSpec((None, None, D, Cq), lambda i, j, k: (j + off, l, 0, 0))
    bspec = lambda off: pl.BlockSpec((1, Cq), lambda i, j, k: (0, j + off))
    ospec = pl.BlockSpec((tm, Cq), lambda i, j, k: (i, j))
    return _matmul(name, "nn", (T // tm, H, 1),
                   [(u, pl.BlockSpec((tm, D), lambda i, j, k: (i, 0)))],
                   [(w, wspec(0)), (w, wspec(H))], [(b_in, bspec(0)), (b_in, bspec(H))],
                   [(S((T, H * Cq), f32), ospec), (S((T, H * Cq), bf16), ospec), (S((T, H * Cq), bf16), ospec)],
                   [(0, 0, 0), (0, 1, 1)], (tm, Cq), 2, epi)


def _glu_du(name, dav, dag, w, l, tm):
    T = dav.shape[0]
    Q, _, D, Cq = w.shape
    H = Q // 2
    tn = _tile(D, (512, 256, 128))
    aspec = pl.BlockSpec((tm, Cq), lambda i, j, k: (i, k))
    wspec = lambda off: pl.BlockSpec((None, None, tn, Cq), lambda i, j, k: (k + off, l, j, 0))
    return _matmul(name, "nt", (T // tm, D // tn, H), [(dav, aspec), (dag, aspec)],
                   [(w, wspec(0)), (w, wspec(H))], [],
                   [(S((T, D), f32), pl.BlockSpec((tm, tn), lambda i, j, k: (i, j)))],
                   [(0, 0, 0), (1, 1, 0)], (tm, tn), 1, _first)[0]


def _ffn_up(name, u, wg, wu, l, tm):
    T, D = u.shape
    Q, _, _, Fq = wg.shape

    def epi(accs, xs):
        g, up = accs
        return g, up, g * jax.nn.sigmoid(g) * up

    wspec = pl.BlockSpec((None, None, D, Fq), lambda i, j, k: (j, l, 0, 0))
    ospec = pl.BlockSpec((None, tm, Fq), lambda i, j, k: (j, i, 0))
    osh = S((Q, T, Fq), bf16)
    return _matmul(name, "nn", (T // tm, Q, 1), [(u, pl.BlockSpec((tm, D), lambda i, j, k: (i, 0)))],
                   [(wg, wspec), (wu, wspec)], [], [(osh, ospec)] * 3, [(0, 0, 0), (0, 1, 1)], (tm, Fq), 2, epi)


def _ffn_down(name, hid, wd, l, resid, tm):
    Q, T, Fq = hid.shape
    D = wd.shape[3]
    tn = _tile(D, (512, 256, 128))
    return _matmul(name, "nn", (T // tm, D // tn, Q),
                   [(hid, pl.BlockSpec((None, tm, Fq), lambda i, j, k: (k, i, 0)))],
                   [(wd, pl.BlockSpec((None, None, Fq, tn), lambda i, j, k: (k, l, 0, j)))],
                   [(resid, pl.BlockSpec((tm, tn), lambda i, j, k: (i, j)))],
                   [(S((T, D), f32), pl.BlockSpec((tm, tn), lambda i, j, k: (i, j)))],
                   [(0, 0, 0)], (tm, tn), 1, lambda accs, xs: (accs[0] + xs[0],))[0]


def _ffn_dhid(name, dy, wd, l, gate, up, tm):
    T, D = dy.shape
    Q, _, Fq, _ = wd.shape

    def epi(accs, xs):
        dh, g, up = accs[0], xs[0].astype(f32), xs[1].astype(f32)
        sg = jax.nn.sigmoid(g)
        return dh * up * (sg * (1.0 + g * (1.0 - sg))), dh * (g * sg)

    cspec = pl.BlockSpec((None, tm, Fq), lambda i, j, k: (j, i, 0))
    osh = S((Q, T, Fq), bf16)
    return _matmul(name, "nt", (T // tm, Q, 1), [(dy, pl.BlockSpec((tm, D), lambda i, j, k: (i, 0)))],
                   [(wd, pl.BlockSpec((None, None, Fq, D), lambda i, j, k: (j, l, 0, 0)))],
                   [(gate, cspec), (up, cspec)], [(osh, cspec)] * 2, [(0, 0, 0)], (tm, Fq), 1, epi)


def _ffn_du(name, dgate, dup, wg, wu, l, tm):
    Q, T, Fq = dgate.shape
    D = wg.shape[2]
    tn = _tile(D, (512, 256, 128))
    aspec = pl.BlockSpec((None, tm, Fq), lambda i, j, k: (k, i, 0))
    wspec = pl.BlockSpec((None, None, tn, Fq), lambda i, j, k: (k, l, j, 0))
    return _matmul(name, "nt", (T // tm, D // tn, Q), [(dgate, aspec), (dup, aspec)], [(wg, wspec), (wu, wspec)], [],
                   [(S((T, D), f32), pl.BlockSpec((tm, tn), lambda i, j, k: (i, j)))],
                   [(0, 0, 0), (1, 1, 0)], (tm, tn), 1, _first)[0]


def _rowwise(name, fn, rows, params, out_dtypes, tm):
    nr, npar = len(rows), len(params)
    T = rows[0].shape[0]
    shp = jax.eval_shape(fn, *[S((tm, r.shape[1]), f32) for r in rows], *[S(p.shape, f32) for p in params])

    def body(*refs):
        r = [x[...].astype(f32) for x in refs[:nr]]
        p = [x[...] for x in refs[nr:nr + npar]]
        for o_ref, o in zip(refs[nr + npar:], fn(*r, *p)):
            o_ref[...] = o.astype(o_ref.dtype)

    row_spec = lambda w: pl.BlockSpec((tm, w), lambda i: (i, 0))
    par_spec = lambda p: pl.BlockSpec(p.shape, lambda i: (0, 0))
    return _pcall(body, name=name, grid=(T // tm,),
                  in_specs=[row_spec(r.shape[1]) for r in rows] + [par_spec(p) for p in params],
                  out_specs=[row_spec(s.shape[1]) for s in shp],
                  out_shape=[S((T, s.shape[1]), dt) for s, dt in zip(shp, out_dtypes)],
                  compiler_params=_params(("parallel",)))(*rows, *params)


def _rowwise_vjp(name, fn, rows, params, cots, drow_dtypes, tm):
    nr, npar, nc = len(rows), len(params), len(cots)
    T = rows[0].shape[0]

    def body(*refs):
        r = [x[...].astype(f32) for x in refs[:nr]]
        p = [x[...] for x in refs[nr:nr + npar]]
        c = tuple(x[...].astype(f32) for x in refs[nr + npar:nr + npar + nc])
        o_refs = refs[nr + npar + nc:]
        _, vjp = jax.vjp(fn, *r, *p)
        grads = vjp(c)
        for o_ref, g in zip(o_refs[:nr], grads[:nr]):
            o_ref[...] = g.astype(o_ref.dtype)

        @pl.when(pl.program_id(0) == 0)
        def _():
            for o_ref in o_refs[nr:]:
                o_ref[...] = jnp.zeros_like(o_ref)

        for o_ref, g in zip(o_refs[nr:], grads[nr:]):
            o_ref[...] += g

    row_spec = lambda w: pl.BlockSpec((tm, w), lambda i: (i, 0))
    par_spec = lambda p: pl.BlockSpec(p.shape, lambda i: (0, 0))
    return _pcall(body, name=name, grid=(T // tm,),
                  in_specs=[row_spec(r.shape[1]) for r in rows] + [par_spec(p) for p in params] + [row_spec(c.shape[1]) for c in cots],
                  out_specs=[row_spec(r.shape[1]) for r in rows] + [par_spec(p) for p in params],
                  out_shape=[S(r.shape, dt) for r, dt in zip(rows, drow_dtypes)] + [S(p.shape, f32) for p in params],
                  compiler_params=_params(("arbitrary",)))(*rows, *params, *cots)


def _rms(h, g):
    return h * lax.rsqrt(jnp.mean(h * h, axis=-1, keepdims=True) + EPS) * g


def _rms_fn(h, g):
    return (_rms(h, g),)


def _rms_res_fn(h, g):
    return _rms(h, g), h


def _rms_res_bias_fn(h, g, b0):
    return _rms(h, g), h + b0


def _ln_silu_fn(c, g, b):
    mu = jnp.mean(c, axis=-1, keepdims=True)
    var = jnp.mean(jnp.square(c - mu), axis=-1, keepdims=True)
    y = (c - mu) * lax.rsqrt(var + EPS) * g + b
    return (y * jax.nn.sigmoid(y),)


def _glu_fn(av, ag, bv, bg):
    return ((av + bv) * jax.nn.sigmoid(ag + bg),)


def _loss_head(h, tgt, Lp, n_real, tm):
    T, D = h.shape

    def body(h_ref, t_ref, dy_ref, part_ref):
        i = pl.program_id(0)
        pos = (i * tm + lax.broadcasted_iota(jnp.int32, (tm, 1), 0)) % Lp
        real = (pos >= N_META_ROWS) & (pos < N_META_ROWS + n_real)
        err = jnp.where(real, h_ref[...] - t_ref[...], 0.0)
        dy_ref[...] = err * (1.0 / D)

        @pl.when(i == 0)
        def _():
            part_ref[...] = jnp.zeros_like(part_ref)

        part_ref[...] += jnp.sum(err * err, axis=0, keepdims=True)

    spec = pl.BlockSpec((tm, D), lambda i: (i, 0))
    return _pcall(body, name="loss_head", grid=(T // tm,), in_specs=[spec, spec],
                  out_specs=[spec, pl.BlockSpec((1, D), lambda i: (0, 0))],
                  out_shape=[S((T, D), f32), S((1, D), f32)], compiler_params=_params(("arbitrary",)))(h, tgt)


CONV_OFF = CONV_PAD - (CONV_TAPS - 1)
WIN_ROWS = ATT_BLOCK + CONV_PAD


def _phases(ph_ref, win):
    n = win.shape[0]
    for b in range(1, 8):
        ph_ref[b - 1, 0:n - 8, :] = win[b:n - 8 + b]

    def tap(o):
        a = (o // 8) * 8
        return win[a:a + ATT_BLOCK] if o % 8 == 0 else ph_ref[o % 8 - 1, a:a + ATT_BLOCK, :]

    return tap


def _phase_scratch(width):
    return pltpu.VMEM((7, WIN_ROWS - 8, width), f32)


def _dwconv_fwd(name, p, dw):
    B, Lp, D = p.shape
    dc = _tile(D, (256, 128))

    def body(p_ref, w_ref, o_ref, ph_ref):
        def tile(win, base):
            tap = _phases(ph_ref, win)
            acc = jnp.zeros((ATT_BLOCK, dc), f32)
            for k in range(CONV_TAPS):
                acc = acc + tap(CONV_OFF + k) * w_ref[k:k + 1, :]
            o_ref[pl.ds(base, ATT_BLOCK), :] = acc

        tile(jnp.concatenate([jnp.zeros((CONV_PAD, dc), f32), p_ref[0:ATT_BLOCK, :]], axis=0), 0)

        def step(r, carry):
            base = pl.multiple_of(r * ATT_BLOCK, ATT_BLOCK)
            tile(p_ref[pl.ds(pl.multiple_of(base - CONV_PAD, CONV_PAD), WIN_ROWS), :], base)
            return carry

        lax.fori_loop(1, Lp // ATT_BLOCK, step, 0)

    seq = pl.BlockSpec((None, Lp, dc), lambda b, j: (b, 0, j))
    return _pcall(body, name=name, grid=(B, D // dc), in_specs=[seq, pl.BlockSpec((CONV_PAD, dc), lambda b, j: (0, j))],
                  out_specs=seq, out_shape=S((B, Lp, D), f32), scratch_shapes=[_phase_scratch(dc)],
                  compiler_params=_params(("parallel", "parallel")))(p, dw)


def _dwconv_bwd(name, dcv, p, dw):
    B, Lp, D = p.shape
    dcw = _tile(D, (256, 128))
    nblk = Lp // ATT_BLOCK
    assert nblk >= 2
    zeros = lambda: jnp.zeros((CONV_PAD, dcw), f32)

    def body(dc_ref, p_ref, w_ref, dp_ref, ddw_ref, dph_ref, pph_ref):
        @pl.when(pl.program_id(1) == 0)
        def _():
            ddw_ref[...] = jnp.zeros_like(ddw_ref)

        def tile(dwin, pwin, base):
            dtap, ptap = _phases(dph_ref, dwin), _phases(pph_ref, pwin)
            dtile = dwin[0:ATT_BLOCK]
            acc = jnp.zeros((ATT_BLOCK, dcw), f32)
            for k in range(CONV_TAPS):
                acc = acc + dtap(CONV_TAPS - 1 - k) * w_ref[k:k + 1, :]
                ddw_ref[k:k + 1, :] += jnp.sum(dtile * ptap(CONV_OFF + k), axis=0, keepdims=True)
            dp_ref[pl.ds(base, ATT_BLOCK), :] = acc

        tile(dc_ref[0:WIN_ROWS, :], jnp.concatenate([zeros(), p_ref[0:ATT_BLOCK, :]], axis=0), 0)

        def step(r, carry):
            base = pl.multiple_of(r * ATT_BLOCK, ATT_BLOCK)
            tile(dc_ref[pl.ds(base, WIN_ROWS), :], p_ref[pl.ds(pl.multiple_of(base - CONV_PAD, CONV_PAD), WIN_ROWS), :], base)
            return carry

        lax.fori_loop(1, nblk - 1, step, 0)
        last = Lp - ATT_BLOCK
        tile(jnp.concatenate([dc_ref[last:Lp, :], zeros()], axis=0), p_ref[last - CONV_PAD:Lp, :], last)

    seq = pl.BlockSpec((None, Lp, dcw), lambda j, b: (b, 0, j))
    wsp = pl.BlockSpec((CONV_PAD, dcw), lambda j, b: (0, j))
    return _pcall(body, name=name, grid=(D // dcw, B), in_specs=[seq, seq, wsp], out_specs=[seq, wsp],
                  out_shape=[S((B, Lp, D), f32), S((CONV_PAD, D), f32)], scratch_shapes=[_phase_scratch(dcw)] * 2,
                  compiler_params=_params(("parallel", "arbitrary")))(dcv, p, dw)


def _band_start(n):
    return pl.multiple_of(jnp.maximum(n - 1, 0) * ATT_BLOCK, ATT_BLOCK)


def _attn_mask(n, R):
    shape = (R * ATT_BLOCK, 3 * ATT_BLOCK)
    qpos = n * ATT_BLOCK + (lax.broadcasted_iota(jnp.int32, shape, 0) & (ATT_BLOCK - 1))
    col = lax.broadcasted_iota(jnp.int32, shape, 1)
    meta_ok = (col < N_META_ROWS) & (col <= qpos)
    band_pos = _band_start(n) + (col - ATT_BLOCK)
    diff = qpos - band_pos
    band_ok = (col >= ATT_BLOCK) & (diff >= 0) & (diff < ATT_BLOCK) & (band_pos >= N_META_ROWS)
    return meta_ok | band_ok


def _row_head(R):
    row = lax.broadcasted_iota(jnp.int32, (R * ATT_BLOCK, R), 0)
    lo = lax.broadcasted_iota(jnp.int32, (R * ATT_BLOCK, R), 1) * ATT_BLOCK
    return ((row >= lo) & (row < lo + ATT_BLOCK)).astype(f32)


def _attn_core(q, kn, vv, qg, sink_vec, row_head, mask):
    qn = _rms(q, qg)
    s = lax.dot_general(qn.astype(bf16), kn.astype(bf16), _DN["nt"], preferred_element_type=f32) * (HEAD_DIM ** -0.5)
    s = jnp.where(mask, s, MASKED)
    sink = jnp.sum(row_head * sink_vec, axis=-1, keepdims=True)
    m = lax.stop_gradient(jnp.maximum(jnp.max(s, axis=-1, keepdims=True), sink))
    p = jnp.exp(s - m)
    denom = jnp.sum(p, axis=-1, keepdims=True) + jnp.exp(sink - m)
    return jnp.dot((p / denom).astype(bf16), vv.astype(bf16), preferred_element_type=f32)


def _keys_of(ref, n):
    return jnp.concatenate([ref[0:ATT_BLOCK, :], ref[pl.ds(_band_start(n), 2 * ATT_BLOCK), :]], axis=0)


def _stack_heads(x, R):
    return x if R == 1 else jnp.concatenate([x[:, r * HEAD_DIM:(r + 1) * HEAD_DIM] for r in range(R)], axis=0)


def _unstack_heads(x, R):
    return x if R == 1 else jnp.concatenate([x[r * ATT_BLOCK:(r + 1) * ATT_BLOCK] for r in range(R)], axis=1)


def _attn_specs(R, Lp):
    qspec = pl.BlockSpec((None, ATT_BLOCK, R * HEAD_DIM), lambda g, b, n: (b, n, g))
    kspec = pl.BlockSpec((None, None, Lp, HEAD_DIM), lambda g, b, n: (g, b, 0, 0))
    gspec = pl.BlockSpec((1, HEAD_DIM), lambda g, b, n: (0, 0))
    sspec = pl.BlockSpec((None, 1, R), lambda g, b, n: (g, 0, 0))
    return qspec, kspec, gspec, sspec


def _attn_fwd(name, q, kn, v, qg, sinks):
    B, Lp, D = q.shape
    G = kn.shape[0]
    R = D // (G * HEAD_DIM)
    qspec, kspec, gspec, sspec = _attn_specs(R, Lp)

    def body(q_ref, k_ref, v_ref, qg_ref, s_ref, o_ref):
        n = pl.program_id(2)
        o = _attn_core(_stack_heads(q_ref[...], R), _keys_of(k_ref, n), _keys_of(v_ref, n),
                       qg_ref[...], s_ref[...], _row_head(R), _attn_mask(n, R))
        o_ref[...] = _unstack_heads(o, R).astype(o_ref.dtype)

    return _pcall(body, name=name, grid=(G, B, Lp // ATT_BLOCK), in_specs=[qspec, kspec, kspec, gspec, sspec],
                  out_specs=qspec, out_shape=S(q.shape, bf16),
                  compiler_params=_params(("parallel", "parallel", "parallel")))(q, kn, v, qg, sinks)


def _attn_bwd(name, q, kn, v, qg, sinks, do):
    B, Lp, D = q.shape
    G = kn.shape[0]
    R = D // (G * HEAD_DIM)
    qspec, kspec, gspec, sspec = _attn_specs(R, Lp)

    def body(q_ref, k_ref, v_ref, qg_ref, s_ref, do_ref, dq_ref, dk_ref, dv_ref, dqg_ref, ds_ref):
        g, b, n = pl.program_id(0), pl.program_id(1), pl.program_id(2)

        @pl.when((g == 0) & (b == 0) & (n == 0))
        def _():
            dqg_ref[...] = jnp.zeros_like(dqg_ref)

        @pl.when((b == 0) & (n == 0))
        def _():
            ds_ref[...] = jnp.zeros_like(ds_ref)

        @pl.when(n == 0)
        def _():
            dk_ref[...] = jnp.zeros_like(dk_ref)
            dv_ref[...] = jnp.zeros_like(dv_ref)

        row_head, mask = _row_head(R), _attn_mask(n, R)
        _, vjp = jax.vjp(lambda q_, k_, v_, a_, s_: _attn_core(q_, k_, v_, a_, s_, row_head, mask),
                         _stack_heads(q_ref[...], R), _keys_of(k_ref, n).astype(f32),
                         _keys_of(v_ref, n).astype(f32), qg_ref[...], s_ref[...])
        dq, dkk, dvv, dqg, dsk = vjp(_stack_heads(do_ref[...].astype(f32), R))
        dq_ref[...] = _unstack_heads(dq, R)
        dqg_ref[...] += dqg
        ds_ref[...] += dsk
        band = pl.ds(_band_start(n), 2 * ATT_BLOCK)
        dk_ref[band, :] += dkk[ATT_BLOCK:]
        dv_ref[band, :] += dvv[ATT_BLOCK:]
        dk_ref[0:ATT_BLOCK, :] += dkk[:ATT_BLOCK]
        dv_ref[0:ATT_BLOCK, :] += dvv[:ATT_BLOCK]

    return _pcall(body, name=name, grid=(G, B, Lp // ATT_BLOCK),
                  in_specs=[qspec, kspec, kspec, gspec, sspec, qspec],
                  out_specs=[qspec, kspec, kspec, gspec, sspec],
                  out_shape=[S(q.shape, f32), S(kn.shape, f32), S(v.shape, f32), S(qg.shape, f32), S(sinks.shape, f32)],
                  compiler_params=_params(("arbitrary", "arbitrary", "arbitrary")))(q, kn, v, qg, sinks, do)


def _kv_heads(name, kv, kg, tm):
    T, W = kv.shape
    G = W // (2 * HEAD_DIM)

    def body(kv_ref, kg_ref, k_ref, v_ref):
        x = kv_ref[...]
        for g in range(G):
            k_ref[g] = _rms(x[:, g * HEAD_DIM:(g + 1) * HEAD_DIM], kg_ref[...]).astype(bf16)
            v_ref[g] = x[:, (G + g) * HEAD_DIM:(G + g + 1) * HEAD_DIM].astype(bf16)

    hspec = pl.BlockSpec((G, tm, HEAD_DIM), lambda i: (0, i, 0))
    return _pcall(body, name=name, grid=(T // tm,),
                  in_specs=[pl.BlockSpec((tm, W), lambda i: (i, 0)), pl.BlockSpec((1, HEAD_DIM), lambda i: (0, 0))],
                  out_specs=[hspec, hspec], out_shape=[S((G, T, HEAD_DIM), bf16)] * 2,
                  compiler_params=_params(("parallel",)))(kv, kg)


def _kv_heads_bwd(name, kv, kg, dkn, dv, tm):
    T, W = kv.shape
    G = W // (2 * HEAD_DIM)

    def body(kv_ref, kg_ref, dk_ref, dv_ref, o_ref, dkg_ref):
        @pl.when(pl.program_id(0) == 0)
        def _():
            dkg_ref[...] = jnp.zeros_like(dkg_ref)

        x = kv_ref[...]
        pieces = []
        for g in range(G):
            _, vjp = jax.vjp(_rms, x[:, g * HEAD_DIM:(g + 1) * HEAD_DIM], kg_ref[...])
            dk, dkg = vjp(dk_ref[g])
            pieces.append(dk)
            dkg_ref[...] += dkg
        o_ref[...] = jnp.concatenate(pieces + [dv_ref[g] for g in range(G)], axis=1)

    hspec = pl.BlockSpec((G, tm, HEAD_DIM), lambda i: (0, i, 0))
    gspec = pl.BlockSpec((1, HEAD_DIM), lambda i: (0, 0))
    return _pcall(body, name=name, grid=(T // tm,),
                  in_specs=[pl.BlockSpec((tm, W), lambda i: (i, 0)), gspec, hspec, hspec],
                  out_specs=[pl.BlockSpec((tm, W), lambda i: (i, 0)), gspec],
                  out_shape=[S((T, W), f32), S((1, HEAD_DIM), f32)], compiler_params=_params(("arbitrary",)))(kv, kg, dkn, dv)


def _place():
    x, y, c = lax.axis_index("x"), lax.axis_index("y"), lax.axis_index("c")
    chips = [(1 - x, y), (x, 1 - y), (1 - x, 1 - y)]
    return x, y, c, chips, [2 * cx + cy for cx, cy in chips]


def _remote(src, dst, send_sem, recv_sem, to):
    return pltpu.make_async_remote_copy(src_ref=src, dst_ref=dst, send_sem=send_sem, recv_sem=recv_sem,
                                        device_id=to, device_id_type=MESH)


def _into_slot(name, shard, idx, slots, dtype):
    Rr, Cc = shard.shape
    tr = _tile(Rr, (512, 256, 128, 64, 32, 16))

    def body(i_ref, x_ref, o_ref):
        o_ref[...] = x_ref[...].astype(o_ref.dtype)

    gs = pltpu.PrefetchScalarGridSpec(
        num_scalar_prefetch=1, grid=(Rr // tr,),
        in_specs=[pl.BlockSpec((tr, Cc), lambda i, i_ref: (i, 0))],
        out_specs=pl.BlockSpec((None, tr, Cc), lambda i, i_ref: (i_ref[0], i, 0)))
    return _pcall(body, name=name, grid_spec=gs, out_shape=S((slots, Rr, Cc), dtype), compiler_params=_params(("parallel",)))(idx, shard)


def _gather_weights(slabs):
    n = len(slabs)

    def body(*refs):
        bufs = refs[n:2 * n]
        send, recv = refs[2 * n:]
        x, y, c, chips, qk = _place()
        me_q, sib = 2 * x + y, (x, y, 1 - c)
        first = [_remote(bufs[p].at[me_q, c], bufs[p].at[me_q, c], send.at[6 * p + k], recv.at[6 * p + k], (*chips[k], c))
                 for p in range(n) for k in range(3)]
        for cp in first:
            cp.start()
        passed = []
        for k in range(3):
            for p in range(n):
                land = bufs[p].at[qk[k], c]
                _remote(land, land, send.at[6 * p + k], recv.at[6 * p + k], sib).wait_recv()
                cp = _remote(land, land, send.at[6 * p + 3 + k], recv.at[6 * p + 3 + k], sib)
                cp.start()
                passed.append(cp)
        for k in range(3):
            for p in range(n):
                land = bufs[p].at[qk[k], 1 - c]
                _remote(land, land, send.at[6 * p + 3 + k], recv.at[6 * p + 3 + k], sib).wait_recv()
        for cp in first + passed:
            cp.wait_send()

    return _pcall(body, name="gather_weights", in_specs=[ANY] * n, out_specs=[ANY] * n,
                  out_shape=[S(s.shape, s.dtype) for s in slabs], input_output_aliases={p: p for p in range(n)},
                  scratch_shapes=[pltpu.SemaphoreType.DMA((6 * n,)), pltpu.SemaphoreType.DMA((6 * n,))])(*slabs)


def _exchange_halves(grads):
    n = len(grads)

    def body(*refs):
        ins, outs = refs[:n], refs[n:2 * n]
        send, recv = refs[2 * n:]
        x, y, c, _, _ = _place()
        sib = (x, y, 1 - c)
        cps = [_remote(ins[p].at[q, 1 - c], outs[p].at[q], send.at[N_CHIPS * p + q], recv.at[N_CHIPS * p + q], sib)
               for p in range(n) for q in range(N_CHIPS)]
        for cp in cps:
            cp.start()
        for cp in cps:
            cp.wait_recv()
        for cp in cps:
            cp.wait_send()

    return _pcall(body, name="exchange_halves", in_specs=[ANY] * n, out_specs=[ANY] * n,
                  out_shape=[S((N_CHIPS,) + g.shape[2:], g.dtype) for g in grads],
                  scratch_shapes=[pltpu.SemaphoreType.DMA((N_CHIPS * n,)), pltpu.SemaphoreType.DMA((N_CHIPS * n,))])(*grads)


def _scatter_to_owners(sums):
    n = len(sums)

    def body(*refs):
        ins, outs = refs[:n], refs[n:2 * n]
        send, recv = refs[2 * n:]
        x, y, c, chips, qk = _place()
        cps = [_remote(ins[p].at[qk[k]], outs[p].at[k], send.at[3 * p + k], recv.at[3 * p + k], (*chips[k], c))
               for p in range(n) for k in range(3)]
        for cp in cps:
            cp.start()
        for cp in cps:
            cp.wait_recv()
        for cp in cps:
            cp.wait_send()

    return _pcall(body, name="scatter_to_owners", in_specs=[ANY] * n, out_specs=[ANY] * n,
                  out_shape=[S((3,) + s.shape[1:], s.dtype) for s in sums],
                  scratch_shapes=[pltpu.SemaphoreType.DMA((3 * n,)), pltpu.SemaphoreType.DMA((3 * n,))])(*sums)


def _share_halves(slabs):
    n = len(slabs)

    def body(*refs):
        bufs = refs[n:2 * n]
        send, recv = refs[2 * n:]
        x, y, c, _, _ = _place()
        sib = (x, y, 1 - c)
        cps = [_remote(bufs[p].at[c], bufs[p].at[c], send.at[p], recv.at[p], sib) for p in range(n)]
        for cp in cps:
            cp.start()
        for p in range(n):
            _remote(bufs[p].at[1 - c], bufs[p].at[1 - c], send.at[p], recv.at[p], sib).wait_recv()
        for cp in cps:
            cp.wait_send()

    return _pcall(body, name="share_halves", in_specs=[ANY] * n, out_specs=[ANY] * n,
                  out_shape=[S(s.shape, s.dtype) for s in slabs], input_output_aliases={p: p for p in range(n)},
                  scratch_shapes=[pltpu.SemaphoreType.DMA((n,)), pltpu.SemaphoreType.DMA((n,))])(*slabs)


def _gather_all_devices(block):
    m_per, ncol = block.shape

    def body(x_ref, out_ref, send_sems, recv_sems, local_sem):
        x, y, c, chips, _ = _place()
        me, sib = (x, y, c), (x, y, 1 - c)

        def rows(px, py, pc):
            return out_ref.at[pl.ds((4 * px + 2 * py + pc) * m_per, m_per), :]

        def copy(k, blk, to, src=None):
            return _remote(rows(*blk) if src is None else src, rows(*blk), send_sems.at[k], recv_sems.at[k], to)

        mine = pltpu.make_async_copy(x_ref, rows(*me), local_sem)
        mine.start()
        first = [copy(0, me, sib, src=x_ref)] + [copy(1 + j, me, (*chip, c), src=x_ref) for j, chip in enumerate(chips)]
        for cp in first:
            cp.start()
        passed = [copy(4 + j, (*chip, c), sib) for j, chip in enumerate(chips)]
        for j, chip in enumerate(chips):
            copy(1 + j, (*chip, c), me).wait_recv()
            passed[j].start()
        copy(0, sib, me).wait_recv()
        for j, chip in enumerate(chips):
            copy(4 + j, (*chip, 1 - c), me).wait_recv()
        for cp in first + passed:
            cp.wait_send()
        mine.wait()

    vm = pl.BlockSpec(memory_space=pltpu.VMEM)
    return _pcall(body, name="gather_small_grads", in_specs=[vm], out_specs=vm,
                  out_shape=S((8 * m_per, ncol), block.dtype),
                  scratch_shapes=[pltpu.SemaphoreType.DMA((7,)), pltpu.SemaphoreType.DMA((7,)), pltpu.SemaphoreType.DMA],
                  compiler_params=pltpu.CompilerParams(vmem_limit_bytes=VMEM_LIMIT_BYTES))(block)


def _sum_pair(name, g, r1, c_idx):
    Q, _, Rr, Cc = g.shape
    tr = _tile(Rr, (512, 256, 128))

    def body(c_ref, g_ref, r_ref, o_ref):
        o_ref[...] = (g_ref[...].astype(f32) + r_ref[...].astype(f32)).astype(o_ref.dtype)

    spec = pl.BlockSpec((None, tr, Cc), lambda q, i, c_ref: (q, i, 0))
    gs = pltpu.PrefetchScalarGridSpec(
        num_scalar_prefetch=1, grid=(Q, Rr // tr),
        in_specs=[pl.BlockSpec((None, None, tr, Cc), lambda q, i, c_ref: (q, c_ref[0], i, 0)), spec], out_specs=spec)
    return _pcall(body, name=name, grid_spec=gs, out_shape=S((Q, Rr, Cc), bf16),
                  compiler_params=_params(("parallel", "parallel")))(c_idx, g, r1)


def _sum_owner(name, s, r2, qc_idx):
    Q, Rr, Cc = s.shape
    tr = _tile(Rr, (512, 256, 128))

    def body(q_ref, s_ref, r_ref, o_ref):
        o_ref[...] = ((s_ref[...].astype(f32) + r_ref[0].astype(f32)) + r_ref[1].astype(f32)) + r_ref[2].astype(f32)

    gs = pltpu.PrefetchScalarGridSpec(
        num_scalar_prefetch=1, grid=(Rr // tr,),
        in_specs=[pl.BlockSpec((None, tr, Cc), lambda i, q_ref: (q_ref[0], i, 0)), pl.BlockSpec((3, tr, Cc), lambda i, q_ref: (0, i, 0))],
        out_specs=pl.BlockSpec((None, tr, Cc), lambda i, q_ref: (q_ref[1], i, 0)))
    return _pcall(body, name=name, grid_spec=gs, out_shape=S((2, Rr, Cc), f32),
                  compiler_params=_params(("parallel",)))(qc_idx, s, r2)


def _sum_devices(stack):
    n, M, C = stack.shape

    def body(s_ref, o_ref):
        acc = s_ref[0]
        for d in range(1, n):
            acc = acc + s_ref[d]
        o_ref[...] = acc

    return _pcall(body, name="sum_small_grads", out_shape=S((M, C), f32), compiler_params=_params())(stack)


def _adamw(name, w, g, m, v):
    Rr, Cc = w.shape
    tr = _tile(Rr, (256, 128, 64, 32, 16, 8))

    def body(w_ref, g_ref, m_ref, v_ref, d_ref, mo_ref, vo_ref):
        g_ = g_ref[...]
        m_ = B1 * m_ref[...] + (1.0 - B1) * g_
        v_ = B2 * v_ref[...] + (1.0 - B2) * jnp.square(g_)
        m_hat = m_ / (1.0 - B1 ** STEP)
        v_hat = v_ / (1.0 - B2 ** STEP)
        d_ref[...] = -LR * (m_hat / (jnp.sqrt(v_hat) + ADAM_EPS) + WD * w_ref[...])
        mo_ref[...] = m_
        vo_ref[...] = v_

    spec = pl.BlockSpec((tr, Cc), lambda i: (i, 0))
    return _pcall(body, name=name, grid=(Rr // tr,), in_specs=[spec] * 4, out_specs=[spec] * 3,
                  out_shape=[S((Rr, Cc), f32)] * 3, compiler_params=_params(("parallel",)))(w, g, m, v)


def _pack(arrs, multiple):
    flat = jnp.concatenate([a.reshape(-1) for a in arrs])
    pad = (-flat.shape[0]) % multiple
    return jnp.pad(flat, (0, pad)).reshape(-1, 128)


def _unpack(slab, shapes):
    flat, out, o = slab.reshape(-1), [], 0
    for shp in shapes:
        n = 1
        for d in shp:
            n *= d
        out.append(flat[o:o + n].reshape(shp))
        o += n
    return out


def kernel(x, meta_tokens, norm_mix, norm_ffn, conv_w_in, conv_b_in, conv_dw, conv_ln_g, conv_ln_b, conv_w_out, conv_b_out, kv_norm, w_kv, k_norm, w_q, q_norm, attn_sinks, w_o, ffn_w_gate, ffn_w_up, ffn_w_down, loss_target, m_meta_tokens, m_norm_mix, m_norm_ffn, m_conv_w_in, m_conv_b_in, m_conv_dw, m_conv_ln_g, m_conv_ln_b, m_conv_w_out, m_conv_b_out, m_kv_norm, m_w_kv, m_k_norm, m_w_q, m_q_norm, m_attn_sinks, m_w_o, m_ffn_w_gate, m_ffn_w_up, m_ffn_w_down, v_meta_tokens, v_norm_mix, v_norm_ffn, v_conv_w_in, v_conv_b_in, v_conv_dw, v_conv_ln_g, v_conv_ln_b, v_conv_w_out, v_conv_b_out, v_kv_norm, v_w_kv, v_k_norm, v_w_q, v_q_norm, v_attn_sinks, v_w_o, v_ffn_w_gate, v_ffn_w_up, v_ffn_w_down):
    Q = N_CHIPS
    B, SEQ, D = x.shape
    L = N_META_ROWS + SEQ
    Lp = -(-L // ATT_BLOCK) * ATT_BLOCK
    T = B * Lp
    NA, NB = conv_w_in.shape[0], w_q.shape[0]
    NL = NA + NB
    Dq = D // Q
    G = N_KV
    R = D // (HEAD_DIM * G)
    KVW = w_kv.shape[1]
    assert NA % 2 == 0 and NB % 2 == 0 and NL % 2 == 0 and (D // Q) % 32 == 0
    tm = _tile(T, (1088, 544, 512, 256, 128))
    tk = _tile(T, (1088, 544, 512, 256, 128))
    tr = _tile(T, (272, 256, 128))
    my_c = lax.axis_index("c").astype(jnp.int32).reshape(1)
    my_q = (2 * lax.axis_index("x") + lax.axis_index("y")).astype(jnp.int32)
    my_qc = jnp.concatenate([my_q.reshape(1), my_c])

    small_shapes = [meta_tokens.shape, conv_b_in.shape, conv_dw.shape, conv_ln_g.shape, conv_ln_b.shape, conv_b_out.shape]
    small = _pack([meta_tokens, conv_b_in, conv_dw, conv_ln_g, conv_ln_b, conv_b_out], 2048)
    big = [conv_w_in, conv_w_out, w_kv, w_q, w_o, ffn_w_gate, ffn_w_up, ffn_w_down]
    slabs = [_into_slot(f"own_slot{i}", a.reshape(-1, a.shape[-1]), my_qc, Q, bf16 if i < len(big) else f32)
             for i, a in enumerate(big + [small])]
    gathered = _gather_weights([s.reshape(Q, 2, s.shape[1] // 2, s.shape[2]) for s in slabs])
    full = lambda a, like: a.reshape((Q,) + like.shape)
    Wcin, Wcout, Wkv, Wq, Wo, Wg, Wu, Wd = [full(a, like) for a, like in zip(gathered[:8], big)]
    Wkv = Wkv.reshape(Q, 1, Dq, KVW)
    parts = [_unpack(gathered[8][q], small_shapes) for q in range(Q)]
    meta_f, b_in_f, dw_f, ln_g_f, ln_b_f, b_out_f = [jnp.concatenate([parts[q][i] for q in range(Q)], axis=-1) for i in range(6)]
    dw_pad = jnp.pad(dw_f, ((0, 0), (0, CONV_PAD - CONV_TAPS), (0, 0)))

    h = jnp.concatenate([jnp.broadcast_to(meta_f[None], (B, N_META_ROWS, D)), x, jnp.zeros((B, Lp - L, D), f32)], axis=1).reshape(T, D)
    tgt = jnp.pad(loss_target, ((0, 0), (N_META_ROWS, Lp - L), (0, 0))).reshape(T, D)
    row = lambda a: a.reshape(1, -1)
    seqs = lambda a: a.reshape(B, Lp, a.shape[-1])
    by_seq = lambda a: a.reshape(G, B, Lp, HEAD_DIM)
    saved = []
    sinks3 = attn_sinks.reshape(NB, G, 1, R)
    for l in range(NL):
        st = {"h_a": h}
        u = _rowwise(f"rms_mix{l}", _rms_fn, [h], [row(norm_mix[l])], [bf16], tr)[0]
        st["u"] = u
        if l < NA:
            p, av, ag = _glu_fwd(f"glu{l}", u, Wcin, l, b_in_f[l:l + 1], tm)
            cv = _dwconv_fwd(f"dwconv{l}", p.reshape(B, Lp, D), dw_pad[l]).reshape(T, D)
            s = _rowwise(f"ln_silu{l}", _ln_silu_fn, [cv], [row(ln_g_f[l]), row(ln_b_f[l])], [bf16], tr)[0]
            h = _proj(f"conv_out{l}", s, Wcout, l, tm, bias=row(b_out_f[l]), resid=h)
            st.update(av=av, ag=ag, p=p, cv=cv, s=s)
        else:
            j = l - NA
            if j == 0:
                kvn = _rowwise("rms_kv", _rms_fn, [h], [row(kv_norm)], [bf16], tr)[0]
                kv = _proj("kv_proj", kvn, Wkv, 0, tm)
                kn, vh = _kv_heads("kv_heads", kv, row(k_norm), tk)
                kn, vh = by_seq(kn), by_seq(vh)
                st.update(kvn=kvn, kv=kv)
            q = seqs(_proj(f"q_proj{j}", u, Wq, j, tm))
            o = _attn_fwd(f"attn{j}", q, kn, vh, row(q_norm[j]), sinks3[j]).reshape(T, D)
            h = _proj(f"o_proj{j}", o, Wo, j, tm, resid=h)
            st.update(q=q, o=o)
        st["h_b"] = h
        u2 = _rowwise(f"rms_ffn{l}", _rms_fn, [h], [row(norm_ffn[l])], [bf16], tr)[0]
        gate, up, hid = _ffn_up(f"ffn_up{l}", u2, Wg, Wu, l, tm)
        h = _ffn_down(f"ffn_down{l}", hid, Wd, l, h, tm)
        st.update(u2=u2, gate=gate, up=up, hid=hid)
        saved.append(st)

    dh, part = _loss_head(h, tgt, Lp, SEQ, tr)
    loss = lax.psum(0.5 / D * jnp.sum(part), ("x", "y", "c"))

    gW = dict(cin=None, cout=None, q=None, o=None, g=None, u=None, d=None)
    g_mix, g_ffn = [None] * NL, [None] * NL
    g_bin, g_dw, g_lng, g_lnb, g_bout = ([None] * NA for _ in range(5))
    g_qn, g_sink = [None] * NB, [None] * NB
    dknp = dvp = None
    zero_row = jnp.zeros((1, D), f32)
    for l in reversed(range(NL)):
        st = saved[l]
        dgate, dup = _ffn_dhid(f"ffn_dhid{l}", dh, Wd, l, st["gate"], st["up"], tm)
        gW["d"] = _dw_rows(f"ffn_dwd{l}", st["hid"], dh, Q, tk, NL, l, gW["d"])
        gW["g"] = _dw_cols(f"ffn_dwg{l}", st["u2"], dgate, Q, tk, NL, l, gW["g"], Q)
        gW["u"] = _dw_cols(f"ffn_dwu{l}", st["u2"], dup, Q, tk, NL, l, gW["u"], Q)
        du2 = _ffn_du(f"ffn_du{l}", dgate, dup, Wg, Wu, l, tm)
        dh, g_ffn[l] = _rowwise_vjp(f"rms_ffn_bwd{l}", _rms_res_fn, [st["h_b"]], [row(norm_ffn[l])], [du2, dh], [f32], tr)
        if l < NA:
            ds = _proj_dx(f"conv_out_dx{l}", dh, Wcout, l, tm)
            gW["cout"] = _dw_rows(f"conv_out_dw{l}", st["s"], dh, Q, tk, NA, l, gW["cout"])
            dcv, g_lng[l], g_lnb[l] = _rowwise_vjp(f"ln_silu_bwd{l}", _ln_silu_fn, [st["cv"]], [row(ln_g_f[l]), row(ln_b_f[l])], [ds], [f32], tr)
            dp, g_dw[l] = _dwconv_bwd(f"dwconv_bwd{l}", dcv.reshape(B, Lp, D), st["p"].reshape(B, Lp, D), dw_pad[l])
            dav, dag, dbv, dbg = _rowwise_vjp(f"glu_bwd{l}", _glu_fn, [st["av"], st["ag"]], [zero_row, zero_row], [dp.reshape(T, D)], [bf16, bf16], tr)
            g_bin[l] = jnp.concatenate([dbv, dbg], axis=1)
            gW["cin"] = _dw_cols(f"glu_dwv{l}", st["u"], dav, Q // 2, tk, NA, l, gW["cin"], Q)
            gW["cin"] = _dw_cols(f"glu_dwg{l}", st["u"], dag, Q // 2, tk, NA, l, gW["cin"], Q, q_off=Q // 2)
            du = _glu_du(f"glu_du{l}", dav, dag, Wcin, l, tm)
            dh, g_mix[l], g_bout[l] = _rowwise_vjp(f"rms_mix_bwd{l}", _rms_res_bias_fn, [st["h_a"]], [row(norm_mix[l]), zero_row], [du, dh], [f32], tr)
        else:
            j = l - NA
            do = _proj_dx(f"o_proj_dx{j}", dh, Wo, j, tm, out_dtype=bf16)
            gW["o"] = _dw_rows(f"o_proj_dw{j}", st["o"], dh, Q, tk, NB, j, gW["o"])
            dq, dk1, dv1, g_qn[j], g_sink[j] = _attn_bwd(f"attn_bwd{j}", st["q"], kn, vh, row(q_norm[j]), sinks3[j], seqs(do))
            dknp, dvp = (dk1, dv1) if dknp is None else (dknp + dk1, dvp + dv1)
            dq = dq.reshape(T, D)
            gW["q"] = _dw_rows(f"q_proj_dw{j}", st["u"], dq, Q, tk, NB, j, gW["q"])
            du = _proj_dx(f"q_proj_dx{j}", dq, Wq, j, tm)
            dh, g_mix[l] = _rowwise_vjp(f"rms_mix_bwd{l}", _rms_res_fn, [st["h_a"]], [row(norm_mix[l])], [du, dh], [f32], tr)
            if j == 0:
                dkv, g_kn = _kv_heads_bwd("kv_heads_bwd", st["kv"], row(k_norm), dknp.reshape(G, T, HEAD_DIM), dvp.reshape(G, T, HEAD_DIM), tk)
                g_wkv = _dw_rows("kv_proj_dw", st["kvn"], dkv, Q, tk, 1, 0, None)
                dkvn = _proj_dx("kv_proj_dx", dkv, Wkv, 0, tm)
                dh, g_kvn = _rowwise_vjp("rms_kv_bwd", _rms_res_fn, [st["h_a"]], [row(kv_norm)], [dkvn, dh], [f32], tr)
    dh3 = dh.reshape(B, Lp, D)
    grad_x = dh3[:, N_META_ROWS:L]
    g_meta = jnp.sum(dh3[:, :N_META_ROWS], axis=0)

    grads4 = [gW["cin"], gW["cout"], g_wkv, gW["q"], gW["o"], gW["g"], gW["u"], gW["d"]]
    from_sib = _exchange_halves(grads4)
    chip_sums = [_sum_pair(f"sum_pair{i}", g, r, my_c) for i, (g, r) in enumerate(zip(grads4, from_sib))]
    from_chips = _scatter_to_owners(chip_sums)
    shard_grads = _share_halves([_sum_owner(f"sum_owner{i}", s, r, my_qc) for i, (s, r) in enumerate(zip(chip_sums, from_chips))])

    names = ["conv_w_in", "conv_w_out", "w_kv", "w_q", "w_o", "ffn_w_gate", "ffn_w_up", "ffn_w_down"]
    ws = dict(zip(names, big))
    ms = dict(zip(names, [m_conv_w_in, m_conv_w_out, m_w_kv, m_w_q, m_w_o, m_ffn_w_gate, m_ffn_w_up, m_ffn_w_down]))
    vs = dict(zip(names, [v_conv_w_in, v_conv_w_out, v_w_kv, v_w_q, v_w_o, v_ffn_w_gate, v_ffn_w_up, v_ffn_w_down]))
    out_g, out_d, out_m, out_v = {}, {}, {}, {}
    for nm, gsh in zip(names, shard_grads):
        w = ws[nm]
        flat = lambda a: a.reshape(-1, w.shape[-1])
        g2 = flat(gsh)
        d2, m2, v2 = _adamw(f"adamw_{nm}", flat(w), g2, flat(ms[nm]), flat(vs[nm]))
        out_g[nm], out_d[nm], out_m[nm], out_v[nm] = (a.reshape(w.shape) for a in (g2, d2, m2, v2))

    small_names = ["norm_mix", "norm_ffn", "kv_norm", "k_norm", "q_norm", "attn_sinks", "meta_tokens", "conv_b_in", "conv_dw", "conv_ln_g", "conv_ln_b", "conv_b_out"]
    small_grads = [jnp.concatenate(g_mix, 0), jnp.concatenate(g_ffn, 0), g_kvn.reshape(-1), g_kn.reshape(-1), jnp.concatenate(g_qn, 0),
                   jnp.stack(g_sink).reshape(NB, G * R), g_meta, jnp.concatenate(g_bin, 0), jnp.stack(g_dw)[:, :CONV_TAPS],
                   jnp.concatenate(g_lng, 0), jnp.concatenate(g_lnb, 0), jnp.concatenate(g_bout, 0)]
    slab = _pack(small_grads, 1024)
    total = _sum_devices(_gather_all_devices(slab).reshape(8, slab.shape[0], 128))
    full_grads = _unpack(total, [g.shape for g in small_grads])
    small_w = dict(zip(small_names, [norm_mix, norm_ffn, kv_norm, k_norm, q_norm, attn_sinks, meta_tokens, conv_b_in, conv_dw, conv_ln_g, conv_ln_b, conv_b_out]))
    small_m = dict(zip(small_names, [m_norm_mix, m_norm_ffn, m_kv_norm, m_k_norm, m_q_norm, m_attn_sinks, m_meta_tokens, m_conv_b_in, m_conv_dw, m_conv_ln_g, m_conv_ln_b, m_conv_b_out]))
    small_v = dict(zip(small_names, [v_norm_mix, v_norm_ffn, v_kv_norm, v_k_norm, v_q_norm, v_attn_sinks, v_meta_tokens, v_conv_b_in, v_conv_dw, v_conv_ln_g, v_conv_ln_b, v_conv_b_out]))
    local_grads = []
    for nm, g in zip(small_names, full_grads):
        w = small_w[nm]
        if g.shape != w.shape:
            wq = w.shape[-1]
            g = lax.dynamic_slice_in_dim(g, my_q * wq, wq, axis=g.ndim - 1)
        local_grads.append(g)
    shapes = [small_w[nm].shape for nm in small_names]
    d_s, m_s, v_s = _adamw("adamw_small", _pack([small_w[nm] for nm in small_names], 1024), _pack(local_grads, 1024),
                           _pack([small_m[nm] for nm in small_names], 1024), _pack([small_v[nm] for nm in small_names], 1024))
    for nm, g, d_, m_, v_ in zip(small_names, local_grads, _unpack(d_s, shapes), _unpack(m_s, shapes), _unpack(v_s, shapes)):
        out_g[nm], out_d[nm], out_m[nm], out_v[nm] = g, d_, m_, v_

    order = ["meta_tokens", "norm_mix", "norm_ffn", "conv_w_in", "conv_b_in", "conv_dw", "conv_ln_g", "conv_ln_b", "conv_w_out", "conv_b_out",
             "kv_norm", "w_kv", "k_norm", "w_q", "q_norm", "attn_sinks", "w_o", "ffn_w_gate", "ffn_w_up", "ffn_w_down"]
    return (loss, grad_x, *[out_g[n] for n in order], *[out_d[n] for n in order], *[out_m[n] for n in order], *[out_v[n] for n in order])
```

```python
import functools

import jax
import jax.numpy as jnp
from jax import lax
from jax.experimental import pallas as pl
from jax.experimental.pallas import tpu as pltpu

f32, bf16 = jnp.float32, jnp.bfloat16

N_META_ROWS = 16
ATT_BLOCK = 128
HEAD_DIM = 64
N_KV = 4
CONV_TAPS = 31
CONV_PAD = 32
EPS = 1e-6
MASKED = -1e30
LR, B1, B2, ADAM_EPS, WD, STEP = 0.001, 0.9, 0.999, 1e-08, 0.01, 10
N_CHIPS = 4
VMEM_LIMIT_BYTES = 56 * 1024 * 1024
MESH = pl.DeviceIdType.MESH
ANY = pl.BlockSpec(memory_space=pl.ANY)
S = jax.ShapeDtypeStruct


def _pcall(body, **kw):
    return pl.pallas_call(body, **kw)


def _params(sem=None):
    return pltpu.CompilerParams(dimension_semantics=sem, vmem_limit_bytes=VMEM_LIMIT_BYTES)


def _tile(n, prefs):
    for p in prefs:
        if n % p == 0:
            return p
    return n


_DN = {"nn": (((1,), (0,)), ((), ())), "nt": (((1,), (1,)), ((), ())), "tn": (((0,), (0,)), ((), ()))}


def _matmul(name, mode, grid, a_ops, b_ops, x_ops, outs, terms, acc_shape, n_acc, epilogue, into=None, carry=None):
    na, nb, nx, no = len(a_ops), len(b_ops), len(x_ops), len(outs)
    nk = grid[2]
    slabs, ici, fwd = carry if carry is not None else ([], [], [])
    nc, ncp = len(slabs), 3 * (len(ici) + len(fwd))
    n_in = na + nb + nx + (0 if into is None else 1) + nc

    def flat2d(v):
        return v.reshape(-1, v.shape[-1]) if v.ndim == 3 else v

    def dots(a_refs, b_refs):
        parts = [None] * n_acc
        for ai, bi, ci in terms:
            d = lax.dot_general(flat2d(a_refs[ai][...]).astype(bf16), flat2d(b_refs[bi][...]).astype(bf16), _DN[mode],
                                preferred_element_type=f32)
            parts[ci] = d if parts[ci] is None else parts[ci] + d
        return parts

    def finish(accs, x_refs, o_refs):
        res = epilogue(accs, [x[...] for x in x_refs])
        for o_ref, r in zip(o_refs, res):
            o_ref[...] = r.reshape(o_ref.shape).astype(o_ref.dtype)

    def compute(a_refs, b_refs, x_refs, o_refs, acc_refs):
        if nk == 1:
            finish(dots(a_refs, b_refs), x_refs, o_refs)
            return
        k = pl.program_id(2)

        @pl.when(k == 0)
        def _():
            for acc in acc_refs:
                acc[...] = jnp.zeros_like(acc)

        for acc, d in zip(acc_refs, dots(a_refs, b_refs)):
            acc[...] += d

        @pl.when(k == nk - 1)
        def _():
            finish([acc[...] for acc in acc_refs], x_refs, o_refs)

    def body(*refs):
        a_refs, b_refs = refs[:na], refs[na:na + nb]
        x_refs = refs[na + nb:na + nb + nx]
        o_refs = refs[n_in:n_in + no]
        bufs = refs[n_in + no:n_in + no + nc]
        scratch = refs[n_in + no + nc:]
        if not nc:
            compute(a_refs, b_refs, x_refs, o_refs, scratch)
            return
        acc_refs, (send, recv) = scratch[:-2], scratch[-2:]
        i, j, k = pl.program_id(0), pl.program_id(1), pl.program_id(2)

        @pl.when((i == 0) & (j == 0) & (k == 0))
        def _():
            for cp in _gather_copies(bufs, ici, fwd, send, recv)[0]:
                cp.start()

        compute(a_refs, b_refs, x_refs, o_refs, acc_refs)

        @pl.when((i == grid[0] - 1) & (j == grid[1] - 1) & (k == nk - 1))
        def _():
            starts, lands = _gather_copies(bufs, ici, fwd, send, recv)
            for cp in lands:
                cp.wait_recv()
            for cp in starts:
                cp.wait_send()

    ops = list(a_ops) + list(b_ops) + list(x_ops)
    aliases = {}
    if into is not None:
        ops.append((into, ANY))
        aliases = {len(ops) - 1: 0}
    for t, s in enumerate(slabs):
        ops.append((s, ANY))
        aliases[len(ops) - 1] = no + t
    outs = list(outs) + [(S(s.shape, s.dtype), ANY) for s in slabs]
    scratch = [pltpu.VMEM(acc_shape, f32)] * (n_acc if nk > 1 else 0)
    if nc:
        scratch += [pltpu.SemaphoreType.DMA((ncp,)), pltpu.SemaphoreType.DMA((ncp,))]
    sem = ("arbitrary",) * 3 if nc else ("parallel", "parallel", "arbitrary")
    res = _pcall(body, name=name, grid=grid, in_specs=[s for _, s in ops], out_specs=[s for _, s in outs],
                 out_shape=[s for s, _ in outs], scratch_shapes=scratch, input_output_aliases=aliases,
                 compiler_params=_params(sem))(*[a for a, _ in ops])
    return res


def _first(accs, xs):
    return (accs[0],)


def _proj(name, a, w, l, tm, bias=None, resid=None, out_dtype=f32, carry=None):
    T = a.shape[0]
    Q, _, Kq, N = w.shape
    tn = _tile(N, (512, 256, 128))
    x_ops, epi = [], _first
    if bias is not None:
        x_ops = [(bias, pl.BlockSpec((1, tn), lambda i, j, k: (0, j))), (resid, pl.BlockSpec((tm, tn), lambda i, j, k: (i, j)))]
        epi = lambda accs, xs: (accs[0] + xs[0] + xs[1],)
    elif resid is not None:
        x_ops = [(resid, pl.BlockSpec((tm, tn), lambda i, j, k: (i, j)))]
        epi = lambda accs, xs: (accs[0] + xs[0],)
    res = _matmul(name, "nn", (T // tm, N // tn, 1),
                  [(a, pl.BlockSpec((tm, Q * Kq), lambda i, j, k: (i, 0)))],
                  [(w, pl.BlockSpec((Q, None, Kq, tn), lambda i, j, k: (0, l, 0, j)))],
                  x_ops, [(S((T, N), out_dtype), pl.BlockSpec((tm, tn), lambda i, j, k: (i, j)))],
                  [(0, 0, 0)], (tm, tn), 1, epi, carry=carry)
    return res[0] if carry is None else (res[0], res[1:])


def _proj_dx(name, dy, w, l, tm, out_dtype=f32):
    T, N = dy.shape
    Q, _, Kq, _ = w.shape
    return _matmul(name, "nt", (T // tm, 1, 1),
                   [(dy, pl.BlockSpec((tm, N), lambda i, j, k: (i, 0)))],
                   [(w, pl.BlockSpec((Q, None, Kq, N), lambda i, j, k: (0, l, 0, 0)))],
                   [], [(S((T, Q * Kq), out_dtype), pl.BlockSpec((tm, Q * Kq), lambda i, j, k: (i, 0)))],
                   [(0, 0, 0)], (tm, Q * Kq), 1, _first)[0]


def _slot(nl, l, blocks_per_layer):
    per_half = nl // 2
    return l // per_half, (l % per_half) * blocks_per_layer


def _dw_rows(name, a, dy, Q, tk, nl, l, into):
    T, N = dy.shape
    tn = _tile(N, (512, 256, 128))
    if a.ndim == 2:
        K = a.shape[1]
        a_op, ni, acc_rows = (a, pl.BlockSpec((tk, K), lambda i, j, k: (k, 0))), 1, K
        if nl == 1:
            osh = S((Q, 2, K // (2 * Q), N), bf16)
            ospec = pl.BlockSpec((Q, 2, K // (2 * Q), tn), lambda i, j, k: (0, 0, 0, j))
        else:
            hf, rb = _slot(nl, l, 1)
            osh = S((Q, 2, (nl // 2) * (K // Q), N), bf16)
            ospec = pl.BlockSpec((Q, None, K // Q, tn), lambda i, j, k: (0, hf, rb, j))
    else:
        Kq = a.shape[2]
        a_op, ni, acc_rows = (a, pl.BlockSpec((None, tk, Kq), lambda i, j, k: (i, k, 0))), Q, Kq
        hf, rb = _slot(nl, l, 1)
        osh = S((Q, 2, (nl // 2) * Kq, N), bf16)
        ospec = pl.BlockSpec((None, None, Kq, tn), lambda i, j, k: (i, hf, rb, j))
    return _matmul(name, "tn", (ni, N // tn, T // tk), [a_op], [(dy, pl.BlockSpec((tk, tn), lambda i, j, k: (k, j)))],
                   [], [(osh, ospec)], [(0, 0, 0)], (acc_rows, tn), 1, _first, into=into)[0]


def _dw_cols(name, a, dyc, Qc, tk, nl, l, into, Q, q_off=0):
    T, K = a.shape
    tkin = _tile(K, (512, 256, 128))
    if dyc.ndim == 3:
        Nq = dyc.shape[2]
        b_op = (dyc, pl.BlockSpec((None, tk, Nq), lambda i, j, k: (j, k, 0)))
    else:
        Nq = dyc.shape[1] // Qc
        b_op = (dyc, pl.BlockSpec((tk, Nq), lambda i, j, k: (k, j)))
    hf, rb = _slot(nl, l, K // tkin)
    ospec = pl.BlockSpec((None, None, tkin, Nq), lambda i, j, k: (j + q_off, hf, rb + i, 0))
    return _matmul(name, "tn", (K // tkin, Qc, T // tk),
                   [(a, pl.BlockSpec((tk, tkin), lambda i, j, k: (k, i)))], [b_op], [],
                   [(S((Q, 2, (nl // 2) * K, Nq), bf16), ospec)], [(0, 0, 0)], (tkin, Nq), 1, _first, into=into)[0]


def _glu_fwd(name, u, w, l, b_in, tm, carry=None):
    T, D = u.shape
    Q, _, _, Cq = w.shape
    H = Q // 2

    def epi(accs, xs):
        av, ag = accs[0] + xs[0], accs[1] + xs[1]
        return av * jax.nn.sigmoid(ag), av, ag

    wspec = lambda off: pl.BlockSpec((None, None, D, Cq), lambda i, j, k: (j + off, l, 0, 0))
    bspec = lambda off: pl.BlockSpec((1, Cq), lambda i, j, k: (0, j + off))
    ospec = pl.BlockSpec((tm, Cq), lambda i, j, k: (i, j))
    return _matmul(name, "nn", (T // tm, H, 1),
                   [(u, pl.BlockSpec((tm, D), lambda i, j, k: (i, 0)))],
                   [(w, wspec(0)), (w, wspec(H))], [(b_in, bspec(0)), (b_in, bspec(H))],
                   [(S((T, H * Cq), f32), ospec), (S((T, H * Cq), bf16), ospec), (S((T, H * Cq), bf16), ospec)],
                   [(0, 0, 0), (0, 1, 1)], (tm, Cq), 2, epi, carry=carry)


def _glu_du(name, dav, dag, w, l, tm):
    T = dav.shape[0]
    Q, _, D, Cq = w.shape
    H = Q // 2
    tn = _tile(D, (512, 256, 128))
    aspec = pl.BlockSpec((tm, Cq), lambda i, j, k: (i, k))
    wspec = lambda off: pl.BlockSpec((None, None, tn, Cq), lambda i, j, k: (k + off, l, j, 0))
    return _matmul(name, "nt", (T // tm, D // tn, H), [(dav, aspec), (dag, aspec)],
                   [(w, wspec(0)), (w, wspec(H))], [],
                   [(S((T, D), f32), pl.BlockSpec((tm, tn), lambda i, j, k: (i, j)))],
                   [(0, 0, 0), (1, 1, 0)], (tm, tn), 1, _first)[0]


def _ffn_up(name, u, wg, wu, l, tm, carry=None):
    T, D = u.shape
    Q, _, _, Fq = wg.shape

    def epi(accs, xs):
        g, up = accs
        return g, up, g * jax.nn.sigmoid(g) * up

    wspec = pl.BlockSpec((None, None, D, Fq), lambda i, j, k: (j, l, 0, 0))
    ospec = pl.BlockSpec((None, tm, Fq), lambda i, j, k: (j, i, 0))
    osh = S((Q, T, Fq), bf16)
    return _matmul(name, "nn", (T // tm, Q, 1), [(u, pl.BlockSpec((tm, D), lambda i, j, k: (i, 0)))],
                   [(wg, wspec), (wu, wspec)], [], [(osh, ospec)] * 3, [(0, 0, 0), (0, 1, 1)], (tm, Fq), 2, epi, carry=carry)


def _ffn_down(name, hid, wd, l, resid, tm, carry=None):
    Q, T, Fq = hid.shape
    D = wd.shape[3]
    tn = _tile(D, (512, 256, 128))
    return _matmul(name, "nn", (T // tm, D // tn, Q),
                   [(hid, pl.BlockSpec((None, tm, Fq), lambda i, j, k: (k, i, 0)))],
                   [(wd, pl.BlockSpec((None, None, Fq, tn), lambda i, j, k: (k, l, 0, j)))],
                   [(resid, pl.BlockSpec((tm, tn), lambda i, j, k: (i, j)))],
                   [(S((T, D), f32), pl.BlockSpec((tm, tn), lambda i, j, k: (i, j)))],
                   [(0, 0, 0)], (tm, tn), 1, lambda accs, xs: (accs[0] + xs[0],), carry=carry)


def _ffn_dhid(name, dy, wd, l, gate, up, tm):
    T, D = dy.shape
    Q, _, Fq, _ = wd.shape

    def epi(accs, xs):
        dh, g, up = accs[0], xs[0].astype(f32), xs[1].astype(f32)
        sg = jax.nn.sigmoid(g)
        return dh * up * (sg * (1.0 + g * (1.0 - sg))), dh * (g * sg)

    cspec = pl.BlockSpec((None, tm, Fq), lambda i, j, k: (j, i, 0))
    osh = S((Q, T, Fq), bf16)
    return _matmul(name, "nt", (T // tm, Q, 1), [(dy, pl.BlockSpec((tm, D), lambda i, j, k: (i, 0)))],
                   [(wd, pl.BlockSpec((None, None, Fq, D), lambda i, j, k: (j, l, 0, 0)))],
                   [(gate, cspec), (up, cspec)], [(osh, cspec)] * 2, [(0, 0, 0)], (tm, Fq), 1, epi)


def _ffn_du(name, dgate, dup, wg, wu, l, tm):
    Q, T, Fq = dgate.shape
    D = wg.shape[2]
    tn = _tile(D, (512, 256, 128))
    aspec = pl.BlockSpec((None, tm, Fq), lambda i, j, k: (k, i, 0))
    wspec = pl.BlockSpec((None, None, tn, Fq), lambda i, j, k: (k, l, j, 0))
    return _matmul(name, "nt", (T // tm, D // tn, Q), [(dgate, aspec), (dup, aspec)], [(wg, wspec), (wu, wspec)], [],
                   [(S((T, D), f32), pl.BlockSpec((tm, tn), lambda i, j, k: (i, j)))],
                   [(0, 0, 0), (1, 1, 0)], (tm, tn), 1, _first)[0]


def _rowwise(name, fn, rows, params, out_dtypes, tm):
    nr, npar = len(rows), len(params)
    T = rows[0].shape[0]
    shp = jax.eval_shape(fn, *[S((tm, r.shape[1]), f32) for r in rows], *[S(p.shape, f32) for p in params])

    def body(*refs):
        r = [x[...].astype(f32) for x in refs[:nr]]
        p = [x[...] for x in refs[nr:nr + npar]]
        for o_ref, o in zip(refs[nr + npar:], fn(*r, *p)):
            o_ref[...] = o.astype(o_ref.dtype)

    row_spec = lambda w: pl.BlockSpec((tm, w), lambda i: (i, 0))
    par_spec = lambda p: pl.BlockSpec(p.shape, lambda i: (0, 0))
    return _pcall(body, name=name, grid=(T // tm,),
                  in_specs=[row_spec(r.shape[1]) for r in rows] + [par_spec(p) for p in params],
                  out_specs=[row_spec(s.shape[1]) for s in shp],
                  out_shape=[S((T, s.shape[1]), dt) for s, dt in zip(shp, out_dtypes)],
                  compiler_params=_params(("parallel",)))(*rows, *params)


def _rowwise_vjp(name, fn, rows, params, cots, drow_dtypes, tm):
    nr, npar, nc = len(rows), len(params), len(cots)
    T = rows[0].shape[0]

    def body(*refs):
        r = [x[...].astype(f32) for x in refs[:nr]]
        p = [x[...] for x in refs[nr:nr + npar]]
        c = tuple(x[...].astype(f32) for x in refs[nr + npar:nr + npar + nc])
        o_refs = refs[nr + npar + nc:]
        _, vjp = jax.vjp(fn, *r, *p)
        grads = vjp(c)
        for o_ref, g in zip(o_refs[:nr], grads[:nr]):
            o_ref[...] = g.astype(o_ref.dtype)

        @pl.when(pl.program_id(0) == 0)
        def _():
            for o_ref in o_refs[nr:]:
                o_ref[...] = jnp.zeros_like(o_ref)

        for o_ref, g in zip(o_refs[nr:], grads[nr:]):
            o_ref[...] += g

    row_spec = lambda w: pl.BlockSpec((tm, w), lambda i: (i, 0))
    par_spec = lambda p: pl.BlockSpec(p.shape, lambda i: (0, 0))
    return _pcall(body, name=name, grid=(T // tm,),
                  in_specs=[row_spec(r.shape[1]) for r in rows] + [par_spec(p) for p in params] + [row_spec(c.shape[1]) for c in cots],
                  out_specs=[row_spec(r.shape[1]) for r in rows] + [par_spec(p) for p in params],
                  out_shape=[S(r.shape, dt) for r, dt in zip(rows, drow_dtypes)] + [S(p.shape, f32) for p in params],
                  compiler_params=_params(("arbitrary",)))(*rows, *params, *cots)


def _rms(h, g):
    return h * lax.rsqrt(jnp.mean(h * h, axis=-1, keepdims=True) + EPS) * g


def _rms_fn(h, g):
    return (_rms(h, g),)


def _rms_res_fn(h, g):
    return _rms(h, g), h


def _rms_res_bias_fn(h, g, b0):
    return _rms(h, g), h + b0


def _ln_silu_fn(c, g, b):
    mu = jnp.mean(c, axis=-1, keepdims=True)
    var = jnp.mean(jnp.square(c - mu), axis=-1, keepdims=True)
    y = (c - mu) * lax.rsqrt(var + EPS) * g + b
    return (y * jax.nn.sigmoid(y),)


def _glu_fn(av, ag, bv, bg):
    return ((av + bv) * jax.nn.sigmoid(ag + bg),)


def _loss_head(h, tgt, Lp, n_real, tm):
    T, D = h.shape

    def body(h_ref, t_ref, dy_ref, part_ref):
        i = pl.program_id(0)
        pos = (i * tm + lax.broadcasted_iota(jnp.int32, (tm, 1), 0)) % Lp
        real = (pos >= N_META_ROWS) & (pos < N_META_ROWS + n_real)
        err = jnp.where(real, h_ref[...] - t_ref[...], 0.0)
        dy_ref[...] = err * (1.0 / D)

        @pl.when(i == 0)
        def _():
            part_ref[...] = jnp.zeros_like(part_ref)

        part_ref[...] += jnp.sum(err * err, axis=0, keepdims=True)

    spec = pl.BlockSpec((tm, D), lambda i: (i, 0))
    return _pcall(body, name="loss_head", grid=(T // tm,), in_specs=[spec, spec],
                  out_specs=[spec, pl.BlockSpec((1, D), lambda i: (0, 0))],
                  out_shape=[S((T, D), f32), S((1, D), f32)], compiler_params=_params(("arbitrary",)))(h, tgt)


CONV_OFF = CONV_PAD - (CONV_TAPS - 1)
WIN_ROWS = ATT_BLOCK + CONV_PAD


def _phases(ph_ref, win):
    n = win.shape[0]
    for b in range(1, 8):
        ph_ref[b - 1, 0:n - 8, :] = win[b:n - 8 + b]

    def tap(o):
        a = (o // 8) * 8
        return win[a:a + ATT_BLOCK] if o % 8 == 0 else ph_ref[o % 8 - 1, a:a + ATT_BLOCK, :]

    return tap


def _phase_scratch(width):
    return pltpu.VMEM((7, WIN_ROWS - 8, width), f32)


def _dwconv_fwd(name, p, dw):
    B, Lp, D = p.shape
    dc = _tile(D, (256, 128))

    def body(p_ref, w_ref, o_ref, ph_ref):
        def tile(win, base):
            tap = _phases(ph_ref, win)
            acc = jnp.zeros((ATT_BLOCK, dc), f32)
            for k in range(CONV_TAPS):
                acc = acc + tap(CONV_OFF + k) * w_ref[k:k + 1, :]
            o_ref[pl.ds(base, ATT_BLOCK), :] = acc

        tile(jnp.concatenate([jnp.zeros((CONV_PAD, dc), f32), p_ref[0:ATT_BLOCK, :]], axis=0), 0)

        def step(r, carry):
            base = pl.multiple_of(r * ATT_BLOCK, ATT_BLOCK)
            tile(p_ref[pl.ds(pl.multiple_of(base - CONV_PAD, CONV_PAD), WIN_ROWS), :], base)
            return carry

        lax.fori_loop(1, Lp // ATT_BLOCK, step, 0)

    seq = pl.BlockSpec((None, Lp, dc), lambda b, j: (b, 0, j))
    return _pcall(body, name=name, grid=(B, D // dc), in_specs=[seq, pl.BlockSpec((CONV_PAD, dc), lambda b, j: (0, j))],
                  out_specs=seq, out_shape=S((B, Lp, D), f32), scratch_shapes=[_phase_scratch(dc)],
                  compiler_params=_params(("parallel", "parallel")))(p, dw)


def _dwconv_bwd(name, dcv, p, dw):
    B, Lp, D = p.shape
    dcw = _tile(D, (256, 128))
    nblk = Lp // ATT_BLOCK
    assert nblk >= 2
    zeros = lambda: jnp.zeros((CONV_PAD, dcw), f32)

    def body(dc_ref, p_ref, w_ref, dp_ref, ddw_ref, dph_ref, pph_ref):
        @pl.when(pl.program_id(1) == 0)
        def _():
            ddw_ref[...] = jnp.zeros_like(ddw_ref)

        def tile(dwin, pwin, base):
            dtap, ptap = _phases(dph_ref, dwin), _phases(pph_ref, pwin)
            dtile = dwin[0:ATT_BLOCK]
            acc = jnp.zeros((ATT_BLOCK, dcw), f32)
            for k in range(CONV_TAPS):
                acc = acc + dtap(CONV_TAPS - 1 - k) * w_ref[k:k + 1, :]
                ddw_ref[k:k + 1, :] += jnp.sum(dtile * ptap(CONV_OFF + k), axis=0, keepdims=True)
            dp_ref[pl.ds(base, ATT_BLOCK), :] = acc

        tile(dc_ref[0:WIN_ROWS, :], jnp.concatenate([zeros(), p_ref[0:ATT_BLOCK, :]], axis=0), 0)

        def step(r, carry):
            base = pl.multiple_of(r * ATT_BLOCK, ATT_BLOCK)
            tile(dc_ref[pl.ds(base, WIN_ROWS), :], p_ref[pl.ds(pl.multiple_of(base - CONV_PAD, CONV_PAD), WIN_ROWS), :], base)
            return carry

        lax.fori_loop(1, nblk - 1, step, 0)
        last = Lp - ATT_BLOCK
        tile(jnp.concatenate([dc_ref[last:Lp, :], zeros()], axis=0), p_ref[last - CONV_PAD:Lp, :], last)

    seq = pl.BlockSpec((None, Lp, dcw), lambda j, b: (b, 0, j))
    wsp = pl.BlockSpec((CONV_PAD, dcw), lambda j, b: (0, j))
    return _pcall(body, name=name, grid=(D // dcw, B), in_specs=[seq, seq, wsp], out_specs=[seq, wsp],
                  out_shape=[S((B, Lp, D), f32), S((CONV_PAD, D), f32)], scratch_shapes=[_phase_scratch(dcw)] * 2,
                  compiler_params=_params(("parallel", "arbitrary")))(dcv, p, dw)


def _band_start(n):
    return pl.multiple_of(jnp.maximum(n - 1, 0) * ATT_BLOCK, ATT_BLOCK)


def _attn_mask(n, R):
    shape = (R * ATT_BLOCK, 3 * ATT_BLOCK)
    qpos = n * ATT_BLOCK + (lax.broadcasted_iota(jnp.int32, shape, 0) & (ATT_BLOCK - 1))
    col = lax.broadcasted_iota(jnp.int32, shape, 1)
    meta_ok = (col < N_META_ROWS) & (col <= qpos)
    band_pos = _band_start(n) + (col - ATT_BLOCK)
    diff = qpos - band_pos
    band_ok = (col >= ATT_BLOCK) & (diff >= 0) & (diff < ATT_BLOCK) & (band_pos >= N_META_ROWS)
    return meta_ok | band_ok


def _row_head(R):
    row = lax.broadcasted_iota(jnp.int32, (R * ATT_BLOCK, R), 0)
    lo = lax.broadcasted_iota(jnp.int32, (R * ATT_BLOCK, R), 1) * ATT_BLOCK
    return ((row >= lo) & (row < lo + ATT_BLOCK)).astype(f32)


def _attn_core(q, kn, vv, qg, sink_vec, row_head, mask):
    qn = _rms(q, qg)
    s = lax.dot_general(qn.astype(bf16), kn.astype(bf16), _DN["nt"], preferred_element_type=f32) * (HEAD_DIM ** -0.5)
    s = jnp.where(mask, s, MASKED)
    sink = jnp.sum(row_head * sink_vec, axis=-1, keepdims=True)
    m = lax.stop_gradient(jnp.maximum(jnp.max(s, axis=-1, keepdims=True), sink))
    p = jnp.exp(s - m)
    denom = jnp.sum(p, axis=-1, keepdims=True) + jnp.exp(sink - m)
    return jnp.dot((p / denom).astype(bf16), vv.astype(bf16), preferred_element_type=f32)


def _keys_of(ref, n):
    return jnp.concatenate([ref[0:ATT_BLOCK, :], ref[pl.ds(_band_start(n), 2 * ATT_BLOCK), :]], axis=0)


def _stack_heads(x, R):
    return x if R == 1 else jnp.concatenate([x[:, r * HEAD_DIM:(r + 1) * HEAD_DIM] for r in range(R)], axis=0)


def _unstack_heads(x, R):
    return x if R == 1 else jnp.concatenate([x[r * ATT_BLOCK:(r + 1) * ATT_BLOCK] for r in range(R)], axis=1)


def _attn_specs(R, Lp):
    qspec = pl.BlockSpec((None, ATT_BLOCK, R * HEAD_DIM), lambda g, b, n: (b, n, g))
    kspec = pl.BlockSpec((None, None, Lp, HEAD_DIM), lambda g, b, n: (g, b, 0, 0))
    gspec = pl.BlockSpec((1, HEAD_DIM), lambda g, b, n: (0, 0))
    sspec = pl.BlockSpec((None, 1, R), lambda g, b, n: (g, 0, 0))
    return qspec, kspec, gspec, sspec


def _attn_fwd(name, q, kn, v, qg, sinks):
    B, Lp, D = q.shape
    G = kn.shape[0]
    R = D // (G * HEAD_DIM)
    qspec, kspec, gspec, sspec = _attn_specs(R, Lp)

    def body(q_ref, k_ref, v_ref, qg_ref, s_ref, o_ref):
        n = pl.program_id(2)
        o = _attn_core(_stack_heads(q_ref[...], R), _keys_of(k_ref, n), _keys_of(v_ref, n),
                       qg_ref[...], s_ref[...], _row_head(R), _attn_mask(n, R))
        o_ref[...] = _unstack_heads(o, R).astype(o_ref.dtype)

    return _pcall(body, name=name, grid=(G, B, Lp // ATT_BLOCK), in_specs=[qspec, kspec, kspec, gspec, sspec],
                  out_specs=qspec, out_shape=S(q.shape, bf16),
                  compiler_params=_params(("parallel", "parallel", "parallel")))(q, kn, v, qg, sinks)


def _attn_bwd(name, q, kn, v, qg, sinks, do):
    B, Lp, D = q.shape
    G = kn.shape[0]
    R = D // (G * HEAD_DIM)
    qspec, kspec, gspec, sspec = _attn_specs(R, Lp)

    def body(q_ref, k_ref, v_ref, qg_ref, s_ref, do_ref, dq_ref, dk_ref, dv_ref, dqg_ref, ds_ref):
        g, b, n = pl.program_id(0), pl.program_id(1), pl.program_id(2)

        @pl.when((g == 0) & (b == 0) & (n == 0))
        def _():
            dqg_ref[...] = jnp.zeros_like(dqg_ref)

        @pl.when((b == 0) & (n == 0))
        def _():
            ds_ref[...] = jnp.zeros_like(ds_ref)

        @pl.when(n == 0)
        def _():
            dk_ref[...] = jnp.zeros_like(dk_ref)
            dv_ref[...] = jnp.zeros_like(dv_ref)

        row_head, mask = _row_head(R), _attn_mask(n, R)
        _, vjp = jax.vjp(lambda q_, k_, v_, a_, s_: _attn_core(q_, k_, v_, a_, s_, row_head, mask),
                         _stack_heads(q_ref[...], R), _keys_of(k_ref, n).astype(f32),
                         _keys_of(v_ref, n).astype(f32), qg_ref[...], s_ref[...])
        dq, dkk, dvv, dqg, dsk = vjp(_stack_heads(do_ref[...].astype(f32), R))
        dq_ref[...] = _unstack_heads(dq, R)
        dqg_ref[...] += dqg
        ds_ref[...] += dsk
        band = pl.ds(_band_start(n), 2 * ATT_BLOCK)
        dk_ref[band, :] += dkk[ATT_BLOCK:]
        dv_ref[band, :] += dvv[ATT_BLOCK:]
        dk_ref[0:ATT_BLOCK, :] += dkk[:ATT_BLOCK]
        dv_ref[0:ATT_BLOCK, :] += dvv[:ATT_BLOCK]

    return _pcall(body, name=name, grid=(G, B, Lp // ATT_BLOCK),
                  in_specs=[qspec, kspec, kspec, gspec, sspec, qspec],
                  out_specs=[qspec, kspec, kspec, gspec, sspec],
                  out_shape=[S(q.shape, f32), S(kn.shape, f32), S(v.shape, f32), S(qg.shape, f32), S(sinks.shape, f32)],
                  compiler_params=_params(("arbitrary", "arbitrary", "arbitrary")))(q, kn, v, qg, sinks, do)


def _kv_heads(name, kv, kg, tm):
    T, W = kv.shape
    G = W // (2 * HEAD_DIM)

    def body(kv_ref, kg_ref, k_ref, v_ref):
        x = kv_ref[...]
        for g in range(G):
            k_ref[g] = _rms(x[:, g * HEAD_DIM:(g + 1) * HEAD_DIM], kg_ref[...]).astype(bf16)
            v_ref[g] = x[:, (G + g) * HEAD_DIM:(G + g + 1) * HEAD_DIM].astype(bf16)

    hspec = pl.BlockSpec((G, tm, HEAD_DIM), lambda i: (0, i, 0))
    return _pcall(body, name=name, grid=(T // tm,),
                  in_specs=[pl.BlockSpec((tm, W), lambda i: (i, 0)), pl.BlockSpec((1, HEAD_DIM), lambda i: (0, 0))],
                  out_specs=[hspec, hspec], out_shape=[S((G, T, HEAD_DIM), bf16)] * 2,
                  compiler_params=_params(("parallel",)))(kv, kg)


def _kv_heads_bwd(name, kv, kg, dkn, dv, tm):
    T, W = kv.shape
    G = W // (2 * HEAD_DIM)

    def body(kv_ref, kg_ref, dk_ref, dv_ref, o_ref, dkg_ref):
        @pl.when(pl.program_id(0) == 0)
        def _():
            dkg_ref[...] = jnp.zeros_like(dkg_ref)

        x = kv_ref[...]
        pieces = []
        for g in range(G):
            _, vjp = jax.vjp(_rms, x[:, g * HEAD_DIM:(g + 1) * HEAD_DIM], kg_ref[...])
            dk, dkg = vjp(dk_ref[g])
            pieces.append(dk)
            dkg_ref[...] += dkg
        o_ref[...] = jnp.concatenate(pieces + [dv_ref[g] for g in range(G)], axis=1)

    hspec = pl.BlockSpec((G, tm, HEAD_DIM), lambda i: (0, i, 0))
    gspec = pl.BlockSpec((1, HEAD_DIM), lambda i: (0, 0))
    return _pcall(body, name=name, grid=(T // tm,),
                  in_specs=[pl.BlockSpec((tm, W), lambda i: (i, 0)), gspec, hspec, hspec],
                  out_specs=[pl.BlockSpec((tm, W), lambda i: (i, 0)), gspec],
                  out_shape=[S((T, W), f32), S((1, HEAD_DIM), f32)], compiler_params=_params(("arbitrary",)))(kv, kg, dkn, dv)


def _place():
    x, y, c = lax.axis_index("x"), lax.axis_index("y"), lax.axis_index("c")
    chips = [(1 - x, y), (x, 1 - y), (1 - x, 1 - y)]
    return x, y, c, chips, [2 * cx + cy for cx, cy in chips]


def _remote(src, dst, send_sem, recv_sem, to):
    return pltpu.make_async_remote_copy(src_ref=src, dst_ref=dst, send_sem=send_sem, recv_sem=recv_sem,
                                        device_id=to, device_id_type=MESH)


def _into_slot(name, w, l, idx, slots, dtype):
    _, Rr, Cc = w.shape
    tr = _tile(Rr, (512, 256, 128, 64, 32, 16))

    def body(i_ref, x_ref, o_ref):
        o_ref[...] = x_ref[...].astype(o_ref.dtype)

    gs = pltpu.PrefetchScalarGridSpec(
        num_scalar_prefetch=1, grid=(Rr // tr,),
        in_specs=[pl.BlockSpec((None, tr, Cc), lambda i, i_ref: (l, i, 0))],
        out_specs=pl.BlockSpec((None, tr, Cc), lambda i, i_ref: (i_ref[0], i, 0)))
    return _pcall(body, name=name, grid_spec=gs, out_shape=S((slots, Rr, Cc), dtype), compiler_params=_params(("parallel",)))(idx, w)


def _gather_copies(bufs, ici, fwd, send, recv):
    x, y, c, chips, qk = _place()
    me_q, sib = 2 * x + y, (x, y, 1 - c)
    starts, lands, s = [], [], 0
    for p in ici:
        for k in range(3):
            mine = bufs[p].at[me_q, c]
            starts.append(_remote(mine, mine, send.at[s], recv.at[s], (*chips[k], c)))
            land = bufs[p].at[qk[k], c]
            lands.append(_remote(land, land, send.at[s], recv.at[s], sib))
            s += 1
    for p in fwd:
        for k in range(3):
            got = bufs[p].at[qk[k], c]
            starts.append(_remote(got, got, send.at[s], recv.at[s], sib))
            land = bufs[p].at[qk[k], 1 - c]
            lands.append(_remote(land, land, send.at[s], recv.at[s], sib))
            s += 1
    return starts, lands


def _gather_call(name, slabs, ici, fwd):
    n = len(slabs)

    def body(*refs):
        bufs = refs[n:2 * n]
        send1, recv1, send2, recv2 = refs[2 * n:]
        for phase, sems in (((ici, []), (send1, recv1)), (([], fwd), (send2, recv2))):
            starts, lands = _gather_copies(bufs, *phase, *sems)
            for cp in starts:
                cp.start()
            for cp in lands:
                cp.wait_recv()
            for cp in starts:
                cp.wait_send()

    sems = [pltpu.SemaphoreType.DMA((max(3 * len(ici), 1),))] * 2 + [pltpu.SemaphoreType.DMA((max(3 * len(fwd), 1),))] * 2
    return _pcall(body, name=name, in_specs=[ANY] * n, out_specs=[ANY] * n, out_shape=[S(s.shape, s.dtype) for s in slabs],
                  input_output_aliases={p: p for p in range(n)}, scratch_shapes=sems)(*slabs)


def _exchange_halves(grads):
    n = len(grads)

    def body(*refs):
        ins, outs = refs[:n], refs[n:2 * n]
        send, recv = refs[2 * n:]
        x, y, c, _, _ = _place()
        sib = (x, y, 1 - c)
        cps = [_remote(ins[p].at[q, 1 - c], outs[p].at[q], send.at[N_CHIPS * p + q], recv.at[N_CHIPS * p + q], sib)
               for p in range(n) for q in range(N_CHIPS)]
        for cp in cps:
            cp.start()
        for cp in cps:
            cp.wait_recv()
        for cp in cps:
            cp.wait_send()

    return _pcall(body, name="exchange_halves", in_specs=[ANY] * n, out_specs=[ANY] * n,
                  out_shape=[S((N_CHIPS,) + g.shape[2:], g.dtype) for g in grads],
                  scratch_shapes=[pltpu.SemaphoreType.DMA((N_CHIPS * n,)), pltpu.SemaphoreType.DMA((N_CHIPS * n,))])(*grads)


def _scatter_to_owners(sums):
    n = len(sums)

    def body(*refs):
        ins, outs = refs[:n], refs[n:2 * n]
        send, recv = refs[2 * n:]
        x, y, c, chips, qk = _place()
        cps = [_remote(ins[p].at[qk[k]], outs[p].at[k], send.at[3 * p + k], recv.at[3 * p + k], (*chips[k], c))
               for p in range(n) for k in range(3)]
        for cp in cps:
            cp.start()
        for cp in cps:
            cp.wait_recv()
        for cp in cps:
            cp.wait_send()

    return _pcall(body, name="scatter_to_owners", in_specs=[ANY] * n, out_specs=[ANY] * n,
                  out_shape=[S((3,) + s.shape[1:], s.dtype) for s in sums],
                  scratch_shapes=[pltpu.SemaphoreType.DMA((3 * n,)), pltpu.SemaphoreType.DMA((3 * n,))])(*sums)


def _share_halves(slabs):
    n = len(slabs)

    def body(*refs):
        bufs = refs[n:2 * n]
        send, recv = refs[2 * n:]
        x, y, c, _, _ = _place()
        sib = (x, y, 1 - c)
        cps = [_remote(bufs[p].at[c], bufs[p].at[c], send.at[p], recv.at[p], sib) for p in range(n)]
        for cp in cps:
            cp.start()
        for p in range(n):
            _remote(bufs[p].at[1 - c], bufs[p].at[1 - c], send.at[p], recv.at[p], sib).wait_recv()
        for cp in cps:
            cp.wait_send()

    return _pcall(body, name="share_halves", in_specs=[ANY] * n, out_specs=[ANY] * n,
                  out_shape=[S(s.shape, s.dtype) for s in slabs], input_output_aliases={p: p for p in range(n)},
                  scratch_shapes=[pltpu.SemaphoreType.DMA((n,)), pltpu.SemaphoreType.DMA((n,))])(*slabs)


def _gather_all_devices(block):
    m_per, ncol = block.shape

    def body(x_ref, out_ref, send_sems, recv_sems, local_sem):
        x, y, c, chips, _ = _place()
        me, sib = (x, y, c), (x, y, 1 - c)

        def rows(px, py, pc):
            return out_ref.at[pl.ds((4 * px + 2 * py + pc) * m_per, m_per), :]

        def copy(k, blk, to, src=None):
            return _remote(rows(*blk) if src is None else src, rows(*blk), send_sems.at[k], recv_sems.at[k], to)

        mine = pltpu.make_async_copy(x_ref, rows(*me), local_sem)
        mine.start()
        first = [copy(0, me, sib, src=x_ref)] + [copy(1 + j, me, (*chip, c), src=x_ref) for j, chip in enumerate(chips)]
        for cp in first:
            cp.start()
        passed = [copy(4 + j, (*chip, c), sib) for j, chip in enumerate(chips)]
        for j, chip in enumerate(chips):
            copy(1 + j, (*chip, c), me).wait_recv()
            passed[j].start()
        copy(0, sib, me).wait_recv()
        for j, chip in enumerate(chips):
            copy(4 + j, (*chip, 1 - c), me).wait_recv()
        for cp in first + passed:
            cp.wait_send()
        mine.wait()

    vm = pl.BlockSpec(memory_space=pltpu.VMEM)
    return _pcall(body, name="gather_small_grads", in_specs=[vm], out_specs=vm,
                  out_shape=S((8 * m_per, ncol), block.dtype),
                  scratch_shapes=[pltpu.SemaphoreType.DMA((7,)), pltpu.SemaphoreType.DMA((7,)), pltpu.SemaphoreType.DMA],
                  compiler_params=pltpu.CompilerParams(vmem_limit_bytes=VMEM_LIMIT_BYTES))(block)


def _sum_pair(name, g, r1, c_idx):
    Q, _, Rr, Cc = g.shape
    tr = _tile(Rr, (512, 256, 128))

    def body(c_ref, g_ref, r_ref, o_ref):
        o_ref[...] = (g_ref[...].astype(f32) + r_ref[...].astype(f32)).astype(o_ref.dtype)

    spec = pl.BlockSpec((None, tr, Cc), lambda q, i, c_ref: (q, i, 0))
    gs = pltpu.PrefetchScalarGridSpec(
        num_scalar_prefetch=1, grid=(Q, Rr // tr),
        in_specs=[pl.BlockSpec((None, None, tr, Cc), lambda q, i, c_ref: (q, c_ref[0], i, 0)), spec], out_specs=spec)
    return _pcall(body, name=name, grid_spec=gs, out_shape=S((Q, Rr, Cc), bf16),
                  compiler_params=_params(("parallel", "parallel")))(c_idx, g, r1)


def _sum_owner(name, s, r2, qc_idx):
    Q, Rr, Cc = s.shape
    tr = _tile(Rr, (512, 256, 128))

    def body(q_ref, s_ref, r_ref, o_ref):
        o_ref[...] = ((s_ref[...].astype(f32) + r_ref[0].astype(f32)) + r_ref[1].astype(f32)) + r_ref[2].astype(f32)

    gs = pltpu.PrefetchScalarGridSpec(
        num_scalar_prefetch=1, grid=(Rr // tr,),
        in_specs=[pl.BlockSpec((None, tr, Cc), lambda i, q_ref: (q_ref[0], i, 0)), pl.BlockSpec((3, tr, Cc), lambda i, q_ref: (0, i, 0))],
        out_specs=pl.BlockSpec((None, tr, Cc), lambda i, q_ref: (q_ref[1], i, 0)))
    return _pcall(body, name=name, grid_spec=gs, out_shape=S((2, Rr, Cc), f32),
                  compiler_params=_params(("parallel",)))(qc_idx, s, r2)


def _sum_devices(stack):
    n, M, C = stack.shape

    def body(s_ref, o_ref):
        acc = s_ref[0]
        for d in range(1, n):
            acc = acc + s_ref[d]
        o_ref[...] = acc

    return _pcall(body, name="sum_small_grads", out_shape=S((M, C), f32), compiler_params=_params())(stack)


def _adamw(name, w, g, m, v):
    Rr, Cc = w.shape
    tr = _tile(Rr, (256, 128, 64, 32, 16, 8))

    def body(w_ref, g_ref, m_ref, v_ref, d_ref, mo_ref, vo_ref):
        g_ = g_ref[...]
        m_ = B1 * m_ref[...] + (1.0 - B1) * g_
        v_ = B2 * v_ref[...] + (1.0 - B2) * jnp.square(g_)
        m_hat = m_ / (1.0 - B1 ** STEP)
        v_hat = v_ / (1.0 - B2 ** STEP)
        d_ref[...] = -LR * (m_hat / (jnp.sqrt(v_hat) + ADAM_EPS) + WD * w_ref[...])
        mo_ref[...] = m_
        vo_ref[...] = v_

    spec = pl.BlockSpec((tr, Cc), lambda i: (i, 0))
    return _pcall(body, name=name, grid=(Rr // tr,), in_specs=[spec] * 4, out_specs=[spec] * 3,
                  out_shape=[S((Rr, Cc), f32)] * 3, compiler_params=_params(("parallel",)))(w, g, m, v)


def _pack(arrs, multiple):
    flat = jnp.concatenate([a.reshape(-1) for a in arrs])
    pad = (-flat.shape[0]) % multiple
    return jnp.pad(flat, (0, pad)).reshape(-1, 128)


def _unpack(slab, shapes):
    flat, out, o = slab.reshape(-1), [], 0
    for shp in shapes:
        n = 1
        for d in shp:
            n *= d
        out.append(flat[o:o + n].reshape(shp))
        o += n
    return out


def kernel(x, meta_tokens, norm_mix, norm_ffn, conv_w_in, conv_b_in, conv_dw, conv_ln_g, conv_ln_b, conv_w_out, conv_b_out, kv_norm, w_kv, k_norm, w_q, q_norm, attn_sinks, w_o, ffn_w_gate, ffn_w_up, ffn_w_down, loss_target, m_meta_tokens, m_norm_mix, m_norm_ffn, m_conv_w_in, m_conv_b_in, m_conv_dw, m_conv_ln_g, m_conv_ln_b, m_conv_w_out, m_conv_b_out, m_kv_norm, m_w_kv, m_k_norm, m_w_q, m_q_norm, m_attn_sinks, m_w_o, m_ffn_w_gate, m_ffn_w_up, m_ffn_w_down, v_meta_tokens, v_norm_mix, v_norm_ffn, v_conv_w_in, v_conv_b_in, v_conv_dw, v_conv_ln_g, v_conv_ln_b, v_conv_w_out, v_conv_b_out, v_kv_norm, v_w_kv, v_k_norm, v_w_q, v_q_norm, v_attn_sinks, v_w_o, v_ffn_w_gate, v_ffn_w_up, v_ffn_w_down):
    Q = N_CHIPS
    B, SEQ, D = x.shape
    L = N_META_ROWS + SEQ
    Lp = -(-L // ATT_BLOCK) * ATT_BLOCK
    T = B * Lp
    NA, NB = conv_w_in.shape[0], w_q.shape[0]
    NL = NA + NB
    Dq = D // Q
    G = N_KV
    R = D // (HEAD_DIM * G)
    KVW = w_kv.shape[1]
    assert NA % 2 == 0 and NB % 2 == 0 and NL % 2 == 0 and (D // Q) % 32 == 0
    tm = _tile(T, (1088, 544, 512, 256, 128))
    tk = _tile(T, (1088, 544, 512, 256, 128))
    tr = _tile(T, (272, 256, 128))
    my_c = lax.axis_index("c").astype(jnp.int32).reshape(1)
    my_q = (2 * lax.axis_index("x") + lax.axis_index("y")).astype(jnp.int32)
    my_qc = jnp.concatenate([my_q.reshape(1), my_c])

    small_shapes = [meta_tokens.shape, conv_b_in.shape, conv_dw.shape, conv_ln_g.shape, conv_ln_b.shape, conv_b_out.shape]
    small = _pack([meta_tokens, conv_b_in, conv_dw, conv_ln_g, conv_ln_b, conv_b_out], 2048)
    big = [conv_w_in, conv_w_out, w_kv, w_q, w_o, ffn_w_gate, ffn_w_up, ffn_w_down]
    keyed = dict(cin=conv_w_in, cout=conv_w_out, kv=w_kv[None], q=w_q, o=w_o, g=ffn_w_gate, u=ffn_w_up, d=ffn_w_down,
                 small=small[None])
    slabs, where = [], {}
    for key, w3 in keyed.items():
        for l in range(w3.shape[0]):
            s = _into_slot(f"own_{key}{l}", w3, l, my_qc, Q, f32 if key == "small" else bf16)
            where[key, l] = len(slabs)
            slabs.append(s.reshape(Q, 2, s.shape[1] // 2, s.shape[2]))

    def W(key, l):
        s = slabs[where[key, l]]
        return s.reshape(Q, 1, 2 * s.shape[2], s.shape[3])

    def layer_slabs(l):
        j = l - NA
        mix = [("cin", l), ("cout", l)] if l < NA else [("q", j), ("o", j)] + ([("kv", 0)] if j == 0 else [])
        return [where[k] for k in mix], [where["g", l], where["u", l]], [where["d", l]]

    def carry_of(ici, fwd):
        idxs = sorted(set(ici) | set(fwd))
        return idxs, ([slabs[i] for i in idxs], [idxs.index(i) for i in ici], [idxs.index(i) for i in fwd])

    def put_back(idxs, new):
        for i, s in zip(idxs, new):
            slabs[i] = s

    first = sum(layer_slabs(0), []) + [where["small", 0]]
    idxs, (sl, ici, fwd) = carry_of(first, first)
    put_back(idxs, _gather_call("gather_first", sl, ici, fwd))
    parts = [_unpack(slabs[where["small", 0]][q], small_shapes) for q in range(Q)]
    meta_f, b_in_f, dw_f, ln_g_f, ln_b_f, b_out_f = [jnp.concatenate([parts[q][i] for q in range(Q)], axis=-1) for i in range(6)]
    dw_pad = jnp.pad(dw_f, ((0, 0), (0, CONV_PAD - CONV_TAPS), (0, 0)))

    h = jnp.concatenate([jnp.broadcast_to(meta_f[None], (B, N_META_ROWS, D)), x, jnp.zeros((B, Lp - L, D), f32)], axis=1).reshape(T, D)
    tgt = jnp.pad(loss_target, ((0, 0), (N_META_ROWS, Lp - L), (0, 0))).reshape(T, D)
    row = lambda a: a.reshape(1, -1)
    seqs = lambda a: a.reshape(B, Lp, a.shape[-1])
    by_seq = lambda a: a.reshape(G, B, Lp, HEAD_DIM)
    saved = []
    sinks3 = attn_sinks.reshape(NB, G, 1, R)
    for l in range(NL):
        st = {"h_a": h}
        u = _rowwise(f"rms_mix{l}", _rms_fn, [h], [row(norm_mix[l])], [bf16], tr)[0]
        st["u"] = u
        idxs, carry = carry_of([], layer_slabs(l)[1]) if l > 0 else ([], None)
        if l < NA:
            p, av, ag, *new = _glu_fwd(f"glu{l}", u, W("cin", l), 0, b_in_f[l:l + 1], tm, carry=carry)
            put_back(idxs, new)
            cv = _dwconv_fwd(f"dwconv{l}", p.reshape(B, Lp, D), dw_pad[l]).reshape(T, D)
            s = _rowwise(f"ln_silu{l}", _ln_silu_fn, [cv], [row(ln_g_f[l]), row(ln_b_f[l])], [bf16], tr)[0]
            h = _proj(f"conv_out{l}", s, W("cout", l), 0, tm, bias=row(b_out_f[l]), resid=h)
            st.update(av=av, ag=ag, p=p, cv=cv, s=s)
        else:
            j = l - NA
            if j == 0:
                kvn = _rowwise("rms_kv", _rms_fn, [h], [row(kv_norm)], [bf16], tr)[0]
                kv = _proj("kv_proj", kvn, W("kv", 0), 0, tm)
                kn, vh = _kv_heads("kv_heads", kv, row(k_norm), tk)
                kn, vh = by_seq(kn), by_seq(vh)
                st.update(kvn=kvn, kv=kv)
            res = _proj(f"q_proj{j}", u, W("q", j), 0, tm, carry=carry)
            q, new = res if carry is not None else (res, [])
            put_back(idxs, new)
            q = seqs(q)
            o = _attn_fwd(f"attn{j}", q, kn, vh, row(q_norm[j]), sinks3[j]).reshape(T, D)
            h = _proj(f"o_proj{j}", o, W("o", j), 0, tm, resid=h)
            st.update(q=q, o=o)
        st["h_b"] = h
        u2 = _rowwise(f"rms_ffn{l}", _rms_fn, [h], [row(norm_ffn[l])], [bf16], tr)[0]
        nxt = layer_slabs(l + 1) if l + 1 < NL else ([], [], [])
        idxs, carry = carry_of(nxt[0] + nxt[2], [])
        gate, up, hid, *new = _ffn_up(f"ffn_up{l}", u2, W("g", l), W("u", l), 0, tm, carry=carry if idxs else None)
        put_back(idxs, new)
        idxs, carry = carry_of(nxt[1], nxt[0] + nxt[2])
        h, *new = _ffn_down(f"ffn_down{l}", hid, W("d", l), 0, h, tm, carry=carry if idxs else None)
        put_back(idxs, new)
        st.update(u2=u2, gate=gate, up=up, hid=hid)
        saved.append(st)

    dh, part = _loss_head(h, tgt, Lp, SEQ, tr)
    loss = lax.psum(0.5 / D * jnp.sum(part), ("x", "y", "c"))

    gW = dict(cin=None, cout=None, q=None, o=None, g=None, u=None, d=None)
    g_mix, g_ffn = [None] * NL, [None] * NL
    g_bin, g_dw, g_lng, g_lnb, g_bout = ([None] * NA for _ in range(5))
    g_qn, g_sink = [None] * NB, [None] * NB
    dknp = dvp = None
    zero_row = jnp.zeros((1, D), f32)
    for l in reversed(range(NL)):
        st = saved[l]
        dgate, dup = _ffn_dhid(f"ffn_dhid{l}", dh, W("d", l), 0, st["gate"], st["up"], tm)
        gW["d"] = _dw_rows(f"ffn_dwd{l}", st["hid"], dh, Q, tk, NL, l, gW["d"])
        gW["g"] = _dw_cols(f"ffn_dwg{l}", st["u2"], dgate, Q, tk, NL, l, gW["g"], Q)
        gW["u"] = _dw_cols(f"ffn_dwu{l}", st["u2"], dup, Q, tk, NL, l, gW["u"], Q)
        du2 = _ffn_du(f"ffn_du{l}", dgate, dup, W("g", l), W("u", l), 0, tm)
        dh, g_ffn[l] = _rowwise_vjp(f"rms_ffn_bwd{l}", _rms_res_fn, [st["h_b"]], [row(norm_ffn[l])], [du2, dh], [f32], tr)
        if l < NA:
            ds = _proj_dx(f"conv_out_dx{l}", dh, W("cout", l), 0, tm)
            gW["cout"] = _dw_rows(f"conv_out_dw{l}", st["s"], dh, Q, tk, NA, l, gW["cout"])
            dcv, g_lng[l], g_lnb[l] = _rowwise_vjp(f"ln_silu_bwd{l}", _ln_silu_fn, [st["cv"]], [row(ln_g_f[l]), row(ln_b_f[l])], [ds], [f32], tr)
            dp, g_dw[l] = _dwconv_bwd(f"dwconv_bwd{l}", dcv.reshape(B, Lp, D), st["p"].reshape(B, Lp, D), dw_pad[l])
            dav, dag, dbv, dbg = _rowwise_vjp(f"glu_bwd{l}", _glu_fn, [st["av"], st["ag"]], [zero_row, zero_row], [dp.reshape(T, D)], [bf16, bf16], tr)
            g_bin[l] = jnp.concatenate([dbv, dbg], axis=1)
            gW["cin"] = _dw_cols(f"glu_dwv{l}", st["u"], dav, Q // 2, tk, NA, l, gW["cin"], Q)
            gW["cin"] = _dw_cols(f"glu_dwg{l}", st["u"], dag, Q // 2, tk, NA, l, gW["cin"], Q, q_off=Q // 2)
            du = _glu_du(f"glu_du{l}", dav, dag, W("cin", l), 0, tm)
            dh, g_mix[l], g_bout[l] = _rowwise_vjp(f"rms_mix_bwd{l}", _rms_res_bias_fn, [st["h_a"]], [row(norm_mix[l]), zero_row], [du, dh], [f32], tr)
        else:
            j = l - NA
            do = _proj_dx(f"o_proj_dx{j}", dh, W("o", j), 0, tm, out_dtype=bf16)
            gW["o"] = _dw_rows(f"o_proj_dw{j}", st["o"], dh, Q, tk, NB, j, gW["o"])
            dq, dk1, dv1, g_qn[j], g_sink[j] = _attn_bwd(f"attn_bwd{j}", st["q"], kn, vh, row(q_norm[j]), sinks3[j], seqs(do))
            dknp, dvp = (dk1, dv1) if dknp is None else (dknp + dk1, dvp + dv1)
            dq = dq.reshape(T, D)
            gW["q"] = _dw_rows(f"q_proj_dw{j}", st["u"], dq, Q, tk, NB, j, gW["q"])
            du = _proj_dx(f"q_proj_dx{j}", dq, W("q", j), 0, tm)
            dh, g_mix[l] = _rowwise_vjp(f"rms_mix_bwd{l}", _rms_res_fn, [st["h_a"]], [row(norm_mix[l])], [du, dh], [f32], tr)
            if j == 0:
                dkv, g_kn = _kv_heads_bwd("kv_heads_bwd", st["kv"], row(k_norm), dknp.reshape(G, T, HEAD_DIM), dvp.reshape(G, T, HEAD_DIM), tk)
                g_wkv = _dw_rows("kv_proj_dw", st["kvn"], dkv, Q, tk, 1, 0, None)
                dkvn = _proj_dx("kv_proj_dx", dkv, W("kv", 0), 0, tm)
                dh, g_kvn = _rowwise_vjp("rms_kv_bwd", _rms_res_fn, [st["h_a"]], [row(kv_norm)], [dkvn, dh], [f32], tr)
    dh3 = dh.reshape(B, Lp, D)
    grad_x = dh3[:, N_META_ROWS:L]
    g_meta = jnp.sum(dh3[:, :N_META_ROWS], axis=0)

    grads4 = [gW["cin"], gW["cout"], g_wkv, gW["q"], gW["o"], gW["g"], gW["u"], gW["d"]]
    from_sib = _exchange_halves(grads4)
    chip_sums = [_sum_pair(f"sum_pair{i}", g, r, my_c) for i, (g, r) in enumerate(zip(grads4, from_sib))]
    from_chips = _scatter_to_owners(chip_sums)
    shard_grads = _share_halves([_sum_owner(f"sum_owner{i}", s, r, my_qc) for i, (s, r) in enumerate(zip(chip_sums, from_chips))])

    names = ["conv_w_in", "conv_w_out", "w_kv", "w_q", "w_o", "ffn_w_gate", "ffn_w_up", "ffn_w_down"]
    ws = dict(zip(names, big))
    ms = dict(zip(names, [m_conv_w_in, m_conv_w_out, m_w_kv, m_w_q, m_w_o, m_ffn_w_gate, m_ffn_w_up, m_ffn_w_down]))
    vs = dict(zip(names, [v_conv_w_in, v_conv_w_out, v_w_kv, v_w_q, v_w_o, v_ffn_w_gate, v_ffn_w_up, v_ffn_w_down]))
    out_g, out_d, out_m, out_v = {}, {}, {}, {}
    for nm, gsh in zip(names, shard_grads):
        w = ws[nm]
        flat = lambda a: a.reshape(-1, w.shape[-1])
        g2 = flat(gsh)
        d2, m2, v2 = _adamw(f"adamw_{nm}", flat(w), g2, flat(ms[nm]), flat(vs[nm]))
        out_g[nm], out_d[nm], out_m[nm], out_v[nm] = (a.reshape(w.shape) for a in (g2, d2, m2, v2))

    small_names = ["norm_mix", "norm_ffn", "kv_norm", "k_norm", "q_norm", "attn_sinks", "meta_tokens", "conv_b_in", "conv_dw", "conv_ln_g", "conv_ln_b", "conv_b_out"]
    small_grads = [jnp.concatenate(g_mix, 0), jnp.concatenate(g_ffn, 0), g_kvn.reshape(-1), g_kn.reshape(-1), jnp.concatenate(g_qn, 0),
                   jnp.stack(g_sink).reshape(NB, G * R), g_meta, jnp.concatenate(g_bin, 0), jnp.stack(g_dw)[:, :CONV_TAPS],
                   jnp.concatenate(g_lng, 0), jnp.concatenate(g_lnb, 0), jnp.concatenate(g_bout, 0)]
    slab = _pack(small_grads, 1024)
    total = _sum_devices(_gather_all_devices(slab).reshape(8, slab.shape[0], 128))
    full_grads = _unpack(total, [g.shape for g in small_grads])
    small_w = dict(zip(small_names, [norm_mix, norm_ffn, kv_norm, k_norm, q_norm, attn_sinks, meta_tokens, conv_b_in, conv_dw, conv_ln_g, conv_ln_b, conv_b_out]))
    small_m = dict(zip(small_names, [m_norm_mix, m_norm_ffn, m_kv_norm, m_k_norm, m_q_norm, m_attn_sinks, m_meta_tokens, m_conv_b_in, m_conv_dw, m_conv_ln_g, m_conv_ln_b, m_conv_b_out]))
    small_v = dict(zip(small_names, [v_norm_mix, v_norm_ffn, v_kv_norm, v_k_norm, v_q_norm, v_attn_sinks, v_meta_tokens, v_conv_b_in, v_conv_dw, v_conv_ln_g, v_conv_ln_b, v_conv_b_out]))
    local_grads = []
    for nm, g in zip(small_names, full_grads):
        w = small_w[nm]
        if g.shape != w.shape:
            wq = w.shape[-1]
            g = lax.dynamic_slice_in_dim(g, my_q * wq, wq, axis=g.ndim - 1)
        local_grads.append(g)
    shapes = [small_w[nm].shape for nm in small_names]
    d_s, m_s, v_s = _adamw("adamw_small", _pack([small_w[nm] for nm in small_names], 1024), _pack(local_grads, 1024),
                           _pack([small_m[nm] for nm in small_names], 1024), _pack([small_v[nm] for nm in small_names], 1024))
    for nm, g, d_, m_, v_ in zip(small_names, local_grads, _unpack(d_s, shapes), _unpack(m_s, shapes), _unpack(v_s, shapes)):
        out_g[nm], out_d[nm], out_m[nm], out_v[nm] = g, d_, m_, v_

    order = ["meta_tokens", "norm_mix", "norm_ffn", "conv_w_in", "conv_b_in", "conv_dw", "conv_ln_g", "conv_ln_b", "conv_w_out", "conv_b_out",
             "kv_norm", "w_kv", "k_norm", "w_q", "q_norm", "attn_sinks", "w_o", "ffn_w_gate", "ffn_w_up", "ffn_w_down"]
    return (loss, grad_x, *[out_g[n] for n in order], *[out_d[n] for n in order], *[out_m[n] for n in order], *[out_v[n] for n in order])
```

```python
import functools

import jax
import jax.numpy as jnp
from jax import lax
from jax.experimental import pallas as pl
from jax.experimental.pallas import tpu as pltpu

f32, bf16 = jnp.float32, jnp.bfloat16

N_META_ROWS = 16
ATT_BLOCK = 128
HEAD_DIM = 64
N_KV = 4
CONV_TAPS = 31
CONV_PAD = 32
EPS = 1e-6
MASKED = -1e30
LR, B1, B2, ADAM_EPS, WD, STEP = 0.001, 0.9, 0.999, 1e-08, 0.01, 10
N_CHIPS = 4
VMEM_LIMIT_BYTES = 56 * 1024 * 1024
MESH = pl.DeviceIdType.MESH
ANY = pl.BlockSpec(memory_space=pl.ANY)
S = jax.ShapeDtypeStruct


def _pcall(body, **kw):
    return pl.pallas_call(body, **kw)


def _params(sem=None):
    return pltpu.CompilerParams(dimension_semantics=sem, vmem_limit_bytes=VMEM_LIMIT_BYTES)


def _tile(n, prefs):
    for p in prefs:
        if n % p == 0:
            return p
    return n


_DN = {"nn": (((1,), (0,)), ((), ())), "nt": (((1,), (1,)), ((), ())), "tn": (((0,), (0,)), ((), ()))}


def _matmul(name, mode, grid, a_ops, b_ops, x_ops, outs, terms, acc_shape, n_acc, epilogue, into=None, carry=None):
    na, nb, nx, no = len(a_ops), len(b_ops), len(x_ops), len(outs)
    nk = grid[2]
    slabs, plan = carry if carry is not None else ([], [])
    nc, ncp = len(slabs), _plan_copies(plan)
    n_in = na + nb + nx + (0 if into is None else 1) + nc

    def flat2d(v):
        return v.reshape(-1, v.shape[-1]) if v.ndim == 3 else v

    def dots(a_refs, b_refs):
        parts = [None] * n_acc
        for ai, bi, ci in terms:
            d = lax.dot_general(flat2d(a_refs[ai][...]).astype(bf16), flat2d(b_refs[bi][...]).astype(bf16), _DN[mode],
                                preferred_element_type=f32)
            parts[ci] = d if parts[ci] is None else parts[ci] + d
        return parts

    def finish(accs, x_refs, o_refs):
        res = epilogue(accs, [x[...] for x in x_refs])
        for o_ref, r in zip(o_refs, res):
            o_ref[...] = r.reshape(o_ref.shape).astype(o_ref.dtype)

    def compute(a_refs, b_refs, x_refs, o_refs, acc_refs):
        if nk == 1:
            finish(dots(a_refs, b_refs), x_refs, o_refs)
            return
        k = pl.program_id(2)

        @pl.when(k == 0)
        def _():
            for acc in acc_refs:
                acc[...] = jnp.zeros_like(acc)

        for acc, d in zip(acc_refs, dots(a_refs, b_refs)):
            acc[...] += d

        @pl.when(k == nk - 1)
        def _():
            finish([acc[...] for acc in acc_refs], x_refs, o_refs)

    def body(*refs):
        a_refs, b_refs = refs[:na], refs[na:na + nb]
        x_refs = refs[na + nb:na + nb + nx]
        o_refs = refs[n_in:n_in + no]
        bufs = refs[n_in + no:n_in + no + nc]
        scratch = refs[n_in + no + nc:]
        if not nc:
            compute(a_refs, b_refs, x_refs, o_refs, scratch)
            return
        acc_refs, (send, recv) = scratch[:-2], scratch[-2:]
        i, j, k = pl.program_id(0), pl.program_id(1), pl.program_id(2)

        @pl.when((i == 0) & (j == 0) & (k == 0))
        def _():
            _run_copies(bufs, plan, send, recv, start=True, wait=False)

        compute(a_refs, b_refs, x_refs, o_refs, acc_refs)

        @pl.when((i == grid[0] - 1) & (j == grid[1] - 1) & (k == nk - 1))
        def _():
            _run_copies(bufs, plan, send, recv, start=False, wait=True)

    ops = list(a_ops) + list(b_ops) + list(x_ops)
    aliases = {}
    if into is not None:
        ops.append((into, ANY))
        aliases = {len(ops) - 1: 0}
    for t, s in enumerate(slabs):
        ops.append((s, ANY))
        aliases[len(ops) - 1] = no + t
    outs = list(outs) + [(S(s.shape, s.dtype), ANY) for s in slabs]
    scratch = [pltpu.VMEM(acc_shape, f32)] * (n_acc if nk > 1 else 0)
    if nc:
        scratch += [pltpu.SemaphoreType.DMA((ncp,)), pltpu.SemaphoreType.DMA((ncp,))]
    sem = ("arbitrary",) * 3 if nc else ("parallel", "parallel", "arbitrary")
    res = _pcall(body, name=name, grid=grid, in_specs=[s for _, s in ops], out_specs=[s for _, s in outs],
                 out_shape=[s for s, _ in outs], scratch_shapes=scratch, input_output_aliases=aliases,
                 compiler_params=_params(sem))(*[a for a, _ in ops])
    return res


def _first(accs, xs):
    return (accs[0],)


def _proj(name, a, w, l, tm, bias=None, resid=None, out_dtype=f32, carry=None):
    T = a.shape[0]
    Q, _, Kq, N = w.shape
    tn = _tile(N, (512, 256, 128))
    x_ops, epi = [], _first
    if bias is not None:
        x_ops = [(bias, pl.BlockSpec((1, tn), lambda i, j, k: (0, j))), (resid, pl.BlockSpec((tm, tn), lambda i, j, k: (i, j)))]
        epi = lambda accs, xs: (accs[0] + xs[0] + xs[1],)
    elif resid is not None:
        x_ops = [(resid, pl.BlockSpec((tm, tn), lambda i, j, k: (i, j)))]
        epi = lambda accs, xs: (accs[0] + xs[0],)
    res = _matmul(name, "nn", (T // tm, N // tn, 1),
                  [(a, pl.BlockSpec((tm, Q * Kq), lambda i, j, k: (i, 0)))],
                  [(w, pl.BlockSpec((Q, None, Kq, tn), lambda i, j, k: (0, l, 0, j)))],
                  x_ops, [(S((T, N), out_dtype), pl.BlockSpec((tm, tn), lambda i, j, k: (i, j)))],
                  [(0, 0, 0)], (tm, tn), 1, epi, carry=carry)
    return res[0] if carry is None else (res[0], res[1:])


def _proj_dx(name, dy, w, l, tm, out_dtype=f32, carry=None):
    T, N = dy.shape
    Q, _, Kq, _ = w.shape
    return _matmul(name, "nt", (T // tm, 1, 1),
                   [(dy, pl.BlockSpec((tm, N), lambda i, j, k: (i, 0)))],
                   [(w, pl.BlockSpec((Q, None, Kq, N), lambda i, j, k: (0, l, 0, 0)))],
                   [], [(S((T, Q * Kq), out_dtype), pl.BlockSpec((tm, Q * Kq), lambda i, j, k: (i, 0)))],
                   [(0, 0, 0)], (tm, Q * Kq), 1, _first, carry=carry)


def _dw_rows(name, a, dy, Q, tk, carry=None):
    T, N = dy.shape
    tn = _tile(N, (512, 256, 128))
    if a.ndim == 2:
        K = a.shape[1]
        a_op, ni, acc_rows = (a, pl.BlockSpec((tk, K), lambda i, j, k: (k, 0))), 1, K
        osh, ospec = S((Q, K // Q, N), bf16), pl.BlockSpec((Q, K // Q, tn), lambda i, j, k: (0, 0, j))
    else:
        Kq = a.shape[2]
        a_op, ni, acc_rows = (a, pl.BlockSpec((None, tk, Kq), lambda i, j, k: (i, k, 0))), Q, Kq
        osh, ospec = S((Q, Kq, N), bf16), pl.BlockSpec((None, Kq, tn), lambda i, j, k: (i, 0, j))
    return _matmul(name, "tn", (ni, N // tn, T // tk), [a_op], [(dy, pl.BlockSpec((tk, tn), lambda i, j, k: (k, j)))],
                   [], [(osh, ospec)], [(0, 0, 0)], (acc_rows, tn), 1, _first, carry=carry)


def _dw_cols(name, a, dyc, Qc, tk, Q, q_off=0, into=None, carry=None):
    T, K = a.shape
    tkin = _tile(K, (512, 256, 128))
    if dyc.ndim == 3:
        Nq = dyc.shape[2]
        b_op = (dyc, pl.BlockSpec((None, tk, Nq), lambda i, j, k: (j, k, 0)))
    else:
        Nq = dyc.shape[1] // Qc
        b_op = (dyc, pl.BlockSpec((tk, Nq), lambda i, j, k: (k, j)))
    ospec = pl.BlockSpec((None, tkin, Nq), lambda i, j, k: (j + q_off, i, 0))
    return _matmul(name, "tn", (K // tkin, Qc, T // tk),
                   [(a, pl.BlockSpec((tk, tkin), lambda i, j, k: (k, i)))], [b_op], [],
                   [(S((Q, K, Nq), bf16), ospec)], [(0, 0, 0)], (tkin, Nq), 1, _first, into=into, carry=carry)


def _glu_fwd(name, u, w, l, b_in, tm, carry=None):
    T, D = u.shape
    Q, _, _, Cq = w.shape
    H = Q // 2

    def epi(accs, xs):
        av, ag = accs[0] + xs[0], accs[1] + xs[1]
        return av * jax.nn.sigmoid(ag), av, ag

    wspec = lambda off: pl.BlockSpec((None, None, D, Cq), lambda i, j, k: (j + off, l, 0, 0))
    bspec = lambda off: pl.BlockSpec((1, Cq), lambda i, j, k: (0, j + off))
    ospec = pl.BlockSpec((tm, Cq), lambda i, j, k: (i, j))
    return _matmul(name, "nn", (T // tm, H, 1),
                   [(u, pl.BlockSpec((tm, D), lambda i, j, k: (i, 0)))],
                   [(w, wspec(0)), (w, wspec(H))], [(b_in, bspec(0)), (b_in, bspec(H))],
                   [(S((T, H * Cq), f32), ospec), (S((T, H * Cq), bf16), ospec), (S((T, H * Cq), bf16), ospec)],
                   [(0, 0, 0), (0, 1, 1)], (tm, Cq), 2, epi, carry=carry)


def _glu_du(name, dav, dag, w, l, tm):
    T = dav.shape[0]
    Q, _, D, Cq = w.shape
    H = Q // 2
    tn = _tile(D, (512, 256, 128))
    aspec = pl.BlockSpec((tm, Cq), lambda i, j, k: (i, k))
    wspec = lambda off: pl.BlockSpec((None, None, tn, Cq), lambda i, j, k: (k + off, l, j, 0))
    return _matmul(name, "nt", (T // tm, D // tn, H), [(dav, aspec), (dag, aspec)],
                   [(w, wspec(0)), (w, wspec(H))], [],
                   [(S((T, D), f32), pl.BlockSpec((tm, tn), lambda i, j, k: (i, j)))],
                   [(0, 0, 0), (1, 1, 0)], (tm, tn), 1, _first)[0]


def _ffn_up(name, u, wg, wu, l, tm, carry=None):
    T, D = u.shape
    Q, _, _, Fq = wg.shape

    def epi(accs, xs):
        g, up = accs
        return g, up, g * jax.nn.sigmoid(g) * up

    wspec = pl.BlockSpec((None, None, D, Fq), lambda i, j, k: (j, l, 0, 0))
    ospec = pl.BlockSpec((None, tm, Fq), lambda i, j, k: (j, i, 0))
    osh = S((Q, T, Fq), bf16)
    return _matmul(name, "nn", (T // tm, Q, 1), [(u, pl.BlockSpec((tm, D), lambda i, j, k: (i, 0)))],
                   [(wg, wspec), (wu, wspec)], [], [(osh, ospec)] * 3, [(0, 0, 0), (0, 1, 1)], (tm, Fq), 2, epi, carry=carry)


def _ffn_down(name, hid, wd, l, resid, tm, carry=None):
    Q, T, Fq = hid.shape
    D = wd.shape[3]
    tn = _tile(D, (512, 256, 128))
    return _matmul(name, "nn", (T // tm, D // tn, Q),
                   [(hid, pl.BlockSpec((None, tm, Fq), lambda i, j, k: (k, i, 0)))],
                   [(wd, pl.BlockSpec((None, None, Fq, tn), lambda i, j, k: (k, l, 0, j)))],
                   [(resid, pl.BlockSpec((tm, tn), lambda i, j, k: (i, j)))],
                   [(S((T, D), f32), pl.BlockSpec((tm, tn), lambda i, j, k: (i, j)))],
                   [(0, 0, 0)], (tm, tn), 1, lambda accs, xs: (accs[0] + xs[0],), carry=carry)


def _ffn_dhid(name, dy, wd, l, gate, up, tm, carry=None):
    T, D = dy.shape
    Q, _, Fq, _ = wd.shape

    def epi(accs, xs):
        dh, g, up = accs[0], xs[0].astype(f32), xs[1].astype(f32)
        sg = jax.nn.sigmoid(g)
        return dh * up * (sg * (1.0 + g * (1.0 - sg))), dh * (g * sg)

    cspec = pl.BlockSpec((None, tm, Fq), lambda i, j, k: (j, i, 0))
    osh = S((Q, T, Fq), bf16)
    return _matmul(name, "nt", (T // tm, Q, 1), [(dy, pl.BlockSpec((tm, D), lambda i, j, k: (i, 0)))],
                   [(wd, pl.BlockSpec((None, None, Fq, D), lambda i, j, k: (j, l, 0, 0)))],
                   [(gate, cspec), (up, cspec)], [(osh, cspec)] * 2, [(0, 0, 0)], (tm, Fq), 1, epi, carry=carry)


def _ffn_du(name, dgate, dup, wg, wu, l, tm, carry=None):
    Q, T, Fq = dgate.shape
    D = wg.shape[2]
    tn = _tile(D, (512, 256, 128))
    aspec = pl.BlockSpec((None, tm, Fq), lambda i, j, k: (k, i, 0))
    wspec = pl.BlockSpec((None, None, tn, Fq), lambda i, j, k: (k, l, j, 0))
    return _matmul(name, "nt", (T // tm, D // tn, Q), [(dgate, aspec), (dup, aspec)], [(wg, wspec), (wu, wspec)], [],
                   [(S((T, D), f32), pl.BlockSpec((tm, tn), lambda i, j, k: (i, j)))],
                   [(0, 0, 0), (1, 1, 0)], (tm, tn), 1, _first, carry=carry)


def _rowwise(name, fn, rows, params, out_dtypes, tm):
    nr, npar = len(rows), len(params)
    T = rows[0].shape[0]
    shp = jax.eval_shape(fn, *[S((tm, r.shape[1]), f32) for r in rows], *[S(p.shape, f32) for p in params])

    def body(*refs):
        r = [x[...].astype(f32) for x in refs[:nr]]
        p = [x[...] for x in refs[nr:nr + npar]]
        for o_ref, o in zip(refs[nr + npar:], fn(*r, *p)):
            o_ref[...] = o.astype(o_ref.dtype)

    row_spec = lambda w: pl.BlockSpec((tm, w), lambda i: (i, 0))
    par_spec = lambda p: pl.BlockSpec(p.shape, lambda i: (0, 0))
    return _pcall(body, name=name, grid=(T // tm,),
                  in_specs=[row_spec(r.shape[1]) for r in rows] + [par_spec(p) for p in params],
                  out_specs=[row_spec(s.shape[1]) for s in shp],
                  out_shape=[S((T, s.shape[1]), dt) for s, dt in zip(shp, out_dtypes)],
                  compiler_params=_params(("parallel",)))(*rows, *params)


def _rowwise_vjp(name, fn, rows, params, cots, drow_dtypes, tm):
    nr, npar, nc = len(rows), len(params), len(cots)
    T = rows[0].shape[0]

    def body(*refs):
        r = [x[...].astype(f32) for x in refs[:nr]]
        p = [x[...] for x in refs[nr:nr + npar]]
        c = tuple(x[...].astype(f32) for x in refs[nr + npar:nr + npar + nc])
        o_refs = refs[nr + npar + nc:]
        _, vjp = jax.vjp(fn, *r, *p)
        grads = vjp(c)
        for o_ref, g in zip(o_refs[:nr], grads[:nr]):
            o_ref[...] = g.astype(o_ref.dtype)

        @pl.when(pl.program_id(0) == 0)
        def _():
            for o_ref in o_refs[nr:]:
                o_ref[...] = jnp.zeros_like(o_ref)

        for o_ref, g in zip(o_refs[nr:], grads[nr:]):
            o_ref[...] += g

    row_spec = lambda w: pl.BlockSpec((tm, w), lambda i: (i, 0))
    par_spec = lambda p: pl.BlockSpec(p.shape, lambda i: (0, 0))
    return _pcall(body, name=name, grid=(T // tm,),
                  in_specs=[row_spec(r.shape[1]) for r in rows] + [par_spec(p) for p in params] + [row_spec(c.shape[1]) for c in cots],
                  out_specs=[row_spec(r.shape[1]) for r in rows] + [par_spec(p) for p in params],
                  out_shape=[S(r.shape, dt) for r, dt in zip(rows, drow_dtypes)] + [S(p.shape, f32) for p in params],
                  compiler_params=_params(("arbitrary",)))(*rows, *params, *cots)


def _rms(h, g):
    return h * lax.rsqrt(jnp.mean(h * h, axis=-1, keepdims=True) + EPS) * g


def _rms_fn(h, g):
    return (_rms(h, g),)


def _rms_res_fn(h, g):
    return _rms(h, g), h


def _rms_res_bias_fn(h, g, b0):
    return _rms(h, g), h + b0


def _ln_silu_fn(c, g, b):
    mu = jnp.mean(c, axis=-1, keepdims=True)
    var = jnp.mean(jnp.square(c - mu), axis=-1, keepdims=True)
    y = (c - mu) * lax.rsqrt(var + EPS) * g + b
    return (y * jax.nn.sigmoid(y),)


def _glu_fn(av, ag, bv, bg):
    return ((av + bv) * jax.nn.sigmoid(ag + bg),)


def _loss_head(h, tgt, Lp, n_real, tm):
    T, D = h.shape

    def body(h_ref, t_ref, dy_ref, part_ref):
        i = pl.program_id(0)
        pos = (i * tm + lax.broadcasted_iota(jnp.int32, (tm, 1), 0)) % Lp
        real = (pos >= N_META_ROWS) & (pos < N_META_ROWS + n_real)
        err = jnp.where(real, h_ref[...] - t_ref[...], 0.0)
        dy_ref[...] = err * (1.0 / D)

        @pl.when(i == 0)
        def _():
            part_ref[...] = jnp.zeros_like(part_ref)

        part_ref[...] += jnp.sum(err * err, axis=0, keepdims=True)

    spec = pl.BlockSpec((tm, D), lambda i: (i, 0))
    return _pcall(body, name="loss_head", grid=(T // tm,), in_specs=[spec, spec],
                  out_specs=[spec, pl.BlockSpec((1, D), lambda i: (0, 0))],
                  out_shape=[S((T, D), f32), S((1, D), f32)], compiler_params=_params(("arbitrary",)))(h, tgt)


CONV_OFF = CONV_PAD - (CONV_TAPS - 1)
WIN_ROWS = ATT_BLOCK + CONV_PAD


def _phases(ph_ref, win):
    n = win.shape[0]
    for b in range(1, 8):
        ph_ref[b - 1, 0:n - 8, :] = win[b:n - 8 + b]

    def tap(o):
        a = (o // 8) * 8
        return win[a:a + ATT_BLOCK] if o % 8 == 0 else ph_ref[o % 8 - 1, a:a + ATT_BLOCK, :]

    return tap


def _phase_scratch(width):
    return pltpu.VMEM((7, WIN_ROWS - 8, width), f32)


def _dwconv_fwd(name, p, dw):
    B, Lp, D = p.shape
    dc = _tile(D, (256, 128))

    def body(p_ref, w_ref, o_ref, ph_ref):
        def tile(win, base):
            tap = _phases(ph_ref, win)
            acc = jnp.zeros((ATT_BLOCK, dc), f32)
            for k in range(CONV_TAPS):
                acc = acc + tap(CONV_OFF + k) * w_ref[k:k + 1, :]
            o_ref[pl.ds(base, ATT_BLOCK), :] = acc

        tile(jnp.concatenate([jnp.zeros((CONV_PAD, dc), f32), p_ref[0:ATT_BLOCK, :]], axis=0), 0)

        def step(r, carry):
            base = pl.multiple_of(r * ATT_BLOCK, ATT_BLOCK)
            tile(p_ref[pl.ds(pl.multiple_of(base - CONV_PAD, CONV_PAD), WIN_ROWS), :], base)
            return carry

        lax.fori_loop(1, Lp // ATT_BLOCK, step, 0)

    seq = pl.BlockSpec((None, Lp, dc), lambda b, j: (b, 0, j))
    return _pcall(body, name=name, grid=(B, D // dc), in_specs=[seq, pl.BlockSpec((CONV_PAD, dc), lambda b, j: (0, j))],
                  out_specs=seq, out_shape=S((B, Lp, D), f32), scratch_shapes=[_phase_scratch(dc)],
                  compiler_params=_params(("parallel", "parallel")))(p, dw)


def _dwconv_bwd(name, dcv, p, dw):
    B, Lp, D = p.shape
    dcw = _tile(D, (256, 128))
    nblk = Lp // ATT_BLOCK
    assert nblk >= 2
    zeros = lambda: jnp.zeros((CONV_PAD, dcw), f32)

    def body(dc_ref, p_ref, w_ref, dp_ref, ddw_ref, dph_ref, pph_ref):
        @pl.when(pl.program_id(1) == 0)
        def _():
            ddw_ref[...] = jnp.zeros_like(ddw_ref)

        def tile(dwin, pwin, base):
            dtap, ptap = _phases(dph_ref, dwin), _phases(pph_ref, pwin)
            dtile = dwin[0:ATT_BLOCK]
            acc = jnp.zeros((ATT_BLOCK, dcw), f32)
            for k in range(CONV_TAPS):
                acc = acc + dtap(CONV_TAPS - 1 - k) * w_ref[k:k + 1, :]
                ddw_ref[k:k + 1, :] += jnp.sum(dtile * ptap(CONV_OFF + k), axis=0, keepdims=True)
            dp_ref[pl.ds(base, ATT_BLOCK), :] = acc

        tile(dc_ref[0:WIN_ROWS, :], jnp.concatenate([zeros(), p_ref[0:ATT_BLOCK, :]], axis=0), 0)

        def step(r, carry):
            base = pl.multiple_of(r * ATT_BLOCK, ATT_BLOCK)
            tile(dc_ref[pl.ds(base, WIN_ROWS), :], p_ref[pl.ds(pl.multiple_of(base - CONV_PAD, CONV_PAD), WIN_ROWS), :], base)
            return carry

        lax.fori_loop(1, nblk - 1, step, 0)
        last = Lp - ATT_BLOCK
        tile(jnp.concatenate([dc_ref[last:Lp, :], zeros()], axis=0), p_ref[last - CONV_PAD:Lp, :], last)

    seq = pl.BlockSpec((None, Lp, dcw), lambda j, b: (b, 0, j))
    wsp = pl.BlockSpec((CONV_PAD, dcw), lambda j, b: (0, j))
    return _pcall(body, name=name, grid=(D // dcw, B), in_specs=[seq, seq, wsp], out_specs=[seq, wsp],
                  out_shape=[S((B, Lp, D), f32), S((CONV_PAD, D), f32)], scratch_shapes=[_phase_scratch(dcw)] * 2,
                  compiler_params=_params(("parallel", "arbitrary")))(dcv, p, dw)


def _band_start(n):
    return pl.multiple_of(jnp.maximum(n - 1, 0) * ATT_BLOCK, ATT_BLOCK)


def _attn_mask(n, R):
    shape = (R * ATT_BLOCK, 3 * ATT_BLOCK)
    qpos = n * ATT_BLOCK + (lax.broadcasted_iota(jnp.int32, shape, 0) & (ATT_BLOCK - 1))
    col = lax.broadcasted_iota(jnp.int32, shape, 1)
    meta_ok = (col < N_META_ROWS) & (col <= qpos)
    band_pos = _band_start(n) + (col - ATT_BLOCK)
    diff = qpos - band_pos
    band_ok = (col >= ATT_BLOCK) & (diff >= 0) & (diff < ATT_BLOCK) & (band_pos >= N_META_ROWS)
    return meta_ok | band_ok


def _row_head(R):
    row = lax.broadcasted_iota(jnp.int32, (R * ATT_BLOCK, R), 0)
    lo = lax.broadcasted_iota(jnp.int32, (R * ATT_BLOCK, R), 1) * ATT_BLOCK
    return ((row >= lo) & (row < lo + ATT_BLOCK)).astype(f32)


def _attn_core(q, kn, vv, qg, sink_vec, row_head, mask):
    qn = _rms(q, qg)
    s = lax.dot_general(qn.astype(bf16), kn.astype(bf16), _DN["nt"], preferred_element_type=f32) * (HEAD_DIM ** -0.5)
    s = jnp.where(mask, s, MASKED)
    sink = jnp.sum(row_head * sink_vec, axis=-1, keepdims=True)
    m = lax.stop_gradient(jnp.maximum(jnp.max(s, axis=-1, keepdims=True), sink))
    p = jnp.exp(s - m)
    denom = jnp.sum(p, axis=-1, keepdims=True) + jnp.exp(sink - m)
    return jnp.dot((p / denom).astype(bf16), vv.astype(bf16), preferred_element_type=f32)


def _keys_of(ref, n):
    return jnp.concatenate([ref[0:ATT_BLOCK, :], ref[pl.ds(_band_start(n), 2 * ATT_BLOCK), :]], axis=0)


def _stack_heads(x, R):
    return x if R == 1 else jnp.concatenate([x[:, r * HEAD_DIM:(r + 1) * HEAD_DIM] for r in range(R)], axis=0)


def _unstack_heads(x, R):
    return x if R == 1 else jnp.concatenate([x[r * ATT_BLOCK:(r + 1) * ATT_BLOCK] for r in range(R)], axis=1)


def _attn_specs(R, Lp):
    qspec = pl.BlockSpec((None, ATT_BLOCK, R * HEAD_DIM), lambda g, b, n: (b, n, g))
    kspec = pl.BlockSpec((None, None, Lp, HEAD_DIM), lambda g, b, n: (g, b, 0, 0))
    gspec = pl.BlockSpec((1, HEAD_DIM), lambda g, b, n: (0, 0))
    sspec = pl.BlockSpec((None, 1, R), lambda g, b, n: (g, 0, 0))
    return qspec, kspec, gspec, sspec


def _attn_fwd(name, q, kn, v, qg, sinks):
    B, Lp, D = q.shape
    G = kn.shape[0]
    R = D // (G * HEAD_DIM)
    qspec, kspec, gspec, sspec = _attn_specs(R, Lp)

    def body(q_ref, k_ref, v_ref, qg_ref, s_ref, o_ref):
        n = pl.program_id(2)
        o = _attn_core(_stack_heads(q_ref[...], R), _keys_of(k_ref, n), _keys_of(v_ref, n),
                       qg_ref[...], s_ref[...], _row_head(R), _attn_mask(n, R))
        o_ref[...] = _unstack_heads(o, R).astype(o_ref.dtype)

    return _pcall(body, name=name, grid=(G, B, Lp // ATT_BLOCK), in_specs=[qspec, kspec, kspec, gspec, sspec],
                  out_specs=qspec, out_shape=S(q.shape, bf16),
                  compiler_params=_params(("parallel", "parallel", "parallel")))(q, kn, v, qg, sinks)


def _attn_bwd(name, q, kn, v, qg, sinks, do):
    B, Lp, D = q.shape
    G = kn.shape[0]
    R = D // (G * HEAD_DIM)
    qspec, kspec, gspec, sspec = _attn_specs(R, Lp)

    def body(q_ref, k_ref, v_ref, qg_ref, s_ref, do_ref, dq_ref, dk_ref, dv_ref, dqg_ref, ds_ref):
        g, b, n = pl.program_id(0), pl.program_id(1), pl.program_id(2)

        @pl.when((g == 0) & (b == 0) & (n == 0))
        def _():
            dqg_ref[...] = jnp.zeros_like(dqg_ref)

        @pl.when((b == 0) & (n == 0))
        def _():
            ds_ref[...] = jnp.zeros_like(ds_ref)

        @pl.when(n == 0)
        def _():
            dk_ref[...] = jnp.zeros_like(dk_ref)
            dv_ref[...] = jnp.zeros_like(dv_ref)

        row_head, mask = _row_head(R), _attn_mask(n, R)
        _, vjp = jax.vjp(lambda q_, k_, v_, a_, s_: _attn_core(q_, k_, v_, a_, s_, row_head, mask),
                         _stack_heads(q_ref[...], R), _keys_of(k_ref, n).astype(f32),
                         _keys_of(v_ref, n).astype(f32), qg_ref[...], s_ref[...])
        dq, dkk, dvv, dqg, dsk = vjp(_stack_heads(do_ref[...].astype(f32), R))
        dq_ref[...] = _unstack_heads(dq, R)
        dqg_ref[...] += dqg
        ds_ref[...] += dsk
        band = pl.ds(_band_start(n), 2 * ATT_BLOCK)
        dk_ref[band, :] += dkk[ATT_BLOCK:]
        dv_ref[band, :] += dvv[ATT_BLOCK:]
        dk_ref[0:ATT_BLOCK, :] += dkk[:ATT_BLOCK]
        dv_ref[0:ATT_BLOCK, :] += dvv[:ATT_BLOCK]

    return _pcall(body, name=name, grid=(G, B, Lp // ATT_BLOCK),
                  in_specs=[qspec, kspec, kspec, gspec, sspec, qspec],
                  out_specs=[qspec, kspec, kspec, gspec, sspec],
                  out_shape=[S(q.shape, f32), S(kn.shape, f32), S(v.shape, f32), S(qg.shape, f32), S(sinks.shape, f32)],
                  compiler_params=_params(("arbitrary", "arbitrary", "arbitrary")))(q, kn, v, qg, sinks, do)


def _kv_heads(name, kv, kg, tm):
    T, W = kv.shape
    G = W // (2 * HEAD_DIM)

    def body(kv_ref, kg_ref, k_ref, v_ref):
        x = kv_ref[...]
        for g in range(G):
            k_ref[g] = _rms(x[:, g * HEAD_DIM:(g + 1) * HEAD_DIM], kg_ref[...]).astype(bf16)
            v_ref[g] = x[:, (G + g) * HEAD_DIM:(G + g + 1) * HEAD_DIM].astype(bf16)

    hspec = pl.BlockSpec((G, tm, HEAD_DIM), lambda i: (0, i, 0))
    return _pcall(body, name=name, grid=(T // tm,),
                  in_specs=[pl.BlockSpec((tm, W), lambda i: (i, 0)), pl.BlockSpec((1, HEAD_DIM), lambda i: (0, 0))],
                  out_specs=[hspec, hspec], out_shape=[S((G, T, HEAD_DIM), bf16)] * 2,
                  compiler_params=_params(("parallel",)))(kv, kg)


def _kv_heads_bwd(name, kv, kg, dkn, dv, tm):
    T, W = kv.shape
    G = W // (2 * HEAD_DIM)

    def body(kv_ref, kg_ref, dk_ref, dv_ref, o_ref, dkg_ref):
        @pl.when(pl.program_id(0) == 0)
        def _():
            dkg_ref[...] = jnp.zeros_like(dkg_ref)

        x = kv_ref[...]
        pieces = []
        for g in range(G):
            _, vjp = jax.vjp(_rms, x[:, g * HEAD_DIM:(g + 1) * HEAD_DIM], kg_ref[...])
            dk, dkg = vjp(dk_ref[g])
            pieces.append(dk)
            dkg_ref[...] += dkg
        o_ref[...] = jnp.concatenate(pieces + [dv_ref[g] for g in range(G)], axis=1)

    hspec = pl.BlockSpec((G, tm, HEAD_DIM), lambda i: (0, i, 0))
    gspec = pl.BlockSpec((1, HEAD_DIM), lambda i: (0, 0))
    return _pcall(body, name=name, grid=(T // tm,),
                  in_specs=[pl.BlockSpec((tm, W), lambda i: (i, 0)), gspec, hspec, hspec],
                  out_specs=[pl.BlockSpec((tm, W), lambda i: (i, 0)), gspec],
                  out_shape=[S((T, W), f32), S((1, HEAD_DIM), f32)], compiler_params=_params(("arbitrary",)))(kv, kg, dkn, dv)


def _place():
    x, y, c = lax.axis_index("x"), lax.axis_index("y"), lax.axis_index("c")
    chips = [(1 - x, y), (x, 1 - y), (1 - x, 1 - y)]
    return x, y, c, chips, [2 * cx + cy for cx, cy in chips]


def _remote(src, dst, send_sem, recv_sem, to):
    return pltpu.make_async_remote_copy(src_ref=src, dst_ref=dst, send_sem=send_sem, recv_sem=recv_sem,
                                        device_id=to, device_id_type=MESH)


def _into_slot(name, w, l, idx, slots, dtype):
    _, Rr, Cc = w.shape
    tr = _tile(Rr, (512, 256, 128, 64, 32, 16))

    def body(i_ref, x_ref, o_ref):
        o_ref[...] = x_ref[...].astype(o_ref.dtype)

    gs = pltpu.PrefetchScalarGridSpec(
        num_scalar_prefetch=1, grid=(Rr // tr,),
        in_specs=[pl.BlockSpec((None, tr, Cc), lambda i, i_ref: (l, i, 0))],
        out_specs=pl.BlockSpec((None, tr, Cc), lambda i, i_ref: (i_ref[0], i, 0)))
    return _pcall(body, name=name, grid_spec=gs, out_shape=S((slots, Rr, Cc), dtype), compiler_params=_params(("parallel",)))(idx, w)


_PLAN_COPIES = dict(ici=3, fwd=3, xchg=N_CHIPS, scat=3, share=1)


def _plan_copies(plan):
    return sum(_PLAN_COPIES[step[0]] for step in plan)


def _comm_copies(bufs, plan, send, recv):
    x, y, c, chips, qk = _place()
    me_q, sib = 2 * x + y, (x, y, 1 - c)
    starts, lands = [], []

    def add(src, dst, to, land):
        s = len(starts)
        starts.append(_remote(src, dst, send.at[s], recv.at[s], to))
        lands.append(_remote(land, land, send.at[s], recv.at[s], sib))

    for step in plan:
        kind = step[0]
        if kind == "ici":
            mine = bufs[step[1]].at[me_q, c]
            for k in range(3):
                add(mine, mine, (*chips[k], c), bufs[step[1]].at[qk[k], c])
        elif kind == "fwd":
            for k in range(3):
                got = bufs[step[1]].at[qk[k], c]
                add(got, got, sib, bufs[step[1]].at[qk[k], 1 - c])
        elif kind == "xchg":
            for q in range(N_CHIPS):
                add(bufs[step[1]].at[q, 1 - c], bufs[step[2]].at[q], sib, bufs[step[2]].at[q])
        elif kind == "scat":
            for k in range(3):
                add(bufs[step[1]].at[qk[k]], bufs[step[2]].at[k], (*chips[k], c), bufs[step[2]].at[k])
        else:
            mine = bufs[step[1]].at[step[2], c]
            add(mine, mine, sib, bufs[step[1]].at[step[2], 1 - c])
    return starts, lands


def _run_copies(bufs, plan, send, recv, start=True, wait=True):
    starts, lands = _comm_copies(bufs, plan, send, recv)
    if start:
        for cp in starts:
            cp.start()
    if wait:
        for cp in lands:
            cp.wait_recv()
        for cp in starts:
            cp.wait_send()


def _comm_call(name, slabs, phases):
    n = len(slabs)

    def body(*refs):
        bufs, sems = refs[n:2 * n], refs[2 * n:]
        for t, plan in enumerate(phases):
            _run_copies(bufs, plan, sems[2 * t], sems[2 * t + 1])

    sems = [pltpu.SemaphoreType.DMA((_plan_copies(plan),)) for plan in phases for _ in range(2)]
    return _pcall(body, name=name, in_specs=[ANY] * n, out_specs=[ANY] * n, out_shape=[S(s.shape, s.dtype) for s in slabs],
                  input_output_aliases={p: p for p in range(n)}, scratch_shapes=sems)(*slabs)


def _gather_all_devices(block):
    m_per, ncol = block.shape

    def body(x_ref, out_ref, send_sems, recv_sems, local_sem):
        x, y, c, chips, _ = _place()
        me, sib = (x, y, c), (x, y, 1 - c)

        def rows(px, py, pc):
            return out_ref.at[pl.ds((4 * px + 2 * py + pc) * m_per, m_per), :]

        def copy(k, blk, to, src=None):
            return _remote(rows(*blk) if src is None else src, rows(*blk), send_sems.at[k], recv_sems.at[k], to)

        mine = pltpu.make_async_copy(x_ref, rows(*me), local_sem)
        mine.start()
        first = [copy(0, me, sib, src=x_ref)] + [copy(1 + j, me, (*chip, c), src=x_ref) for j, chip in enumerate(chips)]
        for cp in first:
            cp.start()
        passed = [copy(4 + j, (*chip, c), sib) for j, chip in enumerate(chips)]
        for j, chip in enumerate(chips):
            copy(1 + j, (*chip, c), me).wait_recv()
            passed[j].start()
        copy(0, sib, me).wait_recv()
        for j, chip in enumerate(chips):
            copy(4 + j, (*chip, 1 - c), me).wait_recv()
        for cp in first + passed:
            cp.wait_send()
        mine.wait()

    vm = pl.BlockSpec(memory_space=pltpu.VMEM)
    return _pcall(body, name="gather_small_grads", in_specs=[vm], out_specs=vm,
                  out_shape=S((8 * m_per, ncol), block.dtype),
                  scratch_shapes=[pltpu.SemaphoreType.DMA((7,)), pltpu.SemaphoreType.DMA((7,)), pltpu.SemaphoreType.DMA],
                  compiler_params=pltpu.CompilerParams(vmem_limit_bytes=VMEM_LIMIT_BYTES))(block)


def _sum_pair(name, g, r1, c_idx):
    Q, _, Rr, Cc = g.shape
    tr = _tile(Rr, (512, 256, 128))

    def body(c_ref, g_ref, r_ref, o_ref):
        o_ref[...] = (g_ref[...].astype(f32) + r_ref[...].astype(f32)).astype(o_ref.dtype)

    spec = pl.BlockSpec((None, tr, Cc), lambda q, i, c_ref: (q, i, 0))
    gs = pltpu.PrefetchScalarGridSpec(
        num_scalar_prefetch=1, grid=(Q, Rr // tr),
        in_specs=[pl.BlockSpec((None, None, tr, Cc), lambda q, i, c_ref: (q, c_ref[0], i, 0)), spec], out_specs=spec)
    return _pcall(body, name=name, grid_spec=gs, out_shape=S((Q, Rr, Cc), bf16),
                  compiler_params=_params(("parallel", "parallel")))(c_idx, g, r1)


def _sum_owner(name, s, r2, qc_idx, l, nl, into):
    Q, Rr, Cc = s.shape
    tr = _tile(Rr, (512, 256, 128))

    def body(q_ref, s_ref, r_ref, *rest):
        o_ref = rest[-1]
        o_ref[...] = ((s_ref[...].astype(f32) + r_ref[0].astype(f32)) + r_ref[1].astype(f32)) + r_ref[2].astype(f32)

    in_specs = [pl.BlockSpec((None, tr, Cc), lambda i, q_ref: (q_ref[0], i, 0)), pl.BlockSpec((3, tr, Cc), lambda i, q_ref: (0, i, 0))]
    gs = pltpu.PrefetchScalarGridSpec(
        num_scalar_prefetch=1, grid=(Rr // tr,), in_specs=in_specs + ([] if into is None else [ANY]),
        out_specs=pl.BlockSpec((None, None, tr, Cc), lambda i, q_ref: (l, q_ref[1], i, 0)))
    return _pcall(body, name=name, grid_spec=gs, out_shape=S((nl, 2, Rr, Cc), f32),
                  input_output_aliases={} if into is None else {3: 0},
                  compiler_params=_params(("parallel",)))(qc_idx, s, r2, *([] if into is None else [into]))


def _sum_devices(stack):
    n, M, C = stack.shape

    def body(s_ref, o_ref):
        acc = s_ref[0]
        for d in range(1, n):
            acc = acc + s_ref[d]
        o_ref[...] = acc

    return _pcall(body, name="sum_small_grads", out_shape=S((M, C), f32), compiler_params=_params())(stack)


def _adamw(name, w, g, m, v):
    Rr, Cc = w.shape
    tr = _tile(Rr, (256, 128, 64, 32, 16, 8))

    def body(w_ref, g_ref, m_ref, v_ref, d_ref, mo_ref, vo_ref):
        g_ = g_ref[...]
        m_ = B1 * m_ref[...] + (1.0 - B1) * g_
        v_ = B2 * v_ref[...] + (1.0 - B2) * jnp.square(g_)
        m_hat = m_ / (1.0 - B1 ** STEP)
        v_hat = v_ / (1.0 - B2 ** STEP)
        d_ref[...] = -LR * (m_hat / (jnp.sqrt(v_hat) + ADAM_EPS) + WD * w_ref[...])
        mo_ref[...] = m_
        vo_ref[...] = v_

    spec = pl.BlockSpec((tr, Cc), lambda i: (i, 0))
    return _pcall(body, name=name, grid=(Rr // tr,), in_specs=[spec] * 4, out_specs=[spec] * 3,
                  out_shape=[S((Rr, Cc), f32)] * 3, compiler_params=_params(("parallel",)))(w, g, m, v)


def _pack(arrs, multiple):
    flat = jnp.concatenate([a.reshape(-1) for a in arrs])
    pad = (-flat.shape[0]) % multiple
    return jnp.pad(flat, (0, pad)).reshape(-1, 128)


def _unpack(slab, shapes):
    flat, out, o = slab.reshape(-1), [], 0
    for shp in shapes:
        n = 1
        for d in shp:
            n *= d
        out.append(flat[o:o + n].reshape(shp))
        o += n
    return out


def kernel(x, meta_tokens, norm_mix, norm_ffn, conv_w_in, conv_b_in, conv_dw, conv_ln_g, conv_ln_b, conv_w_out, conv_b_out, kv_norm, w_kv, k_norm, w_q, q_norm, attn_sinks, w_o, ffn_w_gate, ffn_w_up, ffn_w_down, loss_target, m_meta_tokens, m_norm_mix, m_norm_ffn, m_conv_w_in, m_conv_b_in, m_conv_dw, m_conv_ln_g, m_conv_ln_b, m_conv_w_out, m_conv_b_out, m_kv_norm, m_w_kv, m_k_norm, m_w_q, m_q_norm, m_attn_sinks, m_w_o, m_ffn_w_gate, m_ffn_w_up, m_ffn_w_down, v_meta_tokens, v_norm_mix, v_norm_ffn, v_conv_w_in, v_conv_b_in, v_conv_dw, v_conv_ln_g, v_conv_ln_b, v_conv_w_out, v_conv_b_out, v_kv_norm, v_w_kv, v_k_norm, v_w_q, v_q_norm, v_attn_sinks, v_w_o, v_ffn_w_gate, v_ffn_w_up, v_ffn_w_down):
    Q = N_CHIPS
    B, SEQ, D = x.shape
    L = N_META_ROWS + SEQ
    Lp = -(-L // ATT_BLOCK) * ATT_BLOCK
    T = B * Lp
    NA, NB = conv_w_in.shape[0], w_q.shape[0]
    NL = NA + NB
    Dq = D // Q
    G = N_KV
    R = D // (HEAD_DIM * G)
    KVW = w_kv.shape[1]
    assert NA % 2 == 0 and NB % 2 == 0 and NL % 2 == 0 and (D // Q) % 32 == 0
    tm = _tile(T, (1088, 544, 512, 256, 128))
    tk = _tile(T, (1088, 544, 512, 256, 128))
    tr = _tile(T, (272, 256, 128))
    my_c = lax.axis_index("c").astype(jnp.int32).reshape(1)
    my_q = (2 * lax.axis_index("x") + lax.axis_index("y")).astype(jnp.int32)
    my_qc = jnp.concatenate([my_q.reshape(1), my_c])

    small_shapes = [meta_tokens.shape, conv_b_in.shape, conv_dw.shape, conv_ln_g.shape, conv_ln_b.shape, conv_b_out.shape]
    small = _pack([meta_tokens, conv_b_in, conv_dw, conv_ln_g, conv_ln_b, conv_b_out], 2048)
    big = [conv_w_in, conv_w_out, w_kv, w_q, w_o, ffn_w_gate, ffn_w_up, ffn_w_down]
    keyed = dict(cin=conv_w_in, cout=conv_w_out, kv=w_kv[None], q=w_q, o=w_o, g=ffn_w_gate, u=ffn_w_up, d=ffn_w_down,
                 small=small[None])
    slabs, where = [], {}
    for key, w3 in keyed.items():
        for l in range(w3.shape[0]):
            s = _into_slot(f"own_{key}{l}", w3, l, my_qc, Q, f32 if key == "small" else bf16)
            where[key, l] = len(slabs)
            slabs.append(s.reshape(Q, 2, s.shape[1] // 2, s.shape[2]))

    def W(key, l):
        s = slabs[where[key, l]]
        return s.reshape(Q, 1, 2 * s.shape[2], s.shape[3])

    def layer_slabs(l):
        j = l - NA
        mix = [("cin", l), ("cout", l)] if l < NA else [("q", j), ("o", j)] + ([("kv", 0)] if j == 0 else [])
        return [where[k] for k in mix], [where["g", l], where["u", l]], [where["d", l]]

    def carry_of(ici, fwd):
        idxs = sorted(set(ici) | set(fwd))
        return idxs, ([slabs[i] for i in idxs], [("ici", idxs.index(i)) for i in ici] + [("fwd", idxs.index(i)) for i in fwd])

    def put_back(idxs, new):
        for i, s in zip(idxs, new):
            slabs[i] = s

    first = sum(layer_slabs(0), []) + [where["small", 0]]
    idxs, (sl, plan) = carry_of(first, first)
    put_back(idxs, _comm_call("gather_first", sl, [[s for s in plan if s[0] == "ici"], [s for s in plan if s[0] == "fwd"]]))
    parts = [_unpack(slabs[where["small", 0]][q], small_shapes) for q in range(Q)]
    meta_f, b_in_f, dw_f, ln_g_f, ln_b_f, b_out_f = [jnp.concatenate([parts[q][i] for q in range(Q)], axis=-1) for i in range(6)]
    dw_pad = jnp.pad(dw_f, ((0, 0), (0, CONV_PAD - CONV_TAPS), (0, 0)))

    h = jnp.concatenate([jnp.broadcast_to(meta_f[None], (B, N_META_ROWS, D)), x, jnp.zeros((B, Lp - L, D), f32)], axis=1).reshape(T, D)
    tgt = jnp.pad(loss_target, ((0, 0), (N_META_ROWS, Lp - L), (0, 0))).reshape(T, D)
    row = lambda a: a.reshape(1, -1)
    seqs = lambda a: a.reshape(B, Lp, a.shape[-1])
    by_seq = lambda a: a.reshape(G, B, Lp, HEAD_DIM)
    saved = []
    sinks3 = attn_sinks.reshape(NB, G, 1, R)
    for l in range(NL):
        st = {"h_a": h}
        u = _rowwise(f"rms_mix{l}", _rms_fn, [h], [row(norm_mix[l])], [bf16], tr)[0]
        st["u"] = u
        idxs, carry = carry_of([], layer_slabs(l)[1]) if l > 0 else ([], None)
        if l < NA:
            p, av, ag, *new = _glu_fwd(f"glu{l}", u, W("cin", l), 0, b_in_f[l:l + 1], tm, carry=carry)
            put_back(idxs, new)
            cv = _dwconv_fwd(f"dwconv{l}", p.reshape(B, Lp, D), dw_pad[l]).reshape(T, D)
            s = _rowwise(f"ln_silu{l}", _ln_silu_fn, [cv], [row(ln_g_f[l]), row(ln_b_f[l])], [bf16], tr)[0]
            h = _proj(f"conv_out{l}", s, W("cout", l), 0, tm, bias=row(b_out_f[l]), resid=h)
            st.update(av=av, ag=ag, p=p, cv=cv, s=s)
        else:
            j = l - NA
            if j == 0:
                kvn = _rowwise("rms_kv", _rms_fn, [h], [row(kv_norm)], [bf16], tr)[0]
                kv = _proj("kv_proj", kvn, W("kv", 0), 0, tm)
                kn, vh = _kv_heads("kv_heads", kv, row(k_norm), tk)
                kn, vh = by_seq(kn), by_seq(vh)
                st.update(kvn=kvn, kv=kv)
            res = _proj(f"q_proj{j}", u, W("q", j), 0, tm, carry=carry)
            q, new = res if carry is not None else (res, [])
            put_back(idxs, new)
            q = seqs(q)
            o = _attn_fwd(f"attn{j}", q, kn, vh, row(q_norm[j]), sinks3[j]).reshape(T, D)
            h = _proj(f"o_proj{j}", o, W("o", j), 0, tm, resid=h)
            st.update(q=q, o=o)
        st["h_b"] = h
        u2 = _rowwise(f"rms_ffn{l}", _rms_fn, [h], [row(norm_ffn[l])], [bf16], tr)[0]
        nxt = layer_slabs(l + 1) if l + 1 < NL else ([], [], [])
        idxs, carry = carry_of(nxt[0] + nxt[2], [])
        gate, up, hid, *new = _ffn_up(f"ffn_up{l}", u2, W("g", l), W("u", l), 0, tm, carry=carry if idxs else None)
        put_back(idxs, new)
        idxs, carry = carry_of(nxt[1], nxt[0] + nxt[2])
        h, *new = _ffn_down(f"ffn_down{l}", hid, W("d", l), 0, h, tm, carry=carry if idxs else None)
        put_back(idxs, new)
        st.update(u2=u2, gate=gate, up=up, hid=hid)
        saved.append(st)

    dh, part = _loss_head(h, tgt, Lp, SEQ, tr)
    loss = lax.psum(0.5 / D * jnp.sum(part), ("x", "y", "c"))

    grad_slab = {}
    n_layers = dict(cin=NA, cout=NA, kv=1, q=NB, o=NB, g=NL, u=NL, d=NL)

    def rs_begin(pieces):
        job = []
        for key, lay, g in pieces:
            g4 = g.reshape(Q, 2, g.shape[1] // 2, g.shape[2])
            job.append(dict(key=key, l=lay, g=g4, r1=lax.empty((Q,) + g4.shape[2:], bf16)))
        return job

    def xchg_carry(job):
        n = len(job)
        return [p["g"] for p in job] + [p["r1"] for p in job], [("xchg", t, n + t) for t in range(n)]

    def after_xchg(job, new):
        n = len(job)
        for t, p in enumerate(job):
            p["s"] = _sum_pair(f"sum_pair_{p['key']}{p['l']}", new[t], new[n + t], my_c)
            p["r2"] = lax.empty((3,) + p["s"].shape[1:], bf16)

    def scat_carry(job, keys):
        sel = [p for p in job if p["key"] in keys]
        n = len(sel)
        return sel, ([p["s"] for p in sel] + [p["r2"] for p in sel], [("scat", t, n + t) for t in range(n)])

    def after_scat(sel, new):
        n = len(sel)
        for t, p in enumerate(sel):
            key = p["key"]
            grad_slab[key] = _sum_owner(f"sum_owner_{key}{p['l']}", new[t], new[n + t], my_qc, p["l"], n_layers[key], grad_slab.get(key))

    def share_carry(job):
        return [grad_slab[p["key"]] for p in job], [("share", t, p["l"]) for t, p in enumerate(job)]

    def after_share(job, new):
        for p, s in zip(job, new):
            grad_slab[p["key"]] = s

    def carried(job, make):
        return make(job) if job else None

    MIXER = ("cin", "cout", "q", "o", "kv")
    g_mix, g_ffn = [None] * NL, [None] * NL
    g_bin, g_dw, g_lng, g_lnb, g_bout = ([None] * NA for _ in range(5))
    g_qn, g_sink = [None] * NB, [None] * NB
    dknp = dvp = None
    zero_row = jnp.zeros((1, D), f32)
    job = []
    for l in reversed(range(NL)):
        st = saved[l]
        pieces = []
        dgate, dup, *new = _ffn_dhid(f"ffn_dhid{l}", dh, W("d", l), 0, st["gate"], st["up"], tm, carry=carried(job, xchg_carry))
        if job:
            after_xchg(job, new)
        for nm, key, fn in (("dwd", "d", lambda c: _dw_rows(f"ffn_dwd{l}", st["hid"], dh, Q, tk, carry=c)),
                            ("dwg", "g", lambda c: _dw_cols(f"ffn_dwg{l}", st["u2"], dgate, Q, tk, Q, carry=c)),
                            ("dwu", "u", lambda c: _dw_cols(f"ffn_dwu{l}", st["u2"], dup, Q, tk, Q, carry=c))):
            sel, carry = scat_carry(job, (key,)) if job else ([], None)
            dw, *new = fn(carry)
            after_scat(sel, new)
            pieces.append((key, l, dw))
        sel, carry = scat_carry(job, MIXER) if job else ([], None)
        du2, *new = _ffn_du(f"ffn_du{l}", dgate, dup, W("g", l), W("u", l), 0, tm, carry=carry)
        after_scat(sel, new)
        dh, g_ffn[l] = _rowwise_vjp(f"rms_ffn_bwd{l}", _rms_res_fn, [st["h_b"]], [row(norm_ffn[l])], [du2, dh], [f32], tr)
        if l < NA:
            ds, *new = _proj_dx(f"conv_out_dx{l}", dh, W("cout", l), 0, tm, carry=carried(job, share_carry))
            after_share(job, new)
            pieces.append(("cout", l, _dw_rows(f"conv_out_dw{l}", st["s"], dh, Q, tk)[0]))
            dcv, g_lng[l], g_lnb[l] = _rowwise_vjp(f"ln_silu_bwd{l}", _ln_silu_fn, [st["cv"]], [row(ln_g_f[l]), row(ln_b_f[l])], [ds], [f32], tr)
            dp, g_dw[l] = _dwconv_bwd(f"dwconv_bwd{l}", dcv.reshape(B, Lp, D), st["p"].reshape(B, Lp, D), dw_pad[l])
            dav, dag, dbv, dbg = _rowwise_vjp(f"glu_bwd{l}", _glu_fn, [st["av"], st["ag"]], [zero_row, zero_row], [dp.reshape(T, D)], [bf16, bf16], tr)
            g_bin[l] = jnp.concatenate([dbv, dbg], axis=1)
            dwin = _dw_cols(f"glu_dwv{l}", st["u"], dav, Q // 2, tk, Q)[0]
            pieces.append(("cin", l, _dw_cols(f"glu_dwg{l}", st["u"], dag, Q // 2, tk, Q, q_off=Q // 2, into=dwin)[0]))
            du = _glu_du(f"glu_du{l}", dav, dag, W("cin", l), 0, tm)
            dh, g_mix[l], g_bout[l] = _rowwise_vjp(f"rms_mix_bwd{l}", _rms_res_bias_fn, [st["h_a"]], [row(norm_mix[l]), zero_row], [du, dh], [f32], tr)
        else:
            j = l - NA
            do, *new = _proj_dx(f"o_proj_dx{j}", dh, W("o", j), 0, tm, out_dtype=bf16, carry=carried(job, share_carry))
            after_share(job, new)
            pieces.append(("o", j, _dw_rows(f"o_proj_dw{j}", st["o"], dh, Q, tk)[0]))
            dq, dk1, dv1, g_qn[j], g_sink[j] = _attn_bwd(f"attn_bwd{j}", st["q"], kn, vh, row(q_norm[j]), sinks3[j], seqs(do))
            dknp, dvp = (dk1, dv1) if dknp is None else (dknp + dk1, dvp + dv1)
            dq = dq.reshape(T, D)
            pieces.append(("q", j, _dw_rows(f"q_proj_dw{j}", st["u"], dq, Q, tk)[0]))
            du = _proj_dx(f"q_proj_dx{j}", dq, W("q", j), 0, tm)[0]
            dh, g_mix[l] = _rowwise_vjp(f"rms_mix_bwd{l}", _rms_res_fn, [st["h_a"]], [row(norm_mix[l])], [du, dh], [f32], tr)
            if j == 0:
                dkv, g_kn = _kv_heads_bwd("kv_heads_bwd", st["kv"], row(k_norm), dknp.reshape(G, T, HEAD_DIM), dvp.reshape(G, T, HEAD_DIM), tk)
                pieces.append(("kv", 0, _dw_rows("kv_proj_dw", st["kvn"], dkv, Q, tk)[0]))
                dkvn = _proj_dx("kv_proj_dx", dkv, W("kv", 0), 0, tm)[0]
                dh, g_kvn = _rowwise_vjp("rms_kv_bwd", _rms_res_fn, [st["h_a"]], [row(kv_norm)], [dkvn, dh], [f32], tr)
        job = rs_begin(pieces)
    dh3 = dh.reshape(B, Lp, D)
    grad_x = dh3[:, N_META_ROWS:L]
    g_meta = jnp.sum(dh3[:, :N_META_ROWS], axis=0)

    sl, plan = xchg_carry(job)
    after_xchg(job, _comm_call("rs_exchange", sl, [plan]))
    sel, (sl, plan) = scat_carry(job, tuple(n_layers))
    after_scat(sel, _comm_call("rs_scatter", sl, [plan]))
    sl, plan = share_carry(job)
    after_share(job, _comm_call("rs_share", sl, [plan]))

    names = ["conv_w_in", "conv_w_out", "w_kv", "w_q", "w_o", "ffn_w_gate", "ffn_w_up", "ffn_w_down"]
    ws = dict(zip(names, big))
    ms = dict(zip(names, [m_conv_w_in, m_conv_w_out, m_w_kv, m_w_q, m_w_o, m_ffn_w_gate, m_ffn_w_up, m_ffn_w_down]))
    vs = dict(zip(names, [v_conv_w_in, v_conv_w_out, v_w_kv, v_w_q, v_w_o, v_ffn_w_gate, v_ffn_w_up, v_ffn_w_down]))
    out_g, out_d, out_m, out_v = {}, {}, {}, {}
    for nm, key in zip(names, ("cin", "cout", "kv", "q", "o", "g", "u", "d")):
        w, gsh = ws[nm], grad_slab[key]
        flat = lambda a: a.reshape(-1, w.shape[-1])
        g2 = flat(gsh)
        d2, m2, v2 = _adamw(f"adamw_{nm}", flat(w), g2, flat(ms[nm]), flat(vs[nm]))
        out_g[nm], out_d[nm], out_m[nm], out_v[nm] = (a.reshape(w.shape) for a in (g2, d2, m2, v2))

    small_names = ["norm_mix", "norm_ffn", "kv_norm", "k_norm", "q_norm", "attn_sinks", "meta_tokens", "conv_b_in", "conv_dw", "conv_ln_g", "conv_ln_b", "conv_b_out"]
    small_grads = [jnp.concatenate(g_mix, 0), jnp.concatenate(g_ffn, 0), g_kvn.reshape(-1), g_kn.reshape(-1), jnp.concatenate(g_qn, 0),
                   jnp.stack(g_sink).reshape(NB, G * R), g_meta, jnp.concatenate(g_bin, 0), jnp.stack(g_dw)[:, :CONV_TAPS],
                   jnp.concatenate(g_lng, 0), jnp.concatenate(g_lnb, 0), jnp.concatenate(g_bout, 0)]
    slab = _pack(small_grads, 1024)
    total = _sum_devices(_gather_all_devices(slab).reshape(8, slab.shape[0], 128))
    full_grads = _unpack(total, [g.shape for g in small_grads])
    small_w = dict(zip(small_names, [norm_mix, norm_ffn, kv_norm, k_norm, q_norm, attn_sinks, meta_tokens, conv_b_in, conv_dw, conv_ln_g, conv_ln_b, conv_b_out]))
    small_m = dict(zip(small_names, [m_norm_mix, m_norm_ffn, m_kv_norm, m_k_norm, m_q_norm, m_attn_sinks, m_meta_tokens, m_conv_b_in, m_conv_dw, m_conv_ln_g, m_conv_ln_b, m_conv_b_out]))
    small_v = dict(zip(small_names, [v_norm_mix, v_norm_ffn, v_kv_norm, v_k_norm, v_q_norm, v_attn_sinks, v_meta_tokens, v_conv_b_in, v_conv_dw, v_conv_ln_g, v_conv_ln_b, v_conv_b_out]))
    local_grads = []
    for nm, g in zip(small_names, full_grads):
        w = small_w[nm]
        if g.shape != w.shape:
            wq = w.shape[-1]
            g = lax.dynamic_slice_in_dim(g, my_q * wq, wq, axis=g.ndim - 1)
        local_grads.append(g)
    shapes = [small_w[nm].shape for nm in small_names]
    d_s, m_s, v_s = _adamw("adamw_small", _pack([small_w[nm] for nm in small_names], 1024), _pack(local_grads, 1024),
                           _pack([small_m[nm] for nm in small_names], 1024), _pack([small_v[nm] for nm in small_names], 1024))
    for nm, g, d_, m_, v_ in zip(small_names, local_grads, _unpack(d_s, shapes), _unpack(m_s, shapes), _unpack(v_s, shapes)):
        out_g[nm], out_d[nm], out_m[nm], out_v[nm] = g, d_, m_, v_

    order = ["meta_tokens", "norm_mix", "norm_ffn", "conv_w_in", "conv_b_in", "conv_dw", "conv_ln_g", "conv_ln_b", "conv_w_out", "conv_b_out",
             "kv_norm", "w_kv", "k_norm", "w_q", "q_norm", "attn_sinks", "w_o", "ffn_w_gate", "ffn_w_up", "ffn_w_down"]
    return (loss, grad_x, *[out_g[n] for n in order], *[out_d[n] for n in order], *[out_m[n] for n in order], *[out_v[n] for n in order])
```

```python
import functools

import jax
import jax.numpy as jnp
from jax import lax
from jax.experimental import pallas as pl
from jax.experimental.pallas import tpu as pltpu

f32, bf16 = jnp.float32, jnp.bfloat16

N_META_ROWS = 16
ATT_BLOCK = 128
HEAD_DIM = 64
N_KV = 4
CONV_TAPS = 31
CONV_PAD = 32
EPS = 1e-6
MASKED = -1e30
LR, B1, B2, ADAM_EPS, WD, STEP = 0.001, 0.9, 0.999, 1e-08, 0.01, 10
N_CHIPS = 4
VMEM_LIMIT_BYTES = 56 * 1024 * 1024
MESH = pl.DeviceIdType.MESH
ANY = pl.BlockSpec(memory_space=pl.ANY)
S = jax.ShapeDtypeStruct


def _pcall(body, **kw):
    return pl.pallas_call(body, **kw)


def _params(sem=None):
    return pltpu.CompilerParams(dimension_semantics=sem, vmem_limit_bytes=VMEM_LIMIT_BYTES)


def _tile(n, prefs):
    for p in prefs:
        if n % p == 0:
            return p
    return n


_DN = {"nn": (((1,), (0,)), ((), ())), "nt": (((1,), (1,)), ((), ())), "tn": (((0,), (0,)), ((), ()))}


def _matmul(name, mode, grid, a_ops, b_ops, x_ops, outs, terms, acc_shape, n_acc, epilogue, into=None, carry=None):
    na, nb, nx, no = len(a_ops), len(b_ops), len(x_ops), len(outs)
    nk = grid[2]
    slabs, plan = carry if carry is not None else ([], [])
    nc, ncp = len(slabs), _plan_copies(plan)
    n_in = na + nb + nx + (0 if into is None else 1) + nc

    def flat2d(v):
        return v.reshape(-1, v.shape[-1]) if v.ndim == 3 else v

    def dots(a_refs, b_refs):
        parts = [None] * n_acc
        for ai, bi, ci in terms:
            d = lax.dot_general(flat2d(a_refs[ai][...]).astype(bf16), flat2d(b_refs[bi][...]).astype(bf16), _DN[mode],
                                preferred_element_type=f32)
            parts[ci] = d if parts[ci] is None else parts[ci] + d
        return parts

    def finish(accs, x_refs, o_refs):
        res = epilogue(accs, [x[...] for x in x_refs])
        for o_ref, r in zip(o_refs, res):
            o_ref[...] = r.reshape(o_ref.shape).astype(o_ref.dtype)

    def compute(a_refs, b_refs, x_refs, o_refs, acc_refs):
        if nk == 1:
            finish(dots(a_refs, b_refs), x_refs, o_refs)
            return
        k = pl.program_id(2)

        @pl.when(k == 0)
        def _():
            for acc in acc_refs:
                acc[...] = jnp.zeros_like(acc)

        for acc, d in zip(acc_refs, dots(a_refs, b_refs)):
            acc[...] += d

        @pl.when(k == nk - 1)
        def _():
            finish([acc[...] for acc in acc_refs], x_refs, o_refs)

    def body(*refs):
        a_refs, b_refs = refs[:na], refs[na:na + nb]
        x_refs = refs[na + nb:na + nb + nx]
        o_refs = refs[n_in:n_in + no]
        bufs = refs[n_in + no:n_in + no + nc]
        scratch = refs[n_in + no + nc:]
        if not nc:
            compute(a_refs, b_refs, x_refs, o_refs, scratch)
            return
        acc_refs, (send, recv) = scratch[:-2], scratch[-2:]
        i, j, k = pl.program_id(0), pl.program_id(1), pl.program_id(2)

        @pl.when((i == 0) & (j == 0) & (k == 0))
        def _():
            _run_copies(bufs, plan, send, recv, start=True, wait=False)

        compute(a_refs, b_refs, x_refs, o_refs, acc_refs)

        @pl.when((i == grid[0] - 1) & (j == grid[1] - 1) & (k == nk - 1))
        def _():
            _run_copies(bufs, plan, send, recv, start=False, wait=True)

    ops = list(a_ops) + list(b_ops) + list(x_ops)
    aliases = {}
    if into is not None:
        ops.append((into, ANY))
        aliases = {len(ops) - 1: 0}
    for t, s in enumerate(slabs):
        ops.append((s, ANY))
        aliases[len(ops) - 1] = no + t
    outs = list(outs) + [(S(s.shape, s.dtype), ANY) for s in slabs]
    scratch = [pltpu.VMEM(acc_shape, f32)] * (n_acc if nk > 1 else 0)
    if nc:
        scratch += [pltpu.SemaphoreType.DMA((ncp,)), pltpu.SemaphoreType.DMA((ncp,))]
    sem = ("arbitrary",) * 3 if nc else ("parallel", "parallel", "arbitrary")
    res = _pcall(body, name=name, grid=grid, in_specs=[s for _, s in ops], out_specs=[s for _, s in outs],
                 out_shape=[s for s, _ in outs], scratch_shapes=scratch, input_output_aliases=aliases,
                 compiler_params=_params(sem))(*[a for a, _ in ops])
    return res


def _first(accs, xs):
    return (accs[0],)


def _proj(name, a, w, l, tm, bias=None, resid=None, out_dtype=f32, carry=None):
    T = a.shape[0]
    Q, _, Kq, N = w.shape
    tn = _tile(N, (512, 256, 128))
    x_ops, epi = [], _first
    if bias is not None:
        x_ops = [(bias, pl.BlockSpec((1, tn), lambda i, j, k: (0, j))), (resid, pl.BlockSpec((tm, tn), lambda i, j, k: (i, j)))]
        epi = lambda accs, xs: (accs[0] + xs[0] + xs[1],)
    elif resid is not None:
        x_ops = [(resid, pl.BlockSpec((tm, tn), lambda i, j, k: (i, j)))]
        epi = lambda accs, xs: (accs[0] + xs[0],)
    res = _matmul(name, "nn", (T // tm, N // tn, 1),
                  [(a, pl.BlockSpec((tm, Q * Kq), lambda i, j, k: (i, 0)))],
                  [(w, pl.BlockSpec((Q, None, Kq, tn), lambda i, j, k: (0, l, 0, j)))],
                  x_ops, [(S((T, N), out_dtype), pl.BlockSpec((tm, tn), lambda i, j, k: (i, j)))],
                  [(0, 0, 0)], (tm, tn), 1, epi, carry=carry)
    return res[0] if carry is None else (res[0], res[1:])


def _proj_dx(name, dy, w, l, tm, out_dtype=f32, carry=None):
    T, N = dy.shape
    Q, _, Kq, _ = w.shape
    return _matmul(name, "nt", (T // tm, 1, 1),
                   [(dy, pl.BlockSpec((tm, N), lambda i, j, k: (i, 0)))],
                   [(w, pl.BlockSpec((Q, None, Kq, N), lambda i, j, k: (0, l, 0, 0)))],
                   [], [(S((T, Q * Kq), out_dtype), pl.BlockSpec((tm, Q * Kq), lambda i, j, k: (i, 0)))],
                   [(0, 0, 0)], (tm, Q * Kq), 1, _first, carry=carry)


def _dw_rows(name, a, dy, Q, tk, carry=None):
    T, N = dy.shape
    tn = _tile(N, (512, 256, 128))
    if a.ndim == 2:
        K = a.shape[1]
        a_op, ni, acc_rows = (a, pl.BlockSpec((tk, K), lambda i, j, k: (k, 0))), 1, K
        osh, ospec = S((Q, K // Q, N), bf16), pl.BlockSpec((Q, K // Q, tn), lambda i, j, k: (0, 0, j))
    else:
        Kq = a.shape[2]
        a_op, ni, acc_rows = (a, pl.BlockSpec((None, tk, Kq), lambda i, j, k: (i, k, 0))), Q, Kq
        osh, ospec = S((Q, Kq, N), bf16), pl.BlockSpec((None, Kq, tn), lambda i, j, k: (i, 0, j))
    return _matmul(name, "tn", (ni, N // tn, T // tk), [a_op], [(dy, pl.BlockSpec((tk, tn), lambda i, j, k: (k, j)))],
                   [], [(osh, ospec)], [(0, 0, 0)], (acc_rows, tn), 1, _first, carry=carry)


def _dw_cols(name, a, dyc, Qc, tk, Q, q_off=0, into=None, carry=None):
    T, K = a.shape
    tkin = _tile(K, (512, 256, 128))
    if dyc.ndim == 3:
        Nq = dyc.shape[2]
        b_op = (dyc, pl.BlockSpec((None, tk, Nq), lambda i, j, k: (j, k, 0)))
    else:
        Nq = dyc.shape[1] // Qc
        b_op = (dyc, pl.BlockSpec((tk, Nq), lambda i, j, k: (k, j)))
    ospec = pl.BlockSpec((None, tkin, Nq), lambda i, j, k: (j + q_off, i, 0))
    return _matmul(name, "tn", (K // tkin, Qc, T // tk),
                   [(a, pl.BlockSpec((tk, tkin), lambda i, j, k: (k, i)))], [b_op], [],
                   [(S((Q, K, Nq), bf16), ospec)], [(0, 0, 0)], (tkin, Nq), 1, _first, into=into, carry=carry)


def _glu_fwd(name, u, w, l, b_in, tm, carry=None):
    T, D = u.shape
    Q, _, _, Cq = w.shape
    H = Q // 2

    def epi(accs, xs):
        av, ag = accs[0] + xs[0], accs[1] + xs[1]
        return av * jax.nn.sigmoid(ag), av, ag

    wspec = lambda off: pl.BlockSpec((None, None, D, Cq), lambda i, j, k: (j + off, l, 0, 0))
    bspec = lambda off: pl.BlockSpec((1, Cq), lambda i, j, k: (0, j + off))
    ospec = pl.BlockSpec((tm, Cq), lambda i, j, k: (i, j))
    return _matmul(name, "nn", (T // tm, H, 1),
                   [(u, pl.BlockSpec((tm, D), lambda i, j, k: (i, 0)))],
                   [(w, wspec(0)), (w, wspec(H))], [(b_in, bspec(0)), (b_in, bspec(H))],
                   [(S((T, H * Cq), f32), ospec), (S((T, H * Cq), bf16), ospec), (S((T, H * Cq), bf16), ospec)],
                   [(0, 0, 0), (0, 1, 1)], (tm, Cq), 2, epi, carry=carry)


def _glu_du(name, dav, dag, w, l, tm):
    T = dav.shape[0]
    Q, _, D, Cq = w.shape
    H = Q // 2
    tn = _tile(D, (512, 256, 128))
    aspec = pl.BlockSpec((tm, Cq), lambda i, j, k: (i, k))
    wspec = lambda off: pl.BlockSpec((None, None, tn, Cq), lambda i, j, k: (k + off, l, j, 0))
    return _matmul(name, "nt", (T // tm, D // tn, H), [(dav, aspec), (dag, aspec)],
                   [(w, wspec(0)), (w, wspec(H))], [],
                   [(S((T, D), f32), pl.BlockSpec((tm, tn), lambda i, j, k: (i, j)))],
                   [(0, 0, 0), (1, 1, 0)], (tm, tn), 1, _first)[0]


def _ffn_up(name, u, wg, wu, l, tm, carry=None):
    T, D = u.shape
    Q, _, _, Fq = wg.shape

    def epi(accs, xs):
        g, up = accs
        return g, up, g * jax.nn.sigmoid(g) * up

    wspec = pl.BlockSpec((None, None, D, Fq), lambda i, j, k: (j, l, 0, 0))
    ospec = pl.BlockSpec((None, tm, Fq), lambda i, j, k: (j, i, 0))
    osh = S((Q, T, Fq), bf16)
    return _matmul(name, "nn", (T // tm, Q, 1), [(u, pl.BlockSpec((tm, D), lambda i, j, k: (i, 0)))],
                   [(wg, wspec), (wu, wspec)], [], [(osh, ospec)] * 3, [(0, 0, 0), (0, 1, 1)], (tm, Fq), 2, epi, carry=carry)


def _ffn_down(name, hid, wd, l, resid, tm, carry=None):
    Q, T, Fq = hid.shape
    D = wd.shape[3]
    tn = _tile(D, (512, 256, 128))
    return _matmul(name, "nn", (T // tm, D // tn, Q),
                   [(hid, pl.BlockSpec((None, tm, Fq), lambda i, j, k: (k, i, 0)))],
                   [(wd, pl.BlockSpec((None, None, Fq, tn), lambda i, j, k: (k, l, 0, j)))],
                   [(resid, pl.BlockSpec((tm, tn), lambda i, j, k: (i, j)))],
                   [(S((T, D), f32), pl.BlockSpec((tm, tn), lambda i, j, k: (i, j)))],
                   [(0, 0, 0)], (tm, tn), 1, lambda accs, xs: (accs[0] + xs[0],), carry=carry)


def _ffn_dhid(name, dy, wd, l, gate, up, tm, carry=None):
    T, D = dy.shape
    Q, _, Fq, _ = wd.shape

    def epi(accs, xs):
        dh, g, up = accs[0], xs[0].astype(f32), xs[1].astype(f32)
        sg = jax.nn.sigmoid(g)
        return dh * up * (sg * (1.0 + g * (1.0 - sg))), dh * (g * sg)

    cspec = pl.BlockSpec((None, tm, Fq), lambda i, j, k: (j, i, 0))
    osh = S((Q, T, Fq), bf16)
    return _matmul(name, "nt", (T // tm, Q, 1), [(dy, pl.BlockSpec((tm, D), lambda i, j, k: (i, 0)))],
                   [(wd, pl.BlockSpec((None, None, Fq, D), lambda i, j, k: (j, l, 0, 0)))],
                   [(gate, cspec), (up, cspec)], [(osh, cspec)] * 2, [(0, 0, 0)], (tm, Fq), 1, epi, carry=carry)


def _ffn_du(name, dgate, dup, wg, wu, l, tm, carry=None):
    Q, T, Fq = dgate.shape
    D = wg.shape[2]
    tn = _tile(D, (512, 256, 128))
    aspec = pl.BlockSpec((None, tm, Fq), lambda i, j, k: (k, i, 0))
    wspec = pl.BlockSpec((None, None, tn, Fq), lambda i, j, k: (k, l, j, 0))
    return _matmul(name, "nt", (T // tm, D // tn, Q), [(dgate, aspec), (dup, aspec)], [(wg, wspec), (wu, wspec)], [],
                   [(S((T, D), f32), pl.BlockSpec((tm, tn), lambda i, j, k: (i, j)))],
                   [(0, 0, 0), (1, 1, 0)], (tm, tn), 1, _first, carry=carry)


def _rowwise(name, fn, rows, params, out_dtypes, tm):
    nr, npar = len(rows), len(params)
    T = rows[0].shape[0]
    shp = jax.eval_shape(fn, *[S((tm, r.shape[1]), f32) for r in rows], *[S(p.shape, f32) for p in params])

    def body(*refs):
        r = [x[...].astype(f32) for x in refs[:nr]]
        p = [x[...] for x in refs[nr:nr + npar]]
        for o_ref, o in zip(refs[nr + npar:], fn(*r, *p)):
            o_ref[...] = o.astype(o_ref.dtype)

    row_spec = lambda w: pl.BlockSpec((tm, w), lambda i: (i, 0))
    par_spec = lambda p: pl.BlockSpec(p.shape, lambda i: (0, 0))
    return _pcall(body, name=name, grid=(T // tm,),
                  in_specs=[row_spec(r.shape[1]) for r in rows] + [par_spec(p) for p in params],
                  out_specs=[row_spec(s.shape[1]) for s in shp],
                  out_shape=[S((T, s.shape[1]), dt) for s, dt in zip(shp, out_dtypes)],
                  compiler_params=_params(("parallel",)))(*rows, *params)


def _rowwise_vjp(name, fn, rows, params, cots, drow_dtypes, tm):
    nr, npar, nc = len(rows), len(params), len(cots)
    T = rows[0].shape[0]

    def body(*refs):
        r = [x[...].astype(f32) for x in refs[:nr]]
        p = [x[...] for x in refs[nr:nr + npar]]
        c = tuple(x[...].astype(f32) for x in refs[nr + npar:nr + npar + nc])
        o_refs = refs[nr + npar + nc:]
        _, vjp = jax.vjp(fn, *r, *p)
        grads = vjp(c)
        for o_ref, g in zip(o_refs[:nr], grads[:nr]):
            o_ref[...] = g.astype(o_ref.dtype)

        @pl.when(pl.program_id(0) == 0)
        def _():
            for o_ref in o_refs[nr:]:
                o_ref[...] = jnp.zeros_like(o_ref)

        for o_ref, g in zip(o_refs[nr:], grads[nr:]):
            o_ref[...] += g

    row_spec = lambda w: pl.BlockSpec((tm, w), lambda i: (i, 0))
    par_spec = lambda p: pl.BlockSpec(p.shape, lambda i: (0, 0))
    return _pcall(body, name=name, grid=(T // tm,),
                  in_specs=[row_spec(r.shape[1]) for r in rows] + [par_spec(p) for p in params] + [row_spec(c.shape[1]) for c in cots],
                  out_specs=[row_spec(r.shape[1]) for r in rows] + [par_spec(p) for p in params],
                  out_shape=[S(r.shape, dt) for r, dt in zip(rows, drow_dtypes)] + [S(p.shape, f32) for p in params],
                  compiler_params=_params(("arbitrary",)))(*rows, *params, *cots)


def _rms(h, g):
    return h * lax.rsqrt(jnp.mean(h * h, axis=-1, keepdims=True) + EPS) * g


def _rms_fn(h, g):
    return (_rms(h, g),)


def _rms_res_fn(h, g):
    return _rms(h, g), h


def _rms_res_bias_fn(h, g, b0):
    return _rms(h, g), h + b0


def _ln_silu_fn(c, g, b):
    mu = jnp.mean(c, axis=-1, keepdims=True)
    var = jnp.mean(jnp.square(c - mu), axis=-1, keepdims=True)
    y = (c - mu) * lax.rsqrt(var + EPS) * g + b
    return (y * jax.nn.sigmoid(y),)


def _glu_fn(av, ag, bv, bg):
    return ((av + bv) * jax.nn.sigmoid(ag + bg),)


def _loss_head(h, tgt, Lp, n_real, tm):
    T, D = h.shape

    def body(h_ref, t_ref, dy_ref, part_ref):
        i = pl.program_id(0)
        pos = (i * tm + lax.broadcasted_iota(jnp.int32, (tm, 1), 0)) % Lp
        real = (pos >= N_META_ROWS) & (pos < N_META_ROWS + n_real)
        err = jnp.where(real, h_ref[...] - t_ref[...], 0.0)
        dy_ref[...] = err * (1.0 / D)

        @pl.when(i == 0)
        def _():
            part_ref[...] = jnp.zeros_like(part_ref)

        part_ref[...] += jnp.sum(err * err, axis=0, keepdims=True)

    spec = pl.BlockSpec((tm, D), lambda i: (i, 0))
    return _pcall(body, name="loss_head", grid=(T // tm,), in_specs=[spec, spec],
                  out_specs=[spec, pl.BlockSpec((1, D), lambda i: (0, 0))],
                  out_shape=[S((T, D), f32), S((1, D), f32)], compiler_params=_params(("arbitrary",)))(h, tgt)


CONV_OFF = CONV_PAD - (CONV_TAPS - 1)
WIN_ROWS = ATT_BLOCK + CONV_PAD


def _phases(ph_ref, win):
    n = win.shape[0]
    for b in range(1, 8):
        ph_ref[b - 1, 0:n - 8, :] = win[b:n - 8 + b]

    def tap(o):
        a = (o // 8) * 8
        return win[a:a + ATT_BLOCK] if o % 8 == 0 else ph_ref[o % 8 - 1, a:a + ATT_BLOCK, :]

    return tap


def _phase_scratch(width):
    return pltpu.VMEM((7, WIN_ROWS - 8, width), f32)


def _dwconv_fwd(name, p, dw):
    B, Lp, D = p.shape
    dc = _tile(D, (256, 128))

    def body(p_ref, w_ref, o_ref, ph_ref):
        def tile(win, base):
            tap = _phases(ph_ref, win)
            acc = jnp.zeros((ATT_BLOCK, dc), f32)
            for k in range(CONV_TAPS):
                acc = acc + tap(CONV_OFF + k) * w_ref[k:k + 1, :]
            o_ref[pl.ds(base, ATT_BLOCK), :] = acc

        tile(jnp.concatenate([jnp.zeros((CONV_PAD, dc), f32), p_ref[0:ATT_BLOCK, :]], axis=0), 0)

        def step(r, carry):
            base = pl.multiple_of(r * ATT_BLOCK, ATT_BLOCK)
            tile(p_ref[pl.ds(pl.multiple_of(base - CONV_PAD, CONV_PAD), WIN_ROWS), :], base)
            return carry

        lax.fori_loop(1, Lp // ATT_BLOCK, step, 0)

    seq = pl.BlockSpec((None, Lp, dc), lambda b, j: (b, 0, j))
    return _pcall(body, name=name, grid=(B, D // dc), in_specs=[seq, pl.BlockSpec((CONV_PAD, dc), lambda b, j: (0, j))],
                  out_specs=seq, out_shape=S((B, Lp, D), f32), scratch_shapes=[_phase_scratch(dc)],
                  compiler_params=_params(("parallel", "parallel")))(p, dw)


def _dwconv_bwd(name, dcv, p, dw):
    B, Lp, D = p.shape
    dcw = _tile(D, (256, 128))
    nblk = Lp // ATT_BLOCK
    assert nblk >= 2
    zeros = lambda: jnp.zeros((CONV_PAD, dcw), f32)

    def body(dc_ref, p_ref, w_ref, dp_ref, ddw_ref, dph_ref, pph_ref, part_ref):
        @pl.when(pl.program_id(1) == 0)
        def _():
            ddw_ref[...] = jnp.zeros_like(ddw_ref)

        part_ref[...] = jnp.zeros_like(part_ref)

        def tile(dwin, pwin, base):
            dtap, ptap = _phases(dph_ref, dwin), _phases(pph_ref, pwin)
            dtile = dwin[0:ATT_BLOCK]
            acc = jnp.zeros((ATT_BLOCK, dcw), f32)
            for k in range(CONV_TAPS):
                acc = acc + dtap(CONV_TAPS - 1 - k) * w_ref[k:k + 1, :]
                part_ref[k] += jnp.sum((dtile * ptap(CONV_OFF + k)).reshape(ATT_BLOCK // 8, 8, dcw), axis=0)
            dp_ref[pl.ds(base, ATT_BLOCK), :] = acc

        tile(dc_ref[0:WIN_ROWS, :], jnp.concatenate([zeros(), p_ref[0:ATT_BLOCK, :]], axis=0), 0)

        def step(r, carry):
            base = pl.multiple_of(r * ATT_BLOCK, ATT_BLOCK)
            tile(dc_ref[pl.ds(base, WIN_ROWS), :], p_ref[pl.ds(pl.multiple_of(base - CONV_PAD, CONV_PAD), WIN_ROWS), :], base)
            return carry

        lax.fori_loop(1, nblk - 1, step, 0)
        last = Lp - ATT_BLOCK
        tile(jnp.concatenate([dc_ref[last:Lp, :], zeros()], axis=0), p_ref[last - CONV_PAD:Lp, :], last)
        ddw_ref[...] += jnp.sum(part_ref[...], axis=1)

    seq = pl.BlockSpec((None, Lp, dcw), lambda j, b: (b, 0, j))
    wsp = pl.BlockSpec((CONV_PAD, dcw), lambda j, b: (0, j))
    return _pcall(body, name=name, grid=(D // dcw, B), in_specs=[seq, seq, wsp], out_specs=[seq, wsp],
                  out_shape=[S((B, Lp, D), f32), S((CONV_PAD, D), f32)],
                  scratch_shapes=[_phase_scratch(dcw)] * 2 + [pltpu.VMEM((CONV_PAD, 8, dcw), f32)],
                  compiler_params=_params(("parallel", "arbitrary")))(dcv, p, dw)


def _band_start(n):
    return pl.multiple_of(jnp.maximum(n - 1, 0) * ATT_BLOCK, ATT_BLOCK)


def _attn_bias(R):
    row = jnp.arange(R * ATT_BLOCK)[:, None] % ATT_BLOCK
    col = jnp.arange(3 * ATT_BLOCK)[None, :]
    out = []
    for n in range(3):
        qpos = n * ATT_BLOCK + row
        meta_ok = (col < N_META_ROWS) & (col <= qpos)
        band_pos = max(n - 1, 0) * ATT_BLOCK + (col - ATT_BLOCK)
        diff = qpos - band_pos
        band_ok = (col >= ATT_BLOCK) & (diff >= 0) & (diff < ATT_BLOCK) & (band_pos >= N_META_ROWS)
        out.append(jnp.where(meta_ok | band_ok, 0.0, MASKED))
    return jnp.stack(out).astype(f32)


def _row_head(R):
    row = lax.broadcasted_iota(jnp.int32, (R * ATT_BLOCK, R), 0)
    lo = lax.broadcasted_iota(jnp.int32, (R * ATT_BLOCK, R), 1) * ATT_BLOCK
    return ((row >= lo) & (row < lo + ATT_BLOCK)).astype(f32)


def _attn_core(q, kn, vv, qg, sink_vec, row_head, bias):
    qn = _rms(q, qg)
    s = lax.dot_general(qn.astype(bf16), kn.astype(bf16), _DN["nt"], preferred_element_type=f32) * (HEAD_DIM ** -0.5)
    s = s + bias
    sink = jnp.sum(row_head * sink_vec, axis=-1, keepdims=True)
    m = lax.stop_gradient(jnp.maximum(jnp.max(s, axis=-1, keepdims=True), sink))
    p = jnp.exp(s - m)
    denom = jnp.sum(p, axis=-1, keepdims=True) + jnp.exp(sink - m)
    return jnp.dot((p / denom).astype(bf16), vv.astype(bf16), preferred_element_type=f32)


def _keys_of(ref, n):
    return jnp.concatenate([ref[0:ATT_BLOCK, :], ref[pl.ds(_band_start(n), 2 * ATT_BLOCK), :]], axis=0)


def _stack_heads(x, R):
    return x if R == 1 else jnp.concatenate([x[:, r * HEAD_DIM:(r + 1) * HEAD_DIM] for r in range(R)], axis=0)


def _unstack_heads(x, R):
    return x if R == 1 else jnp.concatenate([x[r * ATT_BLOCK:(r + 1) * ATT_BLOCK] for r in range(R)], axis=1)


def _attn_specs(R, Lp):
    qspec = pl.BlockSpec((None, ATT_BLOCK, R * HEAD_DIM), lambda g, b, n: (b, n, g))
    kspec = pl.BlockSpec((None, None, Lp, HEAD_DIM), lambda g, b, n: (g, b, 0, 0))
    gspec = pl.BlockSpec((1, HEAD_DIM), lambda g, b, n: (0, 0))
    sspec = pl.BlockSpec((None, 1, R), lambda g, b, n: (g, 0, 0))
    bspec = pl.BlockSpec((None, R * ATT_BLOCK, 3 * ATT_BLOCK), lambda g, b, n: (jnp.minimum(n, 2), 0, 0))
    return qspec, kspec, gspec, sspec, bspec


def _attn_fwd(name, q, kn, v, qg, sinks):
    B, Lp, D = q.shape
    G = kn.shape[0]
    R = D // (G * HEAD_DIM)
    qspec, kspec, gspec, sspec, bspec = _attn_specs(R, Lp)

    def body(q_ref, k_ref, v_ref, qg_ref, s_ref, b_ref, o_ref):
        n = pl.program_id(2)
        o = _attn_core(_stack_heads(q_ref[...], R), _keys_of(k_ref, n), _keys_of(v_ref, n),
                       qg_ref[...], s_ref[...], _row_head(R), b_ref[...])
        o_ref[...] = _unstack_heads(o, R).astype(o_ref.dtype)

    return _pcall(body, name=name, grid=(G, B, Lp // ATT_BLOCK), in_specs=[qspec, kspec, kspec, gspec, sspec, bspec],
                  out_specs=qspec, out_shape=S(q.shape, bf16),
                  compiler_params=_params(("parallel", "parallel", "parallel")))(q, kn, v, qg, sinks, _attn_bias(R))


def _attn_bwd(name, q, kn, v, qg, sinks, do):
    B, Lp, D = q.shape
    G = kn.shape[0]
    R = D // (G * HEAD_DIM)
    qspec, kspec, gspec, sspec, bspec = _attn_specs(R, Lp)

    def body(q_ref, k_ref, v_ref, qg_ref, s_ref, b_ref, do_ref, dq_ref, dk_ref, dv_ref, dqg_ref, ds_ref):
        g, b, n = pl.program_id(0), pl.program_id(1), pl.program_id(2)

        @pl.when((g == 0) & (b == 0) & (n == 0))
        def _():
            dqg_ref[...] = jnp.zeros_like(dqg_ref)

        @pl.when((b == 0) & (n == 0))
        def _():
            ds_ref[...] = jnp.zeros_like(ds_ref)

        @pl.when(n == 0)
        def _():
            dk_ref[...] = jnp.zeros_like(dk_ref)
            dv_ref[...] = jnp.zeros_like(dv_ref)

        row_head, bias = _row_head(R), b_ref[...]
        _, vjp = jax.vjp(lambda q_, k_, v_, a_, s_: _attn_core(q_, k_, v_, a_, s_, row_head, bias),
                         _stack_heads(q_ref[...], R), _keys_of(k_ref, n).astype(f32),
                         _keys_of(v_ref, n).astype(f32), qg_ref[...], s_ref[...])
        dq, dkk, dvv, dqg, dsk = vjp(_stack_heads(do_ref[...].astype(f32), R))
        dq_ref[...] = _unstack_heads(dq, R)
        dqg_ref[...] += dqg
        ds_ref[...] += dsk
        band = pl.ds(_band_start(n), 2 * ATT_BLOCK)
        dk_ref[band, :] += dkk[ATT_BLOCK:]
        dv_ref[band, :] += dvv[ATT_BLOCK:]
        dk_ref[0:ATT_BLOCK, :] += dkk[:ATT_BLOCK]
        dv_ref[0:ATT_BLOCK, :] += dvv[:ATT_BLOCK]

    return _pcall(body, name=name, grid=(G, B, Lp // ATT_BLOCK),
                  in_specs=[qspec, kspec, kspec, gspec, sspec, bspec, qspec],
                  out_specs=[qspec, kspec, kspec, gspec, sspec],
                  out_shape=[S(q.shape, f32), S(kn.shape, f32), S(v.shape, f32), S(qg.shape, f32), S(sinks.shape, f32)],
                  compiler_params=_params(("arbitrary", "arbitrary", "arbitrary")))(q, kn, v, qg, sinks, _attn_bias(R), do)


def _kv_heads(name, kv, kg, tm):
    T, W = kv.shape
    G = W // (2 * HEAD_DIM)

    def body(kv_ref, kg_ref, k_ref, v_ref):
        x = kv_ref[...]
        for g in range(G):
            k_ref[g] = _rms(x[:, g * HEAD_DIM:(g + 1) * HEAD_DIM], kg_ref[...]).astype(bf16)
            v_ref[g] = x[:, (G + g) * HEAD_DIM:(G + g + 1) * HEAD_DIM].astype(bf16)

    hspec = pl.BlockSpec((G, tm, HEAD_DIM), lambda i: (0, i, 0))
    return _pcall(body, name=name, grid=(T // tm,),
                  in_specs=[pl.BlockSpec((tm, W), lambda i: (i, 0)), pl.BlockSpec((1, HEAD_DIM), lambda i: (0, 0))],
                  out_specs=[hspec, hspec], out_shape=[S((G, T, HEAD_DIM), bf16)] * 2,
                  compiler_params=_params(("parallel",)))(kv, kg)


def _kv_heads_bwd(name, kv, kg, dkn, dv, tm):
    T, W = kv.shape
    G = W // (2 * HEAD_DIM)

    def body(kv_ref, kg_ref, dk_ref, dv_ref, o_ref, dkg_ref):
        @pl.when(pl.program_id(0) == 0)
        def _():
            dkg_ref[...] = jnp.zeros_like(dkg_ref)

        x = kv_ref[...]
        pieces = []
        for g in range(G):
            _, vjp = jax.vjp(_rms, x[:, g * HEAD_DIM:(g + 1) * HEAD_DIM], kg_ref[...])
            dk, dkg = vjp(dk_ref[g])
            pieces.append(dk)
            dkg_ref[...] += dkg
        o_ref[...] = jnp.concatenate(pieces + [dv_ref[g] for g in range(G)], axis=1)

    hspec = pl.BlockSpec((G, tm, HEAD_DIM), lambda i: (0, i, 0))
    gspec = pl.BlockSpec((1, HEAD_DIM), lambda i: (0, 0))
    return _pcall(body, name=name, grid=(T // tm,),
                  in_specs=[pl.BlockSpec((tm, W), lambda i: (i, 0)), gspec, hspec, hspec],
                  out_specs=[pl.BlockSpec((tm, W), lambda i: (i, 0)), gspec],
                  out_shape=[S((T, W), f32), S((1, HEAD_DIM), f32)], compiler_params=_params(("arbitrary",)))(kv, kg, dkn, dv)


def _place():
    x, y, c = lax.axis_index("x"), lax.axis_index("y"), lax.axis_index("c")
    chips = [(1 - x, y), (x, 1 - y), (1 - x, 1 - y)]
    return x, y, c, chips, [2 * cx + cy for cx, cy in chips]


def _remote(src, dst, send_sem, recv_sem, to):
    return pltpu.make_async_remote_copy(src_ref=src, dst_ref=dst, send_sem=send_sem, recv_sem=recv_sem,
                                        device_id=to, device_id_type=MESH)


def _into_slot(name, w, l, idx, slots, dtype):
    _, Rr, Cc = w.shape
    tr = _tile(Rr, (512, 256, 128, 64, 32, 16))

    def body(i_ref, x_ref, o_ref):
        o_ref[...] = x_ref[...].astype(o_ref.dtype)

    gs = pltpu.PrefetchScalarGridSpec(
        num_scalar_prefetch=1, grid=(Rr // tr,),
        in_specs=[pl.BlockSpec((None, tr, Cc), lambda i, i_ref: (l, i, 0))],
        out_specs=pl.BlockSpec((None, tr, Cc), lambda i, i_ref: (i_ref[0], i, 0)))
    return _pcall(body, name=name, grid_spec=gs, out_shape=S((slots, Rr, Cc), dtype), compiler_params=_params(("parallel",)))(idx, w)


_PLAN_COPIES = dict(ici=3, fwd=3, xchg=N_CHIPS, scat=3, share=1)


def _plan_copies(plan):
    return sum(_PLAN_COPIES[step[0]] for step in plan)


def _comm_copies(bufs, plan, send, recv):
    x, y, c, chips, qk = _place()
    me_q, sib = 2 * x + y, (x, y, 1 - c)
    starts, lands = [], []

    def add(src, dst, to, land):
        s = len(starts)
        starts.append(_remote(src, dst, send.at[s], recv.at[s], to))
        lands.append(_remote(land, land, send.at[s], recv.at[s], sib))

    for step in plan:
        kind = step[0]
        if kind == "ici":
            mine = bufs[step[1]].at[me_q, c]
            for k in range(3):
                add(mine, mine, (*chips[k], c), bufs[step[1]].at[qk[k], c])
        elif kind == "fwd":
            for k in range(3):
                got = bufs[step[1]].at[qk[k], c]
                add(got, got, sib, bufs[step[1]].at[qk[k], 1 - c])
        elif kind == "xchg":
            for q in range(N_CHIPS):
                add(bufs[step[1]].at[q, 1 - c], bufs[step[2]].at[q], sib, bufs[step[2]].at[q])
        elif kind == "scat":
            for k in range(3):
                add(bufs[step[1]].at[qk[k]], bufs[step[2]].at[k], (*chips[k], c), bufs[step[2]].at[k])
        else:
            mine = bufs[step[1]].at[step[2], c]
            add(mine, mine, sib, bufs[step[1]].at[step[2], 1 - c])
    return starts, lands


def _run_copies(bufs, plan, send, recv, start=True, wait=True):
    starts, lands = _comm_copies(bufs, plan, send, recv)
    if start:
        for cp in starts:
            cp.start()
    if wait:
        for cp in lands:
            cp.wait_recv()
        for cp in starts:
            cp.wait_send()


def _comm_call(name, slabs, phases):
    n = len(slabs)

    def body(*refs):
        bufs, sems = refs[n:2 * n], refs[2 * n:]
        for t, plan in enumerate(phases):
            _run_copies(bufs, plan, sems[2 * t], sems[2 * t + 1])

    sems = [pltpu.SemaphoreType.DMA((_plan_copies(plan),)) for plan in phases for _ in range(2)]
    return _pcall(body, name=name, in_specs=[ANY] * n, out_specs=[ANY] * n, out_shape=[S(s.shape, s.dtype) for s in slabs],
                  input_output_aliases={p: p for p in range(n)}, scratch_shapes=sems)(*slabs)


def _gather_all_devices(block):
    m_per, ncol = block.shape

    def body(x_ref, out_ref, send_sems, recv_sems, local_sem):
        x, y, c, chips, _ = _place()
        me, sib = (x, y, c), (x, y, 1 - c)

        def rows(px, py, pc):
            return out_ref.at[pl.ds((4 * px + 2 * py + pc) * m_per, m_per), :]

        def copy(k, blk, to, src=None):
            return _remote(rows(*blk) if src is None else src, rows(*blk), send_sems.at[k], recv_sems.at[k], to)

        mine = pltpu.make_async_copy(x_ref, rows(*me), local_sem)
        mine.start()
        first = [copy(0, me, sib, src=x_ref)] + [copy(1 + j, me, (*chip, c), src=x_ref) for j, chip in enumerate(chips)]
        for cp in first:
            cp.start()
        passed = [copy(4 + j, (*chip, c), sib) for j, chip in enumerate(chips)]
        for j, chip in enumerate(chips):
            copy(1 + j, (*chip, c), me).wait_recv()
            passed[j].start()
        copy(0, sib, me).wait_recv()
        for j, chip in enumerate(chips):
            copy(4 + j, (*chip, 1 - c), me).wait_recv()
        for cp in first + passed:
            cp.wait_send()
        mine.wait()

    vm = pl.BlockSpec(memory_space=pltpu.VMEM)
    return _pcall(body, name="gather_small_grads", in_specs=[vm], out_specs=vm,
                  out_shape=S((8 * m_per, ncol), block.dtype),
                  scratch_shapes=[pltpu.SemaphoreType.DMA((7,)), pltpu.SemaphoreType.DMA((7,)), pltpu.SemaphoreType.DMA],
                  compiler_params=pltpu.CompilerParams(vmem_limit_bytes=VMEM_LIMIT_BYTES))(block)


def _sum_pair(name, g, r1, c_idx):
    Q, _, Rr, Cc = g.shape
    tr = _tile(Rr, (512, 256, 128))

    def body(c_ref, g_ref, r_ref, o_ref):
        o_ref[...] = (g_ref[...].astype(f32) + r_ref[...].astype(f32)).astype(o_ref.dtype)

    spec = pl.BlockSpec((None, tr, Cc), lambda q, i, c_ref: (q, i, 0))
    gs = pltpu.PrefetchScalarGridSpec(
        num_scalar_prefetch=1, grid=(Q, Rr // tr),
        in_specs=[pl.BlockSpec((None, None, tr, Cc), lambda q, i, c_ref: (q, c_ref[0], i, 0)), spec], out_specs=spec)
    return _pcall(body, name=name, grid_spec=gs, out_shape=S((Q, Rr, Cc), bf16),
                  compiler_params=_params(("parallel", "parallel")))(c_idx, g, r1)


def _sum_owner(name, s, r2, qc_idx, l, nl, into):
    Q, Rr, Cc = s.shape
    tr = _tile(Rr, (512, 256, 128))

    def body(q_ref, s_ref, r_ref, *rest):
        o_ref = rest[-1]
        o_ref[...] = ((s_ref[...].astype(f32) + r_ref[0].astype(f32)) + r_ref[1].astype(f32)) + r_ref[2].astype(f32)

    in_specs = [pl.BlockSpec((None, tr, Cc), lambda i, q_ref: (q_ref[0], i, 0)), pl.BlockSpec((3, tr, Cc), lambda i, q_ref: (0, i, 0))]
    gs = pltpu.PrefetchScalarGridSpec(
        num_scalar_prefetch=1, grid=(Rr // tr,), in_specs=in_specs + ([] if into is None else [ANY]),
        out_specs=pl.BlockSpec((None, None, tr, Cc), lambda i, q_ref: (l, q_ref[1], i, 0)))
    return _pcall(body, name=name, grid_spec=gs, out_shape=S((nl, 2, Rr, Cc), f32),
                  input_output_aliases={} if into is None else {3: 0},
                  compiler_params=_params(("parallel",)))(qc_idx, s, r2, *([] if into is None else [into]))


def _sum_devices(stack):
    n, M, C = stack.shape

    def body(s_ref, o_ref):
        acc = s_ref[0]
        for d in range(1, n):
            acc = acc + s_ref[d]
        o_ref[...] = acc

    return _pcall(body, name="sum_small_grads", out_shape=S((M, C), f32), compiler_params=_params())(stack)


def _adamw(name, w, g, m, v):
    Rr, Cc = w.shape
    tr = _tile(Rr, (256, 128, 64, 32, 16, 8))

    def body(w_ref, g_ref, m_ref, v_ref, go_ref, d_ref, mo_ref, vo_ref):
        g_ = g_ref[...]
        go_ref[...] = g_
        m_ = B1 * m_ref[...] + (1.0 - B1) * g_
        v_ = B2 * v_ref[...] + (1.0 - B2) * jnp.square(g_)
        m_hat = m_ / (1.0 - B1 ** STEP)
        v_hat = v_ / (1.0 - B2 ** STEP)
        d_ref[...] = -LR * (m_hat / (jnp.sqrt(v_hat) + ADAM_EPS) + WD * w_ref[...])
        mo_ref[...] = m_
        vo_ref[...] = v_

    spec = pl.BlockSpec((tr, Cc), lambda i: (i, 0))
    return _pcall(body, name=name, grid=(Rr // tr,), in_specs=[spec] * 4, out_specs=[spec] * 4,
                  out_shape=[S((Rr, Cc), f32)] * 4, compiler_params=_params(("parallel",)))(w, g, m, v)


def _pack(arrs, multiple):
    flat = jnp.concatenate([a.reshape(-1) for a in arrs])
    pad = (-flat.shape[0]) % multiple
    return jnp.pad(flat, (0, pad)).reshape(-1, 128)


def _unpack(slab, shapes):
    flat, out, o = slab.reshape(-1), [], 0
    for shp in shapes:
        n = 1
        for d in shp:
            n *= d
        out.append(flat[o:o + n].reshape(shp))
        o += n
    return out


def kernel(x, meta_tokens, norm_mix, norm_ffn, conv_w_in, conv_b_in, conv_dw, conv_ln_g, conv_ln_b, conv_w_out, conv_b_out, kv_norm, w_kv, k_norm, w_q, q_norm, attn_sinks, w_o, ffn_w_gate, ffn_w_up, ffn_w_down, loss_target, m_meta_tokens, m_norm_mix, m_norm_ffn, m_conv_w_in, m_conv_b_in, m_conv_dw, m_conv_ln_g, m_conv_ln_b, m_conv_w_out, m_conv_b_out, m_kv_norm, m_w_kv, m_k_norm, m_w_q, m_q_norm, m_attn_sinks, m_w_o, m_ffn_w_gate, m_ffn_w_up, m_ffn_w_down, v_meta_tokens, v_norm_mix, v_norm_ffn, v_conv_w_in, v_conv_b_in, v_conv_dw, v_conv_ln_g, v_conv_ln_b, v_conv_w_out, v_conv_b_out, v_kv_norm, v_w_kv, v_k_norm, v_w_q, v_q_norm, v_attn_sinks, v_w_o, v_ffn_w_gate, v_ffn_w_up, v_ffn_w_down):
    Q = N_CHIPS
    B, SEQ, D = x.shape
    L = N_META_ROWS + SEQ
    Lp = -(-L // ATT_BLOCK) * ATT_BLOCK
    T = B * Lp
    NA, NB = conv_w_in.shape[0], w_q.shape[0]
    NL = NA + NB
    Dq = D // Q
    G = N_KV
    R = D // (HEAD_DIM * G)
    KVW = w_kv.shape[1]
    assert NA % 2 == 0 and NB % 2 == 0 and NL % 2 == 0 and (D // Q) % 32 == 0
    tm = _tile(T, (1088, 544, 512, 256, 128))
    tk = _tile(T, (1088, 544, 512, 256, 128))
    tr = _tile(T, (272, 256, 128))
    my_c = lax.axis_index("c").astype(jnp.int32).reshape(1)
    my_q = (2 * lax.axis_index("x") + lax.axis_index("y")).astype(jnp.int32)
    my_qc = jnp.concatenate([my_q.reshape(1), my_c])

    small_shapes = [meta_tokens.shape, conv_b_in.shape, conv_dw.shape, conv_ln_g.shape, conv_ln_b.shape, conv_b_out.shape]
    small = _pack([meta_tokens, conv_b_in, conv_dw, conv_ln_g, conv_ln_b, conv_b_out], 2048)
    big = [conv_w_in, conv_w_out, w_kv, w_q, w_o, ffn_w_gate, ffn_w_up, ffn_w_down]
    keyed = dict(cin=conv_w_in, cout=conv_w_out, kv=w_kv[None], q=w_q, o=w_o, g=ffn_w_gate, u=ffn_w_up, d=ffn_w_down,
                 small=small[None])
    slabs, where = [], {}
    for key, w3 in keyed.items():
        for l in range(w3.shape[0]):
            s = _into_slot(f"own_{key}{l}", w3, l, my_qc, Q, f32 if key == "small" else bf16)
            where[key, l] = len(slabs)
            slabs.append(s.reshape(Q, 2, s.shape[1] // 2, s.shape[2]))

    def W(key, l):
        s = slabs[where[key, l]]
        return s.reshape(Q, 1, 2 * s.shape[2], s.shape[3])

    def layer_slabs(l):
        j = l - NA
        mix = [("cin", l), ("cout", l)] if l < NA else [("q", j), ("o", j)] + ([("kv", 0)] if j == 0 else [])
        return [where[k] for k in mix], [where["g", l], where["u", l]], [where["d", l]]

    def carry_of(ici, fwd):
        idxs = sorted(set(ici) | set(fwd))
        return idxs, ([slabs[i] for i in idxs], [("ici", idxs.index(i)) for i in ici] + [("fwd", idxs.index(i)) for i in fwd])

    def put_back(idxs, new):
        for i, s in zip(idxs, new):
            slabs[i] = s

    first = sum(layer_slabs(0), []) + [where["small", 0]]
    idxs, (sl, plan) = carry_of(first, first)
    put_back(idxs, _comm_call("gather_first", sl, [[s for s in plan if s[0] == "ici"], [s for s in plan if s[0] == "fwd"]]))
    parts = [_unpack(slabs[where["small", 0]][q], small_shapes) for q in range(Q)]
    meta_f, b_in_f, dw_f, ln_g_f, ln_b_f, b_out_f = [jnp.concatenate([parts[q][i] for q in range(Q)], axis=-1) for i in range(6)]
    dw_pad = jnp.pad(dw_f, ((0, 0), (0, CONV_PAD - CONV_TAPS), (0, 0)))

    h = jnp.concatenate([jnp.broadcast_to(meta_f[None], (B, N_META_ROWS, D)), x, jnp.zeros((B, Lp - L, D), f32)], axis=1).reshape(T, D)
    tgt = jnp.pad(loss_target, ((0, 0), (N_META_ROWS, Lp - L), (0, 0))).reshape(T, D)
    row = lambda a: a.reshape(1, -1)
    seqs = lambda a: a.reshape(B, Lp, a.shape[-1])
    by_seq = lambda a: a.reshape(G, B, Lp, HEAD_DIM)
    saved = []
    sinks3 = attn_sinks.reshape(NB, G, 1, R)
    for l in range(NL):
        st = {"h_a": h}
        u = _rowwise(f"rms_mix{l}", _rms_fn, [h], [row(norm_mix[l])], [bf16], tr)[0]
        st["u"] = u
        idxs, carry = carry_of([], layer_slabs(l)[1]) if l > 0 else ([], None)
        if l < NA:
            p, av, ag, *new = _glu_fwd(f"glu{l}", u, W("cin", l), 0, b_in_f[l:l + 1], tm, carry=carry)
            put_back(idxs, new)
            cv = _dwconv_fwd(f"dwconv{l}", p.reshape(B, Lp, D), dw_pad[l]).reshape(T, D)
            s = _rowwise(f"ln_silu{l}", _ln_silu_fn, [cv], [row(ln_g_f[l]), row(ln_b_f[l])], [bf16], tr)[0]
            h = _proj(f"conv_out{l}", s, W("cout", l), 0, tm, bias=row(b_out_f[l]), resid=h)
            st.update(av=av, ag=ag, p=p, cv=cv, s=s)
        else:
            j = l - NA
            if j == 0:
                kvn = _rowwise("rms_kv", _rms_fn, [h], [row(kv_norm)], [bf16], tr)[0]
                kv = _proj("kv_proj", kvn, W("kv", 0), 0, tm)
                kn, vh = _kv_heads("kv_heads", kv, row(k_norm), tk)
                kn, vh = by_seq(kn), by_seq(vh)
                st.update(kvn=kvn, kv=kv)
            res = _proj(f"q_proj{j}", u, W("q", j), 0, tm, carry=carry)
            q, new = res if carry is not None else (res, [])
            put_back(idxs, new)
            q = seqs(q)
            o = _attn_fwd(f"attn{j}", q, kn, vh, row(q_norm[j]), sinks3[j]).reshape(T, D)
            h = _proj(f"o_proj{j}", o, W("o", j), 0, tm, resid=h)
            st.update(q=q, o=o)
        st["h_b"] = h
        u2 = _rowwise(f"rms_ffn{l}", _rms_fn, [h], [row(norm_ffn[l])], [bf16], tr)[0]
        nxt = layer_slabs(l + 1) if l + 1 < NL else ([], [], [])
        idxs, carry = carry_of(nxt[0] + nxt[2], [])
        gate, up, hid, *new = _ffn_up(f"ffn_up{l}", u2, W("g", l), W("u", l), 0, tm, carry=carry if idxs else None)
        put_back(idxs, new)
        idxs, carry = carry_of(nxt[1], nxt[0] + nxt[2])
        h, *new = _ffn_down(f"ffn_down{l}", hid, W("d", l), 0, h, tm, carry=carry if idxs else None)
        put_back(idxs, new)
        st.update(u2=u2, gate=gate, up=up, hid=hid)
        saved.append(st)

    dh, part = _loss_head(h, tgt, Lp, SEQ, tr)
    loss = lax.psum(0.5 / D * jnp.sum(part), ("x", "y", "c"))

    grad_slab = {}
    n_layers = dict(cin=NA, cout=NA, kv=1, q=NB, o=NB, g=NL, u=NL, d=NL)

    def rs_begin(pieces):
        job = []
        for key, lay, g in pieces:
            g4 = g.reshape(Q, 2, g.shape[1] // 2, g.shape[2])
            job.append(dict(key=key, l=lay, g=g4, r1=lax.empty((Q,) + g4.shape[2:], bf16)))
        return job

    def xchg_carry(job):
        n = len(job)
        return [p["g"] for p in job] + [p["r1"] for p in job], [("xchg", t, n + t) for t in range(n)]

    def after_xchg(job, new):
        n = len(job)
        for t, p in enumerate(job):
            p["s"] = _sum_pair(f"sum_pair_{p['key']}{p['l']}", new[t], new[n + t], my_c)
            p["r2"] = lax.empty((3,) + p["s"].shape[1:], bf16)

    def scat_carry(job, keys):
        sel = [p for p in job if p["key"] in keys]
        n = len(sel)
        return sel, ([p["s"] for p in sel] + [p["r2"] for p in sel], [("scat", t, n + t) for t in range(n)])

    def after_scat(sel, new):
        n = len(sel)
        for t, p in enumerate(sel):
            key = p["key"]
            grad_slab[key] = _sum_owner(f"sum_owner_{key}{p['l']}", new[t], new[n + t], my_qc, p["l"], n_layers[key], grad_slab.get(key))

    def share_carry(job):
        return [grad_slab[p["key"]] for p in job], [("share", t, p["l"]) for t, p in enumerate(job)]

    def after_share(job, new):
        for p, s in zip(job, new):
            grad_slab[p["key"]] = s

    def carried(job, make):
        return make(job) if job else None

    def merge(*carries):
        slabs_, plan_ = [], []
        for c in carries:
            if c is not None:
                off = len(slabs_)
                slabs_ += c[0]
                plan_ += [(s[0], s[1] + off, s[2]) if s[0] == "share" else (s[0],) + tuple(i + off for i in s[1:]) for s in c[1]]
        return (slabs_, plan_) if slabs_ else None

    MIXER = ("cin", "cout", "q", "o", "kv")
    g_mix, g_ffn = [None] * NL, [None] * NL
    g_bin, g_dw, g_lng, g_lnb, g_bout = ([None] * NA for _ in range(5))
    g_qn, g_sink = [None] * NB, [None] * NB
    dknp = dvp = None
    zero_row = jnp.zeros((1, D), f32)
    job = []
    early = []
    for l in reversed(range(NL)):
        st = saved[l]
        pieces = []
        dgate, dup, *new = _ffn_dhid(f"ffn_dhid{l}", dh, W("d", l), 0, st["gate"], st["up"], tm, carry=carried(job, xchg_carry))
        if job:
            after_xchg(job, new)
        for nm, key, fn in (("dwd", "d", lambda c: _dw_rows(f"ffn_dwd{l}", st["hid"], dh, Q, tk, carry=c)),
                            ("dwg", "g", lambda c: _dw_cols(f"ffn_dwg{l}", st["u2"], dgate, Q, tk, Q, carry=c)),
                            ("dwu", "u", lambda c: _dw_cols(f"ffn_dwu{l}", st["u2"], dup, Q, tk, Q, carry=c))):
            sel, carry = scat_carry(job, (key,)) if job else ([], None)
            dw, *new = fn(carry)
            after_scat(sel, new)
            pieces.append((key, l, dw))
        if l == 0 and l < NA:
            early, pieces = rs_begin(pieces), []
        sel, carry = scat_carry(job, MIXER) if job else ([], None)
        du2, *new = _ffn_du(f"ffn_du{l}", dgate, dup, W("g", l), W("u", l), 0, tm, carry=merge(carry, carried(early, xchg_carry)))
        after_scat(sel, new[:2 * len(sel)])
        if early:
            after_xchg(early, new[2 * len(sel):])
        dh, g_ffn[l] = _rowwise_vjp(f"rms_ffn_bwd{l}", _rms_res_fn, [st["h_b"]], [row(norm_ffn[l])], [du2, dh], [f32], tr)
        if l < NA:
            esel, ecarry = scat_carry(early, ("d",)) if early else ([], None)
            ds, *new = _proj_dx(f"conv_out_dx{l}", dh, W("cout", l), 0, tm, carry=merge(carried(job, share_carry), ecarry))
            after_share(job, new[:len(job)])
            after_scat(esel, new[len(job):])
            pieces.append(("cout", l, _dw_rows(f"conv_out_dw{l}", st["s"], dh, Q, tk)[0]))
            dcv, g_lng[l], g_lnb[l] = _rowwise_vjp(f"ln_silu_bwd{l}", _ln_silu_fn, [st["cv"]], [row(ln_g_f[l]), row(ln_b_f[l])], [ds], [f32], tr)
            dp, g_dw[l] = _dwconv_bwd(f"dwconv_bwd{l}", dcv.reshape(B, Lp, D), st["p"].reshape(B, Lp, D), dw_pad[l])
            dav, dag, dbv, dbg = _rowwise_vjp(f"glu_bwd{l}", _glu_fn, [st["av"], st["ag"]], [zero_row, zero_row], [dp.reshape(T, D)], [bf16, bf16], tr)
            g_bin[l] = jnp.concatenate([dbv, dbg], axis=1)
            esel, ecarry = scat_carry(early, ("g",)) if early else ([], None)
            dwin, *new = _dw_cols(f"glu_dwv{l}", st["u"], dav, Q // 2, tk, Q, carry=ecarry)
            after_scat(esel, new)
            esel, ecarry = scat_carry(early, ("u",)) if early else ([], None)
            dwin, *new = _dw_cols(f"glu_dwg{l}", st["u"], dag, Q // 2, tk, Q, q_off=Q // 2, into=dwin, carry=ecarry)
            after_scat(esel, new)
            pieces.append(("cin", l, dwin))
            du = _glu_du(f"glu_du{l}", dav, dag, W("cin", l), 0, tm)
            dh, g_mix[l], g_bout[l] = _rowwise_vjp(f"rms_mix_bwd{l}", _rms_res_bias_fn, [st["h_a"]], [row(norm_mix[l]), zero_row], [du, dh], [f32], tr)
        else:
            j = l - NA
            do, *new = _proj_dx(f"o_proj_dx{j}", dh, W("o", j), 0, tm, out_dtype=bf16, carry=carried(job, share_carry))
            after_share(job, new)
            pieces.append(("o", j, _dw_rows(f"o_proj_dw{j}", st["o"], dh, Q, tk)[0]))
            dq, dk1, dv1, g_qn[j], g_sink[j] = _attn_bwd(f"attn_bwd{j}", st["q"], kn, vh, row(q_norm[j]), sinks3[j], seqs(do))
            dknp, dvp = (dk1, dv1) if dknp is None else (dknp + dk1, dvp + dv1)
            dq = dq.reshape(T, D)
            pieces.append(("q", j, _dw_rows(f"q_proj_dw{j}", st["u"], dq, Q, tk)[0]))
            du = _proj_dx(f"q_proj_dx{j}", dq, W("q", j), 0, tm)[0]
            dh, g_mix[l] = _rowwise_vjp(f"rms_mix_bwd{l}", _rms_res_fn, [st["h_a"]], [row(norm_mix[l])], [du, dh], [f32], tr)
            if j == 0:
                dkv, g_kn = _kv_heads_bwd("kv_heads_bwd", st["kv"], row(k_norm), dknp.reshape(G, T, HEAD_DIM), dvp.reshape(G, T, HEAD_DIM), tk)
                pieces.append(("kv", 0, _dw_rows("kv_proj_dw", st["kvn"], dkv, Q, tk)[0]))
                dkvn = _proj_dx("kv_proj_dx", dkv, W("kv", 0), 0, tm)[0]
                dh, g_kvn = _rowwise_vjp("rms_kv_bwd", _rms_res_fn, [st["h_a"]], [row(kv_norm)], [dkvn, dh], [f32], tr)
        job = rs_begin(pieces)
    dh3 = dh.reshape(B, Lp, D)
    grad_x = dh3[:, N_META_ROWS:L]
    g_meta = jnp.sum(dh3[:, :N_META_ROWS], axis=0)

    sl, plan = xchg_carry(job)
    after_xchg(job, _comm_call("rs_exchange", sl, [plan]))
    sel, (sl, plan) = scat_carry(job, tuple(n_layers))
    after_scat(sel, _comm_call("rs_scatter", sl, [plan]))
    sl, plan = share_carry(job + early)
    after_share(job + early, _comm_call("rs_share", sl, [plan]))

    names = ["conv_w_in", "conv_w_out", "w_kv", "w_q", "w_o", "ffn_w_gate", "ffn_w_up", "ffn_w_down"]
    ws = dict(zip(names, big))
    ms = dict(zip(names, [m_conv_w_in, m_conv_w_out, m_w_kv, m_w_q, m_w_o, m_ffn_w_gate, m_ffn_w_up, m_ffn_w_down]))
    vs = dict(zip(names, [v_conv_w_in, v_conv_w_out, v_w_kv, v_w_q, v_w_o, v_ffn_w_gate, v_ffn_w_up, v_ffn_w_down]))
    out_g, out_d, out_m, out_v = {}, {}, {}, {}
    for nm, key in zip(names, ("cin", "cout", "kv", "q", "o", "g", "u", "d")):
        w, gsh = ws[nm], grad_slab[key]
        flat = lambda a: a.reshape(-1, w.shape[-1])
        g2, d2, m2, v2 = _adamw(f"adamw_{nm}", flat(w), flat(gsh), flat(ms[nm]), flat(vs[nm]))
        out_g[nm], out_d[nm], out_m[nm], out_v[nm] = (a.reshape(w.shape) for a in (g2, d2, m2, v2))

    small_names = ["norm_mix", "norm_ffn", "kv_norm", "k_norm", "q_norm", "attn_sinks", "meta_tokens", "conv_b_in", "conv_dw", "conv_ln_g", "conv_ln_b", "conv_b_out"]
    small_grads = [jnp.concatenate(g_mix, 0), jnp.concatenate(g_ffn, 0), g_kvn.reshape(-1), g_kn.reshape(-1), jnp.concatenate(g_qn, 0),
                   jnp.stack(g_sink).reshape(NB, G * R), g_meta, jnp.concatenate(g_bin, 0), jnp.stack(g_dw)[:, :CONV_TAPS],
                   jnp.concatenate(g_lng, 0), jnp.concatenate(g_lnb, 0), jnp.concatenate(g_bout, 0)]
    slab = _pack(small_grads, 1024)
    total = _sum_devices(_gather_all_devices(slab).reshape(8, slab.shape[0], 128))
    full_grads = _unpack(total, [g.shape for g in small_grads])
    small_w = dict(zip(small_names, [norm_mix, norm_ffn, kv_norm, k_norm, q_norm, attn_sinks, meta_tokens, conv_b_in, conv_dw, conv_ln_g, conv_ln_b, conv_b_out]))
    small_m = dict(zip(small_names, [m_norm_mix, m_norm_ffn, m_kv_norm, m_k_norm, m_q_norm, m_attn_sinks, m_meta_tokens, m_conv_b_in, m_conv_dw, m_conv_ln_g, m_conv_ln_b, m_conv_b_out]))
    small_v = dict(zip(small_names, [v_norm_mix, v_norm_ffn, v_kv_norm, v_k_norm, v_q_norm, v_attn_sinks, v_meta_tokens, v_conv_b_in, v_conv_dw, v_conv_ln_g, v_conv_ln_b, v_conv_b_out]))
    local_grads = []
    for nm, g in zip(small_names, full_grads):
        w = small_w[nm]
        if g.shape != w.shape:
            wq = w.shape[-1]
            g = lax.dynamic_slice_in_dim(g, my_q * wq, wq, axis=g.ndim - 1)
        local_grads.append(g)
    shapes = [small_w[nm].shape for nm in small_names]
    _, d_s, m_s, v_s = _adamw("adamw_small", _pack([small_w[nm] for nm in small_names], 1024), _pack(local_grads, 1024),
                           _pack([small_m[nm] for nm in small_names], 1024), _pack([small_v[nm] for nm in small_names], 1024))
    for nm, g, d_, m_, v_ in zip(small_names, local_grads, _unpack(d_s, shapes), _unpack(m_s, shapes), _unpack(v_s, shapes)):
        out_g[nm], out_d[nm], out_m[nm], out_v[nm] = g, d_, m_, v_

    order = ["meta_tokens", "norm_mix", "norm_ffn", "conv_w_in", "conv_b_in", "conv_dw", "conv_ln_g", "conv_ln_b", "conv_w_out", "conv_b_out",
             "kv_norm", "w_kv", "k_norm", "w_q", "q_norm", "attn_sinks", "w_o", "ffn_w_gate", "ffn_w_up", "ffn_w_down"]
    return (loss, grad_x, *[out_g[n] for n in order], *[out_d[n] for n in order], *[out_m[n] for n in order], *[out_v[n] for n in order])
```

```python
import functools

import jax
import jax.numpy as jnp
from jax import lax
from jax.experimental import pallas as pl
from jax.experimental.pallas import tpu as pltpu

f32, bf16 = jnp.float32, jnp.bfloat16

N_META_ROWS = 16
ATT_BLOCK = 128
HEAD_DIM = 64
N_KV = 4
CONV_TAPS = 31
CONV_PAD = 32
EPS = 1e-6
MASKED = -1e30
LR, B1, B2, ADAM_EPS, WD, STEP = 0.001, 0.9, 0.999, 1e-08, 0.01, 10
N_CHIPS = 4
VMEM_LIMIT_BYTES = 56 * 1024 * 1024
MESH = pl.DeviceIdType.MESH
ANY = pl.BlockSpec(memory_space=pl.ANY)
S = jax.ShapeDtypeStruct


def _pcall(body, **kw):
    return pl.pallas_call(body, **kw)


def _params(sem=None):
    return pltpu.CompilerParams(dimension_semantics=sem, vmem_limit_bytes=VMEM_LIMIT_BYTES)


def _tile(n, prefs):
    for p in prefs:
        if n % p == 0:
            return p
    return n


_DN = {"nn": (((1,), (0,)), ((), ())), "nt": (((1,), (1,)), ((), ())), "tn": (((0,), (0,)), ((), ()))}


def _matmul(name, mode, grid, a_ops, b_ops, x_ops, outs, terms, acc_shape, n_acc, epilogue, into=None, carry=None):
    na, nb, nx, no = len(a_ops), len(b_ops), len(x_ops), len(outs)
    nk = grid[2]
    slabs, plan = carry if carry is not None else ([], [])
    nc, ncp = len(slabs), _plan_copies(plan)
    n_in = na + nb + nx + (0 if into is None else 1) + nc

    def flat2d(v):
        return v.reshape(-1, v.shape[-1]) if v.ndim == 3 else v

    def dot(a, b):
        return lax.dot_general(a.astype(bf16), b.astype(bf16), _DN[mode], preferred_element_type=f32)

    def dots(a_refs, b_refs):
        parts = [None] * n_acc
        for ai, bi, ci in terms:
            a_ref, b_ref = a_refs[ai], b_refs[bi]
            if len(a_ref.shape) == 3 and len(b_ref.shape) == 3:
                for c in range(a_ref.shape[0]):
                    d = dot(a_ref[c], b_ref[c])
                    parts[ci] = d if parts[ci] is None else parts[ci] + d
            else:
                d = dot(flat2d(a_ref[...]), flat2d(b_ref[...]))
                parts[ci] = d if parts[ci] is None else parts[ci] + d
        return parts

    def finish(accs, x_refs, o_refs):
        res = epilogue(accs, [x[...] for x in x_refs])
        for o_ref, r in zip(o_refs, res):
            o_ref[...] = r.reshape(o_ref.shape).astype(o_ref.dtype)

    def compute(a_refs, b_refs, x_refs, o_refs, acc_refs):
        if nk == 1:
            finish(dots(a_refs, b_refs), x_refs, o_refs)
            return
        k = pl.program_id(2)

        @pl.when(k == 0)
        def _():
            for acc in acc_refs:
                acc[...] = jnp.zeros_like(acc)

        for acc, d in zip(acc_refs, dots(a_refs, b_refs)):
            acc[...] += d

        @pl.when(k == nk - 1)
        def _():
            finish([acc[...] for acc in acc_refs], x_refs, o_refs)

    def body(*refs):
        a_refs, b_refs = refs[:na], refs[na:na + nb]
        x_refs = refs[na + nb:na + nb + nx]
        o_refs = refs[n_in:n_in + no]
        bufs = refs[n_in + no:n_in + no + nc]
        scratch = refs[n_in + no + nc:]
        if not nc:
            compute(a_refs, b_refs, x_refs, o_refs, scratch)
            return
        acc_refs, (send, recv) = scratch[:-2], scratch[-2:]
        i, j, k = pl.program_id(0), pl.program_id(1), pl.program_id(2)

        @pl.when((i == 0) & (j == 0) & (k == 0))
        def _():
            _run_copies(bufs, plan, send, recv, start=True, wait=False)

        compute(a_refs, b_refs, x_refs, o_refs, acc_refs)

        @pl.when((i == grid[0] - 1) & (j == grid[1] - 1) & (k == nk - 1))
        def _():
            _run_copies(bufs, plan, send, recv, start=False, wait=True)

    ops = list(a_ops) + list(b_ops) + list(x_ops)
    aliases = {}
    if into is not None:
        ops.append((into, ANY))
        aliases = {len(ops) - 1: 0}
    for t, s in enumerate(slabs):
        ops.append((s, ANY))
        aliases[len(ops) - 1] = no + t
    outs = list(outs) + [(S(s.shape, s.dtype), ANY) for s in slabs]
    scratch = [pltpu.VMEM(acc_shape, f32)] * (n_acc if nk > 1 else 0)
    if nc:
        scratch += [pltpu.SemaphoreType.DMA((ncp,)), pltpu.SemaphoreType.DMA((ncp,))]
    sem = ("arbitrary",) * 3 if nc else ("parallel", "parallel", "arbitrary")
    res = _pcall(body, name=name, grid=grid, in_specs=[s for _, s in ops], out_specs=[s for _, s in outs],
                 out_shape=[s for s, _ in outs], scratch_shapes=scratch, input_output_aliases=aliases,
                 compiler_params=_params(sem))(*[a for a, _ in ops])
    return res


def _first(accs, xs):
    return (accs[0],)


def _proj(name, a, w, l, tm, bias=None, resid=None, out_dtype=f32, carry=None):
    T = a.shape[0]
    Q, _, Kq, N = w.shape
    tn = _tile(N, (512, 256, 128))
    x_ops, epi = [], _first
    if bias is not None:
        x_ops = [(bias, pl.BlockSpec((1, tn), lambda i, j, k: (0, j))), (resid, pl.BlockSpec((tm, tn), lambda i, j, k: (i, j)))]
        epi = lambda accs, xs: (accs[0] + xs[0] + xs[1],)
    elif resid is not None:
        x_ops = [(resid, pl.BlockSpec((tm, tn), lambda i, j, k: (i, j)))]
        epi = lambda accs, xs: (accs[0] + xs[0],)
    res = _matmul(name, "nn", (T // tm, N // tn, 1),
                  [(a, pl.BlockSpec((tm, Q * Kq), lambda i, j, k: (i, 0)))],
                  [(w, pl.BlockSpec((Q, None, Kq, tn), lambda i, j, k: (0, l, 0, j)))],
                  x_ops, [(S((T, N), out_dtype), pl.BlockSpec((tm, tn), lambda i, j, k: (i, j)))],
                  [(0, 0, 0)], (tm, tn), 1, epi, carry=carry)
    return res[0] if carry is None else (res[0], res[1:])


def _proj_dx(name, dy, w, l, tm, out_dtype=f32, carry=None):
    T, N = dy.shape
    Q, _, Kq, _ = w.shape
    return _matmul(name, "nt", (T // tm, 1, 1),
                   [(dy, pl.BlockSpec((tm, N), lambda i, j, k: (i, 0)))],
                   [(w, pl.BlockSpec((Q, None, Kq, N), lambda i, j, k: (0, l, 0, 0)))],
                   [], [(S((T, Q * Kq), out_dtype), pl.BlockSpec((tm, Q * Kq), lambda i, j, k: (i, 0)))],
                   [(0, 0, 0)], (tm, Q * Kq), 1, _first, carry=carry)


def _dw_rows(name, a, dy, Q, tk, carry=None):
    T, N = dy.shape
    tn = _tile(N, (512, 256, 128))
    if a.ndim == 2:
        K = a.shape[1]
        a_op, ni, acc_rows = (a, pl.BlockSpec((tk, K), lambda i, j, k: (k, 0))), 1, K
        osh, ospec = S((Q, K // Q, N), bf16), pl.BlockSpec((Q, K // Q, tn), lambda i, j, k: (0, 0, j))
    else:
        Kq = a.shape[2]
        a_op, ni, acc_rows = (a, pl.BlockSpec((None, tk, Kq), lambda i, j, k: (i, k, 0))), Q, Kq
        osh, ospec = S((Q, Kq, N), bf16), pl.BlockSpec((None, Kq, tn), lambda i, j, k: (i, 0, j))
    return _matmul(name, "tn", (ni, N // tn, T // tk), [a_op], [(dy, pl.BlockSpec((tk, tn), lambda i, j, k: (k, j)))],
                   [], [(osh, ospec)], [(0, 0, 0)], (acc_rows, tn), 1, _first, carry=carry)


def _dw_cols(name, a, dyc, Qc, tk, Q, q_off=0, into=None, carry=None):
    T, K = a.shape
    tkin = _tile(K, (512, 256, 128))
    if dyc.ndim == 3:
        Nq = dyc.shape[2]
        b_op = (dyc, pl.BlockSpec((None, tk, Nq), lambda i, j, k: (j, k, 0)))
    else:
        Nq = dyc.shape[1] // Qc
        b_op = (dyc, pl.BlockSpec((tk, Nq), lambda i, j, k: (k, j)))
    ospec = pl.BlockSpec((None, tkin, Nq), lambda i, j, k: (j + q_off, i, 0))
    return _matmul(name, "tn", (K // tkin, Qc, T // tk),
                   [(a, pl.BlockSpec((tk, tkin), lambda i, j, k: (k, i)))], [b_op], [],
                   [(S((Q, K, Nq), bf16), ospec)], [(0, 0, 0)], (tkin, Nq), 1, _first, into=into, carry=carry)


def _glu_fwd(name, u, w, l, b_in, tm, carry=None):
    T, D = u.shape
    Q, _, _, Cq = w.shape
    H = Q // 2

    def epi(accs, xs):
        av, ag = accs[0] + xs[0], accs[1] + xs[1]
        return av * jax.nn.sigmoid(ag), av, ag

    wspec = lambda off: pl.BlockSpec((None, None, D, Cq), lambda i, j, k: (j + off, l, 0, 0))
    bspec = lambda off: pl.BlockSpec((1, Cq), lambda i, j, k: (0, j + off))
    ospec = pl.BlockSpec((tm, Cq), lambda i, j, k: (i, j))
    return _matmul(name, "nn", (T // tm, H, 1),
                   [(u, pl.BlockSpec((tm, D), lambda i, j, k: (i, 0)))],
                   [(w, wspec(0)), (w, wspec(H))], [(b_in, bspec(0)), (b_in, bspec(H))],
                   [(S((T, H * Cq), f32), ospec), (S((T, H * Cq), bf16), ospec), (S((T, H * Cq), bf16), ospec)],
                   [(0, 0, 0), (0, 1, 1)], (tm, Cq), 2, epi, carry=carry)


def _glu_du(name, dav, dag, w, l, tm):
    T = dav.shape[0]
    Q, _, D, Cq = w.shape
    H = Q // 2
    tn = _tile(D, (512, 256, 128))
    aspec = pl.BlockSpec((tm, Cq), lambda i, j, k: (i, k))
    wspec = lambda off: pl.BlockSpec((None, None, tn, Cq), lambda i, j, k: (k + off, l, j, 0))
    return _matmul(name, "nt", (T // tm, D // tn, H), [(dav, aspec), (dag, aspec)],
                   [(w, wspec(0)), (w, wspec(H))], [],
                   [(S((T, D), f32), pl.BlockSpec((tm, tn), lambda i, j, k: (i, j)))],
                   [(0, 0, 0), (1, 1, 0)], (tm, tn), 1, _first)[0]


def _ffn_up(name, u, wg, wu, l, tm, carry=None):
    T, D = u.shape
    Q, _, _, Fq = wg.shape

    def epi(accs, xs):
        g, up = accs
        return g, up, g * jax.nn.sigmoid(g) * up

    wspec = pl.BlockSpec((None, None, D, Fq), lambda i, j, k: (j, l, 0, 0))
    ospec = pl.BlockSpec((None, tm, Fq), lambda i, j, k: (j, i, 0))
    osh = S((Q, T, Fq), bf16)
    return _matmul(name, "nn", (T // tm, Q, 1), [(u, pl.BlockSpec((tm, D), lambda i, j, k: (i, 0)))],
                   [(wg, wspec), (wu, wspec)], [], [(osh, ospec)] * 3, [(0, 0, 0), (0, 1, 1)], (tm, Fq), 2, epi, carry=carry)


def _ffn_down(name, hid, wd, l, resid, tm, carry=None):
    Q, T, Fq = hid.shape
    D = wd.shape[3]
    tn = _tile(D, (512, 256, 128))
    return _matmul(name, "nn", (T // tm, D // tn, 1),
                   [(hid, pl.BlockSpec((Q, tm, Fq), lambda i, j, k: (0, i, 0)))],
                   [(wd, pl.BlockSpec((Q, None, Fq, tn), lambda i, j, k: (0, l, 0, j)))],
                   [(resid, pl.BlockSpec((tm, tn), lambda i, j, k: (i, j)))],
                   [(S((T, D), f32), pl.BlockSpec((tm, tn), lambda i, j, k: (i, j)))],
                   [(0, 0, 0)], (tm, tn), 1, lambda accs, xs: (accs[0] + xs[0],), carry=carry)


def _ffn_dhid(name, dy, wd, l, gate, up, tm, carry=None):
    T, D = dy.shape
    Q, _, Fq, _ = wd.shape

    def epi(accs, xs):
        dh, g, up = accs[0], xs[0].astype(f32), xs[1].astype(f32)
        sg = jax.nn.sigmoid(g)
        return dh * up * (sg * (1.0 + g * (1.0 - sg))), dh * (g * sg)

    cspec = pl.BlockSpec((None, tm, Fq), lambda i, j, k: (j, i, 0))
    osh = S((Q, T, Fq), bf16)
    return _matmul(name, "nt", (T // tm, Q, 1), [(dy, pl.BlockSpec((tm, D), lambda i, j, k: (i, 0)))],
                   [(wd, pl.BlockSpec((None, None, Fq, D), lambda i, j, k: (j, l, 0, 0)))],
                   [(gate, cspec), (up, cspec)], [(osh, cspec)] * 2, [(0, 0, 0)], (tm, Fq), 1, epi, carry=carry)


def _ffn_du(name, dgate, dup, wg, wu, l, tm, carry=None):
    Q, T, Fq = dgate.shape
    D = wg.shape[2]
    tn = _tile(D, (512, 256, 128))
    tm = tm // 2 if tm % 32 == 0 and tm > 512 else tm
    aspec = pl.BlockSpec((Q, tm, Fq), lambda i, j, k: (0, i, 0))
    wspec = pl.BlockSpec((Q, None, tn, Fq), lambda i, j, k: (0, l, j, 0))
    return _matmul(name, "nt", (T // tm, D // tn, 1), [(dgate, aspec), (dup, aspec)], [(wg, wspec), (wu, wspec)], [],
                   [(S((T, D), f32), pl.BlockSpec((tm, tn), lambda i, j, k: (i, j)))],
                   [(0, 0, 0), (1, 1, 0)], (tm, tn), 1, _first, carry=carry)


def _rowwise(name, fn, rows, params, out_dtypes, tm):
    nr, npar = len(rows), len(params)
    T = rows[0].shape[0]
    shp = jax.eval_shape(fn, *[S((tm, r.shape[1]), f32) for r in rows], *[S(p.shape, f32) for p in params])

    def body(*refs):
        r = [x[...].astype(f32) for x in refs[:nr]]
        p = [x[...] for x in refs[nr:nr + npar]]
        for o_ref, o in zip(refs[nr + npar:], fn(*r, *p)):
            o_ref[...] = o.astype(o_ref.dtype)

    row_spec = lambda w: pl.BlockSpec((tm, w), lambda i: (i, 0))
    par_spec = lambda p: pl.BlockSpec(p.shape, lambda i: (0, 0))
    return _pcall(body, name=name, grid=(T // tm,),
                  in_specs=[row_spec(r.shape[1]) for r in rows] + [par_spec(p) for p in params],
                  out_specs=[row_spec(s.shape[1]) for s in shp],
                  out_shape=[S((T, s.shape[1]), dt) for s, dt in zip(shp, out_dtypes)],
                  compiler_params=_params(("parallel",)))(*rows, *params)


def _rowwise_vjp(name, fn, rows, params, cots, drow_dtypes, tm):
    nr, npar, nc = len(rows), len(params), len(cots)
    T = rows[0].shape[0]

    def body(*refs):
        r = [x[...].astype(f32) for x in refs[:nr]]
        p = [x[...] for x in refs[nr:nr + npar]]
        c = tuple(x[...].astype(f32) for x in refs[nr + npar:nr + npar + nc])
        o_refs = refs[nr + npar + nc:]
        _, vjp = jax.vjp(fn, *r, *p)
        grads = vjp(c)
        for o_ref, g in zip(o_refs[:nr], grads[:nr]):
            o_ref[...] = g.astype(o_ref.dtype)

        @pl.when(pl.program_id(0) == 0)
        def _():
            for o_ref in o_refs[nr:]:
                o_ref[...] = jnp.zeros_like(o_ref)

        for o_ref, g in zip(o_refs[nr:], grads[nr:]):
            o_ref[...] += g

    row_spec = lambda w: pl.BlockSpec((tm, w), lambda i: (i, 0))
    par_spec = lambda p: pl.BlockSpec(p.shape, lambda i: (0, 0))
    return _pcall(body, name=name, grid=(T // tm,),
                  in_specs=[row_spec(r.shape[1]) for r in rows] + [par_spec(p) for p in params] + [row_spec(c.shape[1]) for c in cots],
                  out_specs=[row_spec(r.shape[1]) for r in rows] + [par_spec(p) for p in params],
                  out_shape=[S(r.shape, dt) for r, dt in zip(rows, drow_dtypes)] + [S(p.shape, f32) for p in params],
                  compiler_params=_params(("arbitrary",)))(*rows, *params, *cots)


def _rms(h, g):
    return h * lax.rsqrt(jnp.mean(h * h, axis=-1, keepdims=True) + EPS) * g


def _rms_fn(h, g):
    return (_rms(h, g),)


def _rms_res_fn(h, g):
    return _rms(h, g), h


def _rms_res_bias_fn(h, g, b0):
    return _rms(h, g), h + b0


def _ln_silu_fn(c, g, b):
    mu = jnp.mean(c, axis=-1, keepdims=True)
    var = jnp.mean(jnp.square(c - mu), axis=-1, keepdims=True)
    y = (c - mu) * lax.rsqrt(var + EPS) * g + b
    return (y * jax.nn.sigmoid(y),)


def _glu_fn(av, ag, bv, bg):
    return ((av + bv) * jax.nn.sigmoid(ag + bg),)


def _loss_head(h, tgt, Lp, n_real, tm):
    T, D = h.shape

    def body(h_ref, t_ref, dy_ref, part_ref):
        i = pl.program_id(0)
        pos = (i * tm + lax.broadcasted_iota(jnp.int32, (tm, 1), 0)) % Lp
        real = (pos >= N_META_ROWS) & (pos < N_META_ROWS + n_real)
        err = jnp.where(real, h_ref[...] - t_ref[...], 0.0)
        dy_ref[...] = err * (1.0 / D)

        @pl.when(i == 0)
        def _():
            part_ref[...] = jnp.zeros_like(part_ref)

        part_ref[...] += jnp.sum(err * err, axis=0, keepdims=True)

    spec = pl.BlockSpec((tm, D), lambda i: (i, 0))
    return _pcall(body, name="loss_head", grid=(T // tm,), in_specs=[spec, spec],
                  out_specs=[spec, pl.BlockSpec((1, D), lambda i: (0, 0))],
                  out_shape=[S((T, D), f32), S((1, D), f32)], compiler_params=_params(("arbitrary",)))(h, tgt)


CONV_OFF = CONV_PAD - (CONV_TAPS - 1)
WIN_ROWS = ATT_BLOCK + CONV_PAD


def _phases(ph_ref, win):
    n = win.shape[0]
    for b in range(1, 8):
        ph_ref[b - 1, 0:n - 8, :] = win[b:n - 8 + b]

    def tap(o):
        a = (o // 8) * 8
        return win[a:a + ATT_BLOCK] if o % 8 == 0 else ph_ref[o % 8 - 1, a:a + ATT_BLOCK, :]

    return tap


def _phase_scratch(width):
    return pltpu.VMEM((7, WIN_ROWS - 8, width), f32)


def _dwconv_fwd(name, p, dw, carry=None):
    B, Lp, D = p.shape
    dc = _tile(D, (256, 128))
    slabs, plan = carry if carry is not None else ([], [])
    nc = len(slabs)

    def body(*refs):
        p_ref, w_ref = refs[:2]
        o_ref, bufs, ph_ref = refs[2 + nc], refs[3 + nc:3 + 2 * nc], refs[3 + 2 * nc]
        first = (pl.program_id(0) == 0) & (pl.program_id(1) == 0)
        last = (pl.program_id(0) == B - 1) & (pl.program_id(1) == D // dc - 1)
        if nc:
            @pl.when(first)
            def _():
                _run_copies(bufs, plan, refs[-2], refs[-1], start=True, wait=False)

        def tile(win, base):
            tap = _phases(ph_ref, win)
            acc = jnp.zeros((ATT_BLOCK, dc), f32)
            for k in range(CONV_TAPS):
                acc = acc + tap(CONV_OFF + k) * w_ref[k:k + 1, :]
            o_ref[pl.ds(base, ATT_BLOCK), :] = acc

        tile(jnp.concatenate([jnp.zeros((CONV_PAD, dc), f32), p_ref[0:ATT_BLOCK, :]], axis=0), 0)

        def step(r, carry):
            base = pl.multiple_of(r * ATT_BLOCK, ATT_BLOCK)
            tile(p_ref[pl.ds(pl.multiple_of(base - CONV_PAD, CONV_PAD), WIN_ROWS), :], base)
            return carry

        lax.fori_loop(1, Lp // ATT_BLOCK, step, 0)
        if nc:
            @pl.when(last)
            def _():
                _run_copies(bufs, plan, refs[-2], refs[-1], start=False, wait=True)

    seq = pl.BlockSpec((None, Lp, dc), lambda b, j: (b, 0, j))
    sems = [pltpu.SemaphoreType.DMA((_plan_copies(plan),))] * 2 if nc else []
    return _pcall(body, name=name, grid=(B, D // dc),
                  in_specs=[seq, pl.BlockSpec((CONV_PAD, dc), lambda b, j: (0, j))] + [ANY] * nc,
                  out_specs=[seq] + [ANY] * nc, out_shape=[S((B, Lp, D), f32)] + [S(s.shape, s.dtype) for s in slabs],
                  input_output_aliases={2 + t: 1 + t for t in range(nc)}, scratch_shapes=[_phase_scratch(dc)] + sems,
                  compiler_params=_params(("arbitrary", "arbitrary") if nc else ("parallel", "parallel")))(p, dw, *slabs)


def _dwconv_bwd(name, dcv, p, dw):
    B, Lp, D = p.shape
    dcw = _tile(D, (256, 128))
    nblk = Lp // ATT_BLOCK
    assert nblk >= 2
    zeros = lambda: jnp.zeros((CONV_PAD, dcw), f32)

    def body(dc_ref, p_ref, w_ref, dp_ref, ddw_ref, dph_ref, pph_ref, part_ref):
        @pl.when(pl.program_id(1) == 0)
        def _():
            ddw_ref[...] = jnp.zeros_like(ddw_ref)

        part_ref[...] = jnp.zeros_like(part_ref)

        def tile(dwin, pwin, base):
            dtap, ptap = _phases(dph_ref, dwin), _phases(pph_ref, pwin)
            dtile = dwin[0:ATT_BLOCK]
            acc = jnp.zeros((ATT_BLOCK, dcw), f32)
            for k in range(CONV_TAPS):
                acc = acc + dtap(CONV_TAPS - 1 - k) * w_ref[k:k + 1, :]
                part_ref[k] += jnp.sum((dtile * ptap(CONV_OFF + k)).reshape(ATT_BLOCK // 8, 8, dcw), axis=0)
            dp_ref[pl.ds(base, ATT_BLOCK), :] = acc

        tile(dc_ref[0:WIN_ROWS, :], jnp.concatenate([zeros(), p_ref[0:ATT_BLOCK, :]], axis=0), 0)

        def step(r, carry):
            base = pl.multiple_of(r * ATT_BLOCK, ATT_BLOCK)
            tile(dc_ref[pl.ds(base, WIN_ROWS), :], p_ref[pl.ds(pl.multiple_of(base - CONV_PAD, CONV_PAD), WIN_ROWS), :], base)
            return carry

        lax.fori_loop(1, nblk - 1, step, 0)
        last = Lp - ATT_BLOCK
        tile(jnp.concatenate([dc_ref[last:Lp, :], zeros()], axis=0), p_ref[last - CONV_PAD:Lp, :], last)
        ddw_ref[...] += jnp.sum(part_ref[...], axis=1)

    seq = pl.BlockSpec((None, Lp, dcw), lambda j, b: (b, 0, j))
    wsp = pl.BlockSpec((CONV_PAD, dcw), lambda j, b: (0, j))
    return _pcall(body, name=name, grid=(D // dcw, B), in_specs=[seq, seq, wsp], out_specs=[seq, wsp],
                  out_shape=[S((B, Lp, D), f32), S((CONV_PAD, D), f32)],
                  scratch_shapes=[_phase_scratch(dcw)] * 2 + [pltpu.VMEM((CONV_PAD, 8, dcw), f32)],
                  compiler_params=_params(("parallel", "arbitrary")))(dcv, p, dw)


def _band_start(n):
    return pl.multiple_of(jnp.maximum(n - 1, 0) * ATT_BLOCK, ATT_BLOCK)


def _attn_bias(R):
    row = jnp.arange(R * ATT_BLOCK)[:, None] % ATT_BLOCK
    col = jnp.arange(3 * ATT_BLOCK)[None, :]
    out = []
    for n in range(3):
        qpos = n * ATT_BLOCK + row
        meta_ok = (col < N_META_ROWS) & (col <= qpos)
        band_pos = max(n - 1, 0) * ATT_BLOCK + (col - ATT_BLOCK)
        diff = qpos - band_pos
        band_ok = (col >= ATT_BLOCK) & (diff >= 0) & (diff < ATT_BLOCK) & (band_pos >= N_META_ROWS)
        out.append(jnp.where(meta_ok | band_ok, 0.0, MASKED))
    return jnp.stack(out).astype(f32)


def _row_head(R):
    row = lax.broadcasted_iota(jnp.int32, (R * ATT_BLOCK, R), 0)
    lo = lax.broadcasted_iota(jnp.int32, (R * ATT_BLOCK, R), 1) * ATT_BLOCK
    return ((row >= lo) & (row < lo + ATT_BLOCK)).astype(f32)


def _attn_core(q, kn, vv, qg, sink_vec, row_head, bias):
    qn = _rms(q, qg)
    s = lax.dot_general(qn.astype(bf16), kn.astype(bf16), _DN["nt"], preferred_element_type=f32) * (HEAD_DIM ** -0.5)
    s = s + bias
    sink = jnp.sum(row_head * sink_vec, axis=-1, keepdims=True)
    m = lax.stop_gradient(jnp.maximum(jnp.max(s, axis=-1, keepdims=True), sink))
    p = jnp.exp(s - m)
    denom = jnp.sum(p, axis=-1, keepdims=True) + jnp.exp(sink - m)
    return jnp.dot((p / denom).astype(bf16), vv.astype(bf16), preferred_element_type=f32)


def _keys_of(ref, n):
    return jnp.concatenate([ref[0:ATT_BLOCK, :], ref[pl.ds(_band_start(n), 2 * ATT_BLOCK), :]], axis=0)


def _stack_heads(x, R):
    return x if R == 1 else jnp.concatenate([x[:, r * HEAD_DIM:(r + 1) * HEAD_DIM] for r in range(R)], axis=0)


def _unstack_heads(x, R):
    return x if R == 1 else jnp.concatenate([x[r * ATT_BLOCK:(r + 1) * ATT_BLOCK] for r in range(R)], axis=1)


def _attn_specs(R, Lp):
    qspec = pl.BlockSpec((None, ATT_BLOCK, R * HEAD_DIM), lambda g, b, n: (b, n, g))
    kspec = pl.BlockSpec((None, None, Lp, HEAD_DIM), lambda g, b, n: (g, b, 0, 0))
    gspec = pl.BlockSpec((1, HEAD_DIM), lambda g, b, n: (0, 0))
    sspec = pl.BlockSpec((None, 1, R), lambda g, b, n: (g, 0, 0))
    bspec = pl.BlockSpec((None, R * ATT_BLOCK, 3 * ATT_BLOCK), lambda g, b, n: (jnp.minimum(n, 2), 0, 0))
    return qspec, kspec, gspec, sspec, bspec


def _attn_fwd(name, q, kn, v, qg, sinks):
    B, Lp, D = q.shape
    G = kn.shape[0]
    R = D // (G * HEAD_DIM)
    qspec, kspec, gspec, sspec, bspec = _attn_specs(R, Lp)

    def body(q_ref, k_ref, v_ref, qg_ref, s_ref, b_ref, o_ref):
        n = pl.program_id(2)
        o = _attn_core(_stack_heads(q_ref[...], R), _keys_of(k_ref, n), _keys_of(v_ref, n),
                       qg_ref[...], s_ref[...], _row_head(R), b_ref[...])
        o_ref[...] = _unstack_heads(o, R).astype(o_ref.dtype)

    return _pcall(body, name=name, grid=(G, B, Lp // ATT_BLOCK), in_specs=[qspec, kspec, kspec, gspec, sspec, bspec],
                  out_specs=qspec, out_shape=S(q.shape, bf16),
                  compiler_params=_params(("parallel", "parallel", "parallel")))(q, kn, v, qg, sinks, _attn_bias(R))


def _attn_bwd(name, q, kn, v, qg, sinks, do):
    B, Lp, D = q.shape
    G = kn.shape[0]
    R = D // (G * HEAD_DIM)
    qspec, kspec, gspec, sspec, bspec = _attn_specs(R, Lp)

    def body(q_ref, k_ref, v_ref, qg_ref, s_ref, b_ref, do_ref, dq_ref, dk_ref, dv_ref, dqg_ref, ds_ref):
        g, b, n = pl.program_id(0), pl.program_id(1), pl.program_id(2)

        @pl.when((g == 0) & (b == 0) & (n == 0))
        def _():
            dqg_ref[...] = jnp.zeros_like(dqg_ref)

        @pl.when((b == 0) & (n == 0))
        def _():
            ds_ref[...] = jnp.zeros_like(ds_ref)

        @pl.when(n == 0)
        def _():
            dk_ref[...] = jnp.zeros_like(dk_ref)
            dv_ref[...] = jnp.zeros_like(dv_ref)

        row_head, bias = _row_head(R), b_ref[...]
        _, vjp = jax.vjp(lambda q_, k_, v_, a_, s_: _attn_core(q_, k_, v_, a_, s_, row_head, bias),
                         _stack_heads(q_ref[...], R), _keys_of(k_ref, n).astype(f32),
                         _keys_of(v_ref, n).astype(f32), qg_ref[...], s_ref[...])
        dq, dkk, dvv, dqg, dsk = vjp(_stack_heads(do_ref[...].astype(f32), R))
        dq_ref[...] = _unstack_heads(dq, R)
        dqg_ref[...] += dqg
        ds_ref[...] += dsk
        band = pl.ds(_band_start(n), 2 * ATT_BLOCK)
        dk_ref[band, :] += dkk[ATT_BLOCK:]
        dv_ref[band, :] += dvv[ATT_BLOCK:]
        dk_ref[0:ATT_BLOCK, :] += dkk[:ATT_BLOCK]
        dv_ref[0:ATT_BLOCK, :] += dvv[:ATT_BLOCK]

    return _pcall(body, name=name, grid=(G, B, Lp // ATT_BLOCK),
                  in_specs=[qspec, kspec, kspec, gspec, sspec, bspec, qspec],
                  out_specs=[qspec, kspec, kspec, gspec, sspec],
                  out_shape=[S(q.shape, f32), S(kn.shape, f32), S(v.shape, f32), S(qg.shape, f32), S(sinks.shape, f32)],
                  compiler_params=_params(("arbitrary", "arbitrary", "arbitrary")))(q, kn, v, qg, sinks, _attn_bias(R), do)


def _kv_heads(name, kv, kg, tm):
    T, W = kv.shape
    G = W // (2 * HEAD_DIM)

    def body(kv_ref, kg_ref, k_ref, v_ref):
        x = kv_ref[...]
        for g in range(G):
            k_ref[g] = _rms(x[:, g * HEAD_DIM:(g + 1) * HEAD_DIM], kg_ref[...]).astype(bf16)
            v_ref[g] = x[:, (G + g) * HEAD_DIM:(G + g + 1) * HEAD_DIM].astype(bf16)

    hspec = pl.BlockSpec((G, tm, HEAD_DIM), lambda i: (0, i, 0))
    return _pcall(body, name=name, grid=(T // tm,),
                  in_specs=[pl.BlockSpec((tm, W), lambda i: (i, 0)), pl.BlockSpec((1, HEAD_DIM), lambda i: (0, 0))],
                  out_specs=[hspec, hspec], out_shape=[S((G, T, HEAD_DIM), bf16)] * 2,
                  compiler_params=_params(("parallel",)))(kv, kg)


def _kv_heads_bwd(name, kv, kg, dkn, dv, tm):
    T, W = kv.shape
    G = W // (2 * HEAD_DIM)

    def body(kv_ref, kg_ref, dk_ref, dv_ref, o_ref, dkg_ref):
        @pl.when(pl.program_id(0) == 0)
        def _():
            dkg_ref[...] = jnp.zeros_like(dkg_ref)

        x = kv_ref[...]
        pieces = []
        for g in range(G):
            _, vjp = jax.vjp(_rms, x[:, g * HEAD_DIM:(g + 1) * HEAD_DIM], kg_ref[...])
            dk, dkg = vjp(dk_ref[g])
            pieces.append(dk)
            dkg_ref[...] += dkg
        o_ref[...] = jnp.concatenate(pieces + [dv_ref[g] for g in range(G)], axis=1)

    hspec = pl.BlockSpec((G, tm, HEAD_DIM), lambda i: (0, i, 0))
    gspec = pl.BlockSpec((1, HEAD_DIM), lambda i: (0, 0))
    return _pcall(body, name=name, grid=(T // tm,),
                  in_specs=[pl.BlockSpec((tm, W), lambda i: (i, 0)), gspec, hspec, hspec],
                  out_specs=[pl.BlockSpec((tm, W), lambda i: (i, 0)), gspec],
                  out_shape=[S((T, W), f32), S((1, HEAD_DIM), f32)], compiler_params=_params(("arbitrary",)))(kv, kg, dkn, dv)


def _place():
    x, y, c = lax.axis_index("x"), lax.axis_index("y"), lax.axis_index("c")
    chips = [(1 - x, y), (x, 1 - y), (1 - x, 1 - y)]
    return x, y, c, chips, [2 * cx + cy for cx, cy in chips]


def _remote(src, dst, send_sem, recv_sem, to):
    return pltpu.make_async_remote_copy(src_ref=src, dst_ref=dst, send_sem=send_sem, recv_sem=recv_sem,
                                        device_id=to, device_id_type=MESH)


def _into_slot(name, w, l, idx, slots, dtype):
    _, Rr, Cc = w.shape
    tr = _tile(Rr, (512, 256, 128, 64, 32, 16))

    def body(i_ref, x_ref, o_ref):
        o_ref[...] = x_ref[...].astype(o_ref.dtype)

    gs = pltpu.PrefetchScalarGridSpec(
        num_scalar_prefetch=1, grid=(Rr // tr,),
        in_specs=[pl.BlockSpec((None, tr, Cc), lambda i, i_ref: (l, i, 0))],
        out_specs=pl.BlockSpec((None, tr, Cc), lambda i, i_ref: (i_ref[0], i, 0)))
    return _pcall(body, name=name, grid_spec=gs, out_shape=S((slots, Rr, Cc), dtype), compiler_params=_params(("parallel",)))(idx, w)


_PLAN_COPIES = dict(ici=3, fwd=3, xchg=N_CHIPS, scat=3, share=1)


def _plan_copies(plan):
    return sum(_PLAN_COPIES[step[0]] for step in plan)


def _comm_copies(bufs, plan, send, recv):
    x, y, c, chips, qk = _place()
    me_q, sib = 2 * x + y, (x, y, 1 - c)
    starts, lands = [], []

    def add(src, dst, to, land):
        s = len(starts)
        starts.append(_remote(src, dst, send.at[s], recv.at[s], to))
        lands.append(_remote(land, land, send.at[s], recv.at[s], sib))

    for step in plan:
        kind = step[0]
        if kind == "ici":
            mine = bufs[step[1]].at[me_q, c]
            for k in range(3):
                add(mine, mine, (*chips[k], c), bufs[step[1]].at[qk[k], c])
        elif kind == "fwd":
            for k in range(3):
                got = bufs[step[1]].at[qk[k], c]
                add(got, got, sib, bufs[step[1]].at[qk[k], 1 - c])
        elif kind == "xchg":
            for q in range(N_CHIPS):
                add(bufs[step[1]].at[q, 1 - c], bufs[step[2]].at[q], sib, bufs[step[2]].at[q])
        elif kind == "scat":
            for k in range(3):
                add(bufs[step[1]].at[qk[k]], bufs[step[2]].at[k], (*chips[k], c), bufs[step[2]].at[k])
        else:
            mine = bufs[step[1]].at[step[2], c]
            add(mine, mine, sib, bufs[step[1]].at[step[2], 1 - c])
    return starts, lands


def _run_copies(bufs, plan, send, recv, start=True, wait=True):
    starts, lands = _comm_copies(bufs, plan, send, recv)
    if start:
        for cp in starts:
            cp.start()
    if wait:
        for cp in lands:
            cp.wait_recv()
        for cp in starts:
            cp.wait_send()


def _comm_call(name, slabs, phases):
    n = len(slabs)

    def body(*refs):
        bufs, sems = refs[n:2 * n], refs[2 * n:]
        for t, plan in enumerate(phases):
            _run_copies(bufs, plan, sems[2 * t], sems[2 * t + 1])

    sems = [pltpu.SemaphoreType.DMA((_plan_copies(plan),)) for plan in phases for _ in range(2)]
    return _pcall(body, name=name, in_specs=[ANY] * n, out_specs=[ANY] * n, out_shape=[S(s.shape, s.dtype) for s in slabs],
                  input_output_aliases={p: p for p in range(n)}, scratch_shapes=sems)(*slabs)


def _gather_all_devices(block):
    m_per, ncol = block.shape

    def body(x_ref, out_ref, send_sems, recv_sems, local_sem):
        x, y, c, chips, _ = _place()
        me, sib = (x, y, c), (x, y, 1 - c)

        def rows(px, py, pc):
            return out_ref.at[pl.ds((4 * px + 2 * py + pc) * m_per, m_per), :]

        def copy(k, blk, to, src=None):
            return _remote(rows(*blk) if src is None else src, rows(*blk), send_sems.at[k], recv_sems.at[k], to)

        mine = pltpu.make_async_copy(x_ref, rows(*me), local_sem)
        mine.start()
        first = [copy(0, me, sib, src=x_ref)] + [copy(1 + j, me, (*chip, c), src=x_ref) for j, chip in enumerate(chips)]
        for cp in first:
            cp.start()
        passed = [copy(4 + j, (*chip, c), sib) for j, chip in enumerate(chips)]
        for j, chip in enumerate(chips):
            copy(1 + j, (*chip, c), me).wait_recv()
            passed[j].start()
        copy(0, sib, me).wait_recv()
        for j, chip in enumerate(chips):
            copy(4 + j, (*chip, 1 - c), me).wait_recv()
        for cp in first + passed:
            cp.wait_send()
        mine.wait()

    vm = pl.BlockSpec(memory_space=pltpu.VMEM)
    return _pcall(body, name="gather_small_grads", in_specs=[vm], out_specs=vm,
                  out_shape=S((8 * m_per, ncol), block.dtype),
                  scratch_shapes=[pltpu.SemaphoreType.DMA((7,)), pltpu.SemaphoreType.DMA((7,)), pltpu.SemaphoreType.DMA],
                  compiler_params=pltpu.CompilerParams(vmem_limit_bytes=VMEM_LIMIT_BYTES))(block)


def _sum_pair(name, g, r1, c_idx):
    Q, _, Rr, Cc = g.shape
    tr = _tile(Rr, (512, 256, 128))

    def body(c_ref, g_ref, r_ref, o_ref):
        o_ref[...] = (g_ref[...].astype(f32) + r_ref[...].astype(f32)).astype(o_ref.dtype)

    spec = pl.BlockSpec((None, tr, Cc), lambda q, i, c_ref: (q, i, 0))
    gs = pltpu.PrefetchScalarGridSpec(
        num_scalar_prefetch=1, grid=(Q, Rr // tr),
        in_specs=[pl.BlockSpec((None, None, tr, Cc), lambda q, i, c_ref: (q, c_ref[0], i, 0)), spec], out_specs=spec)
    return _pcall(body, name=name, grid_spec=gs, out_shape=S((Q, Rr, Cc), bf16),
                  compiler_params=_params(("parallel", "parallel")))(c_idx, g, r1)


def _sum_owner(name, s, r2, qc_idx, l, nl, into):
    Q, Rr, Cc = s.shape
    tr = _tile(Rr, (512, 256, 128))

    def body(q_ref, s_ref, r_ref, *rest):
        o_ref = rest[-1]
        o_ref[...] = ((s_ref[...].astype(f32) + r_ref[0].astype(f32)) + r_ref[1].astype(f32)) + r_ref[2].astype(f32)

    in_specs = [pl.BlockSpec((None, tr, Cc), lambda i, q_ref: (q_ref[0], i, 0)), pl.BlockSpec((3, tr, Cc), lambda i, q_ref: (0, i, 0))]
    gs = pltpu.PrefetchScalarGridSpec(
        num_scalar_prefetch=1, grid=(Rr // tr,), in_specs=in_specs + ([] if into is None else [ANY]),
        out_specs=pl.BlockSpec((None, None, tr, Cc), lambda i, q_ref: (l, q_ref[1], i, 0)))
    return _pcall(body, name=name, grid_spec=gs, out_shape=S((nl, 2, Rr, Cc), f32),
                  input_output_aliases={} if into is None else {3: 0},
                  compiler_params=_params(("parallel",)))(qc_idx, s, r2, *([] if into is None else [into]))


def _sum_devices(stack):
    n, M, C = stack.shape

    def body(s_ref, o_ref):
        acc = s_ref[0]
        for d in range(1, n):
            acc = acc + s_ref[d]
        o_ref[...] = acc

    return _pcall(body, name="sum_small_grads", out_shape=S((M, C), f32), compiler_params=_params())(stack)


def _adamw(name, w, g, m, v):
    Rr, Cc = w.shape
    tr = _tile(Rr, (256, 128, 64, 32, 16, 8))

    def body(w_ref, g_ref, m_ref, v_ref, go_ref, d_ref, mo_ref, vo_ref):
        g_ = g_ref[...]
        go_ref[...] = g_
        m_ = B1 * m_ref[...] + (1.0 - B1) * g_
        v_ = B2 * v_ref[...] + (1.0 - B2) * jnp.square(g_)
        m_hat = m_ / (1.0 - B1 ** STEP)
        v_hat = v_ / (1.0 - B2 ** STEP)
        d_ref[...] = -LR * (m_hat / (jnp.sqrt(v_hat) + ADAM_EPS) + WD * w_ref[...])
        mo_ref[...] = m_
        vo_ref[...] = v_

    spec = pl.BlockSpec((tr, Cc), lambda i: (i, 0))
    return _pcall(body, name=name, grid=(Rr // tr,), in_specs=[spec] * 4, out_specs=[spec] * 4,
                  out_shape=[S((Rr, Cc), f32)] * 4, compiler_params=_params(("parallel",)))(w, g, m, v)


def _pack(arrs, multiple):
    flat = jnp.concatenate([a.reshape(-1) for a in arrs])
    pad = (-flat.shape[0]) % multiple
    return jnp.pad(flat, (0, pad)).reshape(-1, 128)


def _unpack(slab, shapes):
    flat, out, o = slab.reshape(-1), [], 0
    for shp in shapes:
        n = 1
        for d in shp:
            n *= d
        out.append(flat[o:o + n].reshape(shp))
        o += n
    return out


def kernel(x, meta_tokens, norm_mix, norm_ffn, conv_w_in, conv_b_in, conv_dw, conv_ln_g, conv_ln_b, conv_w_out, conv_b_out, kv_norm, w_kv, k_norm, w_q, q_norm, attn_sinks, w_o, ffn_w_gate, ffn_w_up, ffn_w_down, loss_target, m_meta_tokens, m_norm_mix, m_norm_ffn, m_conv_w_in, m_conv_b_in, m_conv_dw, m_conv_ln_g, m_conv_ln_b, m_conv_w_out, m_conv_b_out, m_kv_norm, m_w_kv, m_k_norm, m_w_q, m_q_norm, m_attn_sinks, m_w_o, m_ffn_w_gate, m_ffn_w_up, m_ffn_w_down, v_meta_tokens, v_norm_mix, v_norm_ffn, v_conv_w_in, v_conv_b_in, v_conv_dw, v_conv_ln_g, v_conv_ln_b, v_conv_w_out, v_conv_b_out, v_kv_norm, v_w_kv, v_k_norm, v_w_q, v_q_norm, v_attn_sinks, v_w_o, v_ffn_w_gate, v_ffn_w_up, v_ffn_w_down):
    Q = N_CHIPS
    B, SEQ, D = x.shape
    L = N_META_ROWS + SEQ
    Lp = -(-L // ATT_BLOCK) * ATT_BLOCK
    T = B * Lp
    NA, NB = conv_w_in.shape[0], w_q.shape[0]
    NL = NA + NB
    Dq = D // Q
    G = N_KV
    R = D // (HEAD_DIM * G)
    KVW = w_kv.shape[1]
    assert NA % 2 == 0 and NB % 2 == 0 and NL % 2 == 0 and (D // Q) % 32 == 0
    tm = _tile(T, (1088, 544, 512, 256, 128))
    tk = _tile(T, (2176, 1088, 544, 512, 256, 128))
    tr = _tile(T, (272, 256, 128))
    my_c = lax.axis_index("c").astype(jnp.int32).reshape(1)
    my_q = (2 * lax.axis_index("x") + lax.axis_index("y")).astype(jnp.int32)
    my_qc = jnp.concatenate([my_q.reshape(1), my_c])

    small_shapes = [meta_tokens.shape, conv_b_in.shape, conv_dw.shape, conv_ln_g.shape, conv_ln_b.shape, conv_b_out.shape]
    small = _pack([meta_tokens, conv_b_in, conv_dw, conv_ln_g, conv_ln_b, conv_b_out], 2048)
    big = [conv_w_in, conv_w_out, w_kv, w_q, w_o, ffn_w_gate, ffn_w_up, ffn_w_down]
    keyed = dict(cin=conv_w_in, cout=conv_w_out, kv=w_kv[None], q=w_q, o=w_o, g=ffn_w_gate, u=ffn_w_up, d=ffn_w_down,
                 small=small[None])
    slabs, where = [], {}
    for key, w3 in keyed.items():
        for l in range(w3.shape[0]):
            s = _into_slot(f"own_{key}{l}", w3, l, my_qc, Q, f32 if key == "small" else bf16)
            where[key, l] = len(slabs)
            slabs.append(s.reshape(Q, 2, s.shape[1] // 2, s.shape[2]))

    def W(key, l):
        s = slabs[where[key, l]]
        return s.reshape(Q, 1, 2 * s.shape[2], s.shape[3])

    def layer_slabs(l):
        j = l - NA
        mix = [("cin", l), ("cout", l)] if l < NA else [("q", j), ("o", j)] + ([("kv", 0)] if j == 0 else [])
        return [where[k] for k in mix], [where["g", l], where["u", l]], [where["d", l]]

    def carry_of(ici, fwd):
        idxs = sorted(set(ici) | set(fwd))
        return idxs, ([slabs[i] for i in idxs], [("ici", idxs.index(i)) for i in ici] + [("fwd", idxs.index(i)) for i in fwd])

    def put_back(idxs, new):
        for i, s in zip(idxs, new):
            slabs[i] = s

    assert NA >= 1
    first = [where["cin", 0], where["small", 0]]
    idxs, (sl, plan) = carry_of(first, first)
    put_back(idxs, _comm_call("gather_first", sl, [[s for s in plan if s[0] == "ici"], [s for s in plan if s[0] == "fwd"]]))
    parts = [_unpack(slabs[where["small", 0]][q], small_shapes) for q in range(Q)]
    meta_f, b_in_f, dw_f, ln_g_f, ln_b_f, b_out_f = [jnp.concatenate([parts[q][i] for q in range(Q)], axis=-1) for i in range(6)]
    dw_pad = jnp.pad(dw_f, ((0, 0), (0, CONV_PAD - CONV_TAPS), (0, 0)))

    h = jnp.concatenate([jnp.broadcast_to(meta_f[None], (B, N_META_ROWS, D)), x, jnp.zeros((B, Lp - L, D), f32)], axis=1).reshape(T, D)
    tgt = jnp.pad(loss_target, ((0, 0), (N_META_ROWS, Lp - L), (0, 0))).reshape(T, D)
    row = lambda a: a.reshape(1, -1)
    seqs = lambda a: a.reshape(B, Lp, a.shape[-1])
    by_seq = lambda a: a.reshape(G, B, Lp, HEAD_DIM)
    saved = []
    sinks3 = attn_sinks.reshape(NB, G, 1, R)
    def carrying(ici, fwd):
        idxs, carry = carry_of(ici, fwd)
        return idxs, (carry if idxs else None)

    for l in range(NL):
        st = {"h_a": h}
        u = _rowwise(f"rms_mix{l}", _rms_fn, [h], [row(norm_mix[l])], [bf16], tr)[0]
        st["u"] = u
        mix_s, (g_s, u_s), d_s = layer_slabs(l)
        idxs, carry = carrying([where["cout", 0], g_s], []) if l == 0 else carrying([u_s], [g_s])
        second = [u_s] + d_s if l == 0 else [u_s]
        if l < NA:
            p, av, ag, *new = _glu_fwd(f"glu{l}", u, W("cin", l), 0, b_in_f[l:l + 1], tm, carry=carry)
            put_back(idxs, new)
            idxs, carry = carrying([u_s] + d_s, [where["cout", 0], g_s]) if l == 0 else ([], None)
            cv, *new = _dwconv_fwd(f"dwconv{l}", p.reshape(B, Lp, D), dw_pad[l], carry=carry)
            put_back(idxs, new)
            cv = cv.reshape(T, D)
            s = _rowwise(f"ln_silu{l}", _ln_silu_fn, [cv], [row(ln_g_f[l]), row(ln_b_f[l])], [bf16], tr)[0]
            idxs2, carry2 = carrying([], second)
            res = _proj(f"conv_out{l}", s, W("cout", l), 0, tm, bias=row(b_out_f[l]), resid=h, carry=carry2)
            h, new = res if carry2 is not None else (res, [])
            put_back(idxs2, new)
            st.update(av=av, ag=ag, p=p, cv=cv, s=s)
        else:
            j = l - NA
            if j == 0:
                kvn = _rowwise("rms_kv", _rms_fn, [h], [row(kv_norm)], [bf16], tr)[0]
                kv = _proj("kv_proj", kvn, W("kv", 0), 0, tm)
                kn, vh = _kv_heads("kv_heads", kv, row(k_norm), tm)
                kn, vh = by_seq(kn), by_seq(vh)
                st.update(kvn=kvn, kv=kv)
            res = _proj(f"q_proj{j}", u, W("q", j), 0, tm, carry=carry)
            q, new = res if carry is not None else (res, [])
            put_back(idxs, new)
            q = seqs(q)
            o = _attn_fwd(f"attn{j}", q, kn, vh, row(q_norm[j]), sinks3[j]).reshape(T, D)
            idxs2, carry2 = carrying([], second)
            res = _proj(f"o_proj{j}", o, W("o", j), 0, tm, resid=h, carry=carry2)
            h, new = res if carry2 is not None else (res, [])
            put_back(idxs2, new)
            st.update(q=q, o=o)
        st["h_b"] = h
        u2 = _rowwise(f"rms_ffn{l}", _rms_fn, [h], [row(norm_ffn[l])], [bf16], tr)[0]
        nmix, ngu, nd = layer_slabs(l + 1) if l + 1 < NL else ([], [], [])
        idxs, carry = carrying(nmix + nd, [])
        gate, up, hid, *new = _ffn_up(f"ffn_up{l}", u2, W("g", l), W("u", l), 0, tm, carry=carry)
        put_back(idxs, new)
        idxs, carry = carrying(ngu[:1], nmix + nd)
        h, *new = _ffn_down(f"ffn_down{l}", hid, W("d", l), 0, h, tm, carry=carry)
        put_back(idxs, new)
        st.update(u2=u2, gate=gate, up=up, hid=hid)
        saved.append(st)

    dh, part = _loss_head(h, tgt, Lp, SEQ, tr)
    loss = lax.psum(0.5 / D * jnp.sum(part), ("x", "y", "c"))

    grad_slab = {}
    n_layers = dict(cin=NA, cout=NA, kv=1, q=NB, o=NB, g=NL, u=NL, d=NL)

    def rs_begin(pieces):
        job = []
        for key, lay, g in pieces:
            g4 = g.reshape(Q, 2, g.shape[1] // 2, g.shape[2])
            job.append(dict(key=key, l=lay, g=g4, r1=lax.empty((Q,) + g4.shape[2:], bf16)))
        return job

    def xchg_carry(job):
        n = len(job)
        return [p["g"] for p in job] + [p["r1"] for p in job], [("xchg", t, n + t) for t in range(n)]

    def after_xchg(job, new):
        n = len(job)
        for t, p in enumerate(job):
            p["s"] = _sum_pair(f"sum_pair_{p['key']}{p['l']}", new[t], new[n + t], my_c)
            p["r2"] = lax.empty((3,) + p["s"].shape[1:], bf16)

    def scat_carry(job, keys):
        sel = [p for p in job if p["key"] in keys]
        n = len(sel)
        return sel, ([p["s"] for p in sel] + [p["r2"] for p in sel], [("scat", t, n + t) for t in range(n)])

    def after_scat(sel, new):
        n = len(sel)
        for t, p in enumerate(sel):
            key = p["key"]
            grad_slab[key] = _sum_owner(f"sum_owner_{key}{p['l']}", new[t], new[n + t], my_qc, p["l"], n_layers[key], grad_slab.get(key))

    def share_carry(job):
        return [grad_slab[p["key"]] for p in job], [("share", t, p["l"]) for t, p in enumerate(job)]

    def after_share(job, new):
        for p, s in zip(job, new):
            grad_slab[p["key"]] = s

    def carried(job, make):
        return make(job) if job else None

    def merge(*carries):
        slabs_, plan_ = [], []
        for c in carries:
            if c is not None:
                off = len(slabs_)
                slabs_ += c[0]
                plan_ += [(s[0], s[1] + off, s[2]) if s[0] == "share" else (s[0],) + tuple(i + off for i in s[1:]) for s in c[1]]
        return (slabs_, plan_) if slabs_ else None

    MIXER = ("cin", "cout", "q", "o", "kv")
    g_mix, g_ffn = [None] * NL, [None] * NL
    g_bin, g_dw, g_lng, g_lnb, g_bout = ([None] * NA for _ in range(5))
    g_qn, g_sink = [None] * NB, [None] * NB
    dknp = dvp = None
    zero_row = jnp.zeros((1, D), f32)
    job = []
    early = []
    for l in reversed(range(NL)):
        st = saved[l]
        pieces = []
        dgate, dup, *new = _ffn_dhid(f"ffn_dhid{l}", dh, W("d", l), 0, st["gate"], st["up"], tm, carry=carried(job, xchg_carry))
        if job:
            after_xchg(job, new)
        for nm, key, fn in (("dwd", "d", lambda c: _dw_rows(f"ffn_dwd{l}", st["hid"], dh, Q, tk, carry=c)),
                            ("dwg", "g", lambda c: _dw_cols(f"ffn_dwg{l}", st["u2"], dgate, Q, tk, Q, carry=c)),
                            ("dwu", "u", lambda c: _dw_cols(f"ffn_dwu{l}", st["u2"], dup, Q, tk, Q, carry=c))):
            sel, carry = scat_carry(job, (key,)) if job else ([], None)
            dw, *new = fn(carry)
            after_scat(sel, new)
            pieces.append((key, l, dw))
        if l == 0 and l < NA:
            early, pieces = rs_begin(pieces), []
        sel, carry = scat_carry(job, MIXER) if job else ([], None)
        du2, *new = _ffn_du(f"ffn_du{l}", dgate, dup, W("g", l), W("u", l), 0, tm, carry=merge(carry, carried(early, xchg_carry)))
        after_scat(sel, new[:2 * len(sel)])
        if early:
            after_xchg(early, new[2 * len(sel):])
        dh, g_ffn[l] = _rowwise_vjp(f"rms_ffn_bwd{l}", _rms_res_fn, [st["h_b"]], [row(norm_ffn[l])], [du2, dh], [f32], tr)
        if l < NA:
            esel, ecarry = scat_carry(early, ("d",)) if early else ([], None)
            ds, *new = _proj_dx(f"conv_out_dx{l}", dh, W("cout", l), 0, tm, carry=merge(carried(job, share_carry), ecarry))
            after_share(job, new[:len(job)])
            after_scat(esel, new[len(job):])
            pieces.append(("cout", l, _dw_rows(f"conv_out_dw{l}", st["s"], dh, Q, tk)[0]))
            dcv, g_lng[l], g_lnb[l] = _rowwise_vjp(f"ln_silu_bwd{l}", _ln_silu_fn, [st["cv"]], [row(ln_g_f[l]), row(ln_b_f[l])], [ds], [f32], tr)
            dp, g_dw[l] = _dwconv_bwd(f"dwconv_bwd{l}", dcv.reshape(B, Lp, D), st["p"].reshape(B, Lp, D), dw_pad[l])
            dav, dag, dbv, dbg = _rowwise_vjp(f"glu_bwd{l}", _glu_fn, [st["av"], st["ag"]], [zero_row, zero_row], [dp.reshape(T, D)], [bf16, bf16], tr)
            g_bin[l] = jnp.concatenate([dbv, dbg], axis=1)
            esel, ecarry = scat_carry(early, ("g",)) if early else ([], None)
            dwin, *new = _dw_cols(f"glu_dwv{l}", st["u"], dav, Q // 2, tk, Q, carry=ecarry)
            after_scat(esel, new)
            esel, ecarry = scat_carry(early, ("u",)) if early else ([], None)
            dwin, *new = _dw_cols(f"glu_dwg{l}", st["u"], dag, Q // 2, tk, Q, q_off=Q // 2, into=dwin, carry=ecarry)
            after_scat(esel, new)
            pieces.append(("cin", l, dwin))
            du = _glu_du(f"glu_du{l}", dav, dag, W("cin", l), 0, tm)
            dh, g_mix[l], g_bout[l] = _rowwise_vjp(f"rms_mix_bwd{l}", _rms_res_bias_fn, [st["h_a"]], [row(norm_mix[l]), zero_row], [du, dh], [f32], tr)
        else:
            j = l - NA
            do, *new = _proj_dx(f"o_proj_dx{j}", dh, W("o", j), 0, tm, out_dtype=bf16, carry=carried(job, share_carry))
            after_share(job, new)
            pieces.append(("o", j, _dw_rows(f"o_proj_dw{j}", st["o"], dh, Q, tk)[0]))
            dq, dk1, dv1, g_qn[j], g_sink[j] = _attn_bwd(f"attn_bwd{j}", st["q"], kn, vh, row(q_norm[j]), sinks3[j], seqs(do))
            dknp, dvp = (dk1, dv1) if dknp is None else (dknp + dk1, dvp + dv1)
            dq = dq.reshape(T, D)
            pieces.append(("q", j, _dw_rows(f"q_proj_dw{j}", st["u"], dq, Q, tk)[0]))
            du = _proj_dx(f"q_proj_dx{j}", dq, W("q", j), 0, tm)[0]
            dh, g_mix[l] = _rowwise_vjp(f"rms_mix_bwd{l}", _rms_res_fn, [st["h_a"]], [row(norm_mix[l])], [du, dh], [f32], tr)
            if j == 0:
                dkv, g_kn = _kv_heads_bwd("kv_heads_bwd", st["kv"], row(k_norm), dknp.reshape(G, T, HEAD_DIM), dvp.reshape(G, T, HEAD_DIM), tm)
                pieces.append(("kv", 0, _dw_rows("kv_proj_dw", st["kvn"], dkv, Q, tk)[0]))
                dkvn = _proj_dx("kv_proj_dx", dkv, W("kv", 0), 0, tm)[0]
                dh, g_kvn = _rowwise_vjp("rms_kv_bwd", _rms_res_fn, [st["h_a"]], [row(kv_norm)], [dkvn, dh], [f32], tr)
        job = rs_begin(pieces)
    dh3 = dh.reshape(B, Lp, D)
    grad_x = dh3[:, N_META_ROWS:L]
    g_meta = jnp.sum(dh3[:, :N_META_ROWS], axis=0)

    sl, plan = xchg_carry(job)
    after_xchg(job, _comm_call("rs_exchange", sl, [plan]))
    sel, (sl, plan) = scat_carry(job, tuple(n_layers))
    after_scat(sel, _comm_call("rs_scatter", sl, [plan]))
    sl, plan = share_carry(job + early)
    after_share(job + early, _comm_call("rs_share", sl, [plan]))

    names = ["conv_w_in", "conv_w_out", "w_kv", "w_q", "w_o", "ffn_w_gate", "ffn_w_up", "ffn_w_down"]
    ws = dict(zip(names, big))
    ms = dict(zip(names, [m_conv_w_in, m_conv_w_out, m_w_kv, m_w_q, m_w_o, m_ffn_w_gate, m_ffn_w_up, m_ffn_w_down]))
    vs = dict(zip(names, [v_conv_w_in, v_conv_w_out, v_w_kv, v_w_q, v_w_o, v_ffn_w_gate, v_ffn_w_up, v_ffn_w_down]))
    out_g, out_d, out_m, out_v = {}, {}, {}, {}
    for nm, key in zip(names, ("cin", "cout", "kv", "q", "o", "g", "u", "d")):
        w, gsh = ws[nm], grad_slab[key]
        flat = lambda a: a.reshape(-1, w.shape[-1])
        g2, d2, m2, v2 = _adamw(f"adamw_{nm}", flat(w), flat(gsh), flat(ms[nm]), flat(vs[nm]))
        out_g[nm], out_d[nm], out_m[nm], out_v[nm] = (a.reshape(w.shape) for a in (g2, d2, m2, v2))

    small_names = ["norm_mix", "norm_ffn", "kv_norm", "k_norm", "q_norm", "attn_sinks", "meta_tokens", "conv_b_in", "conv_dw", "conv_ln_g", "conv_ln_b", "conv_b_out"]
    small_grads = [jnp.concatenate(g_mix, 0), jnp.concatenate(g_ffn, 0), g_kvn.reshape(-1), g_kn.reshape(-1), jnp.concatenate(g_qn, 0),
                   jnp.stack(g_sink).reshape(NB, G * R), g_meta, jnp.concatenate(g_bin, 0), jnp.stack(g_dw)[:, :CONV_TAPS],
                   jnp.concatenate(g_lng, 0), jnp.concatenate(g_lnb, 0), jnp.concatenate(g_bout, 0)]
    slab = _pack(small_grads, 1024)
    total = _sum_devices(_gather_all_devices(slab).reshape(8, slab.shape[0], 128))
    full_grads = _unpack(total, [g.shape for g in small_grads])
    small_w = dict(zip(small_names, [norm_mix, norm_ffn, kv_norm, k_norm, q_norm, attn_sinks, meta_tokens, conv_b_in, conv_dw, conv_ln_g, conv_ln_b, conv_b_out]))
    small_m = dict(zip(small_names, [m_norm_mix, m_norm_ffn, m_kv_norm, m_k_norm, m_q_norm, m_attn_sinks, m_meta_tokens, m_conv_b_in, m_conv_dw, m_conv_ln_g, m_conv_ln_b, m_conv_b_out]))
    small_v = dict(zip(small_names, [v_norm_mix, v_norm_ffn, v_kv_norm, v_k_norm, v_q_norm, v_attn_sinks, v_meta_tokens, v_conv_b_in, v_conv_dw, v_conv_ln_g, v_conv_ln_b, v_conv_b_out]))
    local_grads = []
    for nm, g in zip(small_names, full_grads):
        w = small_w[nm]
        if g.shape != w.shape:
            wq = w.shape[-1]
            g = lax.dynamic_slice_in_dim(g, my_q * wq, wq, axis=g.ndim - 1)
        local_grads.append(g)
    shapes = [small_w[nm].shape for nm in small_names]
    _, d_s, m_s, v_s = _adamw("adamw_small", _pack([small_w[nm] for nm in small_names], 1024), _pack(local_grads, 1024),
                           _pack([small_m[nm] for nm in small_names], 1024), _pack([small_v[nm] for nm in small_names], 1024))
    for nm, g, d_, m_, v_ in zip(small_names, local_grads, _unpack(d_s, shapes), _unpack(m_s, shapes), _unpack(v_s, shapes)):
        out_g[nm], out_d[nm], out_m[nm], out_v[nm] = g, d_, m_, v_

    order = ["meta_tokens", "norm_mix", "norm_ffn", "conv_w_in", "conv_b_in", "conv_dw", "conv_ln_g", "conv_ln_b", "conv_w_out", "conv_b_out",
             "kv_norm", "w_kv", "k_norm", "w_q", "q_norm", "attn_sinks", "w_o", "ffn_w_gate", "ffn_w_up", "ffn_w_down"]
    return (loss, grad_x, *[out_g[n] for n in order], *[out_d[n] for n in order], *[out_m[n] for n in order], *[out_v[n] for n in order])
```

```python
import functools

import jax
import jax.numpy as jnp
from jax import lax
from jax.experimental import pallas as pl
from jax.experimental.pallas import tpu as pltpu

f32, bf16 = jnp.float32, jnp.bfloat16

N_META_ROWS = 16
ATT_BLOCK = 128
HEAD_DIM = 64
N_KV = 4
CONV_TAPS = 31
CONV_PAD = 32
EPS = 1e-6
MASKED = -1e30
LR, B1, B2, ADAM_EPS, WD, STEP = 0.001, 0.9, 0.999, 1e-08, 0.01, 10
N_CHIPS = 4
VMEM_LIMIT_BYTES = 56 * 1024 * 1024
MESH = pl.DeviceIdType.MESH
ANY = pl.BlockSpec(memory_space=pl.ANY)
S = jax.ShapeDtypeStruct


def _pcall(body, **kw):
    return pl.pallas_call(body, **kw)


def _params(sem=None):
    return pltpu.CompilerParams(dimension_semantics=sem, vmem_limit_bytes=VMEM_LIMIT_BYTES)


def _tile(n, prefs):
    for p in prefs:
        if n % p == 0:
            return p
    return n


_DN = {"nn": (((1,), (0,)), ((), ())), "nt": (((1,), (1,)), ((), ())), "tn": (((0,), (0,)), ((), ()))}


def _matmul(name, mode, grid, a_ops, b_ops, x_ops, outs, terms, acc_shape, n_acc, epilogue, into=None, carry=None):
    na, nb, nx, no = len(a_ops), len(b_ops), len(x_ops), len(outs)
    nk = grid[2]
    slabs, plan = carry if carry is not None else ([], [])
    nc, ncp = len(slabs), _plan_copies(plan)
    n_in = na + nb + nx + (0 if into is None else 1) + nc

    def flat2d(v):
        return v.reshape(-1, v.shape[-1]) if v.ndim == 3 else v

    def dot(a, b):
        return lax.dot_general(a.astype(bf16), b.astype(bf16), _DN[mode], preferred_element_type=f32)

    def dots(a_refs, b_refs):
        parts = [None] * n_acc
        for ai, bi, ci in terms:
            a_ref, b_ref = a_refs[ai], b_refs[bi]
            if len(a_ref.shape) == 3 and len(b_ref.shape) == 3:
                for c in range(a_ref.shape[0]):
                    d = dot(a_ref[c], b_ref[c])
                    parts[ci] = d if parts[ci] is None else parts[ci] + d
            else:
                d = dot(flat2d(a_ref[...]), flat2d(b_ref[...]))
                parts[ci] = d if parts[ci] is None else parts[ci] + d
        return parts

    def finish(accs, x_refs, o_refs):
        res = epilogue(accs, [x[...] for x in x_refs])
        for o_ref, r in zip(o_refs, res):
            o_ref[...] = r.reshape(o_ref.shape).astype(o_ref.dtype)

    def compute(a_refs, b_refs, x_refs, o_refs, acc_refs):
        if nk == 1:
            finish(dots(a_refs, b_refs), x_refs, o_refs)
            return
        k = pl.program_id(2)

        @pl.when(k == 0)
        def _():
            for acc in acc_refs:
                acc[...] = jnp.zeros_like(acc)

        for acc, d in zip(acc_refs, dots(a_refs, b_refs)):
            acc[...] += d

        @pl.when(k == nk - 1)
        def _():
            finish([acc[...] for acc in acc_refs], x_refs, o_refs)

    def body(*refs):
        a_refs, b_refs = refs[:na], refs[na:na + nb]
        x_refs = refs[na + nb:na + nb + nx]
        o_refs = refs[n_in:n_in + no]
        bufs = refs[n_in + no:n_in + no + nc]
        scratch = refs[n_in + no + nc:]
        if not nc:
            compute(a_refs, b_refs, x_refs, o_refs, scratch)
            return
        acc_refs, (send, recv) = scratch[:-2], scratch[-2:]
        i, j, k = pl.program_id(0), pl.program_id(1), pl.program_id(2)

        @pl.when((i == 0) & (j == 0) & (k == 0))
        def _():
            _run_copies(bufs, plan, send, recv, start=True, wait=False)

        compute(a_refs, b_refs, x_refs, o_refs, acc_refs)

        @pl.when((i == grid[0] - 1) & (j == grid[1] - 1) & (k == nk - 1))
        def _():
            _run_copies(bufs, plan, send, recv, start=False, wait=True)

    ops = list(a_ops) + list(b_ops) + list(x_ops)
    aliases = {}
    if into is not None:
        ops.append((into, ANY))
        aliases = {len(ops) - 1: 0}
    for t, s in enumerate(slabs):
        ops.append((s, ANY))
        aliases[len(ops) - 1] = no + t
    outs = list(outs) + [(S(s.shape, s.dtype), ANY) for s in slabs]
    scratch = [pltpu.VMEM(acc_shape, f32)] * (n_acc if nk > 1 else 0)
    if nc:
        scratch += [pltpu.SemaphoreType.DMA((ncp,)), pltpu.SemaphoreType.DMA((ncp,))]
    sem = ("arbitrary",) * 3 if nc else ("parallel", "parallel", "arbitrary")
    res = _pcall(body, name=name, grid=grid, in_specs=[s for _, s in ops], out_specs=[s for _, s in outs],
                 out_shape=[s for s, _ in outs], scratch_shapes=scratch, input_output_aliases=aliases,
                 compiler_params=_params(sem))(*[a for a, _ in ops])
    return res


def _first(accs, xs):
    return (accs[0],)


def _proj(name, a, w, l, tm, bias=None, resid=None, out_dtype=f32, carry=None):
    T = a.shape[0]
    Q, _, Kq, N = w.shape
    tn = _tile(N, (512, 256, 128))
    x_ops, epi = [], _first
    if bias is not None:
        x_ops = [(bias, pl.BlockSpec((1, tn), lambda i, j, k: (0, j))), (resid, pl.BlockSpec((tm, tn), lambda i, j, k: (i, j)))]
        epi = lambda accs, xs: (accs[0] + xs[0] + xs[1],)
    elif resid is not None:
        x_ops = [(resid, pl.BlockSpec((tm, tn), lambda i, j, k: (i, j)))]
        epi = lambda accs, xs: (accs[0] + xs[0],)
    res = _matmul(name, "nn", (T // tm, N // tn, 1),
                  [(a, pl.BlockSpec((tm, Q * Kq), lambda i, j, k: (i, 0)))],
                  [(w, pl.BlockSpec((Q, None, Kq, tn), lambda i, j, k: (0, l, 0, j)))],
                  x_ops, [(S((T, N), out_dtype), pl.BlockSpec((tm, tn), lambda i, j, k: (i, j)))],
                  [(0, 0, 0)], (tm, tn), 1, epi, carry=carry)
    return res[0] if carry is None else (res[0], res[1:])


def _proj_dx(name, dy, w, l, tm, out_dtype=f32, carry=None):
    T, N = dy.shape
    Q, _, Kq, _ = w.shape
    return _matmul(name, "nt", (T // tm, 1, 1),
                   [(dy, pl.BlockSpec((tm, N), lambda i, j, k: (i, 0)))],
                   [(w, pl.BlockSpec((Q, None, Kq, N), lambda i, j, k: (0, l, 0, 0)))],
                   [], [(S((T, Q * Kq), out_dtype), pl.BlockSpec((tm, Q * Kq), lambda i, j, k: (i, 0)))],
                   [(0, 0, 0)], (tm, Q * Kq), 1, _first, carry=carry)


def _dw_rows(name, a, dy, Q, tk, carry=None):
    T, N = dy.shape
    tn = _tile(N, (512, 256, 128))
    if a.ndim == 2:
        K = a.shape[1]
        a_op, ni, acc_rows = (a, pl.BlockSpec((tk, K), lambda i, j, k: (k, 0))), 1, K
        osh, ospec = S((Q, K // Q, N), bf16), pl.BlockSpec((Q, K // Q, tn), lambda i, j, k: (0, 0, j))
    else:
        Kq = a.shape[2]
        a_op, ni, acc_rows = (a, pl.BlockSpec((None, tk, Kq), lambda i, j, k: (i, k, 0))), Q, Kq
        osh, ospec = S((Q, Kq, N), bf16), pl.BlockSpec((None, Kq, tn), lambda i, j, k: (i, 0, j))
    return _matmul(name, "tn", (ni, N // tn, T // tk), [a_op], [(dy, pl.BlockSpec((tk, tn), lambda i, j, k: (k, j)))],
                   [], [(osh, ospec)], [(0, 0, 0)], (acc_rows, tn), 1, _first, carry=carry)


def _dw_cols(name, a, dyc, Qc, tk, Q, q_off=0, into=None, carry=None):
    T, K = a.shape
    tkin = _tile(K, (512, 256, 128))
    if dyc.ndim == 3:
        Nq = dyc.shape[2]
        b_op = (dyc, pl.BlockSpec((None, tk, Nq), lambda i, j, k: (j, k, 0)))
    else:
        Nq = dyc.shape[1] // Qc
        b_op = (dyc, pl.BlockSpec((tk, Nq), lambda i, j, k: (k, j)))
    ospec = pl.BlockSpec((None, tkin, Nq), lambda i, j, k: (j + q_off, i, 0))
    return _matmul(name, "tn", (K // tkin, Qc, T // tk),
                   [(a, pl.BlockSpec((tk, tkin), lambda i, j, k: (k, i)))], [b_op], [],
                   [(S((Q, K, Nq), bf16), ospec)], [(0, 0, 0)], (tkin, Nq), 1, _first, into=into, carry=carry)


def _glu_fwd(name, u, w, l, b_in, tm, carry=None):
    T, D = u.shape
    Q, _, _, Cq = w.shape
    H = Q // 2

    def epi(accs, xs):
        av, ag = accs[0] + xs[0], accs[1] + xs[1]
        return av * jax.nn.sigmoid(ag), av, ag

    wspec = lambda off: pl.BlockSpec((None, None, D, Cq), lambda i, j, k: (j + off, l, 0, 0))
    bspec = lambda off: pl.BlockSpec((1, Cq), lambda i, j, k: (0, j + off))
    ospec = pl.BlockSpec((tm, Cq), lambda i, j, k: (i, j))
    return _matmul(name, "nn", (T // tm, H, 1),
                   [(u, pl.BlockSpec((tm, D), lambda i, j, k: (i, 0)))],
                   [(w, wspec(0)), (w, wspec(H))], [(b_in, bspec(0)), (b_in, bspec(H))],
                   [(S((T, H * Cq), f32), ospec), (S((T, H * Cq), bf16), ospec), (S((T, H * Cq), bf16), ospec)],
                   [(0, 0, 0), (0, 1, 1)], (tm, Cq), 2, epi, carry=carry)


def _glu_du(name, dav, dag, w, l, tm):
    T = dav.shape[0]
    Q, _, D, Cq = w.shape
    H = Q // 2
    tn = _tile(D, (512, 256, 128))
    aspec = pl.BlockSpec((tm, Cq), lambda i, j, k: (i, k))
    wspec = lambda off: pl.BlockSpec((None, None, tn, Cq), lambda i, j, k: (k + off, l, j, 0))
    return _matmul(name, "nt", (T // tm, D // tn, H), [(dav, aspec), (dag, aspec)],
                   [(w, wspec(0)), (w, wspec(H))], [],
                   [(S((T, D), f32), pl.BlockSpec((tm, tn), lambda i, j, k: (i, j)))],
                   [(0, 0, 0), (1, 1, 0)], (tm, tn), 1, _first)[0]


def _ffn_up(name, u, wg, wu, l, tm, carry=None):
    T, D = u.shape
    Q, _, _, Fq = wg.shape

    def epi(accs, xs):
        g, up = accs
        sg = jax.nn.sigmoid(g)
        silu = g * sg
        return up * (sg * (1.0 + g * (1.0 - sg))), silu, silu * up

    wspec = pl.BlockSpec((None, None, D, Fq), lambda i, j, k: (j, l, 0, 0))
    ospec = pl.BlockSpec((None, tm, Fq), lambda i, j, k: (j, i, 0))
    osh = S((Q, T, Fq), bf16)
    return _matmul(name, "nn", (T // tm, Q, 1), [(u, pl.BlockSpec((tm, D), lambda i, j, k: (i, 0)))],
                   [(wg, wspec), (wu, wspec)], [], [(osh, ospec)] * 3, [(0, 0, 0), (0, 1, 1)], (tm, Fq), 2, epi, carry=carry)


def _ffn_down(name, hid, wd, l, resid, tm, carry=None):
    Q, T, Fq = hid.shape
    D = wd.shape[3]
    tn = _tile(D, (512, 256, 128))
    return _matmul(name, "nn", (T // tm, D // tn, 1),
                   [(hid, pl.BlockSpec((Q, tm, Fq), lambda i, j, k: (0, i, 0)))],
                   [(wd, pl.BlockSpec((Q, None, Fq, tn), lambda i, j, k: (0, l, 0, j)))],
                   [(resid, pl.BlockSpec((tm, tn), lambda i, j, k: (i, j)))],
                   [(S((T, D), f32), pl.BlockSpec((tm, tn), lambda i, j, k: (i, j)))],
                   [(0, 0, 0)], (tm, tn), 1, lambda accs, xs: (accs[0] + xs[0],), carry=carry)


def _ffn_dhid(name, dy, wd, l, dgate_of, dup_of, tm, carry=None):
    T, D = dy.shape
    Q, _, Fq, _ = wd.shape

    def epi(accs, xs):
        return accs[0] * xs[0].astype(f32), accs[0] * xs[1].astype(f32)

    cspec = pl.BlockSpec((None, tm, Fq), lambda i, j, k: (j, i, 0))
    osh = S((Q, T, Fq), bf16)
    return _matmul(name, "nt", (T // tm, Q, 1), [(dy, pl.BlockSpec((tm, D), lambda i, j, k: (i, 0)))],
                   [(wd, pl.BlockSpec((None, None, Fq, D), lambda i, j, k: (j, l, 0, 0)))],
                   [(dgate_of, cspec), (dup_of, cspec)], [(osh, cspec)] * 2, [(0, 0, 0)], (tm, Fq), 1, epi, carry=carry)


def _ffn_du(name, dgate, dup, wg, wu, l, tm, carry=None):
    Q, T, Fq = dgate.shape
    D = wg.shape[2]
    tn = _tile(D, (512, 256, 128))
    tm = tm // 2 if tm % 32 == 0 and tm > 512 else tm
    aspec = pl.BlockSpec((Q, tm, Fq), lambda i, j, k: (0, i, 0))
    wspec = pl.BlockSpec((Q, None, tn, Fq), lambda i, j, k: (0, l, j, 0))
    return _matmul(name, "nt", (T // tm, D // tn, 1), [(dgate, aspec), (dup, aspec)], [(wg, wspec), (wu, wspec)], [],
                   [(S((T, D), f32), pl.BlockSpec((tm, tn), lambda i, j, k: (i, j)))],
                   [(0, 0, 0), (1, 1, 0)], (tm, tn), 1, _first, carry=carry)


def _rowwise(name, fn, rows, params, out_dtypes, tm):
    nr, npar = len(rows), len(params)
    T = rows[0].shape[0]
    shp = jax.eval_shape(fn, *[S((tm, r.shape[1]), f32) for r in rows], *[S(p.shape, f32) for p in params])

    def body(*refs):
        r = [x[...].astype(f32) for x in refs[:nr]]
        p = [x[...] for x in refs[nr:nr + npar]]
        for o_ref, o in zip(refs[nr + npar:], fn(*r, *p)):
            o_ref[...] = o.astype(o_ref.dtype)

    row_spec = lambda w: pl.BlockSpec((tm, w), lambda i: (i, 0))
    par_spec = lambda p: pl.BlockSpec(p.shape, lambda i: (0, 0))
    return _pcall(body, name=name, grid=(T // tm,),
                  in_specs=[row_spec(r.shape[1]) for r in rows] + [par_spec(p) for p in params],
                  out_specs=[row_spec(s.shape[1]) for s in shp],
                  out_shape=[S((T, s.shape[1]), dt) for s, dt in zip(shp, out_dtypes)],
                  compiler_params=_params(("parallel",)))(*rows, *params)


def _rowwise_vjp(name, fn, rows, params, cots, drow_dtypes, tm):
    nr, npar, nc = len(rows), len(params), len(cots)
    T = rows[0].shape[0]

    def body(*refs):
        r = [x[...].astype(f32) for x in refs[:nr]]
        p = [x[...] for x in refs[nr:nr + npar]]
        c = tuple(x[...].astype(f32) for x in refs[nr + npar:nr + npar + nc])
        o_refs = refs[nr + npar + nc:]
        _, vjp = jax.vjp(fn, *r, *p)
        grads = vjp(c)
        for o_ref, g in zip(o_refs[:nr], grads[:nr]):
            o_ref[...] = g.astype(o_ref.dtype)

        @pl.when(pl.program_id(0) == 0)
        def _():
            for o_ref in o_refs[nr:]:
                o_ref[...] = jnp.zeros_like(o_ref)

        for o_ref, g in zip(o_refs[nr:], grads[nr:]):
            o_ref[...] += g

    row_spec = lambda w: pl.BlockSpec((tm, w), lambda i: (i, 0))
    par_spec = lambda p: pl.BlockSpec(p.shape, lambda i: (0, 0))
    return _pcall(body, name=name, grid=(T // tm,),
                  in_specs=[row_spec(r.shape[1]) for r in rows] + [par_spec(p) for p in params] + [row_spec(c.shape[1]) for c in cots],
                  out_specs=[row_spec(r.shape[1]) for r in rows] + [par_spec(p) for p in params],
                  out_shape=[S(r.shape, dt) for r, dt in zip(rows, drow_dtypes)] + [S(p.shape, f32) for p in params],
                  compiler_params=_params(("arbitrary",)))(*rows, *params, *cots)


def _rms(h, g):
    return h * lax.rsqrt(jnp.mean(h * h, axis=-1, keepdims=True) + EPS) * g


def _rms_fn(h, g):
    return (_rms(h, g),)


def _rms_res_fn(h, g):
    return _rms(h, g), h


def _rms_res_bias_fn(h, g, b0):
    return _rms(h, g), h + b0


def _ln_silu_fn(c, g, b):
    mu = jnp.mean(c, axis=-1, keepdims=True)
    var = jnp.mean(jnp.square(c - mu), axis=-1, keepdims=True)
    y = (c - mu) * lax.rsqrt(var + EPS) * g + b
    return (y * jax.nn.sigmoid(y),)


def _glu_fn(av, ag, bv, bg):
    return ((av + bv) * jax.nn.sigmoid(ag + bg),)


def _loss_head(h, tgt, Lp, n_real, tm):
    T, D = h.shape

    def body(h_ref, t_ref, dy_ref, part_ref):
        i = pl.program_id(0)
        pos = (i * tm + lax.broadcasted_iota(jnp.int32, (tm, 1), 0)) % Lp
        real = (pos >= N_META_ROWS) & (pos < N_META_ROWS + n_real)
        err = jnp.where(real, h_ref[...] - t_ref[...], 0.0)
        dy_ref[...] = err * (1.0 / D)

        @pl.when(i == 0)
        def _():
            part_ref[...] = jnp.zeros_like(part_ref)

        part_ref[...] += jnp.sum(err * err, axis=0, keepdims=True)

    spec = pl.BlockSpec((tm, D), lambda i: (i, 0))
    return _pcall(body, name="loss_head", grid=(T // tm,), in_specs=[spec, spec],
                  out_specs=[spec, pl.BlockSpec((1, D), lambda i: (0, 0))],
                  out_shape=[S((T, D), f32), S((1, D), f32)], compiler_params=_params(("arbitrary",)))(h, tgt)


CONV_OFF = CONV_PAD - (CONV_TAPS - 1)
WIN_ROWS = ATT_BLOCK + CONV_PAD


def _phases(ph_ref, win):
    n = win.shape[0]
    for b in range(1, 8):
        ph_ref[b - 1, 0:n - 8, :] = win[b:n - 8 + b]

    def tap(o):
        a = (o // 8) * 8
        return win[a:a + ATT_BLOCK] if o % 8 == 0 else ph_ref[o % 8 - 1, a:a + ATT_BLOCK, :]

    return tap


def _phase_scratch(width):
    return pltpu.VMEM((7, WIN_ROWS - 8, width), f32)


def _dwconv_fwd(name, p, dw, carry=None):
    B, Lp, D = p.shape
    dc = _tile(D, (256, 128))
    slabs, plan = carry if carry is not None else ([], [])
    nc = len(slabs)

    def body(*refs):
        p_ref, w_ref = refs[:2]
        o_ref, bufs, ph_ref = refs[2 + nc], refs[3 + nc:3 + 2 * nc], refs[3 + 2 * nc]
        first = (pl.program_id(0) == 0) & (pl.program_id(1) == 0)
        last = (pl.program_id(0) == B - 1) & (pl.program_id(1) == D // dc - 1)
        if nc:
            @pl.when(first)
            def _():
                _run_copies(bufs, plan, refs[-2], refs[-1], start=True, wait=False)

        def tile(win, base):
            tap = _phases(ph_ref, win)
            acc = jnp.zeros((ATT_BLOCK, dc), f32)
            for k in range(CONV_TAPS):
                acc = acc + tap(CONV_OFF + k) * w_ref[k:k + 1, :]
            o_ref[pl.ds(base, ATT_BLOCK), :] = acc

        tile(jnp.concatenate([jnp.zeros((CONV_PAD, dc), f32), p_ref[0:ATT_BLOCK, :]], axis=0), 0)

        def step(r, carry):
            base = pl.multiple_of(r * ATT_BLOCK, ATT_BLOCK)
            tile(p_ref[pl.ds(pl.multiple_of(base - CONV_PAD, CONV_PAD), WIN_ROWS), :], base)
            return carry

        lax.fori_loop(1, Lp // ATT_BLOCK, step, 0)
        if nc:
            @pl.when(last)
            def _():
                _run_copies(bufs, plan, refs[-2], refs[-1], start=False, wait=True)

    seq = pl.BlockSpec((None, Lp, dc), lambda b, j: (b, 0, j))
    sems = [pltpu.SemaphoreType.DMA((_plan_copies(plan),))] * 2 if nc else []
    return _pcall(body, name=name, grid=(B, D // dc),
                  in_specs=[seq, pl.BlockSpec((CONV_PAD, dc), lambda b, j: (0, j))] + [ANY] * nc,
                  out_specs=[seq] + [ANY] * nc, out_shape=[S((B, Lp, D), f32)] + [S(s.shape, s.dtype) for s in slabs],
                  input_output_aliases={2 + t: 1 + t for t in range(nc)}, scratch_shapes=[_phase_scratch(dc)] + sems,
                  compiler_params=_params(("arbitrary", "arbitrary") if nc else ("parallel", "parallel")))(p, dw, *slabs)


def _dwconv_bwd(name, dcv, p, dw):
    B, Lp, D = p.shape
    dcw = _tile(D, (256, 128))
    nblk = Lp // ATT_BLOCK
    assert nblk >= 2
    zeros = lambda: jnp.zeros((CONV_PAD, dcw), f32)

    def body(dc_ref, p_ref, w_ref, dp_ref, ddw_ref, dph_ref, pph_ref, part_ref):
        @pl.when(pl.program_id(1) == 0)
        def _():
            ddw_ref[...] = jnp.zeros_like(ddw_ref)

        part_ref[...] = jnp.zeros_like(part_ref)

        def tile(dwin, pwin, base):
            dtap, ptap = _phases(dph_ref, dwin), _phases(pph_ref, pwin)
            dtile = dwin[0:ATT_BLOCK]
            acc = jnp.zeros((ATT_BLOCK, dcw), f32)
            for k in range(CONV_TAPS):
                acc = acc + dtap(CONV_TAPS - 1 - k) * w_ref[k:k + 1, :]
                part_ref[k] += jnp.sum((dtile * ptap(CONV_OFF + k)).reshape(ATT_BLOCK // 8, 8, dcw), axis=0)
            dp_ref[pl.ds(base, ATT_BLOCK), :] = acc

        tile(dc_ref[0:WIN_ROWS, :], jnp.concatenate([zeros(), p_ref[0:ATT_BLOCK, :]], axis=0), 0)

        def step(r, carry):
            base = pl.multiple_of(r * ATT_BLOCK, ATT_BLOCK)
            tile(dc_ref[pl.ds(base, WIN_ROWS), :], p_ref[pl.ds(pl.multiple_of(base - CONV_PAD, CONV_PAD), WIN_ROWS), :], base)
            return carry

        lax.fori_loop(1, nblk - 1, step, 0)
        last = Lp - ATT_BLOCK
        tile(jnp.concatenate([dc_ref[last:Lp, :], zeros()], axis=0), p_ref[last - CONV_PAD:Lp, :], last)
        ddw_ref[...] += jnp.sum(part_ref[...], axis=1)

    seq = pl.BlockSpec((None, Lp, dcw), lambda j, b: (b, 0, j))
    wsp = pl.BlockSpec((CONV_PAD, dcw), lambda j, b: (0, j))
    return _pcall(body, name=name, grid=(D // dcw, B), in_specs=[seq, seq, wsp], out_specs=[seq, wsp],
                  out_shape=[S((B, Lp, D), f32), S((CONV_PAD, D), f32)],
                  scratch_shapes=[_phase_scratch(dcw)] * 2 + [pltpu.VMEM((CONV_PAD, 8, dcw), f32)],
                  compiler_params=_params(("parallel", "arbitrary")))(dcv, p, dw)


def _band_start(n):
    return pl.multiple_of(jnp.maximum(n - 1, 0) * ATT_BLOCK, ATT_BLOCK)


def _attn_bias(R):
    row = jnp.arange(R * ATT_BLOCK)[:, None] % ATT_BLOCK
    col = jnp.arange(3 * ATT_BLOCK)[None, :]
    out = []
    for n in range(3):
        qpos = n * ATT_BLOCK + row
        meta_ok = (col < N_META_ROWS) & (col <= qpos)
        band_pos = max(n - 1, 0) * ATT_BLOCK + (col - ATT_BLOCK)
        diff = qpos - band_pos
        band_ok = (col >= ATT_BLOCK) & (diff >= 0) & (diff < ATT_BLOCK) & (band_pos >= N_META_ROWS)
        out.append(jnp.where(meta_ok | band_ok, 0.0, MASKED))
    return jnp.stack(out).astype(f32)


def _row_head(R):
    row = lax.broadcasted_iota(jnp.int32, (R * ATT_BLOCK, R), 0)
    lo = lax.broadcasted_iota(jnp.int32, (R * ATT_BLOCK, R), 1) * ATT_BLOCK
    return ((row >= lo) & (row < lo + ATT_BLOCK)).astype(f32)


def _attn_core(q, kn, vv, qg, sink_vec, row_head, bias):
    qn = _rms(q, qg)
    s = lax.dot_general(qn.astype(bf16), kn.astype(bf16), _DN["nt"], preferred_element_type=f32) * (HEAD_DIM ** -0.5)
    s = s + bias
    sink = jnp.sum(row_head * sink_vec, axis=-1, keepdims=True)
    m = lax.stop_gradient(jnp.maximum(jnp.max(s, axis=-1, keepdims=True), sink))
    p = jnp.exp(s - m)
    denom = jnp.sum(p, axis=-1, keepdims=True) + jnp.exp(sink - m)
    return jnp.dot((p / denom).astype(bf16), vv.astype(bf16), preferred_element_type=f32)


def _keys_of(ref, n):
    return jnp.concatenate([ref[0:ATT_BLOCK, :], ref[pl.ds(_band_start(n), 2 * ATT_BLOCK), :]], axis=0)


def _stack_heads(x, R):
    return x if R == 1 else jnp.concatenate([x[:, r * HEAD_DIM:(r + 1) * HEAD_DIM] for r in range(R)], axis=0)


def _unstack_heads(x, R):
    return x if R == 1 else jnp.concatenate([x[r * ATT_BLOCK:(r + 1) * ATT_BLOCK] for r in range(R)], axis=1)


KV_PER_STEP_FWD = 2
KV_PER_STEP_BWD = 1


def _attn_specs(R, Lp, per):
    qspec = pl.BlockSpec((None, ATT_BLOCK, per * R * HEAD_DIM), lambda g, b, n: (b, n, g))
    kspec = pl.BlockSpec((per, None, Lp, HEAD_DIM), lambda g, b, n: (g, b, 0, 0))
    gspec = pl.BlockSpec((1, HEAD_DIM), lambda g, b, n: (0, 0))
    sspec = pl.BlockSpec((per, 1, R), lambda g, b, n: (g, 0, 0))
    bspec = pl.BlockSpec((None, R * ATT_BLOCK, 3 * ATT_BLOCK), lambda g, b, n: (jnp.minimum(n, 2), 0, 0))
    return qspec, kspec, gspec, sspec, bspec


def _heads_of(x, gi, R):
    return x[:, gi * R * HEAD_DIM:(gi + 1) * R * HEAD_DIM]


def _attn_fwd(name, q, kn, v, qg, sinks):
    B, Lp, D = q.shape
    G = kn.shape[0]
    R = D // (G * HEAD_DIM)
    per = KV_PER_STEP_FWD
    assert G % per == 0
    qspec, kspec, gspec, sspec, bspec = _attn_specs(R, Lp, per)

    def body(q_ref, k_ref, v_ref, qg_ref, s_ref, b_ref, o_ref):
        n = pl.program_id(2)
        q_all, outs = q_ref[...], []
        for gi in range(per):
            o = _attn_core(_stack_heads(_heads_of(q_all, gi, R), R), _keys_of(k_ref.at[gi], n), _keys_of(v_ref.at[gi], n),
                           qg_ref[...], s_ref[gi], _row_head(R), b_ref[...])
            outs.append(_unstack_heads(o, R))
        o_ref[...] = jnp.concatenate(outs, axis=1).astype(o_ref.dtype)

    return _pcall(body, name=name, grid=(G // per, B, Lp // ATT_BLOCK), in_specs=[qspec, kspec, kspec, gspec, sspec, bspec],
                  out_specs=qspec, out_shape=S(q.shape, bf16),
                  compiler_params=_params(("parallel", "parallel", "parallel")))(q, kn, v, qg, sinks, _attn_bias(R))


def _attn_bwd(name, q, kn, v, qg, sinks, do):
    B, Lp, D = q.shape
    G = kn.shape[0]
    R = D // (G * HEAD_DIM)
    per = KV_PER_STEP_BWD
    qspec, kspec, gspec, sspec, bspec = _attn_specs(R, Lp, per)

    def body(q_ref, k_ref, v_ref, qg_ref, s_ref, b_ref, do_ref, dq_ref, dk_ref, dv_ref, dqg_ref, ds_ref):
        g, b, n = pl.program_id(0), pl.program_id(1), pl.program_id(2)

        @pl.when((g == 0) & (b == 0) & (n == 0))
        def _():
            dqg_ref[...] = jnp.zeros_like(dqg_ref)

        @pl.when((b == 0) & (n == 0))
        def _():
            ds_ref[...] = jnp.zeros_like(ds_ref)

        @pl.when(n == 0)
        def _():
            dk_ref[...] = jnp.zeros_like(dk_ref)
            dv_ref[...] = jnp.zeros_like(dv_ref)

        row_head, bias = _row_head(R), b_ref[...]
        q_all, do_all, dqs = q_ref[...], do_ref[...].astype(f32), []
        band = pl.ds(_band_start(n), 2 * ATT_BLOCK)
        for gi in range(per):
            _, vjp = jax.vjp(lambda q_, k_, v_, a_, s_: _attn_core(q_, k_, v_, a_, s_, row_head, bias),
                             _stack_heads(_heads_of(q_all, gi, R), R), _keys_of(k_ref.at[gi], n).astype(f32),
                             _keys_of(v_ref.at[gi], n).astype(f32), qg_ref[...], s_ref[gi])
            dq, dkk, dvv, dqg, dsk = vjp(_stack_heads(_heads_of(do_all, gi, R), R))
            dqs.append(_unstack_heads(dq, R))
            dqg_ref[...] += dqg
            ds_ref[gi] += dsk
            dk_ref[gi, band, :] += dkk[ATT_BLOCK:]
            dv_ref[gi, band, :] += dvv[ATT_BLOCK:]
            dk_ref[gi, 0:ATT_BLOCK, :] += dkk[:ATT_BLOCK]
            dv_ref[gi, 0:ATT_BLOCK, :] += dvv[:ATT_BLOCK]
        dq_ref[...] = jnp.concatenate(dqs, axis=1)

    return _pcall(body, name=name, grid=(G // per, B, Lp // ATT_BLOCK),
                  in_specs=[qspec, kspec, kspec, gspec, sspec, bspec, qspec],
                  out_specs=[qspec, kspec, kspec, gspec, sspec],
                  out_shape=[S(q.shape, f32), S(kn.shape, f32), S(v.shape, f32), S(qg.shape, f32), S(sinks.shape, f32)],
                  compiler_params=_params(("arbitrary", "arbitrary", "arbitrary")))(q, kn, v, qg, sinks, _attn_bias(R), do)


def _kv_heads(name, kv, kg, tm):
    T, W = kv.shape
    G = W // (2 * HEAD_DIM)

    def body(kv_ref, kg_ref, k_ref, v_ref):
        x = kv_ref[...]
        for g in range(G):
            k_ref[g] = _rms(x[:, g * HEAD_DIM:(g + 1) * HEAD_DIM], kg_ref[...]).astype(bf16)
            v_ref[g] = x[:, (G + g) * HEAD_DIM:(G + g + 1) * HEAD_DIM].astype(bf16)

    hspec = pl.BlockSpec((G, tm, HEAD_DIM), lambda i: (0, i, 0))
    return _pcall(body, name=name, grid=(T // tm,),
                  in_specs=[pl.BlockSpec((tm, W), lambda i: (i, 0)), pl.BlockSpec((1, HEAD_DIM), lambda i: (0, 0))],
                  out_specs=[hspec, hspec], out_shape=[S((G, T, HEAD_DIM), bf16)] * 2,
                  compiler_params=_params(("parallel",)))(kv, kg)


def _kv_heads_bwd(name, kv, kg, dkn, dv, tm):
    T, W = kv.shape
    G = W // (2 * HEAD_DIM)

    def body(kv_ref, kg_ref, dk_ref, dv_ref, o_ref, dkg_ref):
        @pl.when(pl.program_id(0) == 0)
        def _():
            dkg_ref[...] = jnp.zeros_like(dkg_ref)

        x = kv_ref[...]
        pieces = []
        for g in range(G):
            _, vjp = jax.vjp(_rms, x[:, g * HEAD_DIM:(g + 1) * HEAD_DIM], kg_ref[...])
            dk, dkg = vjp(dk_ref[g])
            pieces.append(dk)
            dkg_ref[...] += dkg
        o_ref[...] = jnp.concatenate(pieces + [dv_ref[g] for g in range(G)], axis=1)

    hspec = pl.BlockSpec((G, tm, HEAD_DIM), lambda i: (0, i, 0))
    gspec = pl.BlockSpec((1, HEAD_DIM), lambda i: (0, 0))
    return _pcall(body, name=name, grid=(T // tm,),
                  in_specs=[pl.BlockSpec((tm, W), lambda i: (i, 0)), gspec, hspec, hspec],
                  out_specs=[pl.BlockSpec((tm, W), lambda i: (i, 0)), gspec],
                  out_shape=[S((T, W), f32), S((1, HEAD_DIM), f32)], compiler_params=_params(("arbitrary",)))(kv, kg, dkn, dv)


def _place():
    x, y, c = lax.axis_index("x"), lax.axis_index("y"), lax.axis_index("c")
    chips = [(1 - x, y), (x, 1 - y), (1 - x, 1 - y)]
    return x, y, c, chips, [2 * cx + cy for cx, cy in chips]


def _remote(src, dst, send_sem, recv_sem, to):
    return pltpu.make_async_remote_copy(src_ref=src, dst_ref=dst, send_sem=send_sem, recv_sem=recv_sem,
                                        device_id=to, device_id_type=MESH)


def _into_slot(name, w, l, idx, slots, dtype):
    _, Rr, Cc = w.shape
    tr = _tile(Rr, (512, 256, 128, 64, 32, 16))

    def body(i_ref, x_ref, o_ref):
        o_ref[...] = x_ref[...].astype(o_ref.dtype)

    gs = pltpu.PrefetchScalarGridSpec(
        num_scalar_prefetch=1, grid=(Rr // tr,),
        in_specs=[pl.BlockSpec((None, tr, Cc), lambda i, i_ref: (l, i, 0))],
        out_specs=pl.BlockSpec((None, tr, Cc), lambda i, i_ref: (i_ref[0], i, 0)))
    return _pcall(body, name=name, grid_spec=gs, out_shape=S((slots, Rr, Cc), dtype), compiler_params=_params(("parallel",)))(idx, w)


_PLAN_COPIES = dict(ici=3, fwd=3, xchg=N_CHIPS, scat=3, share=1)


def _plan_copies(plan):
    return sum(_PLAN_COPIES[step[0]] for step in plan)


def _comm_copies(bufs, plan, send, recv):
    x, y, c, chips, qk = _place()
    me_q, sib = 2 * x + y, (x, y, 1 - c)
    starts, lands = [], []

    def add(src, dst, to, land):
        s = len(starts)
        starts.append(_remote(src, dst, send.at[s], recv.at[s], to))
        lands.append(_remote(land, land, send.at[s], recv.at[s], sib))

    for step in plan:
        kind = step[0]
        if kind == "ici":
            mine = bufs[step[1]].at[me_q, c]
            for k in range(3):
                add(mine, mine, (*chips[k], c), bufs[step[1]].at[qk[k], c])
        elif kind == "fwd":
            for k in range(3):
                got = bufs[step[1]].at[qk[k], c]
                add(got, got, sib, bufs[step[1]].at[qk[k], 1 - c])
        elif kind == "xchg":
            for q in range(N_CHIPS):
                add(bufs[step[1]].at[q, 1 - c], bufs[step[2]].at[q], sib, bufs[step[2]].at[q])
        elif kind == "scat":
            for k in range(3):
                add(bufs[step[1]].at[qk[k]], bufs[step[2]].at[k], (*chips[k], c), bufs[step[2]].at[k])
        else:
            mine = bufs[step[1]].at[step[2], c]
            add(mine, mine, sib, bufs[step[1]].at[step[2], 1 - c])
    return starts, lands


def _run_copies(bufs, plan, send, recv, start=True, wait=True):
    starts, lands = _comm_copies(bufs, plan, send, recv)
    if start:
        for cp in starts:
            cp.start()
    if wait:
        for cp in lands:
            cp.wait_recv()
        for cp in starts:
            cp.wait_send()


def _comm_call(name, slabs, phases):
    n = len(slabs)

    def body(*refs):
        bufs, sems = refs[n:2 * n], refs[2 * n:]
        for t, plan in enumerate(phases):
            _run_copies(bufs, plan, sems[2 * t], sems[2 * t + 1])

    sems = [pltpu.SemaphoreType.DMA((_plan_copies(plan),)) for plan in phases for _ in range(2)]
    return _pcall(body, name=name, in_specs=[ANY] * n, out_specs=[ANY] * n, out_shape=[S(s.shape, s.dtype) for s in slabs],
                  input_output_aliases={p: p for p in range(n)}, scratch_shapes=sems)(*slabs)


def _gather_all_devices(block):
    m_per, ncol = block.shape

    def body(x_ref, out_ref, send_sems, recv_sems, local_sem):
        x, y, c, chips, _ = _place()
        me, sib = (x, y, c), (x, y, 1 - c)

        def rows(px, py, pc):
            return out_ref.at[pl.ds((4 * px + 2 * py + pc) * m_per, m_per), :]

        def copy(k, blk, to, src=None):
            return _remote(rows(*blk) if src is None else src, rows(*blk), send_sems.at[k], recv_sems.at[k], to)

        mine = pltpu.make_async_copy(x_ref, rows(*me), local_sem)
        mine.start()
        first = [copy(0, me, sib, src=x_ref)] + [copy(1 + j, me, (*chip, c), src=x_ref) for j, chip in enumerate(chips)]
        for cp in first:
            cp.start()
        passed = [copy(4 + j, (*chip, c), sib) for j, chip in enumerate(chips)]
        for j, chip in enumerate(chips):
            copy(1 + j, (*chip, c), me).wait_recv()
            passed[j].start()
        copy(0, sib, me).wait_recv()
        for j, chip in enumerate(chips):
            copy(4 + j, (*chip, 1 - c), me).wait_recv()
        for cp in first + passed:
            cp.wait_send()
        mine.wait()

    vm = pl.BlockSpec(memory_space=pltpu.VMEM)
    return _pcall(body, name="gather_small_grads", in_specs=[vm], out_specs=vm,
                  out_shape=S((8 * m_per, ncol), block.dtype),
                  scratch_shapes=[pltpu.SemaphoreType.DMA((7,)), pltpu.SemaphoreType.DMA((7,)), pltpu.SemaphoreType.DMA],
                  compiler_params=pltpu.CompilerParams(vmem_limit_bytes=VMEM_LIMIT_BYTES))(block)


def _sum_pair(name, g, r1, c_idx):
    Q, _, Rr, Cc = g.shape
    tr = _tile(Rr, (512, 256, 128))

    def body(c_ref, g_ref, r_ref, o_ref):
        o_ref[...] = (g_ref[...].astype(f32) + r_ref[...].astype(f32)).astype(o_ref.dtype)

    spec = pl.BlockSpec((None, tr, Cc), lambda q, i, c_ref: (q, i, 0))
    gs = pltpu.PrefetchScalarGridSpec(
        num_scalar_prefetch=1, grid=(Q, Rr // tr),
        in_specs=[pl.BlockSpec((None, None, tr, Cc), lambda q, i, c_ref: (q, c_ref[0], i, 0)), spec], out_specs=spec)
    return _pcall(body, name=name, grid_spec=gs, out_shape=S((Q, Rr, Cc), bf16),
                  compiler_params=_params(("parallel", "parallel")))(c_idx, g, r1)


def _sum_owner(name, s, r2, qc_idx, l, nl, into):
    Q, Rr, Cc = s.shape
    tr = _tile(Rr, (512, 256, 128))

    def body(q_ref, s_ref, r_ref, *rest):
        o_ref = rest[-1]
        o_ref[...] = ((s_ref[...].astype(f32) + r_ref[0].astype(f32)) + r_ref[1].astype(f32)) + r_ref[2].astype(f32)

    in_specs = [pl.BlockSpec((None, tr, Cc), lambda i, q_ref: (q_ref[0], i, 0)), pl.BlockSpec((3, tr, Cc), lambda i, q_ref: (0, i, 0))]
    gs = pltpu.PrefetchScalarGridSpec(
        num_scalar_prefetch=1, grid=(Rr // tr,), in_specs=in_specs + ([] if into is None else [ANY]),
        out_specs=pl.BlockSpec((None, None, tr, Cc), lambda i, q_ref: (l, q_ref[1], i, 0)))
    return _pcall(body, name=name, grid_spec=gs, out_shape=S((nl, 2, Rr, Cc), f32),
                  input_output_aliases={} if into is None else {3: 0},
                  compiler_params=_params(("parallel",)))(qc_idx, s, r2, *([] if into is None else [into]))


def _sum_devices(stack):
    n, M, C = stack.shape

    def body(s_ref, o_ref):
        acc = s_ref[0]
        for d in range(1, n):
            acc = acc + s_ref[d]
        o_ref[...] = acc

    return _pcall(body, name="sum_small_grads", out_shape=S((M, C), f32), compiler_params=_params())(stack)


def _adamw(name, w, g, m, v):
    Rr, Cc = w.shape
    tr = _tile(Rr, (256, 128, 64, 32, 16, 8))

    def body(w_ref, g_ref, m_ref, v_ref, go_ref, d_ref, mo_ref, vo_ref):
        g_ = g_ref[...]
        go_ref[...] = g_
        m_ = B1 * m_ref[...] + (1.0 - B1) * g_
        v_ = B2 * v_ref[...] + (1.0 - B2) * jnp.square(g_)
        m_hat = m_ / (1.0 - B1 ** STEP)
        v_hat = v_ / (1.0 - B2 ** STEP)
        d_ref[...] = -LR * (m_hat / (jnp.sqrt(v_hat) + ADAM_EPS) + WD * w_ref[...])
        mo_ref[...] = m_
        vo_ref[...] = v_

    spec = pl.BlockSpec((tr, Cc), lambda i: (i, 0))
    return _pcall(body, name=name, grid=(Rr // tr,), in_specs=[spec] * 4, out_specs=[spec] * 4,
                  out_shape=[S((Rr, Cc), f32)] * 4, compiler_params=_params(("parallel",)))(w, g, m, v)


def _pack(arrs, multiple):
    flat = jnp.concatenate([a.reshape(-1) for a in arrs])
    pad = (-flat.shape[0]) % multiple
    return jnp.pad(flat, (0, pad)).reshape(-1, 128)


def _unpack(slab, shapes):
    flat, out, o = slab.reshape(-1), [], 0
    for shp in shapes:
        n = 1
        for d in shp:
            n *= d
        out.append(flat[o:o + n].reshape(shp))
        o += n
    return out


def kernel(x, meta_tokens, norm_mix, norm_ffn, conv_w_in, conv_b_in, conv_dw, conv_ln_g, conv_ln_b, conv_w_out, conv_b_out, kv_norm, w_kv, k_norm, w_q, q_norm, attn_sinks, w_o, ffn_w_gate, ffn_w_up, ffn_w_down, loss_target, m_meta_tokens, m_norm_mix, m_norm_ffn, m_conv_w_in, m_conv_b_in, m_conv_dw, m_conv_ln_g, m_conv_ln_b, m_conv_w_out, m_conv_b_out, m_kv_norm, m_w_kv, m_k_norm, m_w_q, m_q_norm, m_attn_sinks, m_w_o, m_ffn_w_gate, m_ffn_w_up, m_ffn_w_down, v_meta_tokens, v_norm_mix, v_norm_ffn, v_conv_w_in, v_conv_b_in, v_conv_dw, v_conv_ln_g, v_conv_ln_b, v_conv_w_out, v_conv_b_out, v_kv_norm, v_w_kv, v_k_norm, v_w_q, v_q_norm, v_attn_sinks, v_w_o, v_ffn_w_gate, v_ffn_w_up, v_ffn_w_down):
    Q = N_CHIPS
    B, SEQ, D = x.shape
    L = N_META_ROWS + SEQ
    Lp = -(-L // ATT_BLOCK) * ATT_BLOCK
    T = B * Lp
    NA, NB = conv_w_in.shape[0], w_q.shape[0]
    NL = NA + NB
    Dq = D // Q
    G = N_KV
    R = D // (HEAD_DIM * G)
    KVW = w_kv.shape[1]
    assert NA % 2 == 0 and NB % 2 == 0 and NL % 2 == 0 and (D // Q) % 32 == 0
    tm = _tile(T, (1088, 544, 512, 256, 128))
    tk = _tile(T, (2176, 1088, 544, 512, 256, 128))
    tr = _tile(T, (272, 256, 128))
    my_c = lax.axis_index("c").astype(jnp.int32).reshape(1)
    my_q = (2 * lax.axis_index("x") + lax.axis_index("y")).astype(jnp.int32)
    my_qc = jnp.concatenate([my_q.reshape(1), my_c])

    small_shapes = [meta_tokens.shape, conv_b_in.shape, conv_dw.shape, conv_ln_g.shape, conv_ln_b.shape, conv_b_out.shape]
    small = _pack([meta_tokens, conv_b_in, conv_dw, conv_ln_g, conv_ln_b, conv_b_out], 2048)
    big = [conv_w_in, conv_w_out, w_kv, w_q, w_o, ffn_w_gate, ffn_w_up, ffn_w_down]
    keyed = dict(cin=conv_w_in, cout=conv_w_out, kv=w_kv[None], q=w_q, o=w_o, g=ffn_w_gate, u=ffn_w_up, d=ffn_w_down,
                 small=small[None])
    slabs, where = [], {}
    for key, w3 in keyed.items():
        for l in range(w3.shape[0]):
            s = _into_slot(f"own_{key}{l}", w3, l, my_qc, Q, f32 if key == "small" else bf16)
            where[key, l] = len(slabs)
            slabs.append(s.reshape(Q, 2, s.shape[1] // 2, s.shape[2]))

    def W(key, l):
        s = slabs[where[key, l]]
        return s.reshape(Q, 1, 2 * s.shape[2], s.shape[3])

    def layer_slabs(l):
        j = l - NA
        mix = [("cin", l), ("cout", l)] if l < NA else [("q", j), ("o", j)] + ([("kv", 0)] if j == 0 else [])
        return [where[k] for k in mix], [where["g", l], where["u", l]], [where["d", l]]

    def carry_of(ici, fwd):
        idxs = sorted(set(ici) | set(fwd))
        return idxs, ([slabs[i] for i in idxs], [("ici", idxs.index(i)) for i in ici] + [("fwd", idxs.index(i)) for i in fwd])

    def put_back(idxs, new):
        for i, s in zip(idxs, new):
            slabs[i] = s

    assert NA >= 1
    first = [where["cin", 0], where["small", 0]]
    idxs, (sl, plan) = carry_of(first, first)
    put_back(idxs, _comm_call("gather_first", sl, [[s for s in plan if s[0] == "ici"], [s for s in plan if s[0] == "fwd"]]))
    parts = [_unpack(slabs[where["small", 0]][q], small_shapes) for q in range(Q)]
    meta_f, b_in_f, dw_f, ln_g_f, ln_b_f, b_out_f = [jnp.concatenate([parts[q][i] for q in range(Q)], axis=-1) for i in range(6)]
    dw_pad = jnp.pad(dw_f, ((0, 0), (0, CONV_PAD - CONV_TAPS), (0, 0)))

    h = jnp.concatenate([jnp.broadcast_to(meta_f[None], (B, N_META_ROWS, D)), x, jnp.zeros((B, Lp - L, D), f32)], axis=1).reshape(T, D)
    tgt = jnp.pad(loss_target, ((0, 0), (N_META_ROWS, Lp - L), (0, 0))).reshape(T, D)
    row = lambda a: a.reshape(1, -1)
    seqs = lambda a: a.reshape(B, Lp, a.shape[-1])
    by_seq = lambda a: a.reshape(G, B, Lp, HEAD_DIM)
    saved = []
    sinks3 = attn_sinks.reshape(NB, G, 1, R)
    def carrying(ici, fwd):
        idxs, carry = carry_of(ici, fwd)
        return idxs, (carry if idxs else None)

    for l in range(NL):
        st = {"h_a": h}
        u = _rowwise(f"rms_mix{l}", _rms_fn, [h], [row(norm_mix[l])], [bf16], tr)[0]
        st["u"] = u
        mix_s, (g_s, u_s), d_s = layer_slabs(l)
        idxs, carry = carrying([where["cout", 0], g_s], []) if l == 0 else carrying([u_s], [g_s])
        second = [u_s] + d_s if l == 0 else [u_s]
        if l < NA:
            p, av, ag, *new = _glu_fwd(f"glu{l}", u, W("cin", l), 0, b_in_f[l:l + 1], tm, carry=carry)
            put_back(idxs, new)
            idxs, carry = carrying([u_s] + d_s, [where["cout", 0], g_s]) if l == 0 else ([], None)
            cv, *new = _dwconv_fwd(f"dwconv{l}", p.reshape(B, Lp, D), dw_pad[l], carry=carry)
            put_back(idxs, new)
            cv = cv.reshape(T, D)
            s = _rowwise(f"ln_silu{l}", _ln_silu_fn, [cv], [row(ln_g_f[l]), row(ln_b_f[l])], [bf16], tr)[0]
            idxs2, carry2 = carrying([], second)
            res = _proj(f"conv_out{l}", s, W("cout", l), 0, tm, bias=row(b_out_f[l]), resid=h, carry=carry2)
            h, new = res if carry2 is not None else (res, [])
            put_back(idxs2, new)
            st.update(av=av, ag=ag, p=p, cv=cv, s=s)
        else:
            j = l - NA
            if j == 0:
                kvn = _rowwise("rms_kv", _rms_fn, [h], [row(kv_norm)], [bf16], tr)[0]
                kv = _proj("kv_proj", kvn, W("kv", 0), 0, tm)
                kn, vh = _kv_heads("kv_heads", kv, row(k_norm), tm)
                kn, vh = by_seq(kn), by_seq(vh)
                st.update(kvn=kvn, kv=kv)
            res = _proj(f"q_proj{j}", u, W("q", j), 0, tm, carry=carry)
            q, new = res if carry is not None else (res, [])
            put_back(idxs, new)
            q = seqs(q)
            o = _attn_fwd(f"attn{j}", q, kn, vh, row(q_norm[j]), sinks3[j]).reshape(T, D)
            idxs2, carry2 = carrying([], second)
            res = _proj(f"o_proj{j}", o, W("o", j), 0, tm, resid=h, carry=carry2)
            h, new = res if carry2 is not None else (res, [])
            put_back(idxs2, new)
            st.update(q=q, o=o)
        st["h_b"] = h
        u2 = _rowwise(f"rms_ffn{l}", _rms_fn, [h], [row(norm_ffn[l])], [bf16], tr)[0]
        nmix, ngu, nd = layer_slabs(l + 1) if l + 1 < NL else ([], [], [])
        idxs, carry = carrying(nmix + nd, [])
        dgate_of, dup_of, hid, *new = _ffn_up(f"ffn_up{l}", u2, W("g", l), W("u", l), 0, tm, carry=carry)
        put_back(idxs, new)
        idxs, carry = carrying(ngu[:1], nmix + nd)
        h, *new = _ffn_down(f"ffn_down{l}", hid, W("d", l), 0, h, tm, carry=carry)
        put_back(idxs, new)
        st.update(u2=u2, dgate_of=dgate_of, dup_of=dup_of, hid=hid)
        saved.append(st)

    dh, part = _loss_head(h, tgt, Lp, SEQ, tr)
    loss = lax.psum(0.5 / D * jnp.sum(part), ("x", "y", "c"))

    grad_slab = {}
    n_layers = dict(cin=NA, cout=NA, kv=1, q=NB, o=NB, g=NL, u=NL, d=NL)

    def rs_begin(pieces):
        job = []
        for key, lay, g in pieces:
            g4 = g.reshape(Q, 2, g.shape[1] // 2, g.shape[2])
            job.append(dict(key=key, l=lay, g=g4, r1=lax.empty((Q,) + g4.shape[2:], bf16)))
        return job

    def xchg_carry(job):
        n = len(job)
        return [p["g"] for p in job] + [p["r1"] for p in job], [("xchg", t, n + t) for t in range(n)]

    def after_xchg(job, new):
        n = len(job)
        for t, p in enumerate(job):
            p["s"] = _sum_pair(f"sum_pair_{p['key']}{p['l']}", new[t], new[n + t], my_c)
            p["r2"] = lax.empty((3,) + p["s"].shape[1:], bf16)

    def scat_carry(job, keys):
        sel = [p for p in job if p["key"] in keys]
        n = len(sel)
        return sel, ([p["s"] for p in sel] + [p["r2"] for p in sel], [("scat", t, n + t) for t in range(n)])

    def after_scat(sel, new):
        n = len(sel)
        for t, p in enumerate(sel):
            key = p["key"]
            grad_slab[key] = _sum_owner(f"sum_owner_{key}{p['l']}", new[t], new[n + t], my_qc, p["l"], n_layers[key], grad_slab.get(key))

    def share_carry(job):
        return [grad_slab[p["key"]] for p in job], [("share", t, p["l"]) for t, p in enumerate(job)]

    def after_share(job, new):
        for p, s in zip(job, new):
            grad_slab[p["key"]] = s

    def carried(job, make):
        return make(job) if job else None

    def merge(*carries):
        slabs_, plan_ = [], []
        for c in carries:
            if c is not None:
                off = len(slabs_)
                slabs_ += c[0]
                plan_ += [(s[0], s[1] + off, s[2]) if s[0] == "share" else (s[0],) + tuple(i + off for i in s[1:]) for s in c[1]]
        return (slabs_, plan_) if slabs_ else None

    MIXER = ("cin", "cout", "q", "o", "kv")
    g_mix, g_ffn = [None] * NL, [None] * NL
    g_bin, g_dw, g_lng, g_lnb, g_bout = ([None] * NA for _ in range(5))
    g_qn, g_sink = [None] * NB, [None] * NB
    dknp = dvp = None
    zero_row = jnp.zeros((1, D), f32)
    job = []
    early = []
    for l in reversed(range(NL)):
        st = saved[l]
        pieces = []
        dgate, dup, *new = _ffn_dhid(f"ffn_dhid{l}", dh, W("d", l), 0, st["dgate_of"], st["dup_of"], tm, carry=carried(job, xchg_carry))
        if job:
            after_xchg(job, new)
        for nm, key, fn in (("dwd", "d", lambda c: _dw_rows(f"ffn_dwd{l}", st["hid"], dh, Q, tk, carry=c)),
                            ("dwg", "g", lambda c: _dw_cols(f"ffn_dwg{l}", st["u2"], dgate, Q, tk, Q, carry=c)),
                            ("dwu", "u", lambda c: _dw_cols(f"ffn_dwu{l}", st["u2"], dup, Q, tk, Q, carry=c))):
            sel, carry = scat_carry(job, (key,)) if job else ([], None)
            dw, *new = fn(carry)
            after_scat(sel, new)
            pieces.append((key, l, dw))
        if l == 0 and l < NA:
            early, pieces = rs_begin(pieces), []
        sel, carry = scat_carry(job, MIXER) if job else ([], None)
        du2, *new = _ffn_du(f"ffn_du{l}", dgate, dup, W("g", l), W("u", l), 0, tm, carry=merge(carry, carried(early, xchg_carry)))
        after_scat(sel, new[:2 * len(sel)])
        if early:
            after_xchg(early, new[2 * len(sel):])
        dh, g_ffn[l] = _rowwise_vjp(f"rms_ffn_bwd{l}", _rms_res_fn, [st["h_b"]], [row(norm_ffn[l])], [du2, dh], [f32], tr)
        if l < NA:
            esel, ecarry = scat_carry(early, ("d",)) if early else ([], None)
            ds, *new = _proj_dx(f"conv_out_dx{l}", dh, W("cout", l), 0, tm, carry=merge(carried(job, share_carry), ecarry))
            after_share(job, new[:len(job)])
            after_scat(esel, new[len(job):])
            pieces.append(("cout", l, _dw_rows(f"conv_out_dw{l}", st["s"], dh, Q, tk)[0]))
            dcv, g_lng[l], g_lnb[l] = _rowwise_vjp(f"ln_silu_bwd{l}", _ln_silu_fn, [st["cv"]], [row(ln_g_f[l]), row(ln_b_f[l])], [ds], [f32], tr)
            dp, g_dw[l] = _dwconv_bwd(f"dwconv_bwd{l}", dcv.reshape(B, Lp, D), st["p"].reshape(B, Lp, D), dw_pad[l])
            dav, dag, dbv, dbg = _rowwise_vjp(f"glu_bwd{l}", _glu_fn, [st["av"], st["ag"]], [zero_row, zero_row], [dp.reshape(T, D)], [bf16, bf16], tr)
            g_bin[l] = jnp.concatenate([dbv, dbg], axis=1)
            esel, ecarry = scat_carry(early, ("g",)) if early else ([], None)
            dwin, *new = _dw_cols(f"glu_dwv{l}", st["u"], dav, Q // 2, tk, Q, carry=ecarry)
            after_scat(esel, new)
            esel, ecarry = scat_carry(early, ("u",)) if early else ([], None)
            dwin, *new = _dw_cols(f"glu_dwg{l}", st["u"], dag, Q // 2, tk, Q, q_off=Q // 2, into=dwin, carry=ecarry)
            after_scat(esel, new)
            pieces.append(("cin", l, dwin))
            du = _glu_du(f"glu_du{l}", dav, dag, W("cin", l), 0, tm)
            dh, g_mix[l], g_bout[l] = _rowwise_vjp(f"rms_mix_bwd{l}", _rms_res_bias_fn, [st["h_a"]], [row(norm_mix[l]), zero_row], [du, dh], [f32], tr)
        else:
            j = l - NA
            do, *new = _proj_dx(f"o_proj_dx{j}", dh, W("o", j), 0, tm, out_dtype=bf16, carry=carried(job, share_carry))
            after_share(job, new)
            pieces.append(("o", j, _dw_rows(f"o_proj_dw{j}", st["o"], dh, Q, tk)[0]))
            dq, dk1, dv1, g_qn[j], g_sink[j] = _attn_bwd(f"attn_bwd{j}", st["q"], kn, vh, row(q_norm[j]), sinks3[j], seqs(do))
            dknp, dvp = (dk1, dv1) if dknp is None else (dknp + dk1, dvp + dv1)
            dq = dq.reshape(T, D)
            pieces.append(("q", j, _dw_rows(f"q_proj_dw{j}", st["u"], dq, Q, tk)[0]))
            du = _proj_dx(f"q_proj_dx{j}", dq, W("q", j), 0, tm)[0]
            dh, g_mix[l] = _rowwise_vjp(f"rms_mix_bwd{l}", _rms_res_fn, [st["h_a"]], [row(norm_mix[l])], [du, dh], [f32], tr)
            if j == 0:
                dkv, g_kn = _kv_heads_bwd("kv_heads_bwd", st["kv"], row(k_norm), dknp.reshape(G, T, HEAD_DIM), dvp.reshape(G, T, HEAD_DIM), tm)
                pieces.append(("kv", 0, _dw_rows("kv_proj_dw", st["kvn"], dkv, Q, tk)[0]))
                dkvn = _proj_dx("kv_proj_dx", dkv, W("kv", 0), 0, tm)[0]
                dh, g_kvn = _rowwise_vjp("rms_kv_bwd", _rms_res_fn, [st["h_a"]], [row(kv_norm)], [dkvn, dh], [f32], tr)
        job = rs_begin(pieces)
    dh3 = dh.reshape(B, Lp, D)
    grad_x = dh3[:, N_META_ROWS:L]
    g_meta = jnp.sum(dh3[:, :N_META_ROWS], axis=0)

    sl, plan = xchg_carry(job)
    after_xchg(job, _comm_call("rs_exchange", sl, [plan]))
    sel, (sl, plan) = scat_carry(job, tuple(n_layers))
    after_scat(sel, _comm_call("rs_scatter", sl, [plan]))
    sl, plan = share_carry(job + early)
    after_share(job + early, _comm_call("rs_share", sl, [plan]))

    names = ["conv_w_in", "conv_w_out", "w_kv", "w_q", "w_o", "ffn_w_gate", "ffn_w_up", "ffn_w_down"]
    ws = dict(zip(names, big))
    ms = dict(zip(names, [m_conv_w_in, m_conv_w_out, m_w_kv, m_w_q, m_w_o, m_ffn_w_gate, m_ffn_w_up, m_ffn_w_down]))
    vs = dict(zip(names, [v_conv_w_in, v_conv_w_out, v_w_kv, v_w_q, v_w_o, v_ffn_w_gate, v_ffn_w_up, v_ffn_w_down]))
    out_g, out_d, out_m, out_v = {}, {}, {}, {}
    for nm, key in zip(names, ("cin", "cout", "kv", "q", "o", "g", "u", "d")):
        w, gsh = ws[nm], grad_slab[key]
        flat = lambda a: a.reshape(-1, w.shape[-1])
        g2, d2, m2, v2 = _adamw(f"adamw_{nm}", flat(w), flat(gsh), flat(ms[nm]), flat(vs[nm]))
        out_g[nm], out_d[nm], out_m[nm], out_v[nm] = (a.reshape(w.shape) for a in (g2, d2, m2, v2))

    small_names = ["norm_mix", "norm_ffn", "kv_norm", "k_norm", "q_norm", "attn_sinks", "meta_tokens", "conv_b_in", "conv_dw", "conv_ln_g", "conv_ln_b", "conv_b_out"]
    small_grads = [jnp.concatenate(g_mix, 0), jnp.concatenate(g_ffn, 0), g_kvn.reshape(-1), g_kn.reshape(-1), jnp.concatenate(g_qn, 0),
                   jnp.stack(g_sink).reshape(NB, G * R), g_meta, jnp.concatenate(g_bin, 0), jnp.stack(g_dw)[:, :CONV_TAPS],
                   jnp.concatenate(g_lng, 0), jnp.concatenate(g_lnb, 0), jnp.concatenate(g_bout, 0)]
    slab = _pack(small_grads, 1024)
    total = _sum_devices(_gather_all_devices(slab).reshape(8, slab.shape[0], 128))
    full_grads = _unpack(total, [g.shape for g in small_grads])
    small_w = dict(zip(small_names, [norm_mix, norm_ffn, kv_norm, k_norm, q_norm, attn_sinks, meta_tokens, conv_b_in, conv_dw, conv_ln_g, conv_ln_b, conv_b_out]))
    small_m = dict(zip(small_names, [m_norm_mix, m_norm_ffn, m_kv_norm, m_k_norm, m_q_norm, m_attn_sinks, m_meta_tokens, m_conv_b_in, m_conv_dw, m_conv_ln_g, m_conv_ln_b, m_conv_b_out]))
    small_v = dict(zip(small_names, [v_norm_mix, v_norm_ffn, v_kv_norm, v_k_norm, v_q_norm, v_attn_sinks, v_meta_tokens, v_conv_b_in, v_conv_dw, v_conv_ln_g, v_conv_ln_b, v_conv_b_out]))
    local_grads = []
    for nm, g in zip(small_names, full_grads):
        w = small_w[nm]
        if g.shape != w.shape:
            wq = w.shape[-1]
            g = lax.dynamic_slice_in_dim(g, my_q * wq, wq, axis=g.ndim - 1)
        local_grads.append(g)
    shapes = [small_w[nm].shape for nm in small_names]
    _, d_s, m_s, v_s = _adamw("adamw_small", _pack([small_w[nm] for nm in small_names], 1024), _pack(local_grads, 1024),
                           _pack([small_m[nm] for nm in small_names], 1024), _pack([small_v[nm] for nm in small_names], 1024))
    for nm, g, d_, m_, v_ in zip(small_names, local_grads, _unpack(d_s, shapes), _unpack(m_s, shapes), _unpack(v_s, shapes)):
        out_g[nm], out_d[nm], out_m[nm], out_v[nm] = g, d_, m_, v_

    order = ["meta_tokens", "norm_mix", "norm_ffn", "conv_w_in", "conv_b_in", "conv_dw", "conv_ln_g", "conv_ln_b", "conv_w_out", "conv_b_out",
             "kv_norm", "w_kv", "k_norm", "w_q", "q_norm", "attn_sinks", "w_o", "ffn_w_gate", "ffn_w_up", "ffn_w_down"]
    return (loss, grad_x, *[out_g[n] for n in order], *[out_d[n] for n in order], *[out_m[n] for n in order], *[out_v[n] for n in order])
```

```python
import functools

import jax
import jax.numpy as jnp
from jax import lax
from jax.experimental import pallas as pl
from jax.experimental.pallas import tpu as pltpu

f32, bf16 = jnp.float32, jnp.bfloat16

N_META_ROWS = 16
ATT_BLOCK = 128
HEAD_DIM = 64
N_KV = 4
CONV_TAPS = 31
CONV_PAD = 32
EPS = 1e-6
MASKED = -1e30
LR, B1, B2, ADAM_EPS, WD, STEP = 0.001, 0.9, 0.999, 1e-08, 0.01, 10
N_CHIPS = 4
VMEM_LIMIT_BYTES = 56 * 1024 * 1024
MESH = pl.DeviceIdType.MESH
ANY = pl.BlockSpec(memory_space=pl.ANY)
S = jax.ShapeDtypeStruct


def _pcall(body, **kw):
    return pl.pallas_call(body, **kw)


def _params(sem=None):
    return pltpu.CompilerParams(dimension_semantics=sem, vmem_limit_bytes=VMEM_LIMIT_BYTES)


def _tile(n, prefs):
    for p in prefs:
        if n % p == 0:
            return p
    return n


_DN = {"nn": (((1,), (0,)), ((), ())), "nt": (((1,), (1,)), ((), ())), "tn": (((0,), (0,)), ((), ()))}


def _matmul(name, mode, grid, a_ops, b_ops, x_ops, outs, terms, acc_shape, n_acc, epilogue, into=None, carry=None):
    na, nb, nx, no = len(a_ops), len(b_ops), len(x_ops), len(outs)
    nk = grid[2]
    slabs, plan = carry if carry is not None else ([], [])
    nc, ncp = len(slabs), _plan_copies(plan)
    n_in = na + nb + nx + (0 if into is None else 1) + nc

    def flat2d(v):
        return v.reshape(-1, v.shape[-1]) if v.ndim == 3 else v

    def dot(a, b):
        return lax.dot_general(a.astype(bf16), b.astype(bf16), _DN[mode], preferred_element_type=f32)

    def dots(a_refs, b_refs):
        parts = [None] * n_acc
        shared = {}

        def lhs(ai):
            if ai not in shared:
                a = flat2d(a_refs[ai][...]).astype(bf16)
                shared[ai] = a.T if mode == "tn" and n_terms_of[ai] > 1 else a
            return shared[ai]

        for ai, bi, ci in terms:
            a_ref, b_ref = a_refs[ai], b_refs[bi]
            if len(a_ref.shape) == 3 and len(b_ref.shape) == 3:
                for c in range(a_ref.shape[0]):
                    d = dot(a_ref[c], b_ref[c])
                    parts[ci] = d if parts[ci] is None else parts[ci] + d
            else:
                dn = _DN["nn"] if mode == "tn" and n_terms_of[ai] > 1 else _DN[mode]
                d = lax.dot_general(lhs(ai), flat2d(b_ref[...]).astype(bf16), dn, preferred_element_type=f32)
                parts[ci] = d if parts[ci] is None else parts[ci] + d
        return parts

    n_terms_of = {ai: sum(1 for t in terms if t[0] == ai) for ai in range(na)}

    def finish(accs, x_refs, o_refs):
        res = epilogue(accs, [x[...] for x in x_refs])
        for o_ref, r in zip(o_refs, res):
            o_ref[...] = r.reshape(o_ref.shape).astype(o_ref.dtype)

    def compute(a_refs, b_refs, x_refs, o_refs, acc_refs):
        if nk == 1:
            finish(dots(a_refs, b_refs), x_refs, o_refs)
            return
        k = pl.program_id(2)

        @pl.when(k == 0)
        def _():
            for acc in acc_refs:
                acc[...] = jnp.zeros_like(acc)

        for acc, d in zip(acc_refs, dots(a_refs, b_refs)):
            acc[...] += d

        @pl.when(k == nk - 1)
        def _():
            finish([acc[...] for acc in acc_refs], x_refs, o_refs)

    def body(*refs):
        a_refs, b_refs = refs[:na], refs[na:na + nb]
        x_refs = refs[na + nb:na + nb + nx]
        o_refs = refs[n_in:n_in + no]
        bufs = refs[n_in + no:n_in + no + nc]
        scratch = refs[n_in + no + nc:]
        if not nc:
            compute(a_refs, b_refs, x_refs, o_refs, scratch)
            return
        acc_refs, (send, recv) = scratch[:-2], scratch[-2:]
        i, j, k = pl.program_id(0), pl.program_id(1), pl.program_id(2)

        @pl.when((i == 0) & (j == 0) & (k == 0))
        def _():
            _run_copies(bufs, plan, send, recv, start=True, wait=False)

        compute(a_refs, b_refs, x_refs, o_refs, acc_refs)

        @pl.when((i == grid[0] - 1) & (j == grid[1] - 1) & (k == nk - 1))
        def _():
            _run_copies(bufs, plan, send, recv, start=False, wait=True)

    ops = list(a_ops) + list(b_ops) + list(x_ops)
    aliases = {}
    if into is not None:
        ops.append((into, ANY))
        aliases = {len(ops) - 1: 0}
    for t, s in enumerate(slabs):
        ops.append((s, ANY))
        aliases[len(ops) - 1] = no + t
    outs = list(outs) + [(S(s.shape, s.dtype), ANY) for s in slabs]
    scratch = [pltpu.VMEM(acc_shape, f32)] * (n_acc if nk > 1 else 0)
    if nc:
        scratch += [pltpu.SemaphoreType.DMA((ncp,)), pltpu.SemaphoreType.DMA((ncp,))]
    sem = ("arbitrary",) * 3 if nc else ("parallel", "parallel", "arbitrary")
    res = _pcall(body, name=name, grid=grid, in_specs=[s for _, s in ops], out_specs=[s for _, s in outs],
                 out_shape=[s for s, _ in outs], scratch_shapes=scratch, input_output_aliases=aliases,
                 compiler_params=_params(sem))(*[a for a, _ in ops])
    return res


def _first(accs, xs):
    return (accs[0],)


def _proj(name, a, w, l, tm, bias=None, resid=None, out_dtype=f32, carry=None):
    T = a.shape[0]
    Q, _, Kq, N = w.shape
    tn = _tile(N, (512, 256, 128))
    x_ops, epi = [], _first
    if bias is not None:
        x_ops = [(bias, pl.BlockSpec((1, tn), lambda i, j, k: (0, j))), (resid, pl.BlockSpec((tm, tn), lambda i, j, k: (i, j)))]
        epi = lambda accs, xs: (accs[0] + xs[0] + xs[1],)
    elif resid is not None:
        x_ops = [(resid, pl.BlockSpec((tm, tn), lambda i, j, k: (i, j)))]
        epi = lambda accs, xs: (accs[0] + xs[0],)
    res = _matmul(name, "nn", (T // tm, N // tn, 1),
                  [(a, pl.BlockSpec((tm, Q * Kq), lambda i, j, k: (i, 0)))],
                  [(w, pl.BlockSpec((Q, None, Kq, tn), lambda i, j, k: (0, l, 0, j)))],
                  x_ops, [(S((T, N), out_dtype), pl.BlockSpec((tm, tn), lambda i, j, k: (i, j)))],
                  [(0, 0, 0)], (tm, tn), 1, epi, carry=carry)
    return res[0] if carry is None else (res[0], res[1:])


def _proj_dx(name, dy, w, l, tm, out_dtype=f32, carry=None):
    T, N = dy.shape
    Q, _, Kq, _ = w.shape
    return _matmul(name, "nt", (T // tm, 1, 1),
                   [(dy, pl.BlockSpec((tm, N), lambda i, j, k: (i, 0)))],
                   [(w, pl.BlockSpec((Q, None, Kq, N), lambda i, j, k: (0, l, 0, 0)))],
                   [], [(S((T, Q * Kq), out_dtype), pl.BlockSpec((tm, Q * Kq), lambda i, j, k: (i, 0)))],
                   [(0, 0, 0)], (tm, Q * Kq), 1, _first, carry=carry)


def _dw_rows(name, a, dy, Q, tk, carry=None):
    T, N = dy.shape
    tn = _tile(N, (512, 256, 128))
    if a.ndim == 2:
        K = a.shape[1]
        a_op, ni, acc_rows = (a, pl.BlockSpec((tk, K), lambda i, j, k: (k, 0))), 1, K
        osh, ospec = S((Q, K // Q, N), bf16), pl.BlockSpec((Q, K // Q, tn), lambda i, j, k: (0, 0, j))
    else:
        Kq = a.shape[2]
        a_op, ni, acc_rows = (a, pl.BlockSpec((None, tk, Kq), lambda i, j, k: (i, k, 0))), Q, Kq
        osh, ospec = S((Q, Kq, N), bf16), pl.BlockSpec((None, Kq, tn), lambda i, j, k: (i, 0, j))
    return _matmul(name, "tn", (ni, N // tn, T // tk), [a_op], [(dy, pl.BlockSpec((tk, tn), lambda i, j, k: (k, j)))],
                   [], [(osh, ospec)], [(0, 0, 0)], (acc_rows, tn), 1, _first, carry=carry)


def _dw_cols(name, a, dyc, Qc, tk, Q, q_off=0, into=None, carry=None, dyc2=None):
    T, K = a.shape
    tkin = _tile(K, (512, 256, 128))
    if dyc.ndim == 3:
        Nq = dyc.shape[2]
        bspec = pl.BlockSpec((None, tk, Nq), lambda i, j, k: (j, k, 0))
    else:
        Nq = dyc.shape[1] // Qc
        bspec = pl.BlockSpec((tk, Nq), lambda i, j, k: (k, j))
    b_ops = [(dyc, bspec)] + ([] if dyc2 is None else [(dyc2, bspec)])
    ospec = pl.BlockSpec((None, tkin, Nq), lambda i, j, k: (j + q_off, i, 0))
    n = len(b_ops)
    return _matmul(name, "tn", (K // tkin, Qc, T // tk),
                   [(a, pl.BlockSpec((tk, tkin), lambda i, j, k: (k, i)))], b_ops, [],
                   [(S((Q, K, Nq), bf16), ospec)] * n, [(0, t, t) for t in range(n)], (tkin, Nq), n,
                   lambda accs, xs: tuple(accs), into=into, carry=carry)


def _glu_fwd(name, u, w, l, b_in, tm, carry=None):
    T, D = u.shape
    Q, _, _, Cq = w.shape
    H = Q // 2

    def epi(accs, xs):
        av, ag = accs[0] + xs[0], accs[1] + xs[1]
        return av * jax.nn.sigmoid(ag), av, ag

    wspec = lambda off: pl.BlockSpec((None, None, D, Cq), lambda i, j, k: (j + off, l, 0, 0))
    bspec = lambda off: pl.BlockSpec((1, Cq), lambda i, j, k: (0, j + off))
    ospec = pl.BlockSpec((tm, Cq), lambda i, j, k: (i, j))
    return _matmul(name, "nn", (T // tm, H, 1),
                   [(u, pl.BlockSpec((tm, D), lambda i, j, k: (i, 0)))],
                   [(w, wspec(0)), (w, wspec(H))], [(b_in, bspec(0)), (b_in, bspec(H))],
                   [(S((T, H * Cq), f32), ospec), (S((T, H * Cq), bf16), ospec), (S((T, H * Cq), bf16), ospec)],
                   [(0, 0, 0), (0, 1, 1)], (tm, Cq), 2, epi, carry=carry)


def _glu_du(name, dav, dag, w, l, tm):
    T = dav.shape[0]
    Q, _, D, Cq = w.shape
    H = Q // 2
    tn = _tile(D, (512, 256, 128))
    aspec = pl.BlockSpec((tm, Cq), lambda i, j, k: (i, k))
    wspec = lambda off: pl.BlockSpec((None, None, tn, Cq), lambda i, j, k: (k + off, l, j, 0))
    return _matmul(name, "nt", (T // tm, D // tn, H), [(dav, aspec), (dag, aspec)],
                   [(w, wspec(0)), (w, wspec(H))], [],
                   [(S((T, D), f32), pl.BlockSpec((tm, tn), lambda i, j, k: (i, j)))],
                   [(0, 0, 0), (1, 1, 0)], (tm, tn), 1, _first)[0]


def _ffn_up(name, u, wg, wu, l, tm, carry=None):
    T, D = u.shape
    Q, _, _, Fq = wg.shape

    def epi(accs, xs):
        g, up = accs
        sg = jax.nn.sigmoid(g)
        silu = g * sg
        return up * (sg * (1.0 + g * (1.0 - sg))), silu, silu * up

    wspec = pl.BlockSpec((None, None, D, Fq), lambda i, j, k: (j, l, 0, 0))
    ospec = pl.BlockSpec((None, tm, Fq), lambda i, j, k: (j, i, 0))
    osh = S((Q, T, Fq), bf16)
    return _matmul(name, "nn", (T // tm, Q, 1), [(u, pl.BlockSpec((tm, D), lambda i, j, k: (i, 0)))],
                   [(wg, wspec), (wu, wspec)], [], [(osh, ospec)] * 3, [(0, 0, 0), (0, 1, 1)], (tm, Fq), 2, epi, carry=carry)


def _ffn_down(name, hid, wd, l, resid, tm, carry=None):
    Q, T, Fq = hid.shape
    D = wd.shape[3]
    tn = _tile(D, (512, 256, 128))
    return _matmul(name, "nn", (T // tm, D // tn, 1),
                   [(hid, pl.BlockSpec((Q, tm, Fq), lambda i, j, k: (0, i, 0)))],
                   [(wd, pl.BlockSpec((Q, None, Fq, tn), lambda i, j, k: (0, l, 0, j)))],
                   [(resid, pl.BlockSpec((tm, tn), lambda i, j, k: (i, j)))],
                   [(S((T, D), f32), pl.BlockSpec((tm, tn), lambda i, j, k: (i, j)))],
                   [(0, 0, 0)], (tm, tn), 1, lambda accs, xs: (accs[0] + xs[0],), carry=carry)


def _ffn_dhid(name, dy, wd, l, dgate_of, dup_of, tm, carry=None):
    T, D = dy.shape
    Q, _, Fq, _ = wd.shape

    def epi(accs, xs):
        return accs[0] * xs[0].astype(f32), accs[0] * xs[1].astype(f32)

    cspec = pl.BlockSpec((None, tm, Fq), lambda i, j, k: (j, i, 0))
    osh = S((Q, T, Fq), bf16)
    return _matmul(name, "nt", (T // tm, Q, 1), [(dy, pl.BlockSpec((tm, D), lambda i, j, k: (i, 0)))],
                   [(wd, pl.BlockSpec((None, None, Fq, D), lambda i, j, k: (j, l, 0, 0)))],
                   [(dgate_of, cspec), (dup_of, cspec)], [(osh, cspec)] * 2, [(0, 0, 0)], (tm, Fq), 1, epi, carry=carry)


def _ffn_du(name, dgate, dup, wg, wu, l, tm, carry=None):
    Q, T, Fq = dgate.shape
    D = wg.shape[2]
    tn = _tile(D, (512, 256, 128))
    tm = tm // 2 if tm % 32 == 0 and tm > 512 else tm
    aspec = pl.BlockSpec((Q, tm, Fq), lambda i, j, k: (0, i, 0))
    wspec = pl.BlockSpec((Q, None, tn, Fq), lambda i, j, k: (0, l, j, 0))
    return _matmul(name, "nt", (T // tm, D // tn, 1), [(dgate, aspec), (dup, aspec)], [(wg, wspec), (wu, wspec)], [],
                   [(S((T, D), f32), pl.BlockSpec((tm, tn), lambda i, j, k: (i, j)))],
                   [(0, 0, 0), (1, 1, 0)], (tm, tn), 1, _first, carry=carry)


def _rowwise(name, fn, rows, params, out_dtypes, tm):
    nr, npar = len(rows), len(params)
    T = rows[0].shape[0]
    shp = jax.eval_shape(fn, *[S((tm, r.shape[1]), f32) for r in rows], *[S(p.shape, f32) for p in params])

    def body(*refs):
        r = [x[...].astype(f32) for x in refs[:nr]]
        p = [x[...] for x in refs[nr:nr + npar]]
        for o_ref, o in zip(refs[nr + npar:], fn(*r, *p)):
            o_ref[...] = o.astype(o_ref.dtype)

    row_spec = lambda w: pl.BlockSpec((tm, w), lambda i: (i, 0))
    par_spec = lambda p: pl.BlockSpec(p.shape, lambda i: (0, 0))
    return _pcall(body, name=name, grid=(T // tm,),
                  in_specs=[row_spec(r.shape[1]) for r in rows] + [par_spec(p) for p in params],
                  out_specs=[row_spec(s.shape[1]) for s in shp],
                  out_shape=[S((T, s.shape[1]), dt) for s, dt in zip(shp, out_dtypes)],
                  compiler_params=_params(("parallel",)))(*rows, *params)


def _rowwise_vjp(name, fn, rows, params, cots, drow_dtypes, tm):
    nr, npar, nc = len(rows), len(params), len(cots)
    T = rows[0].shape[0]

    def body(*refs):
        r = [x[...].astype(f32) for x in refs[:nr]]
        p = [x[...] for x in refs[nr:nr + npar]]
        c = tuple(x[...].astype(f32) for x in refs[nr + npar:nr + npar + nc])
        o_refs = refs[nr + npar + nc:]
        _, vjp = jax.vjp(fn, *r, *p)
        grads = vjp(c)
        for o_ref, g in zip(o_refs[:nr], grads[:nr]):
            o_ref[...] = g.astype(o_ref.dtype)

        @pl.when(pl.program_id(0) == 0)
        def _():
            for o_ref in o_refs[nr:]:
                o_ref[...] = jnp.zeros_like(o_ref)

        for o_ref, g in zip(o_refs[nr:], grads[nr:]):
            o_ref[...] += g

    row_spec = lambda w: pl.BlockSpec((tm, w), lambda i: (i, 0))
    par_spec = lambda p: pl.BlockSpec(p.shape, lambda i: (0, 0))
    return _pcall(body, name=name, grid=(T // tm,),
                  in_specs=[row_spec(r.shape[1]) for r in rows] + [par_spec(p) for p in params] + [row_spec(c.shape[1]) for c in cots],
                  out_specs=[row_spec(r.shape[1]) for r in rows] + [par_spec(p) for p in params],
                  out_shape=[S(r.shape, dt) for r, dt in zip(rows, drow_dtypes)] + [S(p.shape, f32) for p in params],
                  compiler_params=_params(("arbitrary",)))(*rows, *params, *cots)


def _rms(h, g):
    return h * lax.rsqrt(jnp.mean(h * h, axis=-1, keepdims=True) + EPS) * g


def _rms_fn(h, g):
    return (_rms(h, g),)


def _rms_res_fn(h, g):
    return _rms(h, g), h


def _rms_res_bias_fn(h, g, b0):
    return _rms(h, g), h + b0


def _ln_silu_fn(c, g, b):
    mu = jnp.mean(c, axis=-1, keepdims=True)
    var = jnp.mean(jnp.square(c - mu), axis=-1, keepdims=True)
    y = (c - mu) * lax.rsqrt(var + EPS) * g + b
    return (y * jax.nn.sigmoid(y),)


def _glu_fn(av, ag, bv, bg):
    return ((av + bv) * jax.nn.sigmoid(ag + bg),)


def _take_rows(name, x, start, size):
    B, L, D = x.shape
    dc = _tile(D, (256, 128))

    def body(x_ref, o_ref):
        o_ref[...] = x_ref[start:start + size, :]

    return _pcall(body, name=name, grid=(B, D // dc), in_specs=[pl.BlockSpec((None, L, dc), lambda b, j: (b, 0, j))],
                  out_specs=pl.BlockSpec((None, size, dc), lambda b, j: (b, 0, j)), out_shape=S((B, size, D), x.dtype),
                  compiler_params=_params(("parallel", "parallel")))(x)


def _loss_head(h, tgt, Lp, n_real, tm):
    T, D = h.shape

    def body(h_ref, t_ref, dy_ref, part_ref):
        i = pl.program_id(0)
        pos = (i * tm + lax.broadcasted_iota(jnp.int32, (tm, 1), 0)) % Lp
        real = (pos >= N_META_ROWS) & (pos < N_META_ROWS + n_real)
        err = jnp.where(real, h_ref[...] - t_ref[...], 0.0)
        dy_ref[...] = err * (1.0 / D)

        @pl.when(i == 0)
        def _():
            part_ref[...] = jnp.zeros_like(part_ref)

        part_ref[...] += jnp.sum(err * err, axis=0, keepdims=True)

    spec = pl.BlockSpec((tm, D), lambda i: (i, 0))
    return _pcall(body, name="loss_head", grid=(T // tm,), in_specs=[spec, spec],
                  out_specs=[spec, pl.BlockSpec((1, D), lambda i: (0, 0))],
                  out_shape=[S((T, D), f32), S((1, D), f32)], compiler_params=_params(("arbitrary",)))(h, tgt)


CONV_OFF = CONV_PAD - (CONV_TAPS - 1)
WIN_ROWS = ATT_BLOCK + CONV_PAD


def _phases(ph_ref, win):
    n = win.shape[0]
    for b in range(1, 8):
        ph_ref[b - 1, 0:n - 8, :] = win[b:n - 8 + b]

    def tap(o):
        a = (o // 8) * 8
        return win[a:a + ATT_BLOCK] if o % 8 == 0 else ph_ref[o % 8 - 1, a:a + ATT_BLOCK, :]

    return tap


def _phase_scratch(width):
    return pltpu.VMEM((7, WIN_ROWS - 8, width), f32)


def _dwconv_fwd(name, p, dw, carry=None):
    B, Lp, D = p.shape
    dc = _tile(D, (256, 128))
    slabs, plan = carry if carry is not None else ([], [])
    nc = len(slabs)

    def body(*refs):
        p_ref, w_ref = refs[:2]
        o_ref, bufs, ph_ref = refs[2 + nc], refs[3 + nc:3 + 2 * nc], refs[3 + 2 * nc]
        first = (pl.program_id(0) == 0) & (pl.program_id(1) == 0)
        last = (pl.program_id(0) == B - 1) & (pl.program_id(1) == D // dc - 1)
        if nc:
            @pl.when(first)
            def _():
                _run_copies(bufs, plan, refs[-2], refs[-1], start=True, wait=False)

        def tile(win, base):
            tap = _phases(ph_ref, win)
            acc = jnp.zeros((ATT_BLOCK, dc), f32)
            for k in range(CONV_TAPS):
                acc = acc + tap(CONV_OFF + k) * w_ref[k:k + 1, :]
            o_ref[pl.ds(base, ATT_BLOCK), :] = acc

        tile(jnp.concatenate([jnp.zeros((CONV_PAD, dc), f32), p_ref[0:ATT_BLOCK, :]], axis=0), 0)

        def step(r, carry):
            base = pl.multiple_of(r * ATT_BLOCK, ATT_BLOCK)
            tile(p_ref[pl.ds(pl.multiple_of(base - CONV_PAD, CONV_PAD), WIN_ROWS), :], base)
            return carry

        lax.fori_loop(1, Lp // ATT_BLOCK, step, 0)
        if nc:
            @pl.when(last)
            def _():
                _run_copies(bufs, plan, refs[-2], refs[-1], start=False, wait=True)

    seq = pl.BlockSpec((None, Lp, dc), lambda b, j: (b, 0, j))
    sems = [pltpu.SemaphoreType.DMA((_plan_copies(plan),))] * 2 if nc else []
    return _pcall(body, name=name, grid=(B, D // dc),
                  in_specs=[seq, pl.BlockSpec((CONV_PAD, dc), lambda b, j: (0, j))] + [ANY] * nc,
                  out_specs=[seq] + [ANY] * nc, out_shape=[S((B, Lp, D), f32)] + [S(s.shape, s.dtype) for s in slabs],
                  input_output_aliases={2 + t: 1 + t for t in range(nc)}, scratch_shapes=[_phase_scratch(dc)] + sems,
                  compiler_params=_params(("arbitrary", "arbitrary") if nc else ("parallel", "parallel")))(p, dw, *slabs)


def _dwconv_bwd(name, dcv, p, dw):
    B, Lp, D = p.shape
    dcw = _tile(D, (256, 128))
    nblk = Lp // ATT_BLOCK
    assert nblk >= 2
    zeros = lambda: jnp.zeros((CONV_PAD, dcw), f32)

    def body(dc_ref, p_ref, w_ref, dp_ref, ddw_ref, dph_ref, pph_ref, part_ref):
        @pl.when(pl.program_id(1) == 0)
        def _():
            ddw_ref[...] = jnp.zeros_like(ddw_ref)

        part_ref[...] = jnp.zeros_like(part_ref)

        def tile(dwin, pwin, base):
            dtap, ptap = _phases(dph_ref, dwin), _phases(pph_ref, pwin)
            dtile = dwin[0:ATT_BLOCK]
            acc = jnp.zeros((ATT_BLOCK, dcw), f32)
            for k in range(CONV_TAPS):
                acc = acc + dtap(CONV_TAPS - 1 - k) * w_ref[k:k + 1, :]
                part_ref[k] += jnp.sum((dtile * ptap(CONV_OFF + k)).reshape(ATT_BLOCK // 8, 8, dcw), axis=0)
            dp_ref[pl.ds(base, ATT_BLOCK), :] = acc

        tile(dc_ref[0:WIN_ROWS, :], jnp.concatenate([zeros(), p_ref[0:ATT_BLOCK, :]], axis=0), 0)

        def step(r, carry):
            base = pl.multiple_of(r * ATT_BLOCK, ATT_BLOCK)
            tile(dc_ref[pl.ds(base, WIN_ROWS), :], p_ref[pl.ds(pl.multiple_of(base - CONV_PAD, CONV_PAD), WIN_ROWS), :], base)
            return carry

        lax.fori_loop(1, nblk - 1, step, 0)
        last = Lp - ATT_BLOCK
        tile(jnp.concatenate([dc_ref[last:Lp, :], zeros()], axis=0), p_ref[last - CONV_PAD:Lp, :], last)
        ddw_ref[...] += jnp.sum(part_ref[...], axis=1)

    seq = pl.BlockSpec((None, Lp, dcw), lambda j, b: (b, 0, j))
    wsp = pl.BlockSpec((CONV_PAD, dcw), lambda j, b: (0, j))
    return _pcall(body, name=name, grid=(D // dcw, B), in_specs=[seq, seq, wsp], out_specs=[seq, wsp],
                  out_shape=[S((B, Lp, D), f32), S((CONV_PAD, D), f32)],
                  scratch_shapes=[_phase_scratch(dcw)] * 2 + [pltpu.VMEM((CONV_PAD, 8, dcw), f32)],
                  compiler_params=_params(("parallel", "arbitrary")))(dcv, p, dw)


def _band_start(n):
    return pl.multiple_of(jnp.maximum(n - 1, 0) * ATT_BLOCK, ATT_BLOCK)


def _attn_bias(R):
    row = jnp.arange(R * ATT_BLOCK)[:, None] % ATT_BLOCK
    col = jnp.arange(3 * ATT_BLOCK)[None, :]
    out = []
    for n in range(3):
        qpos = n * ATT_BLOCK + row
        meta_ok = (col < N_META_ROWS) & (col <= qpos)
        band_pos = max(n - 1, 0) * ATT_BLOCK + (col - ATT_BLOCK)
        diff = qpos - band_pos
        band_ok = (col >= ATT_BLOCK) & (diff >= 0) & (diff < ATT_BLOCK) & (band_pos >= N_META_ROWS)
        out.append(jnp.where(meta_ok | band_ok, 0.0, MASKED))
    return jnp.stack(out).astype(f32)


def _row_head(R):
    row = lax.broadcasted_iota(jnp.int32, (R * ATT_BLOCK, R), 0)
    lo = lax.broadcasted_iota(jnp.int32, (R * ATT_BLOCK, R), 1) * ATT_BLOCK
    return ((row >= lo) & (row < lo + ATT_BLOCK)).astype(f32)


def _attn_core(q, kn, vv, qg, sink_vec, row_head, bias):
    qn = _rms(q, qg)
    s = lax.dot_general(qn.astype(bf16), kn.astype(bf16), _DN["nt"], preferred_element_type=f32) * (HEAD_DIM ** -0.5)
    s = s + bias
    sink = jnp.sum(row_head * sink_vec, axis=-1, keepdims=True)
    m = lax.stop_gradient(jnp.maximum(jnp.max(s, axis=-1, keepdims=True), sink))
    p = jnp.exp(s - m)
    denom = jnp.sum(p, axis=-1, keepdims=True) + jnp.exp(sink - m)
    return jnp.dot((p / denom).astype(bf16), vv.astype(bf16), preferred_element_type=f32)


def _keys_of(ref, n):
    return jnp.concatenate([ref[0:ATT_BLOCK, :], ref[pl.ds(_band_start(n), 2 * ATT_BLOCK), :]], axis=0)


def _stack_heads(x, R):
    return x if R == 1 else jnp.concatenate([x[:, r * HEAD_DIM:(r + 1) * HEAD_DIM] for r in range(R)], axis=0)


def _unstack_heads(x, R):
    return x if R == 1 else jnp.concatenate([x[r * ATT_BLOCK:(r + 1) * ATT_BLOCK] for r in range(R)], axis=1)


KV_PER_STEP_FWD = 2
KV_PER_STEP_BWD = 1


def _attn_specs(R, Lp, per):
    qspec = pl.BlockSpec((None, ATT_BLOCK, per * R * HEAD_DIM), lambda g, b, n: (b, n, g))
    kspec = pl.BlockSpec((per, None, Lp, HEAD_DIM), lambda g, b, n: (g, b, 0, 0))
    gspec = pl.BlockSpec((1, HEAD_DIM), lambda g, b, n: (0, 0))
    sspec = pl.BlockSpec((per, 1, R), lambda g, b, n: (g, 0, 0))
    bspec = pl.BlockSpec((None, R * ATT_BLOCK, 3 * ATT_BLOCK), lambda g, b, n: (jnp.minimum(n, 2), 0, 0))
    return qspec, kspec, gspec, sspec, bspec


def _heads_of(x, gi, R):
    return x[:, gi * R * HEAD_DIM:(gi + 1) * R * HEAD_DIM]


def _attn_fwd(name, q, kn, v, qg, sinks):
    B, Lp, D = q.shape
    G = kn.shape[0]
    R = D // (G * HEAD_DIM)
    per = KV_PER_STEP_FWD
    assert G % per == 0
    qspec, kspec, gspec, sspec, bspec = _attn_specs(R, Lp, per)

    def body(q_ref, k_ref, v_ref, qg_ref, s_ref, b_ref, o_ref):
        n = pl.program_id(2)
        q_all, outs = q_ref[...], []
        for gi in range(per):
            o = _attn_core(_stack_heads(_heads_of(q_all, gi, R), R), _keys_of(k_ref.at[gi], n), _keys_of(v_ref.at[gi], n),
                           qg_ref[...], s_ref[gi], _row_head(R), b_ref[...])
            outs.append(_unstack_heads(o, R))
        o_ref[...] = jnp.concatenate(outs, axis=1).astype(o_ref.dtype)

    return _pcall(body, name=name, grid=(G // per, B, Lp // ATT_BLOCK), in_specs=[qspec, kspec, kspec, gspec, sspec, bspec],
                  out_specs=qspec, out_shape=S(q.shape, bf16),
                  compiler_params=_params(("parallel", "parallel", "parallel")))(q, kn, v, qg, sinks, _attn_bias(R))


def _attn_bwd(name, q, kn, v, qg, sinks, do):
    B, Lp, D = q.shape
    G = kn.shape[0]
    R = D // (G * HEAD_DIM)
    per = KV_PER_STEP_BWD
    qspec, kspec, gspec, sspec, bspec = _attn_specs(R, Lp, per)

    def body(q_ref, k_ref, v_ref, qg_ref, s_ref, b_ref, do_ref, dq_ref, dk_ref, dv_ref, dqg_ref, ds_ref):
        g, b, n = pl.program_id(0), pl.program_id(1), pl.program_id(2)

        @pl.when((g == 0) & (b == 0) & (n == 0))
        def _():
            dqg_ref[...] = jnp.zeros_like(dqg_ref)

        @pl.when((b == 0) & (n == 0))
        def _():
            ds_ref[...] = jnp.zeros_like(ds_ref)

        @pl.when(n == 0)
        def _():
            dk_ref[...] = jnp.zeros_like(dk_ref)
            dv_ref[...] = jnp.zeros_like(dv_ref)

        row_head, bias = _row_head(R), b_ref[...]
        q_all, do_all, dqs = q_ref[...], do_ref[...].astype(f32), []
        band = pl.ds(_band_start(n), 2 * ATT_BLOCK)
        for gi in range(per):
            _, vjp = jax.vjp(lambda q_, k_, v_, a_, s_: _attn_core(q_, k_, v_, a_, s_, row_head, bias),
                             _stack_heads(_heads_of(q_all, gi, R), R), _keys_of(k_ref.at[gi], n).astype(f32),
                             _keys_of(v_ref.at[gi], n).astype(f32), qg_ref[...], s_ref[gi])
            dq, dkk, dvv, dqg, dsk = vjp(_stack_heads(_heads_of(do_all, gi, R), R))
            dqs.append(_unstack_heads(dq, R))
            dqg_ref[...] += dqg
            ds_ref[gi] += dsk
            dk_ref[gi, band, :] += dkk[ATT_BLOCK:]
            dv_ref[gi, band, :] += dvv[ATT_BLOCK:]
            dk_ref[gi, 0:ATT_BLOCK, :] += dkk[:ATT_BLOCK]
            dv_ref[gi, 0:ATT_BLOCK, :] += dvv[:ATT_BLOCK]
        dq_ref[...] = jnp.concatenate(dqs, axis=1)

    return _pcall(body, name=name, grid=(G // per, B, Lp // ATT_BLOCK),
                  in_specs=[qspec, kspec, kspec, gspec, sspec, bspec, qspec],
                  out_specs=[qspec, kspec, kspec, gspec, sspec],
                  out_shape=[S(q.shape, f32), S(kn.shape, f32), S(v.shape, f32), S(qg.shape, f32), S(sinks.shape, f32)],
                  compiler_params=_params(("arbitrary", "arbitrary", "arbitrary")))(q, kn, v, qg, sinks, _attn_bias(R), do)


def _kv_heads(name, kv, kg, tm):
    T, W = kv.shape
    G = W // (2 * HEAD_DIM)

    def body(kv_ref, kg_ref, k_ref, v_ref):
        x = kv_ref[...]
        for g in range(G):
            k_ref[g] = _rms(x[:, g * HEAD_DIM:(g + 1) * HEAD_DIM], kg_ref[...]).astype(bf16)
            v_ref[g] = x[:, (G + g) * HEAD_DIM:(G + g + 1) * HEAD_DIM].astype(bf16)

    hspec = pl.BlockSpec((G, tm, HEAD_DIM), lambda i: (0, i, 0))
    return _pcall(body, name=name, grid=(T // tm,),
                  in_specs=[pl.BlockSpec((tm, W), lambda i: (i, 0)), pl.BlockSpec((1, HEAD_DIM), lambda i: (0, 0))],
                  out_specs=[hspec, hspec], out_shape=[S((G, T, HEAD_DIM), bf16)] * 2,
                  compiler_params=_params(("parallel",)))(kv, kg)


def _kv_heads_bwd(name, kv, kg, dkn, dv, tm):
    T, W = kv.shape
    G = W // (2 * HEAD_DIM)

    def body(kv_ref, kg_ref, dk_ref, dv_ref, o_ref, dkg_ref):
        @pl.when(pl.program_id(0) == 0)
        def _():
            dkg_ref[...] = jnp.zeros_like(dkg_ref)

        x = kv_ref[...]
        pieces = []
        for g in range(G):
            _, vjp = jax.vjp(_rms, x[:, g * HEAD_DIM:(g + 1) * HEAD_DIM], kg_ref[...])
            dk, dkg = vjp(dk_ref[g])
            pieces.append(dk)
            dkg_ref[...] += dkg
        o_ref[...] = jnp.concatenate(pieces + [dv_ref[g] for g in range(G)], axis=1)

    hspec = pl.BlockSpec((G, tm, HEAD_DIM), lambda i: (0, i, 0))
    gspec = pl.BlockSpec((1, HEAD_DIM), lambda i: (0, 0))
    return _pcall(body, name=name, grid=(T // tm,),
                  in_specs=[pl.BlockSpec((tm, W), lambda i: (i, 0)), gspec, hspec, hspec],
                  out_specs=[pl.BlockSpec((tm, W), lambda i: (i, 0)), gspec],
                  out_shape=[S((T, W), f32), S((1, HEAD_DIM), f32)], compiler_params=_params(("arbitrary",)))(kv, kg, dkn, dv)


def _place():
    x, y, c = lax.axis_index("x"), lax.axis_index("y"), lax.axis_index("c")
    chips = [(1 - x, y), (x, 1 - y), (1 - x, 1 - y)]
    return x, y, c, chips, [2 * cx + cy for cx, cy in chips]


def _remote(src, dst, send_sem, recv_sem, to):
    return pltpu.make_async_remote_copy(src_ref=src, dst_ref=dst, send_sem=send_sem, recv_sem=recv_sem,
                                        device_id=to, device_id_type=MESH)


def _into_slot(name, w, l, idx, slots, dtype):
    _, Rr, Cc = w.shape
    tr = _tile(Rr, (512, 256, 128, 64, 32, 16))

    def body(i_ref, x_ref, o_ref):
        o_ref[...] = x_ref[...].astype(o_ref.dtype)

    gs = pltpu.PrefetchScalarGridSpec(
        num_scalar_prefetch=1, grid=(Rr // tr,),
        in_specs=[pl.BlockSpec((None, tr, Cc), lambda i, i_ref: (l, i, 0))],
        out_specs=pl.BlockSpec((None, tr, Cc), lambda i, i_ref: (i_ref[0], i, 0)))
    return _pcall(body, name=name, grid_spec=gs, out_shape=S((slots, Rr, Cc), dtype), compiler_params=_params(("parallel",)))(idx, w)


_PLAN_COPIES = dict(ici=3, fwd=3, xchg=N_CHIPS, scat=3, share=1)


def _plan_copies(plan):
    return sum(_PLAN_COPIES[step[0]] for step in plan)


def _comm_copies(bufs, plan, send, recv):
    x, y, c, chips, qk = _place()
    me_q, sib = 2 * x + y, (x, y, 1 - c)
    starts, lands = [], []

    def add(src, dst, to, land):
        s = len(starts)
        starts.append(_remote(src, dst, send.at[s], recv.at[s], to))
        lands.append(_remote(land, land, send.at[s], recv.at[s], sib))

    for step in plan:
        kind = step[0]
        if kind == "ici":
            mine = bufs[step[1]].at[me_q, c]
            for k in range(3):
                add(mine, mine, (*chips[k], c), bufs[step[1]].at[qk[k], c])
        elif kind == "fwd":
            for k in range(3):
                got = bufs[step[1]].at[qk[k], c]
                add(got, got, sib, bufs[step[1]].at[qk[k], 1 - c])
        elif kind == "xchg":
            for q in range(N_CHIPS):
                add(bufs[step[1]].at[q, 1 - c], bufs[step[2]].at[q], sib, bufs[step[2]].at[q])
        elif kind == "scat":
            for k in range(3):
                add(bufs[step[1]].at[qk[k]], bufs[step[2]].at[k], (*chips[k], c), bufs[step[2]].at[k])
        else:
            mine = bufs[step[1]].at[step[2], c]
            add(mine, mine, sib, bufs[step[1]].at[step[2], 1 - c])
    return starts, lands


def _run_copies(bufs, plan, send, recv, start=True, wait=True):
    starts, lands = _comm_copies(bufs, plan, send, recv)
    if start:
        for cp in starts:
            cp.start()
    if wait:
        for cp in lands:
            cp.wait_recv()
        for cp in starts:
            cp.wait_send()


def _comm_call(name, slabs, phases):
    n = len(slabs)

    def body(*refs):
        bufs, sems = refs[n:2 * n], refs[2 * n:]
        for t, plan in enumerate(phases):
            _run_copies(bufs, plan, sems[2 * t], sems[2 * t + 1])

    sems = [pltpu.SemaphoreType.DMA((_plan_copies(plan),)) for plan in phases for _ in range(2)]
    return _pcall(body, name=name, in_specs=[ANY] * n, out_specs=[ANY] * n, out_shape=[S(s.shape, s.dtype) for s in slabs],
                  input_output_aliases={p: p for p in range(n)}, scratch_shapes=sems)(*slabs)


def _gather_all_devices(block):
    m_per, ncol = block.shape

    def body(x_ref, out_ref, send_sems, recv_sems, local_sem):
        x, y, c, chips, _ = _place()
        me, sib = (x, y, c), (x, y, 1 - c)

        def rows(px, py, pc):
            return out_ref.at[pl.ds((4 * px + 2 * py + pc) * m_per, m_per), :]

        def copy(k, blk, to, src=None):
            return _remote(rows(*blk) if src is None else src, rows(*blk), send_sems.at[k], recv_sems.at[k], to)

        mine = pltpu.make_async_copy(x_ref, rows(*me), local_sem)
        mine.start()
        first = [copy(0, me, sib, src=x_ref)] + [copy(1 + j, me, (*chip, c), src=x_ref) for j, chip in enumerate(chips)]
        for cp in first:
            cp.start()
        passed = [copy(4 + j, (*chip, c), sib) for j, chip in enumerate(chips)]
        for j, chip in enumerate(chips):
            copy(1 + j, (*chip, c), me).wait_recv()
            passed[j].start()
        copy(0, sib, me).wait_recv()
        for j, chip in enumerate(chips):
            copy(4 + j, (*chip, 1 - c), me).wait_recv()
        for cp in first + passed:
            cp.wait_send()
        mine.wait()

    vm = pl.BlockSpec(memory_space=pltpu.VMEM)
    return _pcall(body, name="gather_small_grads", in_specs=[vm], out_specs=vm,
                  out_shape=S((8 * m_per, ncol), block.dtype),
                  scratch_shapes=[pltpu.SemaphoreType.DMA((7,)), pltpu.SemaphoreType.DMA((7,)), pltpu.SemaphoreType.DMA],
                  compiler_params=pltpu.CompilerParams(vmem_limit_bytes=VMEM_LIMIT_BYTES))(block)


def _sum_pair(name, g, r1, c_idx):
    Q, _, Rr, Cc = g.shape
    tr = _tile(Rr, (512, 256, 128))

    def body(c_ref, g_ref, r_ref, o_ref):
        o_ref[...] = (g_ref[...].astype(f32) + r_ref[...].astype(f32)).astype(o_ref.dtype)

    spec = pl.BlockSpec((None, tr, Cc), lambda q, i, c_ref: (q, i, 0))
    gs = pltpu.PrefetchScalarGridSpec(
        num_scalar_prefetch=1, grid=(Q, Rr // tr),
        in_specs=[pl.BlockSpec((None, None, tr, Cc), lambda q, i, c_ref: (q, c_ref[0], i, 0)), spec], out_specs=spec)
    return _pcall(body, name=name, grid_spec=gs, out_shape=S((Q, Rr, Cc), bf16),
                  compiler_params=_params(("parallel", "parallel")))(c_idx, g, r1)


def _sum_owner(name, s, r2, qc_idx, l, nl, into):
    Q, Rr, Cc = s.shape
    tr = _tile(Rr, (512, 256, 128))

    def body(q_ref, s_ref, r_ref, *rest):
        o_ref = rest[-1]
        o_ref[...] = ((s_ref[...].astype(f32) + r_ref[0].astype(f32)) + r_ref[1].astype(f32)) + r_ref[2].astype(f32)

    in_specs = [pl.BlockSpec((None, tr, Cc), lambda i, q_ref: (q_ref[0], i, 0)), pl.BlockSpec((3, tr, Cc), lambda i, q_ref: (0, i, 0))]
    gs = pltpu.PrefetchScalarGridSpec(
        num_scalar_prefetch=1, grid=(Rr // tr,), in_specs=in_specs + ([] if into is None else [ANY]),
        out_specs=pl.BlockSpec((None, None, tr, Cc), lambda i, q_ref: (l, q_ref[1], i, 0)))
    return _pcall(body, name=name, grid_spec=gs, out_shape=S((nl, 2, Rr, Cc), f32),
                  input_output_aliases={} if into is None else {3: 0},
                  compiler_params=_params(("parallel",)))(qc_idx, s, r2, *([] if into is None else [into]))


def _sum_devices(stack):
    n, M, C = stack.shape

    def body(s_ref, o_ref):
        acc = s_ref[0]
        for d in range(1, n):
            acc = acc + s_ref[d]
        o_ref[...] = acc

    return _pcall(body, name="sum_small_grads", out_shape=S((M, C), f32), compiler_params=_params())(stack)


def _adamw(name, w, g, m, v):
    Rr, Cc = w.shape
    tr = _tile(Rr, (256, 128, 64, 32, 16, 8))

    def body(w_ref, g_ref, m_ref, v_ref, go_ref, d_ref, mo_ref, vo_ref):
        g_ = g_ref[...]
        go_ref[...] = g_
        m_ = B1 * m_ref[...] + (1.0 - B1) * g_
        v_ = B2 * v_ref[...] + (1.0 - B2) * jnp.square(g_)
        m_hat = m_ / (1.0 - B1 ** STEP)
        v_hat = v_ / (1.0 - B2 ** STEP)
        d_ref[...] = -LR * (m_hat / (jnp.sqrt(v_hat) + ADAM_EPS) + WD * w_ref[...])
        mo_ref[...] = m_
        vo_ref[...] = v_

    spec = pl.BlockSpec((tr, Cc), lambda i: (i, 0))
    return _pcall(body, name=name, grid=(Rr // tr,), in_specs=[spec] * 4, out_specs=[spec] * 4,
                  out_shape=[S((Rr, Cc), f32)] * 4, compiler_params=_params(("parallel",)))(w, g, m, v)


def _pack(arrs, multiple):
    flat = jnp.concatenate([a.reshape(-1) for a in arrs])
    pad = (-flat.shape[0]) % multiple
    return jnp.pad(flat, (0, pad)).reshape(-1, 128)


def _unpack(slab, shapes):
    flat, out, o = slab.reshape(-1), [], 0
    for shp in shapes:
        n = 1
        for d in shp:
            n *= d
        out.append(flat[o:o + n].reshape(shp))
        o += n
    return out


def kernel(x, meta_tokens, norm_mix, norm_ffn, conv_w_in, conv_b_in, conv_dw, conv_ln_g, conv_ln_b, conv_w_out, conv_b_out, kv_norm, w_kv, k_norm, w_q, q_norm, attn_sinks, w_o, ffn_w_gate, ffn_w_up, ffn_w_down, loss_target, m_meta_tokens, m_norm_mix, m_norm_ffn, m_conv_w_in, m_conv_b_in, m_conv_dw, m_conv_ln_g, m_conv_ln_b, m_conv_w_out, m_conv_b_out, m_kv_norm, m_w_kv, m_k_norm, m_w_q, m_q_norm, m_attn_sinks, m_w_o, m_ffn_w_gate, m_ffn_w_up, m_ffn_w_down, v_meta_tokens, v_norm_mix, v_norm_ffn, v_conv_w_in, v_conv_b_in, v_conv_dw, v_conv_ln_g, v_conv_ln_b, v_conv_w_out, v_conv_b_out, v_kv_norm, v_w_kv, v_k_norm, v_w_q, v_q_norm, v_attn_sinks, v_w_o, v_ffn_w_gate, v_ffn_w_up, v_ffn_w_down):
    Q = N_CHIPS
    B, SEQ, D = x.shape
    L = N_META_ROWS + SEQ
    Lp = -(-L // ATT_BLOCK) * ATT_BLOCK
    T = B * Lp
    NA, NB = conv_w_in.shape[0], w_q.shape[0]
    NL = NA + NB
    Dq = D // Q
    G = N_KV
    R = D // (HEAD_DIM * G)
    KVW = w_kv.shape[1]
    assert NA % 2 == 0 and NB % 2 == 0 and NL % 2 == 0 and (D // Q) % 32 == 0
    tm = _tile(T, (1088, 544, 512, 256, 128))
    tk = _tile(T, (2176, 1088, 544, 512, 256, 128))
    tr = _tile(T, (272, 256, 128))
    my_c = lax.axis_index("c").astype(jnp.int32).reshape(1)
    my_q = (2 * lax.axis_index("x") + lax.axis_index("y")).astype(jnp.int32)
    my_qc = jnp.concatenate([my_q.reshape(1), my_c])

    small_shapes = [meta_tokens.shape, conv_b_in.shape, conv_dw.shape, conv_ln_g.shape, conv_ln_b.shape, conv_b_out.shape]
    small = _pack([meta_tokens, conv_b_in, conv_dw, conv_ln_g, conv_ln_b, conv_b_out], 2048)
    big = [conv_w_in, conv_w_out, w_kv, w_q, w_o, ffn_w_gate, ffn_w_up, ffn_w_down]
    keyed = dict(cin=conv_w_in, cout=conv_w_out, kv=w_kv[None], q=w_q, o=w_o, g=ffn_w_gate, u=ffn_w_up, d=ffn_w_down,
                 small=small[None])
    slabs, where = [], {}
    for key, w3 in keyed.items():
        for l in range(w3.shape[0]):
            s = _into_slot(f"own_{key}{l}", w3, l, my_qc, Q, f32 if key == "small" else bf16)
            where[key, l] = len(slabs)
            slabs.append(s.reshape(Q, 2, s.shape[1] // 2, s.shape[2]))

    def W(key, l):
        s = slabs[where[key, l]]
        return s.reshape(Q, 1, 2 * s.shape[2], s.shape[3])

    def layer_slabs(l):
        j = l - NA
        mix = [("cin", l), ("cout", l)] if l < NA else [("q", j), ("o", j)] + ([("kv", 0)] if j == 0 else [])
        return [where[k] for k in mix], [where["g", l], where["u", l]], [where["d", l]]

    def carry_of(ici, fwd):
        idxs = sorted(set(ici) | set(fwd))
        return idxs, ([slabs[i] for i in idxs], [("ici", idxs.index(i)) for i in ici] + [("fwd", idxs.index(i)) for i in fwd])

    def put_back(idxs, new):
        for i, s in zip(idxs, new):
            slabs[i] = s

    assert NA >= 1
    first = [where["cin", 0], where["small", 0]]
    idxs, (sl, plan) = carry_of(first, first)
    put_back(idxs, _comm_call("gather_first", sl, [[s for s in plan if s[0] == "ici"], [s for s in plan if s[0] == "fwd"]]))
    parts = [_unpack(slabs[where["small", 0]][q], small_shapes) for q in range(Q)]
    meta_f, b_in_f, dw_f, ln_g_f, ln_b_f, b_out_f = [jnp.concatenate([parts[q][i] for q in range(Q)], axis=-1) for i in range(6)]
    dw_pad = jnp.pad(dw_f, ((0, 0), (0, CONV_PAD - CONV_TAPS), (0, 0)))

    h = jnp.concatenate([jnp.broadcast_to(meta_f[None], (B, N_META_ROWS, D)), x, jnp.zeros((B, Lp - L, D), f32)], axis=1).reshape(T, D)
    tgt = jnp.pad(loss_target, ((0, 0), (N_META_ROWS, Lp - L), (0, 0))).reshape(T, D)
    row = lambda a: a.reshape(1, -1)
    seqs = lambda a: a.reshape(B, Lp, a.shape[-1])
    by_seq = lambda a: a.reshape(G, B, Lp, HEAD_DIM)
    saved = []
    sinks3 = attn_sinks.reshape(NB, G, 1, R)
    def carrying(ici, fwd):
        idxs, carry = carry_of(ici, fwd)
        return idxs, (carry if idxs else None)

    for l in range(NL):
        st = {"h_a": h}
        u = _rowwise(f"rms_mix{l}", _rms_fn, [h], [row(norm_mix[l])], [bf16], tr)[0]
        st["u"] = u
        mix_s, (g_s, u_s), d_s = layer_slabs(l)
        idxs, carry = carrying([where["cout", 0], g_s], []) if l == 0 else carrying([u_s], [g_s])
        second = [u_s] + d_s if l == 0 else [u_s]
        if l < NA:
            p, av, ag, *new = _glu_fwd(f"glu{l}", u, W("cin", l), 0, b_in_f[l:l + 1], tm, carry=carry)
            put_back(idxs, new)
            idxs, carry = carrying([u_s] + d_s, [where["cout", 0], g_s]) if l == 0 else ([], None)
            cv, *new = _dwconv_fwd(f"dwconv{l}", p.reshape(B, Lp, D), dw_pad[l], carry=carry)
            put_back(idxs, new)
            cv = cv.reshape(T, D)
            s = _rowwise(f"ln_silu{l}", _ln_silu_fn, [cv], [row(ln_g_f[l]), row(ln_b_f[l])], [bf16], tr)[0]
            idxs2, carry2 = carrying([], second)
            res = _proj(f"conv_out{l}", s, W("cout", l), 0, tm, bias=row(b_out_f[l]), resid=h, carry=carry2)
            h, new = res if carry2 is not None else (res, [])
            put_back(idxs2, new)
            st.update(av=av, ag=ag, p=p, cv=cv, s=s)
        else:
            j = l - NA
            if j == 0:
                kvn = _rowwise("rms_kv", _rms_fn, [h], [row(kv_norm)], [bf16], tr)[0]
                kv = _proj("kv_proj", kvn, W("kv", 0), 0, tm)
                kn, vh = _kv_heads("kv_heads", kv, row(k_norm), tm)
                kn, vh = by_seq(kn), by_seq(vh)
                st.update(kvn=kvn, kv=kv)
            res = _proj(f"q_proj{j}", u, W("q", j), 0, tm, carry=carry)
            q, new = res if carry is not None else (res, [])
            put_back(idxs, new)
            q = seqs(q)
            o = _attn_fwd(f"attn{j}", q, kn, vh, row(q_norm[j]), sinks3[j]).reshape(T, D)
            idxs2, carry2 = carrying([], second)
            res = _proj(f"o_proj{j}", o, W("o", j), 0, tm, resid=h, carry=carry2)
            h, new = res if carry2 is not None else (res, [])
            put_back(idxs2, new)
            st.update(q=q, o=o)
        st["h_b"] = h
        u2 = _rowwise(f"rms_ffn{l}", _rms_fn, [h], [row(norm_ffn[l])], [bf16], tr)[0]
        nmix, ngu, nd = layer_slabs(l + 1) if l + 1 < NL else ([], [], [])
        idxs, carry = carrying(nmix + nd, [])
        dgate_of, dup_of, hid, *new = _ffn_up(f"ffn_up{l}", u2, W("g", l), W("u", l), 0, tm, carry=carry)
        put_back(idxs, new)
        idxs, carry = carrying(ngu[:1], nmix + nd)
        h, *new = _ffn_down(f"ffn_down{l}", hid, W("d", l), 0, h, tm, carry=carry)
        put_back(idxs, new)
        st.update(u2=u2, dgate_of=dgate_of, dup_of=dup_of, hid=hid)
        saved.append(st)

    dh, part = _loss_head(h, tgt, Lp, SEQ, tr)
    loss = lax.psum(0.5 / D * jnp.sum(part), ("x", "y", "c"))

    grad_slab = {}
    n_layers = dict(cin=NA, cout=NA, kv=1, q=NB, o=NB, g=NL, u=NL, d=NL)

    def rs_begin(pieces):
        job = []
        for key, lay, g in pieces:
            g4 = g.reshape(Q, 2, g.shape[1] // 2, g.shape[2])
            job.append(dict(key=key, l=lay, g=g4, r1=lax.empty((Q,) + g4.shape[2:], bf16)))
        return job

    def xchg_carry(job):
        n = len(job)
        return [p["g"] for p in job] + [p["r1"] for p in job], [("xchg", t, n + t) for t in range(n)]

    def after_xchg(job, new):
        n = len(job)
        for t, p in enumerate(job):
            p["s"] = _sum_pair(f"sum_pair_{p['key']}{p['l']}", new[t], new[n + t], my_c)
            p["r2"] = lax.empty((3,) + p["s"].shape[1:], bf16)

    def scat_carry(job, keys):
        sel = [p for p in job if p["key"] in keys]
        n = len(sel)
        return sel, ([p["s"] for p in sel] + [p["r2"] for p in sel], [("scat", t, n + t) for t in range(n)])

    def after_scat(sel, new):
        n = len(sel)
        for t, p in enumerate(sel):
            key = p["key"]
            grad_slab[key] = _sum_owner(f"sum_owner_{key}{p['l']}", new[t], new[n + t], my_qc, p["l"], n_layers[key], grad_slab.get(key))

    def share_carry(job):
        return [grad_slab[p["key"]] for p in job], [("share", t, p["l"]) for t, p in enumerate(job)]

    def after_share(job, new):
        for p, s in zip(job, new):
            grad_slab[p["key"]] = s

    def carried(job, make):
        return make(job) if job else None

    def merge(*carries):
        slabs_, plan_ = [], []
        for c in carries:
            if c is not None:
                off = len(slabs_)
                slabs_ += c[0]
                plan_ += [(s[0], s[1] + off, s[2]) if s[0] == "share" else (s[0],) + tuple(i + off for i in s[1:]) for s in c[1]]
        return (slabs_, plan_) if slabs_ else None

    MIXER = ("cin", "cout", "q", "o", "kv")
    g_mix, g_ffn = [None] * NL, [None] * NL
    g_bin, g_dw, g_lng, g_lnb, g_bout = ([None] * NA for _ in range(5))
    g_qn, g_sink = [None] * NB, [None] * NB
    dknp = dvp = None
    zero_row = jnp.zeros((1, D), f32)
    job = []
    early = []
    for l in reversed(range(NL)):
        st = saved[l]
        pieces = []
        dgate, dup, *new = _ffn_dhid(f"ffn_dhid{l}", dh, W("d", l), 0, st["dgate_of"], st["dup_of"], tm, carry=carried(job, xchg_carry))
        if job:
            after_xchg(job, new)
        sel, carry = scat_carry(job, ("d",)) if job else ([], None)
        dwd, *new = _dw_rows(f"ffn_dwd{l}", st["hid"], dh, Q, tk, carry=carry)
        after_scat(sel, new)
        sel, carry = scat_carry(job, ("g", "u")) if job else ([], None)
        dwg, dwu, *new = _dw_cols(f"ffn_dwgu{l}", st["u2"], dgate, Q, tk, Q, carry=carry, dyc2=dup)
        after_scat(sel, new)
        pieces += [("d", l, dwd), ("g", l, dwg), ("u", l, dwu)]
        if l == 0 and l < NA:
            early, pieces = rs_begin(pieces), []
        sel, carry = scat_carry(job, MIXER) if job else ([], None)
        du2, *new = _ffn_du(f"ffn_du{l}", dgate, dup, W("g", l), W("u", l), 0, tm, carry=merge(carry, carried(early, xchg_carry)))
        after_scat(sel, new[:2 * len(sel)])
        if early:
            after_xchg(early, new[2 * len(sel):])
        dh, g_ffn[l] = _rowwise_vjp(f"rms_ffn_bwd{l}", _rms_res_fn, [st["h_b"]], [row(norm_ffn[l])], [du2, dh], [f32], tr)
        if l < NA:
            esel, ecarry = scat_carry(early, ("d",)) if early else ([], None)
            ds, *new = _proj_dx(f"conv_out_dx{l}", dh, W("cout", l), 0, tm, carry=merge(carried(job, share_carry), ecarry))
            after_share(job, new[:len(job)])
            after_scat(esel, new[len(job):])
            pieces.append(("cout", l, _dw_rows(f"conv_out_dw{l}", st["s"], dh, Q, tk)[0]))
            dcv, g_lng[l], g_lnb[l] = _rowwise_vjp(f"ln_silu_bwd{l}", _ln_silu_fn, [st["cv"]], [row(ln_g_f[l]), row(ln_b_f[l])], [ds], [f32], tr)
            dp, g_dw[l] = _dwconv_bwd(f"dwconv_bwd{l}", dcv.reshape(B, Lp, D), st["p"].reshape(B, Lp, D), dw_pad[l])
            dav, dag, dbv, dbg = _rowwise_vjp(f"glu_bwd{l}", _glu_fn, [st["av"], st["ag"]], [zero_row, zero_row], [dp.reshape(T, D)], [bf16, bf16], tr)
            g_bin[l] = jnp.concatenate([dbv, dbg], axis=1)
            esel, ecarry = scat_carry(early, ("g",)) if early else ([], None)
            dwin, *new = _dw_cols(f"glu_dwv{l}", st["u"], dav, Q // 2, tk, Q, carry=ecarry)
            after_scat(esel, new)
            esel, ecarry = scat_carry(early, ("u",)) if early else ([], None)
            dwin, *new = _dw_cols(f"glu_dwg{l}", st["u"], dag, Q // 2, tk, Q, q_off=Q // 2, into=dwin, carry=ecarry)
            after_scat(esel, new)
            pieces.append(("cin", l, dwin))
            du = _glu_du(f"glu_du{l}", dav, dag, W("cin", l), 0, tm)
            dh, g_mix[l], g_bout[l] = _rowwise_vjp(f"rms_mix_bwd{l}", _rms_res_bias_fn, [st["h_a"]], [row(norm_mix[l]), zero_row], [du, dh], [f32], tr)
        else:
            j = l - NA
            do, *new = _proj_dx(f"o_proj_dx{j}", dh, W("o", j), 0, tm, out_dtype=bf16, carry=carried(job, share_carry))
            after_share(job, new)
            pieces.append(("o", j, _dw_rows(f"o_proj_dw{j}", st["o"], dh, Q, tk)[0]))
            dq, dk1, dv1, g_qn[j], g_sink[j] = _attn_bwd(f"attn_bwd{j}", st["q"], kn, vh, row(q_norm[j]), sinks3[j], seqs(do))
            dknp, dvp = (dk1, dv1) if dknp is None else (dknp + dk1, dvp + dv1)
            dq = dq.reshape(T, D)
            pieces.append(("q", j, _dw_rows(f"q_proj_dw{j}", st["u"], dq, Q, tk)[0]))
            du = _proj_dx(f"q_proj_dx{j}", dq, W("q", j), 0, tm)[0]
            dh, g_mix[l] = _rowwise_vjp(f"rms_mix_bwd{l}", _rms_res_fn, [st["h_a"]], [row(norm_mix[l])], [du, dh], [f32], tr)
            if j == 0:
                dkv, g_kn = _kv_heads_bwd("kv_heads_bwd", st["kv"], row(k_norm), dknp.reshape(G, T, HEAD_DIM), dvp.reshape(G, T, HEAD_DIM), tm)
                pieces.append(("kv", 0, _dw_rows("kv_proj_dw", st["kvn"], dkv, Q, tk)[0]))
                dkvn = _proj_dx("kv_proj_dx", dkv, W("kv", 0), 0, tm)[0]
                dh, g_kvn = _rowwise_vjp("rms_kv_bwd", _rms_res_fn, [st["h_a"]], [row(kv_norm)], [dkvn, dh], [f32], tr)
        job = rs_begin(pieces)
    dh3 = dh.reshape(B, Lp, D)
    grad_x = _take_rows("grad_x", dh3, N_META_ROWS, SEQ)
    g_meta = jnp.sum(dh3[:, :N_META_ROWS], axis=0)

    sl, plan = xchg_carry(job)
    after_xchg(job, _comm_call("rs_exchange", sl, [plan]))
    sel, (sl, plan) = scat_carry(job, tuple(n_layers))
    after_scat(sel, _comm_call("rs_scatter", sl, [plan]))
    sl, plan = share_carry(job + early)
    after_share(job + early, _comm_call("rs_share", sl, [plan]))

    names = ["conv_w_in", "conv_w_out", "w_kv", "w_q", "w_o", "ffn_w_gate", "ffn_w_up", "ffn_w_down"]
    ws = dict(zip(names, big))
    ms = dict(zip(names, [m_conv_w_in, m_conv_w_out, m_w_kv, m_w_q, m_w_o, m_ffn_w_gate, m_ffn_w_up, m_ffn_w_down]))
    vs = dict(zip(names, [v_conv_w_in, v_conv_w_out, v_w_kv, v_w_q, v_w_o, v_ffn_w_gate, v_ffn_w_up, v_ffn_w_down]))
    out_g, out_d, out_m, out_v = {}, {}, {}, {}
    for nm, key in zip(names, ("cin", "cout", "kv", "q", "o", "g", "u", "d")):
        w, gsh = ws[nm], grad_slab[key]
        flat = lambda a: a.reshape(-1, w.shape[-1])
        g2, d2, m2, v2 = _adamw(f"adamw_{nm}", flat(w), flat(gsh), flat(ms[nm]), flat(vs[nm]))
        out_g[nm], out_d[nm], out_m[nm], out_v[nm] = (a.reshape(w.shape) for a in (g2, d2, m2, v2))

    small_names = ["norm_mix", "norm_ffn", "kv_norm", "k_norm", "q_norm", "attn_sinks", "meta_tokens", "conv_b_in", "conv_dw", "conv_ln_g", "conv_ln_b", "conv_b_out"]
    small_grads = [jnp.concatenate(g_mix, 0), jnp.concatenate(g_ffn, 0), g_kvn.reshape(-1), g_kn.reshape(-1), jnp.concatenate(g_qn, 0),
                   jnp.stack(g_sink).reshape(NB, G * R), g_meta, jnp.concatenate(g_bin, 0), jnp.stack(g_dw)[:, :CONV_TAPS],
                   jnp.concatenate(g_lng, 0), jnp.concatenate(g_lnb, 0), jnp.concatenate(g_bout, 0)]
    slab = _pack(small_grads, 1024)
    total = _sum_devices(_gather_all_devices(slab).reshape(8, slab.shape[0], 128))
    full_grads = _unpack(total, [g.shape for g in small_grads])
    small_w = dict(zip(small_names, [norm_mix, norm_ffn, kv_norm, k_norm, q_norm, attn_sinks, meta_tokens, conv_b_in, conv_dw, conv_ln_g, conv_ln_b, conv_b_out]))
    small_m = dict(zip(small_names, [m_norm_mix, m_norm_ffn, m_kv_norm, m_k_norm, m_q_norm, m_attn_sinks, m_meta_tokens, m_conv_b_in, m_conv_dw, m_conv_ln_g, m_conv_ln_b, m_conv_b_out]))
    small_v = dict(zip(small_names, [v_norm_mix, v_norm_ffn, v_kv_norm, v_k_norm, v_q_norm, v_attn_sinks, v_meta_tokens, v_conv_b_in, v_conv_dw, v_conv_ln_g, v_conv_ln_b, v_conv_b_out]))
    local_grads = []
    for nm, g in zip(small_names, full_grads):
        w = small_w[nm]
        if g.shape != w.shape:
            wq = w.shape[-1]
            g = lax.dynamic_slice_in_dim(g, my_q * wq, wq, axis=g.ndim - 1)
        local_grads.append(g)
    shapes = [small_w[nm].shape for nm in small_names]
    _, d_s, m_s, v_s = _adamw("adamw_small", _pack([small_w[nm] for nm in small_names], 1024), _pack(local_grads, 1024),
                           _pack([small_m[nm] for nm in small_names], 1024), _pack([small_v[nm] for nm in small_names], 1024))
    for nm, g, d_, m_, v_ in zip(small_names, local_grads, _unpack(d_s, shapes), _unpack(m_s, shapes), _unpack(v_s, shapes)):
        out_g[nm], out_d[nm], out_m[nm], out_v[nm] = g, d_, m_, v_

    order = ["meta_tokens", "norm_mix", "norm_ffn", "conv_w_in", "conv_b_in", "conv_dw", "conv_ln_g", "conv_ln_b", "conv_w_out", "conv_b_out",
             "kv_norm", "w_kv", "k_norm", "w_q", "q_norm", "attn_sinks", "w_o", "ffn_w_gate", "ffn_w_up", "ffn_w_down"]
    return (loss, grad_x, *[out_g[n] for n in order], *[out_d[n] for n in order], *[out_m[n] for n in order], *[out_v[n] for n in order])
```

```python
import functools

import jax
import jax.numpy as jnp
from jax import lax
from jax.experimental import pallas as pl
from jax.experimental.pallas import tpu as pltpu

f32, bf16 = jnp.float32, jnp.bfloat16

N_META_ROWS = 16
ATT_BLOCK = 128
HEAD_DIM = 64
N_KV = 4
CONV_TAPS = 31
CONV_PAD = 32
EPS = 1e-6
MASKED = -1e30
LR, B1, B2, ADAM_EPS, WD, STEP = 0.001, 0.9, 0.999, 1e-08, 0.01, 10
N_CHIPS = 4
VMEM_LIMIT_BYTES = 56 * 1024 * 1024
MESH = pl.DeviceIdType.MESH
ANY = pl.BlockSpec(memory_space=pl.ANY)
S = jax.ShapeDtypeStruct


def _pcall(body, **kw):
    return pl.pallas_call(body, **kw)


def _params(sem=None):
    return pltpu.CompilerParams(dimension_semantics=sem, vmem_limit_bytes=VMEM_LIMIT_BYTES)


def _tile(n, prefs):
    for p in prefs:
        if n % p == 0:
            return p
    return n


_DN = {"nn": (((1,), (0,)), ((), ())), "nt": (((1,), (1,)), ((), ())), "tn": (((0,), (0,)), ((), ()))}


def _matmul(name, mode, grid, a_ops, b_ops, x_ops, outs, terms, acc_shape, n_acc, epilogue, into=None, carry=None):
    na, nb, nx, no = len(a_ops), len(b_ops), len(x_ops), len(outs)
    nk = grid[2]
    slabs, plan = carry if carry is not None else ([], [])
    nc, ncp = len(slabs), _plan_copies(plan)
    n_in = na + nb + nx + (0 if into is None else 1) + nc

    def flat2d(v):
        return v.reshape(-1, v.shape[-1]) if v.ndim == 3 else v

    def dot(a, b):
        return lax.dot_general(a.astype(bf16), b.astype(bf16), _DN[mode], preferred_element_type=f32)

    def dots(a_refs, b_refs):
        parts = [None] * n_acc
        shared = {}

        def lhs(ai):
            if ai not in shared:
                a = flat2d(a_refs[ai][...]).astype(bf16)
                shared[ai] = a.T if mode == "tn" and n_terms_of[ai] > 1 else a
            return shared[ai]

        for ai, bi, ci in terms:
            a_ref, b_ref = a_refs[ai], b_refs[bi]
            if len(a_ref.shape) == 3 and len(b_ref.shape) == 3:
                for c in range(a_ref.shape[0]):
                    d = dot(a_ref[c], b_ref[c])
                    parts[ci] = d if parts[ci] is None else parts[ci] + d
            else:
                dn = _DN["nn"] if mode == "tn" and n_terms_of[ai] > 1 else _DN[mode]
                d = lax.dot_general(lhs(ai), flat2d(b_ref[...]).astype(bf16), dn, preferred_element_type=f32)
                parts[ci] = d if parts[ci] is None else parts[ci] + d
        return parts

    n_terms_of = {ai: sum(1 for t in terms if t[0] == ai) for ai in range(na)}

    def finish(accs, x_refs, o_refs):
        res = epilogue(accs, [x[...] for x in x_refs])
        for o_ref, r in zip(o_refs, res):
            o_ref[...] = r.reshape(o_ref.shape).astype(o_ref.dtype)

    def compute(a_refs, b_refs, x_refs, o_refs, acc_refs):
        if nk == 1:
            finish(dots(a_refs, b_refs), x_refs, o_refs)
            return
        k = pl.program_id(2)

        @pl.when(k == 0)
        def _():
            for acc in acc_refs:
                acc[...] = jnp.zeros_like(acc)

        for acc, d in zip(acc_refs, dots(a_refs, b_refs)):
            acc[...] += d

        @pl.when(k == nk - 1)
        def _():
            finish([acc[...] for acc in acc_refs], x_refs, o_refs)

    def body(*refs):
        a_refs, b_refs = refs[:na], refs[na:na + nb]
        x_refs = refs[na + nb:na + nb + nx]
        o_refs = refs[n_in:n_in + no]
        bufs = refs[n_in + no:n_in + no + nc]
        scratch = refs[n_in + no + nc:]
        if not nc:
            compute(a_refs, b_refs, x_refs, o_refs, scratch)
            return
        acc_refs, (send, recv) = scratch[:-2], scratch[-2:]
        i, j, k = pl.program_id(0), pl.program_id(1), pl.program_id(2)

        @pl.when((i == 0) & (j == 0) & (k == 0))
        def _():
            _run_copies(bufs, plan, send, recv, start=True, wait=False)

        compute(a_refs, b_refs, x_refs, o_refs, acc_refs)

        @pl.when((i == grid[0] - 1) & (j == grid[1] - 1) & (k == nk - 1))
        def _():
            _run_copies(bufs, plan, send, recv, start=False, wait=True)

    ops = list(a_ops) + list(b_ops) + list(x_ops)
    aliases = {}
    if into is not None:
        ops.append((into, ANY))
        aliases = {len(ops) - 1: 0}
    for t, s in enumerate(slabs):
        ops.append((s, ANY))
        aliases[len(ops) - 1] = no + t
    outs = list(outs) + [(S(s.shape, s.dtype), ANY) for s in slabs]
    scratch = [pltpu.VMEM(acc_shape, f32)] * (n_acc if nk > 1 else 0)
    if nc:
        scratch += [pltpu.SemaphoreType.DMA((ncp,)), pltpu.SemaphoreType.DMA((ncp,))]
    sem = ("arbitrary",) * 3 if nc else ("parallel", "parallel", "arbitrary")
    res = _pcall(body, name=name, grid=grid, in_specs=[s for _, s in ops], out_specs=[s for _, s in outs],
                 out_shape=[s for s, _ in outs], scratch_shapes=scratch, input_output_aliases=aliases,
                 compiler_params=_params(sem))(*[a for a, _ in ops])
    return res


def _first(accs, xs):
    return (accs[0],)


def _proj(name, a, w, l, tm, bias=None, resid=None, out_dtype=f32, carry=None, norm=None):
    T = a.shape[0]
    Q, _, Kq, N = w.shape
    tn = N if norm is not None else _tile(N, (512, 256, 128))
    x_ops = []
    if bias is not None:
        x_ops.append((bias, pl.BlockSpec((1, tn), lambda i, j, k: (0, j))))
    if resid is not None:
        x_ops.append((resid, pl.BlockSpec((tm, tn), lambda i, j, k: (i, j))))
    n_add = len(x_ops)
    ospec = pl.BlockSpec((tm, tn), lambda i, j, k: (i, j))
    outs = [(S((T, N), out_dtype), ospec)]
    if norm is not None:
        x_ops.append((norm, pl.BlockSpec((1, tn), lambda i, j, k: (0, j))))
        outs.append((S((T, N), bf16), ospec))

    def epi(accs, xs):
        y = accs[0]
        for x in xs[:n_add]:
            y = y + x
        return (y,) if norm is None else (y, _rms(y, xs[n_add]))

    res = _matmul(name, "nn", (T // tm, N // tn, 1),
                  [(a, pl.BlockSpec((tm, Q * Kq), lambda i, j, k: (i, 0)))],
                  [(w, pl.BlockSpec((Q, None, Kq, tn), lambda i, j, k: (0, l, 0, j)))],
                  x_ops, outs, [(0, 0, 0)], (tm, tn), 1, epi, carry=carry)
    if norm is not None:
        return res
    return res[0] if carry is None else (res[0], res[1:])


def _proj_dx(name, dy, w, l, tm, out_dtype=f32, carry=None):
    T, N = dy.shape
    Q, _, Kq, _ = w.shape
    return _matmul(name, "nt", (T // tm, 1, 1),
                   [(dy, pl.BlockSpec((tm, N), lambda i, j, k: (i, 0)))],
                   [(w, pl.BlockSpec((Q, None, Kq, N), lambda i, j, k: (0, l, 0, 0)))],
                   [], [(S((T, Q * Kq), out_dtype), pl.BlockSpec((tm, Q * Kq), lambda i, j, k: (i, 0)))],
                   [(0, 0, 0)], (tm, Q * Kq), 1, _first, carry=carry)


def _dw_rows(name, a, dy, Q, tk, carry=None):
    T, N = dy.shape
    tn = _tile(N, (512, 256, 128))
    if a.ndim == 2:
        K = a.shape[1]
        a_op, ni, acc_rows = (a, pl.BlockSpec((tk, K), lambda i, j, k: (k, 0))), 1, K
        osh, ospec = S((Q, K // Q, N), bf16), pl.BlockSpec((Q, K // Q, tn), lambda i, j, k: (0, 0, j))
    else:
        Kq = a.shape[2]
        a_op, ni, acc_rows = (a, pl.BlockSpec((None, tk, Kq), lambda i, j, k: (i, k, 0))), Q, Kq
        osh, ospec = S((Q, Kq, N), bf16), pl.BlockSpec((None, Kq, tn), lambda i, j, k: (i, 0, j))
    return _matmul(name, "tn", (ni, N // tn, T // tk), [a_op], [(dy, pl.BlockSpec((tk, tn), lambda i, j, k: (k, j)))],
                   [], [(osh, ospec)], [(0, 0, 0)], (acc_rows, tn), 1, _first, carry=carry)


def _dw_cols(name, a, dyc, Qc, tk, Q, q_off=0, into=None, carry=None, dyc2=None):
    T, K = a.shape
    tkin = _tile(K, (512, 256, 128))
    if dyc.ndim == 3:
        Nq = dyc.shape[2]
        bspec = pl.BlockSpec((None, tk, Nq), lambda i, j, k: (j, k, 0))
    else:
        Nq = dyc.shape[1] // Qc
        bspec = pl.BlockSpec((tk, Nq), lambda i, j, k: (k, j))
    b_ops = [(dyc, bspec)] + ([] if dyc2 is None else [(dyc2, bspec)])
    ospec = pl.BlockSpec((None, tkin, Nq), lambda i, j, k: (j + q_off, i, 0))
    n = len(b_ops)
    return _matmul(name, "tn", (K // tkin, Qc, T // tk),
                   [(a, pl.BlockSpec((tk, tkin), lambda i, j, k: (k, i)))], b_ops, [],
                   [(S((Q, K, Nq), bf16), ospec)] * n, [(0, t, t) for t in range(n)], (tkin, Nq), n,
                   lambda accs, xs: tuple(accs), into=into, carry=carry)


def _glu_fwd(name, u, w, l, b_in, tm, carry=None):
    T, D = u.shape
    Q, _, _, Cq = w.shape
    H = Q // 2

    def epi(accs, xs):
        av, ag = accs[0] + xs[0], accs[1] + xs[1]
        return av * jax.nn.sigmoid(ag), av, ag

    wspec = lambda off: pl.BlockSpec((None, None, D, Cq), lambda i, j, k: (j + off, l, 0, 0))
    bspec = lambda off: pl.BlockSpec((1, Cq), lambda i, j, k: (0, j + off))
    ospec = pl.BlockSpec((tm, Cq), lambda i, j, k: (i, j))
    return _matmul(name, "nn", (T // tm, H, 1),
                   [(u, pl.BlockSpec((tm, D), lambda i, j, k: (i, 0)))],
                   [(w, wspec(0)), (w, wspec(H))], [(b_in, bspec(0)), (b_in, bspec(H))],
                   [(S((T, H * Cq), f32), ospec), (S((T, H * Cq), bf16), ospec), (S((T, H * Cq), bf16), ospec)],
                   [(0, 0, 0), (0, 1, 1)], (tm, Cq), 2, epi, carry=carry)


def _glu_du(name, dav, dag, w, l, tm):
    T = dav.shape[0]
    Q, _, D, Cq = w.shape
    H = Q // 2
    tn = _tile(D, (512, 256, 128))
    aspec = pl.BlockSpec((tm, Cq), lambda i, j, k: (i, k))
    wspec = lambda off: pl.BlockSpec((None, None, tn, Cq), lambda i, j, k: (k + off, l, j, 0))
    return _matmul(name, "nt", (T // tm, D // tn, H), [(dav, aspec), (dag, aspec)],
                   [(w, wspec(0)), (w, wspec(H))], [],
                   [(S((T, D), f32), pl.BlockSpec((tm, tn), lambda i, j, k: (i, j)))],
                   [(0, 0, 0), (1, 1, 0)], (tm, tn), 1, _first)[0]


def _ffn_up(name, u, wg, wu, l, tm, carry=None):
    T, D = u.shape
    Q, _, _, Fq = wg.shape

    def epi(accs, xs):
        g, up = accs
        sg = jax.nn.sigmoid(g)
        silu = g * sg
        return up * (sg * (1.0 + g * (1.0 - sg))), silu, silu * up

    wspec = pl.BlockSpec((None, None, D, Fq), lambda i, j, k: (j, l, 0, 0))
    ospec = pl.BlockSpec((None, tm, Fq), lambda i, j, k: (j, i, 0))
    osh = S((Q, T, Fq), bf16)
    return _matmul(name, "nn", (T // tm, Q, 1), [(u, pl.BlockSpec((tm, D), lambda i, j, k: (i, 0)))],
                   [(wg, wspec), (wu, wspec)], [], [(osh, ospec)] * 3, [(0, 0, 0), (0, 1, 1)], (tm, Fq), 2, epi, carry=carry)


def _ffn_down(name, hid, wd, l, resid, tm, carry=None):
    Q, T, Fq = hid.shape
    D = wd.shape[3]
    tn = _tile(D, (512, 256, 128))
    return _matmul(name, "nn", (T // tm, D // tn, 1),
                   [(hid, pl.BlockSpec((Q, tm, Fq), lambda i, j, k: (0, i, 0)))],
                   [(wd, pl.BlockSpec((Q, None, Fq, tn), lambda i, j, k: (0, l, 0, j)))],
                   [(resid, pl.BlockSpec((tm, tn), lambda i, j, k: (i, j)))],
                   [(S((T, D), f32), pl.BlockSpec((tm, tn), lambda i, j, k: (i, j)))],
                   [(0, 0, 0)], (tm, tn), 1, lambda accs, xs: (accs[0] + xs[0],), carry=carry)


def _ffn_dhid(name, dy, wd, l, dgate_of, dup_of, tm, carry=None):
    T, D = dy.shape
    Q, _, Fq, _ = wd.shape

    def epi(accs, xs):
        return accs[0] * xs[0].astype(f32), accs[0] * xs[1].astype(f32)

    cspec = pl.BlockSpec((None, tm, Fq), lambda i, j, k: (j, i, 0))
    osh = S((Q, T, Fq), bf16)
    return _matmul(name, "nt", (T // tm, Q, 1), [(dy, pl.BlockSpec((tm, D), lambda i, j, k: (i, 0)))],
                   [(wd, pl.BlockSpec((None, None, Fq, D), lambda i, j, k: (j, l, 0, 0)))],
                   [(dgate_of, cspec), (dup_of, cspec)], [(osh, cspec)] * 2, [(0, 0, 0)], (tm, Fq), 1, epi, carry=carry)


def _ffn_du(name, dgate, dup, wg, wu, l, tm, carry=None):
    Q, T, Fq = dgate.shape
    D = wg.shape[2]
    tn = _tile(D, (512, 256, 128))
    tm = tm // 2 if tm % 32 == 0 and tm > 512 else tm
    aspec = pl.BlockSpec((Q, tm, Fq), lambda i, j, k: (0, i, 0))
    wspec = pl.BlockSpec((Q, None, tn, Fq), lambda i, j, k: (0, l, j, 0))
    return _matmul(name, "nt", (T // tm, D // tn, 1), [(dgate, aspec), (dup, aspec)], [(wg, wspec), (wu, wspec)], [],
                   [(S((T, D), f32), pl.BlockSpec((tm, tn), lambda i, j, k: (i, j)))],
                   [(0, 0, 0), (1, 1, 0)], (tm, tn), 1, _first, carry=carry)


def _rowwise(name, fn, rows, params, out_dtypes, tm):
    nr, npar = len(rows), len(params)
    T = rows[0].shape[0]
    shp = jax.eval_shape(fn, *[S((tm, r.shape[1]), f32) for r in rows], *[S(p.shape, f32) for p in params])

    def body(*refs):
        r = [x[...].astype(f32) for x in refs[:nr]]
        p = [x[...] for x in refs[nr:nr + npar]]
        for o_ref, o in zip(refs[nr + npar:], fn(*r, *p)):
            o_ref[...] = o.astype(o_ref.dtype)

    row_spec = lambda w: pl.BlockSpec((tm, w), lambda i: (i, 0))
    par_spec = lambda p: pl.BlockSpec(p.shape, lambda i: (0, 0))
    return _pcall(body, name=name, grid=(T // tm,),
                  in_specs=[row_spec(r.shape[1]) for r in rows] + [par_spec(p) for p in params],
                  out_specs=[row_spec(s.shape[1]) for s in shp],
                  out_shape=[S((T, s.shape[1]), dt) for s, dt in zip(shp, out_dtypes)],
                  compiler_params=_params(("parallel",)))(*rows, *params)


def _rowwise_vjp(name, fn, rows, params, cots, drow_dtypes, tm):
    nr, npar, nc = len(rows), len(params), len(cots)
    T = rows[0].shape[0]

    def body(*refs):
        r = [x[...].astype(f32) for x in refs[:nr]]
        p = [x[...] for x in refs[nr:nr + npar]]
        c = tuple(x[...].astype(f32) for x in refs[nr + npar:nr + npar + nc])
        o_refs = refs[nr + npar + nc:]
        _, vjp = jax.vjp(fn, *r, *p)
        grads = vjp(c)
        for o_ref, g in zip(o_refs[:nr], grads[:nr]):
            o_ref[...] = g.astype(o_ref.dtype)

        @pl.when(pl.program_id(0) == 0)
        def _():
            for o_ref in o_refs[nr:]:
                o_ref[...] = jnp.zeros_like(o_ref)

        for o_ref, g in zip(o_refs[nr:], grads[nr:]):
            o_ref[...] += g

    row_spec = lambda w: pl.BlockSpec((tm, w), lambda i: (i, 0))
    par_spec = lambda p: pl.BlockSpec(p.shape, lambda i: (0, 0))
    return _pcall(body, name=name, grid=(T // tm,),
                  in_specs=[row_spec(r.shape[1]) for r in rows] + [par_spec(p) for p in params] + [row_spec(c.shape[1]) for c in cots],
                  out_specs=[row_spec(r.shape[1]) for r in rows] + [par_spec(p) for p in params],
                  out_shape=[S(r.shape, dt) for r, dt in zip(rows, drow_dtypes)] + [S(p.shape, f32) for p in params],
                  compiler_params=_params(("arbitrary",)))(*rows, *params, *cots)


def _rms(h, g):
    return h * lax.rsqrt(jnp.mean(h * h, axis=-1, keepdims=True) + EPS) * g


def _rms_fn(h, g):
    return (_rms(h, g),)


def _rms_res_fn(h, g):
    return _rms(h, g), h


def _rms_res_bias_fn(h, g, b0):
    return _rms(h, g), h + b0


def _ln_silu_fn(c, g, b):
    mu = jnp.mean(c, axis=-1, keepdims=True)
    var = jnp.mean(jnp.square(c - mu), axis=-1, keepdims=True)
    y = (c - mu) * lax.rsqrt(var + EPS) * g + b
    return (y * jax.nn.sigmoid(y),)


def _glu_fn(av, ag, bv, bg):
    return ((av + bv) * jax.nn.sigmoid(ag + bg),)


def _take_rows(name, x, start, size):
    B, L, D = x.shape
    dc = _tile(D, (256, 128))

    def body(x_ref, o_ref):
        o_ref[...] = x_ref[start:start + size, :]

    return _pcall(body, name=name, grid=(B, D // dc), in_specs=[pl.BlockSpec((None, L, dc), lambda b, j: (b, 0, j))],
                  out_specs=pl.BlockSpec((None, size, dc), lambda b, j: (b, 0, j)), out_shape=S((B, size, D), x.dtype),
                  compiler_params=_params(("parallel", "parallel")))(x)


def _loss_head(h, tgt, Lp, n_real, tm):
    T, D = h.shape

    def body(h_ref, t_ref, dy_ref, part_ref):
        i = pl.program_id(0)
        pos = (i * tm + lax.broadcasted_iota(jnp.int32, (tm, 1), 0)) % Lp
        real = (pos >= N_META_ROWS) & (pos < N_META_ROWS + n_real)
        err = jnp.where(real, h_ref[...] - t_ref[...], 0.0)
        dy_ref[...] = err * (1.0 / D)

        @pl.when(i == 0)
        def _():
            part_ref[...] = jnp.zeros_like(part_ref)

        part_ref[...] += jnp.sum(err * err, axis=0, keepdims=True)

    spec = pl.BlockSpec((tm, D), lambda i: (i, 0))
    return _pcall(body, name="loss_head", grid=(T // tm,), in_specs=[spec, spec],
                  out_specs=[spec, pl.BlockSpec((1, D), lambda i: (0, 0))],
                  out_shape=[S((T, D), f32), S((1, D), f32)], compiler_params=_params(("arbitrary",)))(h, tgt)


CONV_OFF = CONV_PAD - (CONV_TAPS - 1)
WIN_ROWS = ATT_BLOCK + CONV_PAD


def _phases(ph_ref, win):
    n = win.shape[0]
    for b in range(1, 8):
        ph_ref[b - 1, 0:n - 8, :] = win[b:n - 8 + b]

    def tap(o):
        a = (o // 8) * 8
        return win[a:a + ATT_BLOCK] if o % 8 == 0 else ph_ref[o % 8 - 1, a:a + ATT_BLOCK, :]

    return tap


def _phase_scratch(width):
    return pltpu.VMEM((7, WIN_ROWS - 8, width), f32)


def _dwconv_fwd(name, p, dw, carry=None):
    B, Lp, D = p.shape
    dc = _tile(D, (256, 128))
    slabs, plan = carry if carry is not None else ([], [])
    nc = len(slabs)

    def body(*refs):
        p_ref, w_ref = refs[:2]
        o_ref, bufs, ph_ref = refs[2 + nc], refs[3 + nc:3 + 2 * nc], refs[3 + 2 * nc]
        first = (pl.program_id(0) == 0) & (pl.program_id(1) == 0)
        last = (pl.program_id(0) == B - 1) & (pl.program_id(1) == D // dc - 1)
        if nc:
            @pl.when(first)
            def _():
                _run_copies(bufs, plan, refs[-2], refs[-1], start=True, wait=False)

        def tile(win, base):
            tap = _phases(ph_ref, win)
            acc = jnp.zeros((ATT_BLOCK, dc), f32)
            for k in range(CONV_TAPS):
                acc = acc + tap(CONV_OFF + k) * w_ref[k:k + 1, :]
            o_ref[pl.ds(base, ATT_BLOCK), :] = acc

        tile(jnp.concatenate([jnp.zeros((CONV_PAD, dc), f32), p_ref[0:ATT_BLOCK, :]], axis=0), 0)

        def step(r, carry):
            base = pl.multiple_of(r * ATT_BLOCK, ATT_BLOCK)
            tile(p_ref[pl.ds(pl.multiple_of(base - CONV_PAD, CONV_PAD), WIN_ROWS), :], base)
            return carry

        lax.fori_loop(1, Lp // ATT_BLOCK, step, 0)
        if nc:
            @pl.when(last)
            def _():
                _run_copies(bufs, plan, refs[-2], refs[-1], start=False, wait=True)

    seq = pl.BlockSpec((None, Lp, dc), lambda b, j: (b, 0, j))
    sems = [pltpu.SemaphoreType.DMA((_plan_copies(plan),))] * 2 if nc else []
    return _pcall(body, name=name, grid=(B, D // dc),
                  in_specs=[seq, pl.BlockSpec((CONV_PAD, dc), lambda b, j: (0, j))] + [ANY] * nc,
                  out_specs=[seq] + [ANY] * nc, out_shape=[S((B, Lp, D), f32)] + [S(s.shape, s.dtype) for s in slabs],
                  input_output_aliases={2 + t: 1 + t for t in range(nc)}, scratch_shapes=[_phase_scratch(dc)] + sems,
                  compiler_params=_params(("arbitrary", "arbitrary") if nc else ("parallel", "parallel")))(p, dw, *slabs)


def _dwconv_bwd(name, dcv, p, dw):
    B, Lp, D = p.shape
    dcw = _tile(D, (256, 128))
    nblk = Lp // ATT_BLOCK
    assert nblk >= 2
    zeros = lambda: jnp.zeros((CONV_PAD, dcw), f32)

    def body(dc_ref, p_ref, w_ref, dp_ref, ddw_ref, dph_ref, pph_ref, part_ref):
        @pl.when(pl.program_id(1) == 0)
        def _():
            ddw_ref[...] = jnp.zeros_like(ddw_ref)

        part_ref[...] = jnp.zeros_like(part_ref)

        def tile(dwin, pwin, base):
            dtap, ptap = _phases(dph_ref, dwin), _phases(pph_ref, pwin)
            dtile = dwin[0:ATT_BLOCK]
            acc = jnp.zeros((ATT_BLOCK, dcw), f32)
            for k in range(CONV_TAPS):
                acc = acc + dtap(CONV_TAPS - 1 - k) * w_ref[k:k + 1, :]
                part_ref[k] += jnp.sum((dtile * ptap(CONV_OFF + k)).reshape(ATT_BLOCK // 8, 8, dcw), axis=0)
            dp_ref[pl.ds(base, ATT_BLOCK), :] = acc

        tile(dc_ref[0:WIN_ROWS, :], jnp.concatenate([zeros(), p_ref[0:ATT_BLOCK, :]], axis=0), 0)

        def step(r, carry):
            base = pl.multiple_of(r * ATT_BLOCK, ATT_BLOCK)
            tile(dc_ref[pl.ds(base, WIN_ROWS), :], p_ref[pl.ds(pl.multiple_of(base - CONV_PAD, CONV_PAD), WIN_ROWS), :], base)
            return carry

        lax.fori_loop(1, nblk - 1, step, 0)
        last = Lp - ATT_BLOCK
        tile(jnp.concatenate([dc_ref[last:Lp, :], zeros()], axis=0), p_ref[last - CONV_PAD:Lp, :], last)
        ddw_ref[...] += jnp.sum(part_ref[...], axis=1)

    seq = pl.BlockSpec((None, Lp, dcw), lambda j, b: (b, 0, j))
    wsp = pl.BlockSpec((CONV_PAD, dcw), lambda j, b: (0, j))
    return _pcall(body, name=name, grid=(D // dcw, B), in_specs=[seq, seq, wsp], out_specs=[seq, wsp],
                  out_shape=[S((B, Lp, D), f32), S((CONV_PAD, D), f32)],
                  scratch_shapes=[_phase_scratch(dcw)] * 2 + [pltpu.VMEM((CONV_PAD, 8, dcw), f32)],
                  compiler_params=_params(("parallel", "arbitrary")))(dcv, p, dw)


def _band_start(n):
    return pl.multiple_of(jnp.maximum(n - 1, 0) * ATT_BLOCK, ATT_BLOCK)


def _attn_bias(R):
    row = jnp.arange(R * ATT_BLOCK)[:, None] % ATT_BLOCK
    col = jnp.arange(3 * ATT_BLOCK)[None, :]
    out = []
    for n in range(3):
        qpos = n * ATT_BLOCK + row
        meta_ok = (col < N_META_ROWS) & (col <= qpos)
        band_pos = max(n - 1, 0) * ATT_BLOCK + (col - ATT_BLOCK)
        diff = qpos - band_pos
        band_ok = (col >= ATT_BLOCK) & (diff >= 0) & (diff < ATT_BLOCK) & (band_pos >= N_META_ROWS)
        out.append(jnp.where(meta_ok | band_ok, 0.0, MASKED))
    return jnp.stack(out).astype(f32)


def _row_head(R):
    row = lax.broadcasted_iota(jnp.int32, (R * ATT_BLOCK, R), 0)
    lo = lax.broadcasted_iota(jnp.int32, (R * ATT_BLOCK, R), 1) * ATT_BLOCK
    return ((row >= lo) & (row < lo + ATT_BLOCK)).astype(f32)


def _attn_core(q, kn, vv, qg, sink_vec, row_head, bias):
    qn = _rms(q, qg)
    s = lax.dot_general(qn.astype(bf16), kn.astype(bf16), _DN["nt"], preferred_element_type=f32) * (HEAD_DIM ** -0.5)
    s = s + bias
    sink = jnp.sum(row_head * sink_vec, axis=-1, keepdims=True)
    m = lax.stop_gradient(jnp.maximum(jnp.max(s, axis=-1, keepdims=True), sink))
    p = jnp.exp(s - m)
    denom = jnp.sum(p, axis=-1, keepdims=True) + jnp.exp(sink - m)
    return jnp.dot((p / denom).astype(bf16), vv.astype(bf16), preferred_element_type=f32)


def _keys_of(ref, n):
    return jnp.concatenate([ref[0:ATT_BLOCK, :], ref[pl.ds(_band_start(n), 2 * ATT_BLOCK), :]], axis=0)


def _stack_heads(x, R):
    return x if R == 1 else jnp.concatenate([x[:, r * HEAD_DIM:(r + 1) * HEAD_DIM] for r in range(R)], axis=0)


def _unstack_heads(x, R):
    return x if R == 1 else jnp.concatenate([x[r * ATT_BLOCK:(r + 1) * ATT_BLOCK] for r in range(R)], axis=1)


KV_PER_STEP_FWD = 2
KV_PER_STEP_BWD = 1


def _attn_specs(R, Lp, per):
    qspec = pl.BlockSpec((None, ATT_BLOCK, per * R * HEAD_DIM), lambda g, b, n: (b, n, g))
    kspec = pl.BlockSpec((per, None, Lp, HEAD_DIM), lambda g, b, n: (g, b, 0, 0))
    gspec = pl.BlockSpec((1, HEAD_DIM), lambda g, b, n: (0, 0))
    sspec = pl.BlockSpec((per, 1, R), lambda g, b, n: (g, 0, 0))
    bspec = pl.BlockSpec((None, R * ATT_BLOCK, 3 * ATT_BLOCK), lambda g, b, n: (jnp.minimum(n, 2), 0, 0))
    return qspec, kspec, gspec, sspec, bspec


def _heads_of(x, gi, R):
    return x[:, gi * R * HEAD_DIM:(gi + 1) * R * HEAD_DIM]


def _attn_fwd(name, q, kn, v, qg, sinks, carry=None):
    B, Lp, D = q.shape
    G = kn.shape[0]
    R = D // (G * HEAD_DIM)
    per = KV_PER_STEP_FWD
    assert G % per == 0
    qspec, kspec, gspec, sspec, bspec = _attn_specs(R, Lp, per)
    slabs, plan = carry if carry is not None else ([], [])
    nc = len(slabs)
    grid = (G // per, B, Lp // ATT_BLOCK)

    def body(*refs):
        q_ref, k_ref, v_ref, qg_ref, s_ref, b_ref = refs[:6]
        o_ref, bufs = refs[6 + nc], refs[7 + nc:7 + 2 * nc]
        g, b, n = pl.program_id(0), pl.program_id(1), pl.program_id(2)
        if nc:
            @pl.when((g == 0) & (b == 0) & (n == 0))
            def _():
                _run_copies(bufs, plan, refs[-2], refs[-1], start=True, wait=False)

        q_all, outs = q_ref[...], []
        for gi in range(per):
            o = _attn_core(_stack_heads(_heads_of(q_all, gi, R), R), _keys_of(k_ref.at[gi], n), _keys_of(v_ref.at[gi], n),
                           qg_ref[...], s_ref[gi], _row_head(R), b_ref[...])
            outs.append(_unstack_heads(o, R))
        o_ref[...] = jnp.concatenate(outs, axis=1).astype(o_ref.dtype)
        if nc:
            @pl.when((g == grid[0] - 1) & (b == grid[1] - 1) & (n == grid[2] - 1))
            def _():
                _run_copies(bufs, plan, refs[-2], refs[-1], start=False, wait=True)

    sems = [pltpu.SemaphoreType.DMA((_plan_copies(plan),))] * 2 if nc else []
    return _pcall(body, name=name, grid=grid, in_specs=[qspec, kspec, kspec, gspec, sspec, bspec] + [ANY] * nc,
                  out_specs=[qspec] + [ANY] * nc, out_shape=[S(q.shape, bf16)] + [S(s.shape, s.dtype) for s in slabs],
                  input_output_aliases={6 + t: 1 + t for t in range(nc)}, scratch_shapes=sems,
                  compiler_params=_params(("arbitrary",) * 3 if nc else ("parallel",) * 3))(q, kn, v, qg, sinks, _attn_bias(R), *slabs)


def _attn_bwd(name, q, kn, v, qg, sinks, do):
    B, Lp, D = q.shape
    G = kn.shape[0]
    R = D // (G * HEAD_DIM)
    per = KV_PER_STEP_BWD
    qspec, kspec, gspec, sspec, bspec = _attn_specs(R, Lp, per)

    def body(q_ref, k_ref, v_ref, qg_ref, s_ref, b_ref, do_ref, dq_ref, dk_ref, dv_ref, dqg_ref, ds_ref):
        g, b, n = pl.program_id(0), pl.program_id(1), pl.program_id(2)

        @pl.when((g == 0) & (b == 0) & (n == 0))
        def _():
            dqg_ref[...] = jnp.zeros_like(dqg_ref)

        @pl.when((b == 0) & (n == 0))
        def _():
            ds_ref[...] = jnp.zeros_like(ds_ref)

        @pl.when(n == 0)
        def _():
            dk_ref[...] = jnp.zeros_like(dk_ref)
            dv_ref[...] = jnp.zeros_like(dv_ref)

        row_head, bias = _row_head(R), b_ref[...]
        q_all, do_all, dqs = q_ref[...], do_ref[...].astype(f32), []
        band = pl.ds(_band_start(n), 2 * ATT_BLOCK)
        for gi in range(per):
            _, vjp = jax.vjp(lambda q_, k_, v_, a_, s_: _attn_core(q_, k_, v_, a_, s_, row_head, bias),
                             _stack_heads(_heads_of(q_all, gi, R), R), _keys_of(k_ref.at[gi], n).astype(f32),
                             _keys_of(v_ref.at[gi], n).astype(f32), qg_ref[...], s_ref[gi])
            dq, dkk, dvv, dqg, dsk = vjp(_stack_heads(_heads_of(do_all, gi, R), R))
            dqs.append(_unstack_heads(dq, R))
            dqg_ref[...] += dqg
            ds_ref[gi] += dsk
            dk_ref[gi, band, :] += dkk[ATT_BLOCK:]
            dv_ref[gi, band, :] += dvv[ATT_BLOCK:]
            dk_ref[gi, 0:ATT_BLOCK, :] += dkk[:ATT_BLOCK]
            dv_ref[gi, 0:ATT_BLOCK, :] += dvv[:ATT_BLOCK]
        dq_ref[...] = jnp.concatenate(dqs, axis=1)

    return _pcall(body, name=name, grid=(G // per, B, Lp // ATT_BLOCK),
                  in_specs=[qspec, kspec, kspec, gspec, sspec, bspec, qspec],
                  out_specs=[qspec, kspec, kspec, gspec, sspec],
                  out_shape=[S(q.shape, f32), S(kn.shape, f32), S(v.shape, f32), S(qg.shape, f32), S(sinks.shape, f32)],
                  compiler_params=_params(("arbitrary", "arbitrary", "arbitrary")))(q, kn, v, qg, sinks, _attn_bias(R), do)


def _kv_heads(name, kv, kg, tm):
    T, W = kv.shape
    G = W // (2 * HEAD_DIM)

    def body(kv_ref, kg_ref, k_ref, v_ref):
        x = kv_ref[...]
        for g in range(G):
            k_ref[g] = _rms(x[:, g * HEAD_DIM:(g + 1) * HEAD_DIM], kg_ref[...]).astype(bf16)
            v_ref[g] = x[:, (G + g) * HEAD_DIM:(G + g + 1) * HEAD_DIM].astype(bf16)

    hspec = pl.BlockSpec((G, tm, HEAD_DIM), lambda i: (0, i, 0))
    return _pcall(body, name=name, grid=(T // tm,),
                  in_specs=[pl.BlockSpec((tm, W), lambda i: (i, 0)), pl.BlockSpec((1, HEAD_DIM), lambda i: (0, 0))],
                  out_specs=[hspec, hspec], out_shape=[S((G, T, HEAD_DIM), bf16)] * 2,
                  compiler_params=_params(("parallel",)))(kv, kg)


def _kv_heads_bwd(name, kv, kg, dkn, dv, tm):
    T, W = kv.shape
    G = W // (2 * HEAD_DIM)

    def body(kv_ref, kg_ref, dk_ref, dv_ref, o_ref, dkg_ref):
        @pl.when(pl.program_id(0) == 0)
        def _():
            dkg_ref[...] = jnp.zeros_like(dkg_ref)

        x = kv_ref[...]
        pieces = []
        for g in range(G):
            _, vjp = jax.vjp(_rms, x[:, g * HEAD_DIM:(g + 1) * HEAD_DIM], kg_ref[...])
            dk, dkg = vjp(dk_ref[g])
            pieces.append(dk)
            dkg_ref[...] += dkg
        o_ref[...] = jnp.concatenate(pieces + [dv_ref[g] for g in range(G)], axis=1)

    hspec = pl.BlockSpec((G, tm, HEAD_DIM), lambda i: (0, i, 0))
    gspec = pl.BlockSpec((1, HEAD_DIM), lambda i: (0, 0))
    return _pcall(body, name=name, grid=(T // tm,),
                  in_specs=[pl.BlockSpec((tm, W), lambda i: (i, 0)), gspec, hspec, hspec],
                  out_specs=[pl.BlockSpec((tm, W), lambda i: (i, 0)), gspec],
                  out_shape=[S((T, W), f32), S((1, HEAD_DIM), f32)], compiler_params=_params(("arbitrary",)))(kv, kg, dkn, dv)


def _place():
    x, y, c = lax.axis_index("x"), lax.axis_index("y"), lax.axis_index("c")
    chips = [(1 - x, y), (x, 1 - y), (1 - x, 1 - y)]
    return x, y, c, chips, [2 * cx + cy for cx, cy in chips]


def _remote(src, dst, send_sem, recv_sem, to):
    return pltpu.make_async_remote_copy(src_ref=src, dst_ref=dst, send_sem=send_sem, recv_sem=recv_sem,
                                        device_id=to, device_id_type=MESH)


def _into_slot(name, w, l, idx, slots, dtype):
    _, Rr, Cc = w.shape
    tr = _tile(Rr, (512, 256, 128, 64, 32, 16))

    def body(i_ref, x_ref, o_ref):
        o_ref[...] = x_ref[...].astype(o_ref.dtype)

    gs = pltpu.PrefetchScalarGridSpec(
        num_scalar_prefetch=1, grid=(Rr // tr,),
        in_specs=[pl.BlockSpec((None, tr, Cc), lambda i, i_ref: (l, i, 0))],
        out_specs=pl.BlockSpec((None, tr, Cc), lambda i, i_ref: (i_ref[0], i, 0)))
    return _pcall(body, name=name, grid_spec=gs, out_shape=S((slots, Rr, Cc), dtype), compiler_params=_params(("parallel",)))(idx, w)


_PLAN_COPIES = dict(ici=3, fwd=3, xchg=N_CHIPS, scat=3, share=1)


def _plan_copies(plan):
    return sum(_PLAN_COPIES[step[0]] for step in plan)


def _comm_copies(bufs, plan, send, recv):
    x, y, c, chips, qk = _place()
    me_q, sib = 2 * x + y, (x, y, 1 - c)
    starts, lands = [], []

    def add(src, dst, to, land):
        s = len(starts)
        starts.append(_remote(src, dst, send.at[s], recv.at[s], to))
        lands.append(_remote(land, land, send.at[s], recv.at[s], sib))

    for step in plan:
        kind = step[0]
        if kind == "ici":
            mine = bufs[step[1]].at[me_q, c]
            for k in range(3):
                add(mine, mine, (*chips[k], c), bufs[step[1]].at[qk[k], c])
        elif kind == "fwd":
            for k in range(3):
                got = bufs[step[1]].at[qk[k], c]
                add(got, got, sib, bufs[step[1]].at[qk[k], 1 - c])
        elif kind == "xchg":
            for q in range(N_CHIPS):
                add(bufs[step[1]].at[q, 1 - c], bufs[step[2]].at[q], sib, bufs[step[2]].at[q])
        elif kind == "scat":
            for k in range(3):
                add(bufs[step[1]].at[qk[k]], bufs[step[2]].at[k], (*chips[k], c), bufs[step[2]].at[k])
        else:
            mine = bufs[step[1]].at[step[2], c]
            add(mine, mine, sib, bufs[step[1]].at[step[2], 1 - c])
    return starts, lands


def _run_copies(bufs, plan, send, recv, start=True, wait=True):
    starts, lands = _comm_copies(bufs, plan, send, recv)
    if start:
        for cp in starts:
            cp.start()
    if wait:
        for cp in lands:
            cp.wait_recv()
        for cp in starts:
            cp.wait_send()


def _comm_call(name, slabs, phases):
    n = len(slabs)

    def body(*refs):
        bufs, sems = refs[n:2 * n], refs[2 * n:]
        for t, plan in enumerate(phases):
            _run_copies(bufs, plan, sems[2 * t], sems[2 * t + 1])

    sems = [pltpu.SemaphoreType.DMA((_plan_copies(plan),)) for plan in phases for _ in range(2)]
    return _pcall(body, name=name, in_specs=[ANY] * n, out_specs=[ANY] * n, out_shape=[S(s.shape, s.dtype) for s in slabs],
                  input_output_aliases={p: p for p in range(n)}, scratch_shapes=sems)(*slabs)


def _gather_all_devices(block):
    m_per, ncol = block.shape

    def body(x_ref, out_ref, send_sems, recv_sems, local_sem):
        x, y, c, chips, _ = _place()
        me, sib = (x, y, c), (x, y, 1 - c)

        def rows(px, py, pc):
            return out_ref.at[pl.ds((4 * px + 2 * py + pc) * m_per, m_per), :]

        def copy(k, blk, to, src=None):
            return _remote(rows(*blk) if src is None else src, rows(*blk), send_sems.at[k], recv_sems.at[k], to)

        mine = pltpu.make_async_copy(x_ref, rows(*me), local_sem)
        mine.start()
        first = [copy(0, me, sib, src=x_ref)] + [copy(1 + j, me, (*chip, c), src=x_ref) for j, chip in enumerate(chips)]
        for cp in first:
            cp.start()
        passed = [copy(4 + j, (*chip, c), sib) for j, chip in enumerate(chips)]
        for j, chip in enumerate(chips):
            copy(1 + j, (*chip, c), me).wait_recv()
            passed[j].start()
        copy(0, sib, me).wait_recv()
        for j, chip in enumerate(chips):
            copy(4 + j, (*chip, 1 - c), me).wait_recv()
        for cp in first + passed:
            cp.wait_send()
        mine.wait()

    vm = pl.BlockSpec(memory_space=pltpu.VMEM)
    return _pcall(body, name="gather_small_grads", in_specs=[vm], out_specs=vm,
                  out_shape=S((8 * m_per, ncol), block.dtype),
                  scratch_shapes=[pltpu.SemaphoreType.DMA((7,)), pltpu.SemaphoreType.DMA((7,)), pltpu.SemaphoreType.DMA],
                  compiler_params=pltpu.CompilerParams(vmem_limit_bytes=VMEM_LIMIT_BYTES))(block)


def _sum_pair(name, g, r1, c_idx):
    Q, _, Rr, Cc = g.shape
    tr = _tile(Rr, (512, 256, 128))

    def body(c_ref, g_ref, r_ref, o_ref):
        o_ref[...] = (g_ref[...].astype(f32) + r_ref[...].astype(f32)).astype(o_ref.dtype)

    spec = pl.BlockSpec((None, tr, Cc), lambda q, i, c_ref: (q, i, 0))
    gs = pltpu.PrefetchScalarGridSpec(
        num_scalar_prefetch=1, grid=(Q, Rr // tr),
        in_specs=[pl.BlockSpec((None, None, tr, Cc), lambda q, i, c_ref: (q, c_ref[0], i, 0)), spec], out_specs=spec)
    return _pcall(body, name=name, grid_spec=gs, out_shape=S((Q, Rr, Cc), bf16),
                  compiler_params=_params(("parallel", "parallel")))(c_idx, g, r1)


def _sum_owner(name, s, r2, qc_idx, l, nl, into):
    Q, Rr, Cc = s.shape
    tr = _tile(Rr, (512, 256, 128))

    def body(q_ref, s_ref, r_ref, *rest):
        o_ref = rest[-1]
        o_ref[...] = ((s_ref[...].astype(f32) + r_ref[0].astype(f32)) + r_ref[1].astype(f32)) + r_ref[2].astype(f32)

    in_specs = [pl.BlockSpec((None, tr, Cc), lambda i, q_ref: (q_ref[0], i, 0)), pl.BlockSpec((3, tr, Cc), lambda i, q_ref: (0, i, 0))]
    gs = pltpu.PrefetchScalarGridSpec(
        num_scalar_prefetch=1, grid=(Rr // tr,), in_specs=in_specs + ([] if into is None else [ANY]),
        out_specs=pl.BlockSpec((None, None, tr, Cc), lambda i, q_ref: (l, q_ref[1], i, 0)))
    return _pcall(body, name=name, grid_spec=gs, out_shape=S((nl, 2, Rr, Cc), f32),
                  input_output_aliases={} if into is None else {3: 0},
                  compiler_params=_params(("parallel",)))(qc_idx, s, r2, *([] if into is None else [into]))


def _sum_devices(stack):
    n, M, C = stack.shape

    def body(s_ref, o_ref):
        acc = s_ref[0]
        for d in range(1, n):
            acc = acc + s_ref[d]
        o_ref[...] = acc

    return _pcall(body, name="sum_small_grads", out_shape=S((M, C), f32), compiler_params=_params())(stack)


def _adamw(name, w, g, m, v):
    Rr, Cc = w.shape
    tr = _tile(Rr, (256, 128, 64, 32, 16, 8))

    def body(w_ref, g_ref, m_ref, v_ref, go_ref, d_ref, mo_ref, vo_ref):
        g_ = g_ref[...]
        go_ref[...] = g_
        m_ = B1 * m_ref[...] + (1.0 - B1) * g_
        v_ = B2 * v_ref[...] + (1.0 - B2) * jnp.square(g_)
        m_hat = m_ / (1.0 - B1 ** STEP)
        v_hat = v_ / (1.0 - B2 ** STEP)
        d_ref[...] = -LR * (m_hat / (jnp.sqrt(v_hat) + ADAM_EPS) + WD * w_ref[...])
        mo_ref[...] = m_
        vo_ref[...] = v_

    spec = pl.BlockSpec((tr, Cc), lambda i: (i, 0))
    return _pcall(body, name=name, grid=(Rr // tr,), in_specs=[spec] * 4, out_specs=[spec] * 4,
                  out_shape=[S((Rr, Cc), f32)] * 4, compiler_params=_params(("parallel",)))(w, g, m, v)


def _pack(arrs, multiple):
    flat = jnp.concatenate([a.reshape(-1) for a in arrs])
    pad = (-flat.shape[0]) % multiple
    return jnp.pad(flat, (0, pad)).reshape(-1, 128)


def _unpack(slab, shapes):
    flat, out, o = slab.reshape(-1), [], 0
    for shp in shapes:
        n = 1
        for d in shp:
            n *= d
        out.append(flat[o:o + n].reshape(shp))
        o += n
    return out


def kernel(x, meta_tokens, norm_mix, norm_ffn, conv_w_in, conv_b_in, conv_dw, conv_ln_g, conv_ln_b, conv_w_out, conv_b_out, kv_norm, w_kv, k_norm, w_q, q_norm, attn_sinks, w_o, ffn_w_gate, ffn_w_up, ffn_w_down, loss_target, m_meta_tokens, m_norm_mix, m_norm_ffn, m_conv_w_in, m_conv_b_in, m_conv_dw, m_conv_ln_g, m_conv_ln_b, m_conv_w_out, m_conv_b_out, m_kv_norm, m_w_kv, m_k_norm, m_w_q, m_q_norm, m_attn_sinks, m_w_o, m_ffn_w_gate, m_ffn_w_up, m_ffn_w_down, v_meta_tokens, v_norm_mix, v_norm_ffn, v_conv_w_in, v_conv_b_in, v_conv_dw, v_conv_ln_g, v_conv_ln_b, v_conv_w_out, v_conv_b_out, v_kv_norm, v_w_kv, v_k_norm, v_w_q, v_q_norm, v_attn_sinks, v_w_o, v_ffn_w_gate, v_ffn_w_up, v_ffn_w_down):
    Q = N_CHIPS
    B, SEQ, D = x.shape
    L = N_META_ROWS + SEQ
    Lp = -(-L // ATT_BLOCK) * ATT_BLOCK
    T = B * Lp
    NA, NB = conv_w_in.shape[0], w_q.shape[0]
    NL = NA + NB
    Dq = D // Q
    G = N_KV
    R = D // (HEAD_DIM * G)
    KVW = w_kv.shape[1]
    assert NA % 2 == 0 and NB % 2 == 0 and NL % 2 == 0 and (D // Q) % 32 == 0
    tm = _tile(T, (1088, 544, 512, 256, 128))
    tk = _tile(T, (2176, 1088, 544, 512, 256, 128))
    tr = _tile(T, (272, 256, 128))
    my_c = lax.axis_index("c").astype(jnp.int32).reshape(1)
    my_q = (2 * lax.axis_index("x") + lax.axis_index("y")).astype(jnp.int32)
    my_qc = jnp.concatenate([my_q.reshape(1), my_c])

    small_shapes = [meta_tokens.shape, conv_b_in.shape, conv_dw.shape, conv_ln_g.shape, conv_ln_b.shape, conv_b_out.shape]
    small = _pack([meta_tokens, conv_b_in, conv_dw, conv_ln_g, conv_ln_b, conv_b_out], 2048)
    big = [conv_w_in, conv_w_out, w_kv, w_q, w_o, ffn_w_gate, ffn_w_up, ffn_w_down]
    keyed = dict(cin=conv_w_in, cout=conv_w_out, kv=w_kv[None], q=w_q, o=w_o, g=ffn_w_gate, u=ffn_w_up, d=ffn_w_down,
                 small=small[None])
    slabs, where = [], {}
    for key, w3 in keyed.items():
        for l in range(w3.shape[0]):
            s = _into_slot(f"own_{key}{l}", w3, l, my_qc, Q, f32 if key == "small" else bf16)
            where[key, l] = len(slabs)
            slabs.append(s.reshape(Q, 2, s.shape[1] // 2, s.shape[2]))

    def W(key, l):
        s = slabs[where[key, l]]
        return s.reshape(Q, 1, 2 * s.shape[2], s.shape[3])

    def layer_slabs(l):
        j = l - NA
        mix = [("cin", l), ("cout", l)] if l < NA else [("q", j), ("o", j)] + ([("kv", 0)] if j == 0 else [])
        return [where[k] for k in mix], [where["g", l], where["u", l]], [where["d", l]]

    def carry_of(ici, fwd):
        idxs = sorted(set(ici) | set(fwd))
        return idxs, ([slabs[i] for i in idxs], [("ici", idxs.index(i)) for i in ici] + [("fwd", idxs.index(i)) for i in fwd])

    def put_back(idxs, new):
        for i, s in zip(idxs, new):
            slabs[i] = s

    assert NA >= 1
    first = [where["cin", 0], where["small", 0]]
    idxs, (sl, plan) = carry_of(first, first)
    put_back(idxs, _comm_call("gather_first", sl, [[s for s in plan if s[0] == "ici"], [s for s in plan if s[0] == "fwd"]]))
    parts = [_unpack(slabs[where["small", 0]][q], small_shapes) for q in range(Q)]
    meta_f, b_in_f, dw_f, ln_g_f, ln_b_f, b_out_f = [jnp.concatenate([parts[q][i] for q in range(Q)], axis=-1) for i in range(6)]
    dw_pad = jnp.pad(dw_f, ((0, 0), (0, CONV_PAD - CONV_TAPS), (0, 0)))

    h = jnp.concatenate([jnp.broadcast_to(meta_f[None], (B, N_META_ROWS, D)), x, jnp.zeros((B, Lp - L, D), f32)], axis=1).reshape(T, D)
    tgt = jnp.pad(loss_target, ((0, 0), (N_META_ROWS, Lp - L), (0, 0))).reshape(T, D)
    row = lambda a: a.reshape(1, -1)
    seqs = lambda a: a.reshape(B, Lp, a.shape[-1])
    by_seq = lambda a: a.reshape(G, B, Lp, HEAD_DIM)
    saved = []
    sinks3 = attn_sinks.reshape(NB, G, 1, R)
    def carrying(ici, fwd):
        idxs, carry = carry_of(ici, fwd)
        return idxs, (carry if idxs else None)

    for l in range(NL):
        st = {"h_a": h}
        u = _rowwise(f"rms_mix{l}", _rms_fn, [h], [row(norm_mix[l])], [bf16], tr)[0]
        st["u"] = u
        mix_s, (g_s, u_s), d_s = layer_slabs(l)
        idxs, carry = carrying([where["cout", 0], g_s], []) if l == 0 else carrying([], [g_s])
        second = [u_s] + d_s if l == 0 else [u_s]
        if l < NA:
            p, av, ag, *new = _glu_fwd(f"glu{l}", u, W("cin", l), 0, b_in_f[l:l + 1], tm, carry=carry)
            put_back(idxs, new)
            idxs, carry = carrying([u_s] + d_s, [where["cout", 0], g_s]) if l == 0 else carrying([u_s], [])
            cv, *new = _dwconv_fwd(f"dwconv{l}", p.reshape(B, Lp, D), dw_pad[l], carry=carry)
            put_back(idxs, new)
            cv = cv.reshape(T, D)
            s = _rowwise(f"ln_silu{l}", _ln_silu_fn, [cv], [row(ln_g_f[l]), row(ln_b_f[l])], [bf16], tr)[0]
            idxs2, carry2 = carrying([], second)
            h, u2, *new = _proj(f"conv_out{l}", s, W("cout", l), 0, tm, bias=row(b_out_f[l]), resid=h, carry=carry2,
                                norm=row(norm_ffn[l]))
            put_back(idxs2, new)
            st.update(av=av, ag=ag, p=p, cv=cv, s=s)
        else:
            j = l - NA
            if j == 0:
                kvn = _rowwise("rms_kv", _rms_fn, [h], [row(kv_norm)], [bf16], tr)[0]
                kv = _proj("kv_proj", kvn, W("kv", 0), 0, tm)
                kn, vh = _kv_heads("kv_heads", kv, row(k_norm), tm)
                kn, vh = by_seq(kn), by_seq(vh)
                st.update(kvn=kvn, kv=kv)
            res = _proj(f"q_proj{j}", u, W("q", j), 0, tm, carry=carry)
            q, new = res if carry is not None else (res, [])
            put_back(idxs, new)
            q = seqs(q)
            idxs, carry = carrying([u_s], [])
            o, *new = _attn_fwd(f"attn{j}", q, kn, vh, row(q_norm[j]), sinks3[j], carry=carry)
            put_back(idxs, new)
            o = o.reshape(T, D)
            idxs2, carry2 = carrying([], second)
            h, u2, *new = _proj(f"o_proj{j}", o, W("o", j), 0, tm, resid=h, carry=carry2, norm=row(norm_ffn[l]))
            put_back(idxs2, new)
            st.update(q=q, o=o)
        st["h_b"] = h
        nmix, ngu, nd = layer_slabs(l + 1) if l + 1 < NL else ([], [], [])
        idxs, carry = carrying(nmix + nd, [])
        dgate_of, dup_of, hid, *new = _ffn_up(f"ffn_up{l}", u2, W("g", l), W("u", l), 0, tm, carry=carry)
        put_back(idxs, new)
        idxs, carry = carrying(ngu[:1], nmix + nd)
        h, *new = _ffn_down(f"ffn_down{l}", hid, W("d", l), 0, h, tm, carry=carry)
        put_back(idxs, new)
        st.update(u2=u2, dgate_of=dgate_of, dup_of=dup_of, hid=hid)
        saved.append(st)

    dh, part = _loss_head(h, tgt, Lp, SEQ, tr)
    loss = lax.psum(0.5 / D * jnp.sum(part), ("x", "y", "c"))

    grad_slab = {}
    n_layers = dict(cin=NA, cout=NA, kv=1, q=NB, o=NB, g=NL, u=NL, d=NL)

    def rs_begin(pieces):
        job = []
        for key, lay, g in pieces:
            g4 = g.reshape(Q, 2, g.shape[1] // 2, g.shape[2])
            job.append(dict(key=key, l=lay, g=g4, r1=lax.empty((Q,) + g4.shape[2:], bf16)))
        return job

    def xchg_carry(job):
        n = len(job)
        return [p["g"] for p in job] + [p["r1"] for p in job], [("xchg", t, n + t) for t in range(n)]

    def after_xchg(job, new):
        n = len(job)
        for t, p in enumerate(job):
            p["s"] = _sum_pair(f"sum_pair_{p['key']}{p['l']}", new[t], new[n + t], my_c)
            p["r2"] = lax.empty((3,) + p["s"].shape[1:], bf16)

    def scat_carry(job, keys):
        sel = [p for p in job if p["key"] in keys]
        n = len(sel)
        return sel, ([p["s"] for p in sel] + [p["r2"] for p in sel], [("scat", t, n + t) for t in range(n)])

    def after_scat(sel, new):
        n = len(sel)
        for t, p in enumerate(sel):
            key = p["key"]
            grad_slab[key] = _sum_owner(f"sum_owner_{key}{p['l']}", new[t], new[n + t], my_qc, p["l"], n_layers[key], grad_slab.get(key))

    def share_carry(job):
        return [grad_slab[p["key"]] for p in job], [("share", t, p["l"]) for t, p in enumerate(job)]

    def after_share(job, new):
        for p, s in zip(job, new):
            grad_slab[p["key"]] = s

    def carried(job, make):
        return make(job) if job else None

    def merge(*carries):
        slabs_, plan_ = [], []
        for c in carries:
            if c is not None:
                off = len(slabs_)
                slabs_ += c[0]
                plan_ += [(s[0], s[1] + off, s[2]) if s[0] == "share" else (s[0],) + tuple(i + off for i in s[1:]) for s in c[1]]
        return (slabs_, plan_) if slabs_ else None

    MIXER = ("cin", "cout", "q", "o", "kv")
    g_mix, g_ffn = [None] * NL, [None] * NL
    g_bin, g_dw, g_lng, g_lnb, g_bout = ([None] * NA for _ in range(5))
    g_qn, g_sink = [None] * NB, [None] * NB
    dknp = dvp = None
    zero_row = jnp.zeros((1, D), f32)
    job = []
    early = []
    for l in reversed(range(NL)):
        st = saved[l]
        pieces = []
        dgate, dup, *new = _ffn_dhid(f"ffn_dhid{l}", dh, W("d", l), 0, st["dgate_of"], st["dup_of"], tm, carry=carried(job, xchg_carry))
        if job:
            after_xchg(job, new)
        sel, carry = scat_carry(job, ("d",)) if job else ([], None)
        dwd, *new = _dw_rows(f"ffn_dwd{l}", st["hid"], dh, Q, tk, carry=carry)
        after_scat(sel, new)
        sel, carry = scat_carry(job, ("g", "u")) if job else ([], None)
        dwg, dwu, *new = _dw_cols(f"ffn_dwgu{l}", st["u2"], dgate, Q, tk, Q, carry=carry, dyc2=dup)
        after_scat(sel, new)
        pieces += [("d", l, dwd), ("g", l, dwg), ("u", l, dwu)]
        if l == 0 and l < NA:
            early, pieces = rs_begin(pieces), []
        sel, carry = scat_carry(job, MIXER) if job else ([], None)
        du2, *new = _ffn_du(f"ffn_du{l}", dgate, dup, W("g", l), W("u", l), 0, tm, carry=merge(carry, carried(early, xchg_carry)))
        after_scat(sel, new[:2 * len(sel)])
        if early:
            after_xchg(early, new[2 * len(sel):])
        dh, g_ffn[l] = _rowwise_vjp(f"rms_ffn_bwd{l}", _rms_res_fn, [st["h_b"]], [row(norm_ffn[l])], [du2, dh], [f32], tr)
        if l < NA:
            esel, ecarry = scat_carry(early, ("d",)) if early else ([], None)
            ds, *new = _proj_dx(f"conv_out_dx{l}", dh, W("cout", l), 0, tm, carry=merge(carried(job, share_carry), ecarry))
            after_share(job, new[:len(job)])
            after_scat(esel, new[len(job):])
            pieces.append(("cout", l, _dw_rows(f"conv_out_dw{l}", st["s"], dh, Q, tk)[0]))
            dcv, g_lng[l], g_lnb[l] = _rowwise_vjp(f"ln_silu_bwd{l}", _ln_silu_fn, [st["cv"]], [row(ln_g_f[l]), row(ln_b_f[l])], [ds], [f32], tr)
            dp, g_dw[l] = _dwconv_bwd(f"dwconv_bwd{l}", dcv.reshape(B, Lp, D), st["p"].reshape(B, Lp, D), dw_pad[l])
            dav, dag, dbv, dbg = _rowwise_vjp(f"glu_bwd{l}", _glu_fn, [st["av"], st["ag"]], [zero_row, zero_row], [dp.reshape(T, D)], [bf16, bf16], tr)
            g_bin[l] = jnp.concatenate([dbv, dbg], axis=1)
            esel, ecarry = scat_carry(early, ("g",)) if early else ([], None)
            dwin, *new = _dw_cols(f"glu_dwv{l}", st["u"], dav, Q // 2, tk, Q, carry=ecarry)
            after_scat(esel, new)
            esel, ecarry = scat_carry(early, ("u",)) if early else ([], None)
            dwin, *new = _dw_cols(f"glu_dwg{l}", st["u"], dag, Q // 2, tk, Q, q_off=Q // 2, into=dwin, carry=ecarry)
            after_scat(esel, new)
            pieces.append(("cin", l, dwin))
            du = _glu_du(f"glu_du{l}", dav, dag, W("cin", l), 0, tm)
            dh, g_mix[l], g_bout[l] = _rowwise_vjp(f"rms_mix_bwd{l}", _rms_res_bias_fn, [st["h_a"]], [row(norm_mix[l]), zero_row], [du, dh], [f32], tr)
        else:
            j = l - NA
            do, *new = _proj_dx(f"o_proj_dx{j}", dh, W("o", j), 0, tm, out_dtype=bf16, carry=carried(job, share_carry))
            after_share(job, new)
            pieces.append(("o", j, _dw_rows(f"o_proj_dw{j}", st["o"], dh, Q, tk)[0]))
            dq, dk1, dv1, g_qn[j], g_sink[j] = _attn_bwd(f"attn_bwd{j}", st["q"], kn, vh, row(q_norm[j]), sinks3[j], seqs(do))
            dknp, dvp = (dk1, dv1) if dknp is None else (dknp + dk1, dvp + dv1)
            dq = dq.reshape(T, D)
            pieces.append(("q", j, _dw_rows(f"q_proj_dw{j}", st["u"], dq, Q, tk)[0]))
            du = _proj_dx(f"q_proj_dx{j}", dq, W("q", j), 0, tm)[0]
            dh, g_mix[l] = _rowwise_vjp(f"rms_mix_bwd{l}", _rms_res_fn, [st["h_a"]], [row(norm_mix[l])], [du, dh], [f32], tr)
            if j == 0:
                dkv, g_kn = _kv_heads_bwd("kv_heads_bwd", st["kv"], row(k_norm), dknp.reshape(G, T, HEAD_DIM), dvp.reshape(G, T, HEAD_DIM), tm)
                pieces.append(("kv", 0, _dw_rows("kv_proj_dw", st["kvn"], dkv, Q, tk)[0]))
                dkvn = _proj_dx("kv_proj_dx", dkv, W("kv", 0), 0, tm)[0]
                dh, g_kvn = _rowwise_vjp("rms_kv_bwd", _rms_res_fn, [st["h_a"]], [row(kv_norm)], [dkvn, dh], [f32], tr)
        job = rs_begin(pieces)
    dh3 = dh.reshape(B, Lp, D)
    grad_x = _take_rows("grad_x", dh3, N_META_ROWS, SEQ)
    g_meta = jnp.sum(dh3[:, :N_META_ROWS], axis=0)

    sl, plan = xchg_carry(job)
    after_xchg(job, _comm_call("rs_exchange", sl, [plan]))
    sel, (sl, plan) = scat_carry(job, tuple(n_layers))
    after_scat(sel, _comm_call("rs_scatter", sl, [plan]))
    sl, plan = share_carry(job + early)
    after_share(job + early, _comm_call("rs_share", sl, [plan]))

    names = ["conv_w_in", "conv_w_out", "w_kv", "w_q", "w_o", "ffn_w_gate", "ffn_w_up", "ffn_w_down"]
    ws = dict(zip(names, big))
    ms = dict(zip(names, [m_conv_w_in, m_conv_w_out, m_w_kv, m_w_q, m_w_o, m_ffn_w_gate, m_ffn_w_up, m_ffn_w_down]))
    vs = dict(zip(names, [v_conv_w_in, v_conv_w_out, v_w_kv, v_w_q, v_w_o, v_ffn_w_gate, v_ffn_w_up, v_ffn_w_down]))
    out_g, out_d, out_m, out_v = {}, {}, {}, {}
    for nm, key in zip(names, ("cin", "cout", "kv", "q", "o", "g", "u", "d")):
        w, gsh = ws[nm], grad_slab[key]
        flat = lambda a: a.reshape(-1, w.shape[-1])
        g2, d2, m2, v2 = _adamw(f"adamw_{nm}", flat(w), flat(gsh), flat(ms[nm]), flat(vs[nm]))
        out_g[nm], out_d[nm], out_m[nm], out_v[nm] = (a.reshape(w.shape) for a in (g2, d2, m2, v2))

    small_names = ["norm_mix", "norm_ffn", "kv_norm", "k_norm", "q_norm", "attn_sinks", "meta_tokens", "conv_b_in", "conv_dw", "conv_ln_g", "conv_ln_b", "conv_b_out"]
    small_grads = [jnp.concatenate(g_mix, 0), jnp.concatenate(g_ffn, 0), g_kvn.reshape(-1), g_kn.reshape(-1), jnp.concatenate(g_qn, 0),
                   jnp.stack(g_sink).reshape(NB, G * R), g_meta, jnp.concatenate(g_bin, 0), jnp.stack(g_dw)[:, :CONV_TAPS],
                   jnp.concatenate(g_lng, 0), jnp.concatenate(g_lnb, 0), jnp.concatenate(g_bout, 0)]
    slab = _pack(small_grads, 1024)
    total = _sum_devices(_gather_all_devices(slab).reshape(8, slab.shape[0], 128))
    full_grads = _unpack(total, [g.shape for g in small_grads])
    small_w = dict(zip(small_names, [norm_mix, norm_ffn, kv_norm, k_norm, q_norm, attn_sinks, meta_tokens, conv_b_in, conv_dw, conv_ln_g, conv_ln_b, conv_b_out]))
    small_m = dict(zip(small_names, [m_norm_mix, m_norm_ffn, m_kv_norm, m_k_norm, m_q_norm, m_attn_sinks, m_meta_tokens, m_conv_b_in, m_conv_dw, m_conv_ln_g, m_conv_ln_b, m_conv_b_out]))
    small_v = dict(zip(small_names, [v_norm_mix, v_norm_ffn, v_kv_norm, v_k_norm, v_q_norm, v_attn_sinks, v_meta_tokens, v_conv_b_in, v_conv_dw, v_conv_ln_g, v_conv_ln_b, v_conv_b_out]))
    local_grads = []
    for nm, g in zip(small_names, full_grads):
        w = small_w[nm]
        if g.shape != w.shape:
            wq = w.shape[-1]
            g = lax.dynamic_slice_in_dim(g, my_q * wq, wq, axis=g.ndim - 1)
        local_grads.append(g)
    shapes = [small_w[nm].shape for nm in small_names]
    _, d_s, m_s, v_s = _adamw("adamw_small", _pack([small_w[nm] for nm in small_names], 1024), _pack(local_grads, 1024),
                           _pack([small_m[nm] for nm in small_names], 1024), _pack([small_v[nm] for nm in small_names], 1024))
    for nm, g, d_, m_, v_ in zip(small_names, local_grads, _unpack(d_s, shapes), _unpack(m_s, shapes), _unpack(v_s, shapes)):
        out_g[nm], out_d[nm], out_m[nm], out_v[nm] = g, d_, m_, v_

    order = ["meta_tokens", "norm_mix", "norm_ffn", "conv_w_in", "conv_b_in", "conv_dw", "conv_ln_g", "conv_ln_b", "conv_w_out", "conv_b_out",
             "kv_norm", "w_kv", "k_norm", "w_q", "q_norm", "attn_sinks", "w_o", "ffn_w_gate", "ffn_w_up", "ffn_w_down"]
    return (loss, grad_x, *[out_g[n] for n in order], *[out_d[n] for n in order], *[out_m[n] for n in order], *[out_v[n] for n in order])
```

```python
import functools

import jax
import jax.numpy as jnp
from jax import lax
from jax.experimental import pallas as pl
from jax.experimental.pallas import tpu as pltpu

f32, bf16 = jnp.float32, jnp.bfloat16

N_META_ROWS = 16
ATT_BLOCK = 128
HEAD_DIM = 64
N_KV = 4
CONV_TAPS = 31
CONV_PAD = 32
EPS = 1e-6
MASKED = -1e30
LR, B1, B2, ADAM_EPS, WD, STEP = 0.001, 0.9, 0.999, 1e-08, 0.01, 10
N_CHIPS = 4
VMEM_LIMIT_BYTES = 56 * 1024 * 1024
MESH = pl.DeviceIdType.MESH
ANY = pl.BlockSpec(memory_space=pl.ANY)
S = jax.ShapeDtypeStruct


def _pcall(body, **kw):
    return pl.pallas_call(body, **kw)


def _params(sem=None):
    return pltpu.CompilerParams(dimension_semantics=sem, vmem_limit_bytes=VMEM_LIMIT_BYTES)


def _tile(n, prefs):
    for p in prefs:
        if n % p == 0:
            return p
    return n


_DN = {"nn": (((1,), (0,)), ((), ())), "nt": (((1,), (1,)), ((), ())), "tn": (((0,), (0,)), ((), ()))}


def _matmul(name, mode, grid, a_ops, b_ops, x_ops, outs, terms, acc_shape, n_acc, epilogue, into=None, carry=None):
    na, nb, nx, no = len(a_ops), len(b_ops), len(x_ops), len(outs)
    nk = grid[2]
    slabs, plan = carry if carry is not None else ([], [])
    nc, ncp = len(slabs), _plan_copies(plan)
    n_in = na + nb + nx + (0 if into is None else 1) + nc

    def flat2d(v):
        return v.reshape(-1, v.shape[-1]) if v.ndim == 3 else v

    def dot(a, b):
        return lax.dot_general(a.astype(bf16), b.astype(bf16), _DN[mode], preferred_element_type=f32)

    def dots(a_refs, b_refs):
        parts = [None] * n_acc
        shared = {}

        def lhs(ai):
            if ai not in shared:
                a = flat2d(a_refs[ai][...]).astype(bf16)
                shared[ai] = a.T if mode == "tn" and n_terms_of[ai] > 1 else a
            return shared[ai]

        for ai, bi, ci in terms:
            a_ref, b_ref = a_refs[ai], b_refs[bi]
            if len(a_ref.shape) == 3 and len(b_ref.shape) == 3:
                for c in range(a_ref.shape[0]):
                    d = dot(a_ref[c], b_ref[c])
                    parts[ci] = d if parts[ci] is None else parts[ci] + d
            else:
                dn = _DN["nn"] if mode == "tn" and n_terms_of[ai] > 1 else _DN[mode]
                d = lax.dot_general(lhs(ai), flat2d(b_ref[...]).astype(bf16), dn, preferred_element_type=f32)
                parts[ci] = d if parts[ci] is None else parts[ci] + d
        return parts

    n_terms_of = {ai: sum(1 for t in terms if t[0] == ai) for ai in range(na)}

    def finish(accs, x_refs, o_refs):
        res = epilogue(accs, [x[...] for x in x_refs])
        for o_ref, r in zip(o_refs, res):
            o_ref[...] = r.reshape(o_ref.shape).astype(o_ref.dtype)

    def compute(a_refs, b_refs, x_refs, o_refs, acc_refs):
        if nk == 1:
            finish(dots(a_refs, b_refs), x_refs, o_refs)
            return
        k = pl.program_id(2)

        @pl.when(k == 0)
        def _():
            for acc in acc_refs:
                acc[...] = jnp.zeros_like(acc)

        for acc, d in zip(acc_refs, dots(a_refs, b_refs)):
            acc[...] += d

        @pl.when(k == nk - 1)
        def _():
            finish([acc[...] for acc in acc_refs], x_refs, o_refs)

    def body(*refs):
        a_refs, b_refs = refs[:na], refs[na:na + nb]
        x_refs = refs[na + nb:na + nb + nx]
        o_refs = refs[n_in:n_in + no]
        bufs = refs[n_in + no:n_in + no + nc]
        scratch = refs[n_in + no + nc:]
        if not nc:
            compute(a_refs, b_refs, x_refs, o_refs, scratch)
            return
        acc_refs, (send, recv) = scratch[:-2], scratch[-2:]
        i, j, k = pl.program_id(0), pl.program_id(1), pl.program_id(2)

        @pl.when((i == 0) & (j == 0) & (k == 0))
        def _():
            _run_copies(bufs, plan, send, recv, start=True, wait=False)

        compute(a_refs, b_refs, x_refs, o_refs, acc_refs)

        @pl.when((i == grid[0] - 1) & (j == grid[1] - 1) & (k == nk - 1))
        def _():
            _run_copies(bufs, plan, send, recv, start=False, wait=True)

    ops = list(a_ops) + list(b_ops) + list(x_ops)
    aliases = {}
    if into is not None:
        ops.append((into, ANY))
        aliases = {len(ops) - 1: 0}
    for t, s in enumerate(slabs):
        ops.append((s, ANY))
        aliases[len(ops) - 1] = no + t
    outs = list(outs) + [(S(s.shape, s.dtype), ANY) for s in slabs]
    scratch = [pltpu.VMEM(acc_shape, f32)] * (n_acc if nk > 1 else 0)
    if nc:
        scratch += [pltpu.SemaphoreType.DMA((ncp,)), pltpu.SemaphoreType.DMA((ncp,))]
    sem = ("arbitrary",) * 3 if nc else ("parallel", "parallel", "arbitrary")
    res = _pcall(body, name=name, grid=grid, in_specs=[s for _, s in ops], out_specs=[s for _, s in outs],
                 out_shape=[s for s, _ in outs], scratch_shapes=scratch, input_output_aliases=aliases,
                 compiler_params=_params(sem))(*[a for a, _ in ops])
    return res


def _first(accs, xs):
    return (accs[0],)


def _proj(name, a, w, l, tm, bias=None, resid=None, out_dtype=f32, carry=None, norm=None):
    T = a.shape[0]
    Q, _, Kq, N = w.shape
    tn = N if norm is not None else _tile(N, (512, 256, 128))
    x_ops = []
    if bias is not None:
        x_ops.append((bias, pl.BlockSpec((1, tn), lambda i, j, k: (0, j))))
    if resid is not None:
        x_ops.append((resid, pl.BlockSpec((tm, tn), lambda i, j, k: (i, j))))
    n_add = len(x_ops)
    ospec = pl.BlockSpec((tm, tn), lambda i, j, k: (i, j))
    outs = [(S((T, N), out_dtype), ospec)]
    if norm is not None:
        x_ops.append((norm, pl.BlockSpec((1, tn), lambda i, j, k: (0, j))))
        outs.append((S((T, N), bf16), ospec))

    def epi(accs, xs):
        y = accs[0]
        for x in xs[:n_add]:
            y = y + x
        return (y,) if norm is None else (y, _rms(y, xs[n_add]))

    res = _matmul(name, "nn", (T // tm, N // tn, 1),
                  [(a, pl.BlockSpec((tm, Q * Kq), lambda i, j, k: (i, 0)))],
                  [(w, pl.BlockSpec((Q, None, Kq, tn), lambda i, j, k: (0, l, 0, j)))],
                  x_ops, outs, [(0, 0, 0)], (tm, tn), 1, epi, carry=carry)
    if norm is not None:
        return res
    return res[0] if carry is None else (res[0], res[1:])


def _proj_dx(name, dy, w, l, tm, out_dtype=f32, carry=None):
    T, N = dy.shape
    Q, _, Kq, _ = w.shape
    return _matmul(name, "nt", (T // tm, 1, 1),
                   [(dy, pl.BlockSpec((tm, N), lambda i, j, k: (i, 0)))],
                   [(w, pl.BlockSpec((Q, None, Kq, N), lambda i, j, k: (0, l, 0, 0)))],
                   [], [(S((T, Q * Kq), out_dtype), pl.BlockSpec((tm, Q * Kq), lambda i, j, k: (i, 0)))],
                   [(0, 0, 0)], (tm, Q * Kq), 1, _first, carry=carry)


def _dw_rows(name, a, dy, Q, tk, carry=None):
    T, N = dy.shape
    tn = _tile(N, (512, 256, 128))
    if a.ndim == 2:
        K = a.shape[1]
        a_op, ni, acc_rows = (a, pl.BlockSpec((tk, K), lambda i, j, k: (k, 0))), 1, K
        osh, ospec = S((Q, K // Q, N), bf16), pl.BlockSpec((Q, K // Q, tn), lambda i, j, k: (0, 0, j))
    else:
        Kq = a.shape[2]
        a_op, ni, acc_rows = (a, pl.BlockSpec((None, tk, Kq), lambda i, j, k: (i, k, 0))), Q, Kq
        osh, ospec = S((Q, Kq, N), bf16), pl.BlockSpec((None, Kq, tn), lambda i, j, k: (i, 0, j))
    return _matmul(name, "tn", (ni, N // tn, T // tk), [a_op], [(dy, pl.BlockSpec((tk, tn), lambda i, j, k: (k, j)))],
                   [], [(osh, ospec)], [(0, 0, 0)], (acc_rows, tn), 1, _first, carry=carry)


def _dw_cols(name, a, dyc, Qc, tk, Q, q_off=0, into=None, carry=None, dyc2=None):
    T, K = a.shape
    tkin = _tile(K, (512, 256, 128))
    if dyc.ndim == 3:
        Nq = dyc.shape[2]
        bspec = pl.BlockSpec((None, tk, Nq), lambda i, j, k: (j, k, 0))
    else:
        Nq = dyc.shape[1] // Qc
        bspec = pl.BlockSpec((tk, Nq), lambda i, j, k: (k, j))
    b_ops = [(dyc, bspec)] + ([] if dyc2 is None else [(dyc2, bspec)])
    ospec = pl.BlockSpec((None, tkin, Nq), lambda i, j, k: (j + q_off, i, 0))
    n = len(b_ops)
    return _matmul(name, "tn", (K // tkin, Qc, T // tk),
                   [(a, pl.BlockSpec((tk, tkin), lambda i, j, k: (k, i)))], b_ops, [],
                   [(S((Q, K, Nq), bf16), ospec)] * n, [(0, t, t) for t in range(n)], (tkin, Nq), n,
                   lambda accs, xs: tuple(accs), into=into, carry=carry)


def _glu_fwd(name, u, w, l, b_in, tm, carry=None):
    T, D = u.shape
    Q, _, _, Cq = w.shape
    H = Q // 2

    def epi(accs, xs):
        av, ag = accs[0] + xs[0], accs[1] + xs[1]
        return av * jax.nn.sigmoid(ag), av, ag

    wspec = lambda off: pl.BlockSpec((None, None, D, Cq), lambda i, j, k: (j + off, l, 0, 0))
    bspec = lambda off: pl.BlockSpec((1, Cq), lambda i, j, k: (0, j + off))
    ospec = pl.BlockSpec((tm, Cq), lambda i, j, k: (i, j))
    return _matmul(name, "nn", (T // tm, H, 1),
                   [(u, pl.BlockSpec((tm, D), lambda i, j, k: (i, 0)))],
                   [(w, wspec(0)), (w, wspec(H))], [(b_in, bspec(0)), (b_in, bspec(H))],
                   [(S((T, H * Cq), f32), ospec), (S((T, H * Cq), bf16), ospec), (S((T, H * Cq), bf16), ospec)],
                   [(0, 0, 0), (0, 1, 1)], (tm, Cq), 2, epi, carry=carry)


def _glu_du(name, dav, dag, w, l, tm):
    T = dav.shape[0]
    Q, _, D, Cq = w.shape
    H = Q // 2
    tn = _tile(D, (512, 256, 128))
    aspec = pl.BlockSpec((tm, Cq), lambda i, j, k: (i, k))
    wspec = lambda off: pl.BlockSpec((None, None, tn, Cq), lambda i, j, k: (k + off, l, j, 0))
    return _matmul(name, "nt", (T // tm, D // tn, H), [(dav, aspec), (dag, aspec)],
                   [(w, wspec(0)), (w, wspec(H))], [],
                   [(S((T, D), f32), pl.BlockSpec((tm, tn), lambda i, j, k: (i, j)))],
                   [(0, 0, 0), (1, 1, 0)], (tm, tn), 1, _first)[0]


def _ffn_up(name, u, wg, wu, l, tm, carry=None):
    T, D = u.shape
    Q, _, _, Fq = wg.shape

    def epi(accs, xs):
        g, up = accs
        sg = jax.nn.sigmoid(g)
        silu = g * sg
        return up * (sg * (1.0 + g * (1.0 - sg))), silu, silu * up

    wspec = pl.BlockSpec((None, None, D, Fq), lambda i, j, k: (j, l, 0, 0))
    ospec = pl.BlockSpec((None, tm, Fq), lambda i, j, k: (j, i, 0))
    osh = S((Q, T, Fq), bf16)
    return _matmul(name, "nn", (T // tm, Q, 1), [(u, pl.BlockSpec((tm, D), lambda i, j, k: (i, 0)))],
                   [(wg, wspec), (wu, wspec)], [], [(osh, ospec)] * 3, [(0, 0, 0), (0, 1, 1)], (tm, Fq), 2, epi, carry=carry)


def _ffn_down(name, hid, wd, l, resid, tm, carry=None, norms=()):
    Q, T, Fq = hid.shape
    D = wd.shape[3]
    if norms:
        tn, tm = D, (tm // 2 if tm % 32 == 0 and tm > 512 else tm)
    else:
        tn = _tile(D, (512, 256, 128))
    ospec = pl.BlockSpec((tm, tn), lambda i, j, k: (i, j))
    gspec = pl.BlockSpec((1, tn), lambda i, j, k: (0, j))

    def epi(accs, xs):
        h = accs[0] + xs[0]
        return (h,) + tuple(_rms(h, g) for g in xs[1:])

    return _matmul(name, "nn", (T // tm, D // tn, 1),
                   [(hid, pl.BlockSpec((Q, tm, Fq), lambda i, j, k: (0, i, 0)))],
                   [(wd, pl.BlockSpec((Q, None, Fq, tn), lambda i, j, k: (0, l, 0, j)))],
                   [(resid, ospec)] + [(g, gspec) for g in norms],
                   [(S((T, D), f32), ospec)] + [(S((T, D), bf16), ospec)] * len(norms),
                   [(0, 0, 0)], (tm, tn), 1, epi, carry=carry)


def _ffn_dhid(name, dy, wd, l, dgate_of, dup_of, tm, carry=None):
    T, D = dy.shape
    Q, _, Fq, _ = wd.shape

    def epi(accs, xs):
        return accs[0] * xs[0].astype(f32), accs[0] * xs[1].astype(f32)

    cspec = pl.BlockSpec((None, tm, Fq), lambda i, j, k: (j, i, 0))
    osh = S((Q, T, Fq), bf16)
    return _matmul(name, "nt", (T // tm, Q, 1), [(dy, pl.BlockSpec((tm, D), lambda i, j, k: (i, 0)))],
                   [(wd, pl.BlockSpec((None, None, Fq, D), lambda i, j, k: (j, l, 0, 0)))],
                   [(dgate_of, cspec), (dup_of, cspec)], [(osh, cspec)] * 2, [(0, 0, 0)], (tm, Fq), 1, epi, carry=carry)


def _ffn_du(name, dgate, dup, wg, wu, l, tm, carry=None):
    Q, T, Fq = dgate.shape
    D = wg.shape[2]
    tn = _tile(D, (512, 256, 128))
    tm = tm // 2 if tm % 32 == 0 and tm > 512 else tm
    aspec = pl.BlockSpec((Q, tm, Fq), lambda i, j, k: (0, i, 0))
    wspec = pl.BlockSpec((Q, None, tn, Fq), lambda i, j, k: (0, l, j, 0))
    return _matmul(name, "nt", (T // tm, D // tn, 1), [(dgate, aspec), (dup, aspec)], [(wg, wspec), (wu, wspec)], [],
                   [(S((T, D), f32), pl.BlockSpec((tm, tn), lambda i, j, k: (i, j)))],
                   [(0, 0, 0), (1, 1, 0)], (tm, tn), 1, _first, carry=carry)


def _rowwise(name, fn, rows, params, out_dtypes, tm):
    nr, npar = len(rows), len(params)
    T = rows[0].shape[0]
    shp = jax.eval_shape(fn, *[S((tm, r.shape[1]), f32) for r in rows], *[S(p.shape, f32) for p in params])

    def body(*refs):
        r = [x[...].astype(f32) for x in refs[:nr]]
        p = [x[...] for x in refs[nr:nr + npar]]
        for o_ref, o in zip(refs[nr + npar:], fn(*r, *p)):
            o_ref[...] = o.astype(o_ref.dtype)

    row_spec = lambda w: pl.BlockSpec((tm, w), lambda i: (i, 0))
    par_spec = lambda p: pl.BlockSpec(p.shape, lambda i: (0, 0))
    return _pcall(body, name=name, grid=(T // tm,),
                  in_specs=[row_spec(r.shape[1]) for r in rows] + [par_spec(p) for p in params],
                  out_specs=[row_spec(s.shape[1]) for s in shp],
                  out_shape=[S((T, s.shape[1]), dt) for s, dt in zip(shp, out_dtypes)],
                  compiler_params=_params(("parallel",)))(*rows, *params)


def _rowwise_vjp(name, fn, rows, params, cots, drow_dtypes, tm):
    nr, npar, nc = len(rows), len(params), len(cots)
    T = rows[0].shape[0]

    def body(*refs):
        r = [x[...].astype(f32) for x in refs[:nr]]
        p = [x[...] for x in refs[nr:nr + npar]]
        c = tuple(x[...].astype(f32) for x in refs[nr + npar:nr + npar + nc])
        o_refs = refs[nr + npar + nc:]
        _, vjp = jax.vjp(fn, *r, *p)
        grads = vjp(c)
        for o_ref, g in zip(o_refs[:nr], grads[:nr]):
            o_ref[...] = g.astype(o_ref.dtype)

        @pl.when(pl.program_id(0) == 0)
        def _():
            for o_ref in o_refs[nr:]:
                o_ref[...] = jnp.zeros_like(o_ref)

        for o_ref, g in zip(o_refs[nr:], grads[nr:]):
            o_ref[...] += g

    row_spec = lambda w: pl.BlockSpec((tm, w), lambda i: (i, 0))
    par_spec = lambda p: pl.BlockSpec(p.shape, lambda i: (0, 0))
    return _pcall(body, name=name, grid=(T // tm,),
                  in_specs=[row_spec(r.shape[1]) for r in rows] + [par_spec(p) for p in params] + [row_spec(c.shape[1]) for c in cots],
                  out_specs=[row_spec(r.shape[1]) for r in rows] + [par_spec(p) for p in params],
                  out_shape=[S(r.shape, dt) for r, dt in zip(rows, drow_dtypes)] + [S(p.shape, f32) for p in params],
                  compiler_params=_params(("arbitrary",)))(*rows, *params, *cots)


def _rms(h, g):
    return h * lax.rsqrt(jnp.mean(h * h, axis=-1, keepdims=True) + EPS) * g


def _rms_fn(h, g):
    return (_rms(h, g),)


def _rms_res_fn(h, g):
    return _rms(h, g), h


def _rms_res_bias_fn(h, g, b0):
    return _rms(h, g), h + b0


def _ln_silu_fn(c, g, b):
    mu = jnp.mean(c, axis=-1, keepdims=True)
    var = jnp.mean(jnp.square(c - mu), axis=-1, keepdims=True)
    y = (c - mu) * lax.rsqrt(var + EPS) * g + b
    return (y * jax.nn.sigmoid(y),)


def _glu_fn(av, ag, bv, bg):
    return ((av + bv) * jax.nn.sigmoid(ag + bg),)


def _take_rows(name, x, start, size):
    B, L, D = x.shape
    dc = _tile(D, (256, 128))

    def body(x_ref, o_ref):
        o_ref[...] = x_ref[start:start + size, :]

    return _pcall(body, name=name, grid=(B, D // dc), in_specs=[pl.BlockSpec((None, L, dc), lambda b, j: (b, 0, j))],
                  out_specs=pl.BlockSpec((None, size, dc), lambda b, j: (b, 0, j)), out_shape=S((B, size, D), x.dtype),
                  compiler_params=_params(("parallel", "parallel")))(x)


def _loss_head(h, tgt, Lp, n_real, tm):
    T, D = h.shape

    def body(h_ref, t_ref, dy_ref, part_ref):
        i = pl.program_id(0)
        pos = (i * tm + lax.broadcasted_iota(jnp.int32, (tm, 1), 0)) % Lp
        real = (pos >= N_META_ROWS) & (pos < N_META_ROWS + n_real)
        err = jnp.where(real, h_ref[...] - t_ref[...], 0.0)
        dy_ref[...] = err * (1.0 / D)

        @pl.when(i == 0)
        def _():
            part_ref[...] = jnp.zeros_like(part_ref)

        part_ref[...] += jnp.sum(err * err, axis=0, keepdims=True)

    spec = pl.BlockSpec((tm, D), lambda i: (i, 0))
    return _pcall(body, name="loss_head", grid=(T // tm,), in_specs=[spec, spec],
                  out_specs=[spec, pl.BlockSpec((1, D), lambda i: (0, 0))],
                  out_shape=[S((T, D), f32), S((1, D), f32)], compiler_params=_params(("arbitrary",)))(h, tgt)


CONV_OFF = CONV_PAD - (CONV_TAPS - 1)
WIN_ROWS = ATT_BLOCK + CONV_PAD


def _phases(ph_ref, win):
    n = win.shape[0]
    for b in range(1, 8):
        ph_ref[b - 1, 0:n - 8, :] = win[b:n - 8 + b]

    def tap(o):
        a = (o // 8) * 8
        return win[a:a + ATT_BLOCK] if o % 8 == 0 else ph_ref[o % 8 - 1, a:a + ATT_BLOCK, :]

    return tap


def _phase_scratch(width):
    return pltpu.VMEM((7, WIN_ROWS - 8, width), f32)


def _dwconv_fwd(name, p, dw, carry=None):
    B, Lp, D = p.shape
    dc = _tile(D, (256, 128))
    slabs, plan = carry if carry is not None else ([], [])
    nc = len(slabs)

    def body(*refs):
        p_ref, w_ref = refs[:2]
        o_ref, bufs, ph_ref = refs[2 + nc], refs[3 + nc:3 + 2 * nc], refs[3 + 2 * nc]
        first = (pl.program_id(0) == 0) & (pl.program_id(1) == 0)
        last = (pl.program_id(0) == B - 1) & (pl.program_id(1) == D // dc - 1)
        if nc:
            @pl.when(first)
            def _():
                _run_copies(bufs, plan, refs[-2], refs[-1], start=True, wait=False)

        def tile(win, base):
            tap = _phases(ph_ref, win)
            acc = jnp.zeros((ATT_BLOCK, dc), f32)
            for k in range(CONV_TAPS):
                acc = acc + tap(CONV_OFF + k) * w_ref[k:k + 1, :]
            o_ref[pl.ds(base, ATT_BLOCK), :] = acc

        tile(jnp.concatenate([jnp.zeros((CONV_PAD, dc), f32), p_ref[0:ATT_BLOCK, :]], axis=0), 0)

        def step(r, carry):
            base = pl.multiple_of(r * ATT_BLOCK, ATT_BLOCK)
            tile(p_ref[pl.ds(pl.multiple_of(base - CONV_PAD, CONV_PAD), WIN_ROWS), :], base)
            return carry

        lax.fori_loop(1, Lp // ATT_BLOCK, step, 0)
        if nc:
            @pl.when(last)
            def _():
                _run_copies(bufs, plan, refs[-2], refs[-1], start=False, wait=True)

    seq = pl.BlockSpec((None, Lp, dc), lambda b, j: (b, 0, j))
    sems = [pltpu.SemaphoreType.DMA((_plan_copies(plan),))] * 2 if nc else []
    return _pcall(body, name=name, grid=(B, D // dc),
                  in_specs=[seq, pl.BlockSpec((CONV_PAD, dc), lambda b, j: (0, j))] + [ANY] * nc,
                  out_specs=[seq] + [ANY] * nc, out_shape=[S((B, Lp, D), f32)] + [S(s.shape, s.dtype) for s in slabs],
                  input_output_aliases={2 + t: 1 + t for t in range(nc)}, scratch_shapes=[_phase_scratch(dc)] + sems,
                  compiler_params=_params(("arbitrary", "arbitrary") if nc else ("parallel", "parallel")))(p, dw, *slabs)


def _dwconv_bwd(name, dcv, p, dw):
    B, Lp, D = p.shape
    dcw = _tile(D, (256, 128))
    nblk = Lp // ATT_BLOCK
    assert nblk >= 2
    zeros = lambda: jnp.zeros((CONV_PAD, dcw), f32)

    def body(dc_ref, p_ref, w_ref, dp_ref, ddw_ref, dph_ref, pph_ref, part_ref):
        @pl.when(pl.program_id(1) == 0)
        def _():
            ddw_ref[...] = jnp.zeros_like(ddw_ref)

        part_ref[...] = jnp.zeros_like(part_ref)

        def tile(dwin, pwin, base):
            dtap, ptap = _phases(dph_ref, dwin), _phases(pph_ref, pwin)
            dtile = dwin[0:ATT_BLOCK]
            acc = jnp.zeros((ATT_BLOCK, dcw), f32)
            for k in range(CONV_TAPS):
                acc = acc + dtap(CONV_TAPS - 1 - k) * w_ref[k:k + 1, :]
                part_ref[k] += jnp.sum((dtile * ptap(CONV_OFF + k)).reshape(ATT_BLOCK // 8, 8, dcw), axis=0)
            dp_ref[pl.ds(base, ATT_BLOCK), :] = acc

        tile(dc_ref[0:WIN_ROWS, :], jnp.concatenate([zeros(), p_ref[0:ATT_BLOCK, :]], axis=0), 0)

        def step(r, carry):
            base = pl.multiple_of(r * ATT_BLOCK, ATT_BLOCK)
            tile(dc_ref[pl.ds(base, WIN_ROWS), :], p_ref[pl.ds(pl.multiple_of(base - CONV_PAD, CONV_PAD), WIN_ROWS), :], base)
            return carry

        lax.fori_loop(1, nblk - 1, step, 0)
        last = Lp - ATT_BLOCK
        tile(jnp.concatenate([dc_ref[last:Lp, :], zeros()], axis=0), p_ref[last - CONV_PAD:Lp, :], last)
        ddw_ref[...] += jnp.sum(part_ref[...], axis=1)

    seq = pl.BlockSpec((None, Lp, dcw), lambda j, b: (b, 0, j))
    wsp = pl.BlockSpec((CONV_PAD, dcw), lambda j, b: (0, j))
    return _pcall(body, name=name, grid=(D // dcw, B), in_specs=[seq, seq, wsp], out_specs=[seq, wsp],
                  out_shape=[S((B, Lp, D), f32), S((CONV_PAD, D), f32)],
                  scratch_shapes=[_phase_scratch(dcw)] * 2 + [pltpu.VMEM((CONV_PAD, 8, dcw), f32)],
                  compiler_params=_params(("parallel", "arbitrary")))(dcv, p, dw)


def _band_start(n):
    return pl.multiple_of(jnp.maximum(n - 1, 0) * ATT_BLOCK, ATT_BLOCK)


def _attn_bias(R):
    row = jnp.arange(R * ATT_BLOCK)[:, None] % ATT_BLOCK
    col = jnp.arange(3 * ATT_BLOCK)[None, :]
    out = []
    for n in range(3):
        qpos = n * ATT_BLOCK + row
        meta_ok = (col < N_META_ROWS) & (col <= qpos)
        band_pos = max(n - 1, 0) * ATT_BLOCK + (col - ATT_BLOCK)
        diff = qpos - band_pos
        band_ok = (col >= ATT_BLOCK) & (diff >= 0) & (diff < ATT_BLOCK) & (band_pos >= N_META_ROWS)
        out.append(jnp.where(meta_ok | band_ok, 0.0, MASKED))
    return jnp.stack(out).astype(f32)


def _row_head(R):
    row = lax.broadcasted_iota(jnp.int32, (R * ATT_BLOCK, R), 0)
    lo = lax.broadcasted_iota(jnp.int32, (R * ATT_BLOCK, R), 1) * ATT_BLOCK
    return ((row >= lo) & (row < lo + ATT_BLOCK)).astype(f32)


def _attn_core(q, kn, vv, qg, sink_vec, row_head, bias):
    qn = _rms(q, qg)
    s = lax.dot_general(qn.astype(bf16), kn.astype(bf16), _DN["nt"], preferred_element_type=f32) * (HEAD_DIM ** -0.5)
    s = s + bias
    sink = jnp.sum(row_head * sink_vec, axis=-1, keepdims=True)
    m = lax.stop_gradient(jnp.maximum(jnp.max(s, axis=-1, keepdims=True), sink))
    p = jnp.exp(s - m)
    denom = jnp.sum(p, axis=-1, keepdims=True) + jnp.exp(sink - m)
    return jnp.dot((p / denom).astype(bf16), vv.astype(bf16), preferred_element_type=f32)


def _keys_of(ref, n):
    return jnp.concatenate([ref[0:ATT_BLOCK, :], ref[pl.ds(_band_start(n), 2 * ATT_BLOCK), :]], axis=0)


def _stack_heads(x, R):
    return x if R == 1 else jnp.concatenate([x[:, r * HEAD_DIM:(r + 1) * HEAD_DIM] for r in range(R)], axis=0)


def _unstack_heads(x, R):
    return x if R == 1 else jnp.concatenate([x[r * ATT_BLOCK:(r + 1) * ATT_BLOCK] for r in range(R)], axis=1)


KV_PER_STEP_FWD = 2
KV_PER_STEP_BWD = 1


def _attn_specs(R, Lp, per):
    qspec = pl.BlockSpec((None, ATT_BLOCK, per * R * HEAD_DIM), lambda g, b, n: (b, n, g))
    kspec = pl.BlockSpec((per, None, Lp, HEAD_DIM), lambda g, b, n: (g, b, 0, 0))
    gspec = pl.BlockSpec((1, HEAD_DIM), lambda g, b, n: (0, 0))
    sspec = pl.BlockSpec((per, 1, R), lambda g, b, n: (g, 0, 0))
    bspec = pl.BlockSpec((None, R * ATT_BLOCK, 3 * ATT_BLOCK), lambda g, b, n: (jnp.minimum(n, 2), 0, 0))
    return qspec, kspec, gspec, sspec, bspec


def _heads_of(x, gi, R):
    return x[:, gi * R * HEAD_DIM:(gi + 1) * R * HEAD_DIM]


def _attn_fwd(name, q, kn, v, qg, sinks, carry=None):
    B, Lp, D = q.shape
    G = kn.shape[0]
    R = D // (G * HEAD_DIM)
    per = KV_PER_STEP_FWD
    assert G % per == 0
    qspec, kspec, gspec, sspec, bspec = _attn_specs(R, Lp, per)
    slabs, plan = carry if carry is not None else ([], [])
    nc = len(slabs)
    grid = (G // per, B, Lp // ATT_BLOCK)

    def body(*refs):
        q_ref, k_ref, v_ref, qg_ref, s_ref, b_ref = refs[:6]
        o_ref, bufs = refs[6 + nc], refs[7 + nc:7 + 2 * nc]
        g, b, n = pl.program_id(0), pl.program_id(1), pl.program_id(2)
        if nc:
            @pl.when((g == 0) & (b == 0) & (n == 0))
            def _():
                _run_copies(bufs, plan, refs[-2], refs[-1], start=True, wait=False)

        q_all, outs = q_ref[...], []
        for gi in range(per):
            o = _attn_core(_stack_heads(_heads_of(q_all, gi, R), R), _keys_of(k_ref.at[gi], n), _keys_of(v_ref.at[gi], n),
                           qg_ref[...], s_ref[gi], _row_head(R), b_ref[...])
            outs.append(_unstack_heads(o, R))
        o_ref[...] = jnp.concatenate(outs, axis=1).astype(o_ref.dtype)
        if nc:
            @pl.when((g == grid[0] - 1) & (b == grid[1] - 1) & (n == grid[2] - 1))
            def _():
                _run_copies(bufs, plan, refs[-2], refs[-1], start=False, wait=True)

    sems = [pltpu.SemaphoreType.DMA((_plan_copies(plan),))] * 2 if nc else []
    return _pcall(body, name=name, grid=grid, in_specs=[qspec, kspec, kspec, gspec, sspec, bspec] + [ANY] * nc,
                  out_specs=[qspec] + [ANY] * nc, out_shape=[S(q.shape, bf16)] + [S(s.shape, s.dtype) for s in slabs],
                  input_output_aliases={6 + t: 1 + t for t in range(nc)}, scratch_shapes=sems,
                  compiler_params=_params(("arbitrary",) * 3 if nc else ("parallel",) * 3))(q, kn, v, qg, sinks, _attn_bias(R), *slabs)


def _attn_bwd(name, q, kn, v, qg, sinks, do):
    B, Lp, D = q.shape
    G = kn.shape[0]
    R = D // (G * HEAD_DIM)
    per = KV_PER_STEP_BWD
    qspec, kspec, gspec, sspec, bspec = _attn_specs(R, Lp, per)

    def body(q_ref, k_ref, v_ref, qg_ref, s_ref, b_ref, do_ref, dq_ref, dk_ref, dv_ref, dqg_ref, ds_ref):
        g, b, n = pl.program_id(0), pl.program_id(1), pl.program_id(2)

        @pl.when((g == 0) & (b == 0) & (n == 0))
        def _():
            dqg_ref[...] = jnp.zeros_like(dqg_ref)

        @pl.when((b == 0) & (n == 0))
        def _():
            ds_ref[...] = jnp.zeros_like(ds_ref)

        @pl.when(n == 0)
        def _():
            dk_ref[...] = jnp.zeros_like(dk_ref)
            dv_ref[...] = jnp.zeros_like(dv_ref)

        row_head, bias = _row_head(R), b_ref[...]
        q_all, do_all, dqs = q_ref[...], do_ref[...].astype(f32), []
        band = pl.ds(_band_start(n), 2 * ATT_BLOCK)
        for gi in range(per):
            _, vjp = jax.vjp(lambda q_, k_, v_, a_, s_: _attn_core(q_, k_, v_, a_, s_, row_head, bias),
                             _stack_heads(_heads_of(q_all, gi, R), R), _keys_of(k_ref.at[gi], n).astype(f32),
                             _keys_of(v_ref.at[gi], n).astype(f32), qg_ref[...], s_ref[gi])
            dq, dkk, dvv, dqg, dsk = vjp(_stack_heads(_heads_of(do_all, gi, R), R))
            dqs.append(_unstack_heads(dq, R))
            dqg_ref[...] += dqg
            ds_ref[gi] += dsk
            dk_ref[gi, band, :] += dkk[ATT_BLOCK:]
            dv_ref[gi, band, :] += dvv[ATT_BLOCK:]
            dk_ref[gi, 0:ATT_BLOCK, :] += dkk[:ATT_BLOCK]
            dv_ref[gi, 0:ATT_BLOCK, :] += dvv[:ATT_BLOCK]
        dq_ref[...] = jnp.concatenate(dqs, axis=1)

    return _pcall(body, name=name, grid=(G // per, B, Lp // ATT_BLOCK),
                  in_specs=[qspec, kspec, kspec, gspec, sspec, bspec, qspec],
                  out_specs=[qspec, kspec, kspec, gspec, sspec],
                  out_shape=[S(q.shape, f32), S(kn.shape, f32), S(v.shape, f32), S(qg.shape, f32), S(sinks.shape, f32)],
                  compiler_params=_params(("arbitrary", "arbitrary", "arbitrary")))(q, kn, v, qg, sinks, _attn_bias(R), do)


def _kv_heads(name, kv, kg, tm):
    T, W = kv.shape
    G = W // (2 * HEAD_DIM)

    def body(kv_ref, kg_ref, k_ref, v_ref):
        x = kv_ref[...]
        for g in range(G):
            k_ref[g] = _rms(x[:, g * HEAD_DIM:(g + 1) * HEAD_DIM], kg_ref[...]).astype(bf16)
            v_ref[g] = x[:, (G + g) * HEAD_DIM:(G + g + 1) * HEAD_DIM].astype(bf16)

    hspec = pl.BlockSpec((G, tm, HEAD_DIM), lambda i: (0, i, 0))
    return _pcall(body, name=name, grid=(T // tm,),
                  in_specs=[pl.BlockSpec((tm, W), lambda i: (i, 0)), pl.BlockSpec((1, HEAD_DIM), lambda i: (0, 0))],
                  out_specs=[hspec, hspec], out_shape=[S((G, T, HEAD_DIM), bf16)] * 2,
                  compiler_params=_params(("parallel",)))(kv, kg)


def _kv_heads_bwd(name, kv, kg, dkn, dv, tm):
    T, W = kv.shape
    G = W // (2 * HEAD_DIM)

    def body(kv_ref, kg_ref, dk_ref, dv_ref, o_ref, dkg_ref):
        @pl.when(pl.program_id(0) == 0)
        def _():
            dkg_ref[...] = jnp.zeros_like(dkg_ref)

        x = kv_ref[...]
        pieces = []
        for g in range(G):
            _, vjp = jax.vjp(_rms, x[:, g * HEAD_DIM:(g + 1) * HEAD_DIM], kg_ref[...])
            dk, dkg = vjp(dk_ref[g])
            pieces.append(dk)
            dkg_ref[...] += dkg
        o_ref[...] = jnp.concatenate(pieces + [dv_ref[g] for g in range(G)], axis=1)

    hspec = pl.BlockSpec((G, tm, HEAD_DIM), lambda i: (0, i, 0))
    gspec = pl.BlockSpec((1, HEAD_DIM), lambda i: (0, 0))
    return _pcall(body, name=name, grid=(T // tm,),
                  in_specs=[pl.BlockSpec((tm, W), lambda i: (i, 0)), gspec, hspec, hspec],
                  out_specs=[pl.BlockSpec((tm, W), lambda i: (i, 0)), gspec],
                  out_shape=[S((T, W), f32), S((1, HEAD_DIM), f32)], compiler_params=_params(("arbitrary",)))(kv, kg, dkn, dv)


def _place():
    x, y, c = lax.axis_index("x"), lax.axis_index("y"), lax.axis_index("c")
    chips = [(1 - x, y), (x, 1 - y), (1 - x, 1 - y)]
    return x, y, c, chips, [2 * cx + cy for cx, cy in chips]


def _remote(src, dst, send_sem, recv_sem, to):
    return pltpu.make_async_remote_copy(src_ref=src, dst_ref=dst, send_sem=send_sem, recv_sem=recv_sem,
                                        device_id=to, device_id_type=MESH)


def _into_slot(name, w, l, idx, slots, dtype):
    _, Rr, Cc = w.shape
    tr = _tile(Rr, (512, 256, 128, 64, 32, 16))

    def body(i_ref, x_ref, o_ref):
        o_ref[...] = x_ref[...].astype(o_ref.dtype)

    gs = pltpu.PrefetchScalarGridSpec(
        num_scalar_prefetch=1, grid=(Rr // tr,),
        in_specs=[pl.BlockSpec((None, tr, Cc), lambda i, i_ref: (l, i, 0))],
        out_specs=pl.BlockSpec((None, tr, Cc), lambda i, i_ref: (i_ref[0], i, 0)))
    return _pcall(body, name=name, grid_spec=gs, out_shape=S((slots, Rr, Cc), dtype), compiler_params=_params(("parallel",)))(idx, w)


_PLAN_COPIES = dict(ici=3, fwd=3, xchg=N_CHIPS, scat=3, share=1)


def _plan_copies(plan):
    return sum(_PLAN_COPIES[step[0]] for step in plan)


def _comm_copies(bufs, plan, send, recv):
    x, y, c, chips, qk = _place()
    me_q, sib = 2 * x + y, (x, y, 1 - c)
    starts, lands = [], []

    def add(src, dst, to, land):
        s = len(starts)
        starts.append(_remote(src, dst, send.at[s], recv.at[s], to))
        lands.append(_remote(land, land, send.at[s], recv.at[s], sib))

    for step in plan:
        kind = step[0]
        if kind == "ici":
            mine = bufs[step[1]].at[me_q, c]
            for k in range(3):
                add(mine, mine, (*chips[k], c), bufs[step[1]].at[qk[k], c])
        elif kind == "fwd":
            for k in range(3):
                got = bufs[step[1]].at[qk[k], c]
                add(got, got, sib, bufs[step[1]].at[qk[k], 1 - c])
        elif kind == "xchg":
            for q in range(N_CHIPS):
                add(bufs[step[1]].at[q, 1 - c], bufs[step[2]].at[q], sib, bufs[step[2]].at[q])
        elif kind == "scat":
            for k in range(3):
                add(bufs[step[1]].at[qk[k]], bufs[step[2]].at[k], (*chips[k], c), bufs[step[2]].at[k])
        else:
            mine = bufs[step[1]].at[step[2], c]
            add(mine, mine, sib, bufs[step[1]].at[step[2], 1 - c])
    return starts, lands


def _run_copies(bufs, plan, send, recv, start=True, wait=True):
    starts, lands = _comm_copies(bufs, plan, send, recv)
    if start:
        for cp in starts:
            cp.start()
    if wait:
        for cp in lands:
            cp.wait_recv()
        for cp in starts:
            cp.wait_send()


def _comm_call(name, slabs, phases):
    n = len(slabs)

    def body(*refs):
        bufs, sems = refs[n:2 * n], refs[2 * n:]
        for t, plan in enumerate(phases):
            _run_copies(bufs, plan, sems[2 * t], sems[2 * t + 1])

    sems = [pltpu.SemaphoreType.DMA((_plan_copies(plan),)) for plan in phases for _ in range(2)]
    return _pcall(body, name=name, in_specs=[ANY] * n, out_specs=[ANY] * n, out_shape=[S(s.shape, s.dtype) for s in slabs],
                  input_output_aliases={p: p for p in range(n)}, scratch_shapes=sems)(*slabs)


def _gather_all_devices(block):
    m_per, ncol = block.shape

    def body(x_ref, out_ref, send_sems, recv_sems, local_sem):
        x, y, c, chips, _ = _place()
        me, sib = (x, y, c), (x, y, 1 - c)

        def rows(px, py, pc):
            return out_ref.at[pl.ds((4 * px + 2 * py + pc) * m_per, m_per), :]

        def copy(k, blk, to, src=None):
            return _remote(rows(*blk) if src is None else src, rows(*blk), send_sems.at[k], recv_sems.at[k], to)

        mine = pltpu.make_async_copy(x_ref, rows(*me), local_sem)
        mine.start()
        first = [copy(0, me, sib, src=x_ref)] + [copy(1 + j, me, (*chip, c), src=x_ref) for j, chip in enumerate(chips)]
        for cp in first:
            cp.start()
        passed = [copy(4 + j, (*chip, c), sib) for j, chip in enumerate(chips)]
        for j, chip in enumerate(chips):
            copy(1 + j, (*chip, c), me).wait_recv()
            passed[j].start()
        copy(0, sib, me).wait_recv()
        for j, chip in enumerate(chips):
            copy(4 + j, (*chip, 1 - c), me).wait_recv()
        for cp in first + passed:
            cp.wait_send()
        mine.wait()

    vm = pl.BlockSpec(memory_space=pltpu.VMEM)
    return _pcall(body, name="gather_small_grads", in_specs=[vm], out_specs=vm,
                  out_shape=S((8 * m_per, ncol), block.dtype),
                  scratch_shapes=[pltpu.SemaphoreType.DMA((7,)), pltpu.SemaphoreType.DMA((7,)), pltpu.SemaphoreType.DMA],
                  compiler_params=pltpu.CompilerParams(vmem_limit_bytes=VMEM_LIMIT_BYTES))(block)


def _sum_pair(name, g, r1, c_idx):
    Q, _, Rr, Cc = g.shape
    tr = _tile(Rr, (512, 256, 128))

    def body(c_ref, g_ref, r_ref, o_ref):
        o_ref[...] = (g_ref[...].astype(f32) + r_ref[...].astype(f32)).astype(o_ref.dtype)

    spec = pl.BlockSpec((None, tr, Cc), lambda q, i, c_ref: (q, i, 0))
    gs = pltpu.PrefetchScalarGridSpec(
        num_scalar_prefetch=1, grid=(Q, Rr // tr),
        in_specs=[pl.BlockSpec((None, None, tr, Cc), lambda q, i, c_ref: (q, c_ref[0], i, 0)), spec], out_specs=spec)
    return _pcall(body, name=name, grid_spec=gs, out_shape=S((Q, Rr, Cc), bf16),
                  compiler_params=_params(("parallel", "parallel")))(c_idx, g, r1)


def _sum_owner(name, s, r2, qc_idx, l, nl, into):
    Q, Rr, Cc = s.shape
    tr = _tile(Rr, (512, 256, 128))

    def body(q_ref, s_ref, r_ref, *rest):
        o_ref = rest[-1]
        o_ref[...] = ((s_ref[...].astype(f32) + r_ref[0].astype(f32)) + r_ref[1].astype(f32)) + r_ref[2].astype(f32)

    in_specs = [pl.BlockSpec((None, tr, Cc), lambda i, q_ref: (q_ref[0], i, 0)), pl.BlockSpec((3, tr, Cc), lambda i, q_ref: (0, i, 0))]
    gs = pltpu.PrefetchScalarGridSpec(
        num_scalar_prefetch=1, grid=(Rr // tr,), in_specs=in_specs + ([] if into is None else [ANY]),
        out_specs=pl.BlockSpec((None, None, tr, Cc), lambda i, q_ref: (l, q_ref[1], i, 0)))
    return _pcall(body, name=name, grid_spec=gs, out_shape=S((nl, 2, Rr, Cc), f32),
                  input_output_aliases={} if into is None else {3: 0},
                  compiler_params=_params(("parallel",)))(qc_idx, s, r2, *([] if into is None else [into]))


def _sum_devices(stack):
    n, M, C = stack.shape

    def body(s_ref, o_ref):
        acc = s_ref[0]
        for d in range(1, n):
            acc = acc + s_ref[d]
        o_ref[...] = acc

    return _pcall(body, name="sum_small_grads", out_shape=S((M, C), f32), compiler_params=_params())(stack)


def _adamw(name, w, g, m, v):
    Rr, Cc = w.shape
    tr = _tile(Rr, (256, 128, 64, 32, 16, 8))

    def body(w_ref, g_ref, m_ref, v_ref, go_ref, d_ref, mo_ref, vo_ref):
        g_ = g_ref[...]
        go_ref[...] = g_
        m_ = B1 * m_ref[...] + (1.0 - B1) * g_
        v_ = B2 * v_ref[...] + (1.0 - B2) * jnp.square(g_)
        m_hat = m_ / (1.0 - B1 ** STEP)
        v_hat = v_ / (1.0 - B2 ** STEP)
        d_ref[...] = -LR * (m_hat / (jnp.sqrt(v_hat) + ADAM_EPS) + WD * w_ref[...])
        mo_ref[...] = m_
        vo_ref[...] = v_

    spec = pl.BlockSpec((tr, Cc), lambda i: (i, 0))
    return _pcall(body, name=name, grid=(Rr // tr,), in_specs=[spec] * 4, out_specs=[spec] * 4,
                  out_shape=[S((Rr, Cc), f32)] * 4, compiler_params=_params(("parallel",)))(w, g, m, v)


def _pack(arrs, multiple):
    flat = jnp.concatenate([a.reshape(-1) for a in arrs])
    pad = (-flat.shape[0]) % multiple
    return jnp.pad(flat, (0, pad)).reshape(-1, 128)


def _unpack(slab, shapes):
    flat, out, o = slab.reshape(-1), [], 0
    for shp in shapes:
        n = 1
        for d in shp:
            n *= d
        out.append(flat[o:o + n].reshape(shp))
        o += n
    return out


def kernel(x, meta_tokens, norm_mix, norm_ffn, conv_w_in, conv_b_in, conv_dw, conv_ln_g, conv_ln_b, conv_w_out, conv_b_out, kv_norm, w_kv, k_norm, w_q, q_norm, attn_sinks, w_o, ffn_w_gate, ffn_w_up, ffn_w_down, loss_target, m_meta_tokens, m_norm_mix, m_norm_ffn, m_conv_w_in, m_conv_b_in, m_conv_dw, m_conv_ln_g, m_conv_ln_b, m_conv_w_out, m_conv_b_out, m_kv_norm, m_w_kv, m_k_norm, m_w_q, m_q_norm, m_attn_sinks, m_w_o, m_ffn_w_gate, m_ffn_w_up, m_ffn_w_down, v_meta_tokens, v_norm_mix, v_norm_ffn, v_conv_w_in, v_conv_b_in, v_conv_dw, v_conv_ln_g, v_conv_ln_b, v_conv_w_out, v_conv_b_out, v_kv_norm, v_w_kv, v_k_norm, v_w_q, v_q_norm, v_attn_sinks, v_w_o, v_ffn_w_gate, v_ffn_w_up, v_ffn_w_down):
    Q = N_CHIPS
    B, SEQ, D = x.shape
    L = N_META_ROWS + SEQ
    Lp = -(-L // ATT_BLOCK) * ATT_BLOCK
    T = B * Lp
    NA, NB = conv_w_in.shape[0], w_q.shape[0]
    NL = NA + NB
    Dq = D // Q
    G = N_KV
    R = D // (HEAD_DIM * G)
    KVW = w_kv.shape[1]
    assert NA % 2 == 0 and NB % 2 == 0 and NL % 2 == 0 and (D // Q) % 32 == 0
    tm = _tile(T, (1088, 544, 512, 256, 128))
    tk = _tile(T, (2176, 1088, 544, 512, 256, 128))
    tr = _tile(T, (272, 256, 128))
    my_c = lax.axis_index("c").astype(jnp.int32).reshape(1)
    my_q = (2 * lax.axis_index("x") + lax.axis_index("y")).astype(jnp.int32)
    my_qc = jnp.concatenate([my_q.reshape(1), my_c])

    small_shapes = [meta_tokens.shape, conv_b_in.shape, conv_dw.shape, conv_ln_g.shape, conv_ln_b.shape, conv_b_out.shape]
    small = _pack([meta_tokens, conv_b_in, conv_dw, conv_ln_g, conv_ln_b, conv_b_out], 2048)
    big = [conv_w_in, conv_w_out, w_kv, w_q, w_o, ffn_w_gate, ffn_w_up, ffn_w_down]
    keyed = dict(cin=conv_w_in, cout=conv_w_out, kv=w_kv[None], q=w_q, o=w_o, g=ffn_w_gate, u=ffn_w_up, d=ffn_w_down,
                 small=small[None])
    slabs, where = [], {}
    for key, w3 in keyed.items():
        for l in range(w3.shape[0]):
            s = _into_slot(f"own_{key}{l}", w3, l, my_qc, Q, f32 if key == "small" else bf16)
            where[key, l] = len(slabs)
            slabs.append(s.reshape(Q, 2, s.shape[1] // 2, s.shape[2]))

    def W(key, l):
        s = slabs[where[key, l]]
        return s.reshape(Q, 1, 2 * s.shape[2], s.shape[3])

    def layer_slabs(l):
        j = l - NA
        mix = [("cin", l), ("cout", l)] if l < NA else [("q", j), ("o", j)] + ([("kv", 0)] if j == 0 else [])
        return [where[k] for k in mix], [where["g", l], where["u", l]], [where["d", l]]

    def carry_of(ici, fwd):
        idxs = sorted(set(ici) | set(fwd))
        return idxs, ([slabs[i] for i in idxs], [("ici", idxs.index(i)) for i in ici] + [("fwd", idxs.index(i)) for i in fwd])

    def put_back(idxs, new):
        for i, s in zip(idxs, new):
            slabs[i] = s

    assert NA >= 1
    first = [where["cin", 0], where["small", 0]]
    idxs, (sl, plan) = carry_of(first, first)
    put_back(idxs, _comm_call("gather_first", sl, [[s for s in plan if s[0] == "ici"], [s for s in plan if s[0] == "fwd"]]))
    parts = [_unpack(slabs[where["small", 0]][q], small_shapes) for q in range(Q)]
    meta_f, b_in_f, dw_f, ln_g_f, ln_b_f, b_out_f = [jnp.concatenate([parts[q][i] for q in range(Q)], axis=-1) for i in range(6)]
    dw_pad = jnp.pad(dw_f, ((0, 0), (0, CONV_PAD - CONV_TAPS), (0, 0)))

    h = jnp.concatenate([jnp.broadcast_to(meta_f[None], (B, N_META_ROWS, D)), x, jnp.zeros((B, Lp - L, D), f32)], axis=1).reshape(T, D)
    tgt = jnp.pad(loss_target, ((0, 0), (N_META_ROWS, Lp - L), (0, 0))).reshape(T, D)
    row = lambda a: a.reshape(1, -1)
    seqs = lambda a: a.reshape(B, Lp, a.shape[-1])
    by_seq = lambda a: a.reshape(G, B, Lp, HEAD_DIM)
    saved = []
    sinks3 = attn_sinks.reshape(NB, G, 1, R)
    def carrying(ici, fwd):
        idxs, carry = carry_of(ici, fwd)
        return idxs, (carry if idxs else None)

    normed = []
    for l in range(NL):
        st = {"h_a": h}
        u = normed[0] if normed else _rowwise(f"rms_mix{l}", _rms_fn, [h], [row(norm_mix[l])], [bf16], tr)[0]
        st["u"] = u
        mix_s, (g_s, u_s), d_s = layer_slabs(l)
        idxs, carry = carrying([where["cout", 0], g_s], []) if l == 0 else carrying([], [g_s])
        second = [u_s] + d_s if l == 0 else [u_s]
        if l < NA:
            p, av, ag, *new = _glu_fwd(f"glu{l}", u, W("cin", l), 0, b_in_f[l:l + 1], tm, carry=carry)
            put_back(idxs, new)
            idxs, carry = carrying([u_s] + d_s, [where["cout", 0], g_s]) if l == 0 else carrying([u_s], [])
            cv, *new = _dwconv_fwd(f"dwconv{l}", p.reshape(B, Lp, D), dw_pad[l], carry=carry)
            put_back(idxs, new)
            cv = cv.reshape(T, D)
            s = _rowwise(f"ln_silu{l}", _ln_silu_fn, [cv], [row(ln_g_f[l]), row(ln_b_f[l])], [bf16], tr)[0]
            idxs2, carry2 = carrying([], second)
            h, u2, *new = _proj(f"conv_out{l}", s, W("cout", l), 0, tm, bias=row(b_out_f[l]), resid=h, carry=carry2,
                                norm=row(norm_ffn[l]))
            put_back(idxs2, new)
            st.update(av=av, ag=ag, p=p, cv=cv, s=s)
        else:
            j = l - NA
            if j == 0:
                kvn = normed[1] if len(normed) > 1 else _rowwise("rms_kv", _rms_fn, [h], [row(kv_norm)], [bf16], tr)[0]
                kv = _proj("kv_proj", kvn, W("kv", 0), 0, tm)
                kn, vh = _kv_heads("kv_heads", kv, row(k_norm), tm)
                kn, vh = by_seq(kn), by_seq(vh)
                st.update(kvn=kvn, kv=kv)
            res = _proj(f"q_proj{j}", u, W("q", j), 0, tm, carry=carry)
            q, new = res if carry is not None else (res, [])
            put_back(idxs, new)
            q = seqs(q)
            idxs, carry = carrying([u_s], [])
            o, *new = _attn_fwd(f"attn{j}", q, kn, vh, row(q_norm[j]), sinks3[j], carry=carry)
            put_back(idxs, new)
            o = o.reshape(T, D)
            idxs2, carry2 = carrying([], second)
            h, u2, *new = _proj(f"o_proj{j}", o, W("o", j), 0, tm, resid=h, carry=carry2, norm=row(norm_ffn[l]))
            put_back(idxs2, new)
            st.update(q=q, o=o)
        st["h_b"] = h
        nmix, ngu, nd = layer_slabs(l + 1) if l + 1 < NL else ([], [], [])
        idxs, carry = carrying(nmix + nd, [])
        dgate_of, dup_of, hid, *new = _ffn_up(f"ffn_up{l}", u2, W("g", l), W("u", l), 0, tm, carry=carry)
        put_back(idxs, new)
        idxs, carry = carrying(ngu[:1], nmix + nd)
        gains = [] if l + 1 == NL else [row(norm_mix[l + 1])] + ([row(kv_norm)] if l + 1 == NA else [])
        h, *rest = _ffn_down(f"ffn_down{l}", hid, W("d", l), 0, h, tm, carry=carry, norms=gains)
        normed, new = rest[:len(gains)], rest[len(gains):]
        put_back(idxs, new)
        st.update(u2=u2, dgate_of=dgate_of, dup_of=dup_of, hid=hid)
        saved.append(st)

    dh, part = _loss_head(h, tgt, Lp, SEQ, tr)
    loss = lax.psum(0.5 / D * jnp.sum(part), ("x", "y", "c"))

    grad_slab = {}
    n_layers = dict(cin=NA, cout=NA, kv=1, q=NB, o=NB, g=NL, u=NL, d=NL)

    def rs_begin(pieces):
        job = []
        for key, lay, g in pieces:
            g4 = g.reshape(Q, 2, g.shape[1] // 2, g.shape[2])
            job.append(dict(key=key, l=lay, g=g4, r1=lax.empty((Q,) + g4.shape[2:], bf16)))
        return job

    def xchg_carry(job):
        n = len(job)
        return [p["g"] for p in job] + [p["r1"] for p in job], [("xchg", t, n + t) for t in range(n)]

    def after_xchg(job, new):
        n = len(job)
        for t, p in enumerate(job):
            p["s"] = _sum_pair(f"sum_pair_{p['key']}{p['l']}", new[t], new[n + t], my_c)
            p["r2"] = lax.empty((3,) + p["s"].shape[1:], bf16)

    def scat_carry(job, keys):
        sel = [p for p in job if p["key"] in keys]
        n = len(sel)
        return sel, ([p["s"] for p in sel] + [p["r2"] for p in sel], [("scat", t, n + t) for t in range(n)])

    def after_scat(sel, new):
        n = len(sel)
        for t, p in enumerate(sel):
            key = p["key"]
            grad_slab[key] = _sum_owner(f"sum_owner_{key}{p['l']}", new[t], new[n + t], my_qc, p["l"], n_layers[key], grad_slab.get(key))

    def share_carry(job):
        return [grad_slab[p["key"]] for p in job], [("share", t, p["l"]) for t, p in enumerate(job)]

    def after_share(job, new):
        for p, s in zip(job, new):
            grad_slab[p["key"]] = s

    def carried(job, make):
        return make(job) if job else None

    def merge(*carries):
        slabs_, plan_ = [], []
        for c in carries:
            if c is not None:
                off = len(slabs_)
                slabs_ += c[0]
                plan_ += [(s[0], s[1] + off, s[2]) if s[0] == "share" else (s[0],) + tuple(i + off for i in s[1:]) for s in c[1]]
        return (slabs_, plan_) if slabs_ else None

    MIXER = ("cin", "cout", "q", "o", "kv")
    g_mix, g_ffn = [None] * NL, [None] * NL
    g_bin, g_dw, g_lng, g_lnb, g_bout = ([None] * NA for _ in range(5))
    g_qn, g_sink = [None] * NB, [None] * NB
    dknp = dvp = None
    zero_row = jnp.zeros((1, D), f32)
    job = []
    early = []
    for l in reversed(range(NL)):
        st = saved[l]
        pieces = []
        dgate, dup, *new = _ffn_dhid(f"ffn_dhid{l}", dh, W("d", l), 0, st["dgate_of"], st["dup_of"], tm, carry=carried(job, xchg_carry))
        if job:
            after_xchg(job, new)
        sel, carry = scat_carry(job, ("d",)) if job else ([], None)
        dwd, *new = _dw_rows(f"ffn_dwd{l}", st["hid"], dh, Q, tk, carry=carry)
        after_scat(sel, new)
        sel, carry = scat_carry(job, ("g", "u")) if job else ([], None)
        dwg, dwu, *new = _dw_cols(f"ffn_dwgu{l}", st["u2"], dgate, Q, tk, Q, carry=carry, dyc2=dup)
        after_scat(sel, new)
        pieces += [("d", l, dwd), ("g", l, dwg), ("u", l, dwu)]
        if l == 0 and l < NA:
            early, pieces = rs_begin(pieces), []
        sel, carry = scat_carry(job, MIXER) if job else ([], None)
        du2, *new = _ffn_du(f"ffn_du{l}", dgate, dup, W("g", l), W("u", l), 0, tm, carry=merge(carry, carried(early, xchg_carry)))
        after_scat(sel, new[:2 * len(sel)])
        if early:
            after_xchg(early, new[2 * len(sel):])
        dh, g_ffn[l] = _rowwise_vjp(f"rms_ffn_bwd{l}", _rms_res_fn, [st["h_b"]], [row(norm_ffn[l])], [du2, dh], [f32], tr)
        if l < NA:
            esel, ecarry = scat_carry(early, ("d",)) if early else ([], None)
            ds, *new = _proj_dx(f"conv_out_dx{l}", dh, W("cout", l), 0, tm, carry=merge(carried(job, share_carry), ecarry))
            after_share(job, new[:len(job)])
            after_scat(esel, new[len(job):])
            pieces.append(("cout", l, _dw_rows(f"conv_out_dw{l}", st["s"], dh, Q, tk)[0]))
            dcv, g_lng[l], g_lnb[l] = _rowwise_vjp(f"ln_silu_bwd{l}", _ln_silu_fn, [st["cv"]], [row(ln_g_f[l]), row(ln_b_f[l])], [ds], [f32], tr)
            dp, g_dw[l] = _dwconv_bwd(f"dwconv_bwd{l}", dcv.reshape(B, Lp, D), st["p"].reshape(B, Lp, D), dw_pad[l])
            dav, dag, dbv, dbg = _rowwise_vjp(f"glu_bwd{l}", _glu_fn, [st["av"], st["ag"]], [zero_row, zero_row], [dp.reshape(T, D)], [bf16, bf16], tr)
            g_bin[l] = jnp.concatenate([dbv, dbg], axis=1)
            esel, ecarry = scat_carry(early, ("g",)) if early else ([], None)
            dwin, *new = _dw_cols(f"glu_dwv{l}", st["u"], dav, Q // 2, tk, Q, carry=ecarry)
            after_scat(esel, new)
            esel, ecarry = scat_carry(early, ("u",)) if early else ([], None)
            dwin, *new = _dw_cols(f"glu_dwg{l}", st["u"], dag, Q // 2, tk, Q, q_off=Q // 2, into=dwin, carry=ecarry)
            after_scat(esel, new)
            pieces.append(("cin", l, dwin))
            du = _glu_du(f"glu_du{l}", dav, dag, W("cin", l), 0, tm)
            dh, g_mix[l], g_bout[l] = _rowwise_vjp(f"rms_mix_bwd{l}", _rms_res_bias_fn, [st["h_a"]], [row(norm_mix[l]), zero_row], [du, dh], [f32], tr)
        else:
            j = l - NA
            do, *new = _proj_dx(f"o_proj_dx{j}", dh, W("o", j), 0, tm, out_dtype=bf16, carry=carried(job, share_carry))
            after_share(job, new)
            pieces.append(("o", j, _dw_rows(f"o_proj_dw{j}", st["o"], dh, Q, tk)[0]))
            dq, dk1, dv1, g_qn[j], g_sink[j] = _attn_bwd(f"attn_bwd{j}", st["q"], kn, vh, row(q_norm[j]), sinks3[j], seqs(do))
            dknp, dvp = (dk1, dv1) if dknp is None else (dknp + dk1, dvp + dv1)
            dq = dq.reshape(T, D)
            pieces.append(("q", j, _dw_rows(f"q_proj_dw{j}", st["u"], dq, Q, tk)[0]))
            du = _proj_dx(f"q_proj_dx{j}", dq, W("q", j), 0, tm)[0]
            dh, g_mix[l] = _rowwise_vjp(f"rms_mix_bwd{l}", _rms_res_fn, [st["h_a"]], [row(norm_mix[l])], [du, dh], [f32], tr)
            if j == 0:
                dkv, g_kn = _kv_heads_bwd("kv_heads_bwd", st["kv"], row(k_norm), dknp.reshape(G, T, HEAD_DIM), dvp.reshape(G, T, HEAD_DIM), tm)
                pieces.append(("kv", 0, _dw_rows("kv_proj_dw", st["kvn"], dkv, Q, tk)[0]))
                dkvn = _proj_dx("kv_proj_dx", dkv, W("kv", 0), 0, tm)[0]
                dh, g_kvn = _rowwise_vjp("rms_kv_bwd", _rms_res_fn, [st["h_a"]], [row(kv_norm)], [dkvn, dh], [f32], tr)
        job = rs_begin(pieces)
    dh3 = dh.reshape(B, Lp, D)
    grad_x = _take_rows("grad_x", dh3, N_META_ROWS, SEQ)
    g_meta = jnp.sum(dh3[:, :N_META_ROWS], axis=0)

    sl, plan = xchg_carry(job)
    after_xchg(job, _comm_call("rs_exchange", sl, [plan]))
    sel, (sl, plan) = scat_carry(job, tuple(n_layers))
    after_scat(sel, _comm_call("rs_scatter", sl, [plan]))
    sl, plan = share_carry(job + early)
    after_share(job + early, _comm_call("rs_share", sl, [plan]))

    names = ["conv_w_in", "conv_w_out", "w_kv", "w_q", "w_o", "ffn_w_gate", "ffn_w_up", "ffn_w_down"]
    ws = dict(zip(names, big))
    ms = dict(zip(names, [m_conv_w_in, m_conv_w_out, m_w_kv, m_w_q, m_w_o, m_ffn_w_gate, m_ffn_w_up, m_ffn_w_down]))
    vs = dict(zip(names, [v_conv_w_in, v_conv_w_out, v_w_kv, v_w_q, v_w_o, v_ffn_w_gate, v_ffn_w_up, v_ffn_w_down]))
    out_g, out_d, out_m, out_v = {}, {}, {}, {}
    for nm, key in zip(names, ("cin", "cout", "kv", "q", "o", "g", "u", "d")):
        w, gsh = ws[nm], grad_slab[key]
        flat = lambda a: a.reshape(-1, w.shape[-1])
        g2, d2, m2, v2 = _adamw(f"adamw_{nm}", flat(w), flat(gsh), flat(ms[nm]), flat(vs[nm]))
        out_g[nm], out_d[nm], out_m[nm], out_v[nm] = (a.reshape(w.shape) for a in (g2, d2, m2, v2))

    small_names = ["norm_mix", "norm_ffn", "kv_norm", "k_norm", "q_norm", "attn_sinks", "meta_tokens", "conv_b_in", "conv_dw", "conv_ln_g", "conv_ln_b", "conv_b_out"]
    small_grads = [jnp.concatenate(g_mix, 0), jnp.concatenate(g_ffn, 0), g_kvn.reshape(-1), g_kn.reshape(-1), jnp.concatenate(g_qn, 0),
                   jnp.stack(g_sink).reshape(NB, G * R), g_meta, jnp.concatenate(g_bin, 0), jnp.stack(g_dw)[:, :CONV_TAPS],
                   jnp.concatenate(g_lng, 0), jnp.concatenate(g_lnb, 0), jnp.concatenate(g_bout, 0)]
    slab = _pack(small_grads, 1024)
    total = _sum_devices(_gather_all_devices(slab).reshape(8, slab.shape[0], 128))
    full_grads = _unpack(total, [g.shape for g in small_grads])
    small_w = dict(zip(small_names, [norm_mix, norm_ffn, kv_norm, k_norm, q_norm, attn_sinks, meta_tokens, conv_b_in, conv_dw, conv_ln_g, conv_ln_b, conv_b_out]))
    small_m = dict(zip(small_names, [m_norm_mix, m_norm_ffn, m_kv_norm, m_k_norm, m_q_norm, m_attn_sinks, m_meta_tokens, m_conv_b_in, m_conv_dw, m_conv_ln_g, m_conv_ln_b, m_conv_b_out]))
    small_v = dict(zip(small_names, [v_norm_mix, v_norm_ffn, v_kv_norm, v_k_norm, v_q_norm, v_attn_sinks, v_meta_tokens, v_conv_b_in, v_conv_dw, v_conv_ln_g, v_conv_ln_b, v_conv_b_out]))
    local_grads = []
    for nm, g in zip(small_names, full_grads):
        w = small_w[nm]
        if g.shape != w.shape:
            wq = w.shape[-1]
            g = lax.dynamic_slice_in_dim(g, my_q * wq, wq, axis=g.ndim - 1)
        local_grads.append(g)
    shapes = [small_w[nm].shape for nm in small_names]
    _, d_s, m_s, v_s = _adamw("adamw_small", _pack([small_w[nm] for nm in small_names], 1024), _pack(local_grads, 1024),
                           _pack([small_m[nm] for nm in small_names], 1024), _pack([small_v[nm] for nm in small_names], 1024))
    for nm, g, d_, m_, v_ in zip(small_names, local_grads, _unpack(d_s, shapes), _unpack(m_s, shapes), _unpack(v_s, shapes)):
        out_g[nm], out_d[nm], out_m[nm], out_v[nm] = g, d_, m_, v_

    order = ["meta_tokens", "norm_mix", "norm_ffn", "conv_w_in", "conv_b_in", "conv_dw", "conv_ln_g", "conv_ln_b", "conv_w_out", "conv_b_out",
             "kv_norm", "w_kv", "k_norm", "w_q", "q_norm", "attn_sinks", "w_o", "ffn_w_gate", "ffn_w_up", "ffn_w_down"]
    return (loss, grad_x, *[out_g[n] for n in order], *[out_d[n] for n in order], *[out_m[n] for n in order], *[out_v[n] for n in order])
```

```python
import functools

import jax
import jax.numpy as jnp
from jax import lax
from jax.experimental import pallas as pl
from jax.experimental.pallas import tpu as pltpu

f32, bf16 = jnp.float32, jnp.bfloat16

N_META_ROWS = 16
ATT_BLOCK = 128
HEAD_DIM = 64
N_KV = 4
CONV_TAPS = 31
CONV_PAD = 32
EPS = 1e-6
MASKED = -1e30
LR, B1, B2, ADAM_EPS, WD, STEP = 0.001, 0.9, 0.999, 1e-08, 0.01, 10
N_CHIPS = 4
VMEM_LIMIT_BYTES = 56 * 1024 * 1024
MESH = pl.DeviceIdType.MESH
ANY = pl.BlockSpec(memory_space=pl.ANY)
S = jax.ShapeDtypeStruct


def _pcall(body, **kw):
    return pl.pallas_call(body, **kw)


def _params(sem=None):
    return pltpu.CompilerParams(dimension_semantics=sem, vmem_limit_bytes=VMEM_LIMIT_BYTES)


def _tile(n, prefs):
    for p in prefs:
        if n % p == 0:
            return p
    return n


_DN = {"nn": (((1,), (0,)), ((), ())), "nt": (((1,), (1,)), ((), ())), "tn": (((0,), (0,)), ((), ()))}


def _matmul(name, mode, grid, a_ops, b_ops, x_ops, outs, terms, acc_shape, n_acc, epilogue, into=None, carry=None):
    na, nb, nx, no = len(a_ops), len(b_ops), len(x_ops), len(outs)
    nk = grid[2]
    slabs, plan = carry if carry is not None else ([], [])
    nc, ncp = len(slabs), _plan_copies(plan)
    n_in = na + nb + nx + (0 if into is None else 1) + nc

    def flat2d(v):
        return v.reshape(-1, v.shape[-1]) if v.ndim == 3 else v

    def dot(a, b):
        return lax.dot_general(a.astype(bf16), b.astype(bf16), _DN[mode], preferred_element_type=f32)

    def dots(a_refs, b_refs):
        parts = [None] * n_acc
        shared = {}

        def lhs(ai):
            if ai not in shared:
                a = flat2d(a_refs[ai][...]).astype(bf16)
                shared[ai] = a.T if mode == "tn" and n_terms_of[ai] > 1 else a
            return shared[ai]

        for ai, bi, ci in terms:
            a_ref, b_ref = a_refs[ai], b_refs[bi]
            if len(a_ref.shape) == 3 and len(b_ref.shape) == 3:
                for c in range(a_ref.shape[0]):
                    d = dot(a_ref[c], b_ref[c])
                    parts[ci] = d if parts[ci] is None else parts[ci] + d
            else:
                dn = _DN["nn"] if mode == "tn" and n_terms_of[ai] > 1 else _DN[mode]
                d = lax.dot_general(lhs(ai), flat2d(b_ref[...]).astype(bf16), dn, preferred_element_type=f32)
                parts[ci] = d if parts[ci] is None else parts[ci] + d
        return parts

    n_terms_of = {ai: sum(1 for t in terms if t[0] == ai) for ai in range(na)}

    def finish(accs, x_refs, o_refs):
        res = epilogue(accs, [x[...] for x in x_refs])
        for o_ref, r in zip(o_refs, res):
            o_ref[...] = r.reshape(o_ref.shape).astype(o_ref.dtype)

    def compute(a_refs, b_refs, x_refs, o_refs, acc_refs):
        if nk == 1:
            finish(dots(a_refs, b_refs), x_refs, o_refs)
            return
        k = pl.program_id(2)

        @pl.when(k == 0)
        def _():
            for acc in acc_refs:
                acc[...] = jnp.zeros_like(acc)

        for acc, d in zip(acc_refs, dots(a_refs, b_refs)):
            acc[...] += d

        @pl.when(k == nk - 1)
        def _():
            finish([acc[...] for acc in acc_refs], x_refs, o_refs)

    def body(*refs):
        a_refs, b_refs = refs[:na], refs[na:na + nb]
        x_refs = refs[na + nb:na + nb + nx]
        o_refs = refs[n_in:n_in + no]
        bufs = refs[n_in + no:n_in + no + nc]
        scratch = refs[n_in + no + nc:]
        if not nc:
            compute(a_refs, b_refs, x_refs, o_refs, scratch)
            return
        acc_refs, (send, recv) = scratch[:-2], scratch[-2:]
        i, j, k = pl.program_id(0), pl.program_id(1), pl.program_id(2)

        @pl.when((i == 0) & (j == 0) & (k == 0))
        def _():
            _run_copies(bufs, plan, send, recv, start=True, wait=False)

        compute(a_refs, b_refs, x_refs, o_refs, acc_refs)

        @pl.when((i == grid[0] - 1) & (j == grid[1] - 1) & (k == nk - 1))
        def _():
            _run_copies(bufs, plan, send, recv, start=False, wait=True)

    ops = list(a_ops) + list(b_ops) + list(x_ops)
    aliases = {}
    if into is not None:
        ops.append((into, ANY))
        aliases = {len(ops) - 1: 0}
    for t, s in enumerate(slabs):
        ops.append((s, ANY))
        aliases[len(ops) - 1] = no + t
    outs = list(outs) + [(S(s.shape, s.dtype), ANY) for s in slabs]
    scratch = [pltpu.VMEM(acc_shape, f32)] * (n_acc if nk > 1 else 0)
    if nc:
        scratch += [pltpu.SemaphoreType.DMA((ncp,)), pltpu.SemaphoreType.DMA((ncp,))]
    sem = ("arbitrary",) * 3 if nc else ("parallel", "parallel", "arbitrary")
    res = _pcall(body, name=name, grid=grid, in_specs=[s for _, s in ops], out_specs=[s for _, s in outs],
                 out_shape=[s for s, _ in outs], scratch_shapes=scratch, input_output_aliases=aliases,
                 compiler_params=_params(sem))(*[a for a, _ in ops])
    return res


def _first(accs, xs):
    return (accs[0],)


def _proj(name, a, w, l, tm, bias=None, resid=None, out_dtype=f32, carry=None, norm=None):
    T = a.shape[0]
    Q, _, Kq, N = w.shape
    tn = N if norm is not None else _tile(N, (512, 256, 128))
    x_ops = []
    if bias is not None:
        x_ops.append((bias, pl.BlockSpec((1, tn), lambda i, j, k: (0, j))))
    if resid is not None:
        x_ops.append((resid, pl.BlockSpec((tm, tn), lambda i, j, k: (i, j))))
    n_add = len(x_ops)
    ospec = pl.BlockSpec((tm, tn), lambda i, j, k: (i, j))
    outs = [(S((T, N), out_dtype), ospec)]
    if norm is not None:
        x_ops.append((norm, pl.BlockSpec((1, tn), lambda i, j, k: (0, j))))
        outs.append((S((T, N), bf16), ospec))

    def epi(accs, xs):
        y = accs[0]
        for x in xs[:n_add]:
            y = y + x
        return (y,) if norm is None else (y, _rms(y, xs[n_add]))

    res = _matmul(name, "nn", (T // tm, N // tn, 1),
                  [(a, pl.BlockSpec((tm, Q * Kq), lambda i, j, k: (i, 0)))],
                  [(w, pl.BlockSpec((Q, None, Kq, tn), lambda i, j, k: (0, l, 0, j)))],
                  x_ops, outs, [(0, 0, 0)], (tm, tn), 1, epi, carry=carry)
    if norm is not None:
        return res
    return res[0] if carry is None else (res[0], res[1:])


def _proj_dx(name, dy, w, l, tm, out_dtype=f32, carry=None):
    T, N = dy.shape
    Q, _, Kq, _ = w.shape
    return _matmul(name, "nt", (T // tm, 1, 1),
                   [(dy, pl.BlockSpec((tm, N), lambda i, j, k: (i, 0)))],
                   [(w, pl.BlockSpec((Q, None, Kq, N), lambda i, j, k: (0, l, 0, 0)))],
                   [], [(S((T, Q * Kq), out_dtype), pl.BlockSpec((tm, Q * Kq), lambda i, j, k: (i, 0)))],
                   [(0, 0, 0)], (tm, Q * Kq), 1, _first, carry=carry)


def _dw_rows(name, a, dy, Q, tk, carry=None):
    T, N = dy.shape
    tn = _tile(N, (512, 256, 128))
    if a.ndim == 2:
        K = a.shape[1]
        a_op, ni, acc_rows = (a, pl.BlockSpec((tk, K), lambda i, j, k: (k, 0))), 1, K
        osh, ospec = S((Q, K // Q, N), bf16), pl.BlockSpec((Q, K // Q, tn), lambda i, j, k: (0, 0, j))
    else:
        Kq = a.shape[2]
        a_op, ni, acc_rows = (a, pl.BlockSpec((None, tk, Kq), lambda i, j, k: (i, k, 0))), Q, Kq
        osh, ospec = S((Q, Kq, N), bf16), pl.BlockSpec((None, Kq, tn), lambda i, j, k: (i, 0, j))
    return _matmul(name, "tn", (ni, N // tn, T // tk), [a_op], [(dy, pl.BlockSpec((tk, tn), lambda i, j, k: (k, j)))],
                   [], [(osh, ospec)], [(0, 0, 0)], (acc_rows, tn), 1, _first, carry=carry)


def _dw_cols(name, a, dyc, Qc, tk, Q, q_off=0, into=None, carry=None, dyc2=None):
    T, K = a.shape
    tkin = _tile(K, (512, 256, 128))
    if dyc.ndim == 3:
        Nq = dyc.shape[2]
        bspec = pl.BlockSpec((None, tk, Nq), lambda i, j, k: (j, k, 0))
    else:
        Nq = dyc.shape[1] // Qc
        bspec = pl.BlockSpec((tk, Nq), lambda i, j, k: (k, j))
    b_ops = [(dyc, bspec)] + ([] if dyc2 is None else [(dyc2, bspec)])
    ospec = pl.BlockSpec((None, tkin, Nq), lambda i, j, k: (j + q_off, i, 0))
    n = len(b_ops)
    return _matmul(name, "tn", (K // tkin, Qc, T // tk),
                   [(a, pl.BlockSpec((tk, tkin), lambda i, j, k: (k, i)))], b_ops, [],
                   [(S((Q, K, Nq), bf16), ospec)] * n, [(0, t, t) for t in range(n)], (tkin, Nq), n,
                   lambda accs, xs: tuple(accs), into=into, carry=carry)


def _glu_fwd(name, u, w, l, b_in, tm, carry=None):
    T, D = u.shape
    Q, _, _, Cq = w.shape
    H = Q // 2

    def epi(accs, xs):
        av, ag = accs[0] + xs[0], accs[1] + xs[1]
        return av * jax.nn.sigmoid(ag), av, ag

    wspec = lambda off: pl.BlockSpec((None, None, D, Cq), lambda i, j, k: (j + off, l, 0, 0))
    bspec = lambda off: pl.BlockSpec((1, Cq), lambda i, j, k: (0, j + off))
    ospec = pl.BlockSpec((tm, Cq), lambda i, j, k: (i, j))
    return _matmul(name, "nn", (T // tm, H, 1),
                   [(u, pl.BlockSpec((tm, D), lambda i, j, k: (i, 0)))],
                   [(w, wspec(0)), (w, wspec(H))], [(b_in, bspec(0)), (b_in, bspec(H))],
                   [(S((T, H * Cq), f32), ospec), (S((T, H * Cq), bf16), ospec), (S((T, H * Cq), bf16), ospec)],
                   [(0, 0, 0), (0, 1, 1)], (tm, Cq), 2, epi, carry=carry)


def _glu_du(name, dav, dag, w, l, tm):
    T = dav.shape[0]
    Q, _, D, Cq = w.shape
    H = Q // 2
    tn = _tile(D, (512, 256, 128))
    aspec = pl.BlockSpec((tm, Cq), lambda i, j, k: (i, k))
    wspec = lambda off: pl.BlockSpec((None, None, tn, Cq), lambda i, j, k: (k + off, l, j, 0))
    return _matmul(name, "nt", (T // tm, D // tn, H), [(dav, aspec), (dag, aspec)],
                   [(w, wspec(0)), (w, wspec(H))], [],
                   [(S((T, D), f32), pl.BlockSpec((tm, tn), lambda i, j, k: (i, j)))],
                   [(0, 0, 0), (1, 1, 0)], (tm, tn), 1, _first)[0]


def _ffn_up(name, u, wg, wu, l, tm, carry=None):
    T, D = u.shape
    Q, _, _, Fq = wg.shape

    def epi(accs, xs):
        g, up = accs
        sg = jax.nn.sigmoid(g)
        silu = g * sg
        return up * (sg * (1.0 + g * (1.0 - sg))), silu, silu * up

    wspec = pl.BlockSpec((None, None, D, Fq), lambda i, j, k: (j, l, 0, 0))
    ospec = pl.BlockSpec((None, tm, Fq), lambda i, j, k: (j, i, 0))
    osh = S((Q, T, Fq), bf16)
    return _matmul(name, "nn", (T // tm, Q, 1), [(u, pl.BlockSpec((tm, D), lambda i, j, k: (i, 0)))],
                   [(wg, wspec), (wu, wspec)], [], [(osh, ospec)] * 3, [(0, 0, 0), (0, 1, 1)], (tm, Fq), 2, epi, carry=carry)


def _ffn_down(name, hid, wd, l, resid, tm, carry=None, norms=()):
    Q, T, Fq = hid.shape
    D = wd.shape[3]
    if norms:
        tn, tm = D, (tm // 2 if tm % 32 == 0 and tm > 512 else tm)
    else:
        tn = _tile(D, (512, 256, 128))
    ospec = pl.BlockSpec((tm, tn), lambda i, j, k: (i, j))
    gspec = pl.BlockSpec((1, tn), lambda i, j, k: (0, j))

    def epi(accs, xs):
        h = accs[0] + xs[0]
        return (h,) + tuple(_rms(h, g) for g in xs[1:])

    return _matmul(name, "nn", (T // tm, D // tn, 1),
                   [(hid, pl.BlockSpec((Q, tm, Fq), lambda i, j, k: (0, i, 0)))],
                   [(wd, pl.BlockSpec((Q, None, Fq, tn), lambda i, j, k: (0, l, 0, j)))],
                   [(resid, ospec)] + [(g, gspec) for g in norms],
                   [(S((T, D), f32), ospec)] + [(S((T, D), bf16), ospec)] * len(norms),
                   [(0, 0, 0)], (tm, tn), 1, epi, carry=carry)


def _ffn_dhid(name, dy, wd, l, dgate_of, dup_of, tm, carry=None):
    T, D = dy.shape
    Q, _, Fq, _ = wd.shape

    def epi(accs, xs):
        return accs[0] * xs[0].astype(f32), accs[0] * xs[1].astype(f32)

    cspec = pl.BlockSpec((None, tm, Fq), lambda i, j, k: (j, i, 0))
    osh = S((Q, T, Fq), bf16)
    return _matmul(name, "nt", (T // tm, Q, 1), [(dy, pl.BlockSpec((tm, D), lambda i, j, k: (i, 0)))],
                   [(wd, pl.BlockSpec((None, None, Fq, D), lambda i, j, k: (j, l, 0, 0)))],
                   [(dgate_of, cspec), (dup_of, cspec)], [(osh, cspec)] * 2, [(0, 0, 0)], (tm, Fq), 1, epi, carry=carry)


def _ffn_du(name, dgate, dup, wg, wu, l, tm, carry=None):
    Q, T, Fq = dgate.shape
    D = wg.shape[2]
    tn = _tile(D, (512, 256, 128))
    tm = tm // 2 if tm % 32 == 0 and tm > 512 else tm
    aspec = pl.BlockSpec((Q, tm, Fq), lambda i, j, k: (0, i, 0))
    wspec = pl.BlockSpec((Q, None, tn, Fq), lambda i, j, k: (0, l, j, 0))
    return _matmul(name, "nt", (T // tm, D // tn, 1), [(dgate, aspec), (dup, aspec)], [(wg, wspec), (wu, wspec)], [],
                   [(S((T, D), f32), pl.BlockSpec((tm, tn), lambda i, j, k: (i, j)))],
                   [(0, 0, 0), (1, 1, 0)], (tm, tn), 1, _first, carry=carry)


def _rowwise(name, fn, rows, params, out_dtypes, tm):
    nr, npar = len(rows), len(params)
    T = rows[0].shape[0]
    shp = jax.eval_shape(fn, *[S((tm, r.shape[1]), f32) for r in rows], *[S(p.shape, f32) for p in params])

    def body(*refs):
        r = [x[...].astype(f32) for x in refs[:nr]]
        p = [x[...] for x in refs[nr:nr + npar]]
        for o_ref, o in zip(refs[nr + npar:], fn(*r, *p)):
            o_ref[...] = o.astype(o_ref.dtype)

    row_spec = lambda w: pl.BlockSpec((tm, w), lambda i: (i, 0))
    par_spec = lambda p: pl.BlockSpec(p.shape, lambda i: (0, 0))
    return _pcall(body, name=name, grid=(T // tm,),
                  in_specs=[row_spec(r.shape[1]) for r in rows] + [par_spec(p) for p in params],
                  out_specs=[row_spec(s.shape[1]) for s in shp],
                  out_shape=[S((T, s.shape[1]), dt) for s, dt in zip(shp, out_dtypes)],
                  compiler_params=_params(("parallel",)))(*rows, *params)


def _rowwise_vjp(name, fn, rows, params, cots, drow_dtypes, tm, also_bf16=False):
    nr, npar, nc = len(rows), len(params), len(cots)
    T = rows[0].shape[0]

    def body(*refs):
        r = [x[...].astype(f32) for x in refs[:nr]]
        p = [x[...] for x in refs[nr:nr + npar]]
        c = tuple(x[...].astype(f32) for x in refs[nr + npar:nr + npar + nc])
        o_refs = refs[nr + npar + nc:]
        _, vjp = jax.vjp(fn, *r, *p)
        grads = vjp(c)
        for o_ref, g in zip(o_refs[:nr], grads[:nr]):
            o_ref[...] = g.astype(o_ref.dtype)
        if also_bf16:
            o_refs[nr + npar][...] = grads[0].astype(bf16)

        @pl.when(pl.program_id(0) == 0)
        def _():
            for o_ref in o_refs[nr:nr + npar]:
                o_ref[...] = jnp.zeros_like(o_ref)

        for o_ref, g in zip(o_refs[nr:nr + npar], grads[nr:]):
            o_ref[...] += g

    row_spec = lambda w: pl.BlockSpec((tm, w), lambda i: (i, 0))
    par_spec = lambda p: pl.BlockSpec(p.shape, lambda i: (0, 0))
    extra = [rows[0]] if also_bf16 else []
    return _pcall(body, name=name, grid=(T // tm,),
                  in_specs=[row_spec(r.shape[1]) for r in rows] + [par_spec(p) for p in params] + [row_spec(c.shape[1]) for c in cots],
                  out_specs=[row_spec(r.shape[1]) for r in rows] + [par_spec(p) for p in params] + [row_spec(r.shape[1]) for r in extra],
                  out_shape=[S(r.shape, dt) for r, dt in zip(rows, drow_dtypes)] + [S(p.shape, f32) for p in params]
                  + [S(r.shape, bf16) for r in extra],
                  compiler_params=_params(("arbitrary",)))(*rows, *params, *cots)


def _rms(h, g):
    return h * lax.rsqrt(jnp.mean(h * h, axis=-1, keepdims=True) + EPS) * g


def _rms_fn(h, g):
    return (_rms(h, g),)


def _rms_res_fn(h, g):
    return _rms(h, g), h


def _rms_res_bias_fn(h, g, b0):
    return _rms(h, g), h + b0


def _ln_silu_fn(c, g, b):
    mu = jnp.mean(c, axis=-1, keepdims=True)
    var = jnp.mean(jnp.square(c - mu), axis=-1, keepdims=True)
    y = (c - mu) * lax.rsqrt(var + EPS) * g + b
    return (y * jax.nn.sigmoid(y),)


def _glu_fn(av, ag, bv, bg):
    return ((av + bv) * jax.nn.sigmoid(ag + bg),)


def _take_rows(name, x, start, size):
    B, L, D = x.shape
    dc = _tile(D, (256, 128))

    def body(x_ref, o_ref):
        o_ref[...] = x_ref[start:start + size, :]

    return _pcall(body, name=name, grid=(B, D // dc), in_specs=[pl.BlockSpec((None, L, dc), lambda b, j: (b, 0, j))],
                  out_specs=pl.BlockSpec((None, size, dc), lambda b, j: (b, 0, j)), out_shape=S((B, size, D), x.dtype),
                  compiler_params=_params(("parallel", "parallel")))(x)


def _loss_head(h, tgt, Lp, n_real, tm):
    T, D = h.shape

    def body(h_ref, t_ref, dy_ref, dyb_ref, part_ref):
        i = pl.program_id(0)
        pos = (i * tm + lax.broadcasted_iota(jnp.int32, (tm, 1), 0)) % Lp
        real = (pos >= N_META_ROWS) & (pos < N_META_ROWS + n_real)
        err = jnp.where(real, h_ref[...] - t_ref[...], 0.0)
        dy_ref[...] = err * (1.0 / D)
        dyb_ref[...] = (err * (1.0 / D)).astype(bf16)

        @pl.when(i == 0)
        def _():
            part_ref[...] = jnp.zeros_like(part_ref)

        part_ref[...] += jnp.sum(err * err, axis=0, keepdims=True)

    spec = pl.BlockSpec((tm, D), lambda i: (i, 0))
    return _pcall(body, name="loss_head", grid=(T // tm,), in_specs=[spec, spec],
                  out_specs=[spec, spec, pl.BlockSpec((1, D), lambda i: (0, 0))],
                  out_shape=[S((T, D), f32), S((T, D), bf16), S((1, D), f32)], compiler_params=_params(("arbitrary",)))(h, tgt)


CONV_OFF = CONV_PAD - (CONV_TAPS - 1)
WIN_ROWS = ATT_BLOCK + CONV_PAD


def _phases(ph_ref, win):
    n = win.shape[0]
    for b in range(1, 8):
        ph_ref[b - 1, 0:n - 8, :] = win[b:n - 8 + b]

    def tap(o):
        a = (o // 8) * 8
        return win[a:a + ATT_BLOCK] if o % 8 == 0 else ph_ref[o % 8 - 1, a:a + ATT_BLOCK, :]

    return tap


def _phase_scratch(width):
    return pltpu.VMEM((7, WIN_ROWS - 8, width), f32)


def _dwconv_fwd(name, p, dw, carry=None):
    B, Lp, D = p.shape
    dc = _tile(D, (256, 128))
    slabs, plan = carry if carry is not None else ([], [])
    nc = len(slabs)

    def body(*refs):
        p_ref, w_ref = refs[:2]
        o_ref, bufs, ph_ref = refs[2 + nc], refs[3 + nc:3 + 2 * nc], refs[3 + 2 * nc]
        first = (pl.program_id(0) == 0) & (pl.program_id(1) == 0)
        last = (pl.program_id(0) == B - 1) & (pl.program_id(1) == D // dc - 1)
        if nc:
            @pl.when(first)
            def _():
                _run_copies(bufs, plan, refs[-2], refs[-1], start=True, wait=False)

        def tile(win, base):
            tap = _phases(ph_ref, win)
            acc = jnp.zeros((ATT_BLOCK, dc), f32)
            for k in range(CONV_TAPS):
                acc = acc + tap(CONV_OFF + k) * w_ref[k:k + 1, :]
            o_ref[pl.ds(base, ATT_BLOCK), :] = acc

        tile(jnp.concatenate([jnp.zeros((CONV_PAD, dc), f32), p_ref[0:ATT_BLOCK, :]], axis=0), 0)

        def step(r, carry):
            base = pl.multiple_of(r * ATT_BLOCK, ATT_BLOCK)
            tile(p_ref[pl.ds(pl.multiple_of(base - CONV_PAD, CONV_PAD), WIN_ROWS), :], base)
            return carry

        lax.fori_loop(1, Lp // ATT_BLOCK, step, 0)
        if nc:
            @pl.when(last)
            def _():
                _run_copies(bufs, plan, refs[-2], refs[-1], start=False, wait=True)

    seq = pl.BlockSpec((None, Lp, dc), lambda b, j: (b, 0, j))
    sems = [pltpu.SemaphoreType.DMA((_plan_copies(plan),))] * 2 if nc else []
    return _pcall(body, name=name, grid=(B, D // dc),
                  in_specs=[seq, pl.BlockSpec((CONV_PAD, dc), lambda b, j: (0, j))] + [ANY] * nc,
                  out_specs=[seq] + [ANY] * nc, out_shape=[S((B, Lp, D), f32)] + [S(s.shape, s.dtype) for s in slabs],
                  input_output_aliases={2 + t: 1 + t for t in range(nc)}, scratch_shapes=[_phase_scratch(dc)] + sems,
                  compiler_params=_params(("arbitrary", "arbitrary") if nc else ("parallel", "parallel")))(p, dw, *slabs)


def _dwconv_bwd(name, dcv, p, dw):
    B, Lp, D = p.shape
    dcw = _tile(D, (256, 128))
    nblk = Lp // ATT_BLOCK
    assert nblk >= 2
    zeros = lambda: jnp.zeros((CONV_PAD, dcw), f32)

    def body(dc_ref, p_ref, w_ref, dp_ref, ddw_ref, dph_ref, pph_ref, part_ref):
        @pl.when(pl.program_id(1) == 0)
        def _():
            ddw_ref[...] = jnp.zeros_like(ddw_ref)

        part_ref[...] = jnp.zeros_like(part_ref)

        def tile(dwin, pwin, base):
            dtap, ptap = _phases(dph_ref, dwin), _phases(pph_ref, pwin)
            dtile = dwin[0:ATT_BLOCK]
            acc = jnp.zeros((ATT_BLOCK, dcw), f32)
            for k in range(CONV_TAPS):
                acc = acc + dtap(CONV_TAPS - 1 - k) * w_ref[k:k + 1, :]
                part_ref[k] += jnp.sum((dtile * ptap(CONV_OFF + k)).reshape(ATT_BLOCK // 8, 8, dcw), axis=0)
            dp_ref[pl.ds(base, ATT_BLOCK), :] = acc

        tile(dc_ref[0:WIN_ROWS, :], jnp.concatenate([zeros(), p_ref[0:ATT_BLOCK, :]], axis=0), 0)

        def step(r, carry):
            base = pl.multiple_of(r * ATT_BLOCK, ATT_BLOCK)
            tile(dc_ref[pl.ds(base, WIN_ROWS), :], p_ref[pl.ds(pl.multiple_of(base - CONV_PAD, CONV_PAD), WIN_ROWS), :], base)
            return carry

        lax.fori_loop(1, nblk - 1, step, 0)
        last = Lp - ATT_BLOCK
        tile(jnp.concatenate([dc_ref[last:Lp, :], zeros()], axis=0), p_ref[last - CONV_PAD:Lp, :], last)
        ddw_ref[...] += jnp.sum(part_ref[...], axis=1)

    seq = pl.BlockSpec((None, Lp, dcw), lambda j, b: (b, 0, j))
    wsp = pl.BlockSpec((CONV_PAD, dcw), lambda j, b: (0, j))
    return _pcall(body, name=name, grid=(D // dcw, B), in_specs=[seq, seq, wsp], out_specs=[seq, wsp],
                  out_shape=[S((B, Lp, D), f32), S((CONV_PAD, D), f32)],
                  scratch_shapes=[_phase_scratch(dcw)] * 2 + [pltpu.VMEM((CONV_PAD, 8, dcw), f32)],
                  compiler_params=_params(("parallel", "arbitrary")))(dcv, p, dw)


def _band_start(n):
    return pl.multiple_of(jnp.maximum(n - 1, 0) * ATT_BLOCK, ATT_BLOCK)


def _attn_bias(R):
    row = jnp.arange(R * ATT_BLOCK)[:, None] % ATT_BLOCK
    col = jnp.arange(3 * ATT_BLOCK)[None, :]
    out = []
    for n in range(3):
        qpos = n * ATT_BLOCK + row
        meta_ok = (col < N_META_ROWS) & (col <= qpos)
        band_pos = max(n - 1, 0) * ATT_BLOCK + (col - ATT_BLOCK)
        diff = qpos - band_pos
        band_ok = (col >= ATT_BLOCK) & (diff >= 0) & (diff < ATT_BLOCK) & (band_pos >= N_META_ROWS)
        out.append(jnp.where(meta_ok | band_ok, 0.0, MASKED))
    return jnp.stack(out).astype(f32)


def _row_head(R):
    row = lax.broadcasted_iota(jnp.int32, (R * ATT_BLOCK, R), 0)
    lo = lax.broadcasted_iota(jnp.int32, (R * ATT_BLOCK, R), 1) * ATT_BLOCK
    return ((row >= lo) & (row < lo + ATT_BLOCK)).astype(f32)


def _attn_core(q, kn, vv, qg, sink_vec, row_head, bias):
    qn = _rms(q, qg)
    s = lax.dot_general(qn.astype(bf16), kn.astype(bf16), _DN["nt"], preferred_element_type=f32) * (HEAD_DIM ** -0.5)
    s = s + bias
    sink = jnp.sum(row_head * sink_vec, axis=-1, keepdims=True)
    m = lax.stop_gradient(jnp.maximum(jnp.max(s, axis=-1, keepdims=True), sink))
    p = jnp.exp(s - m)
    denom = jnp.sum(p, axis=-1, keepdims=True) + jnp.exp(sink - m)
    return jnp.dot((p / denom).astype(bf16), vv.astype(bf16), preferred_element_type=f32)


def _keys_of(ref, n):
    return jnp.concatenate([ref[0:ATT_BLOCK, :], ref[pl.ds(_band_start(n), 2 * ATT_BLOCK), :]], axis=0)


def _stack_heads(x, R):
    return x if R == 1 else jnp.concatenate([x[:, r * HEAD_DIM:(r + 1) * HEAD_DIM] for r in range(R)], axis=0)


def _unstack_heads(x, R):
    return x if R == 1 else jnp.concatenate([x[r * ATT_BLOCK:(r + 1) * ATT_BLOCK] for r in range(R)], axis=1)


KV_PER_STEP_FWD = 2
KV_PER_STEP_BWD = 1


def _attn_specs(R, Lp, per):
    qspec = pl.BlockSpec((None, ATT_BLOCK, per * R * HEAD_DIM), lambda g, b, n: (b, n, g))
    kspec = pl.BlockSpec((per, None, Lp, HEAD_DIM), lambda g, b, n: (g, b, 0, 0))
    gspec = pl.BlockSpec((1, HEAD_DIM), lambda g, b, n: (0, 0))
    sspec = pl.BlockSpec((per, 1, R), lambda g, b, n: (g, 0, 0))
    bspec = pl.BlockSpec((None, R * ATT_BLOCK, 3 * ATT_BLOCK), lambda g, b, n: (jnp.minimum(n, 2), 0, 0))
    return qspec, kspec, gspec, sspec, bspec


def _heads_of(x, gi, R):
    return x[:, gi * R * HEAD_DIM:(gi + 1) * R * HEAD_DIM]


def _attn_fwd(name, q, kn, v, qg, sinks, carry=None):
    B, Lp, D = q.shape
    G = kn.shape[0]
    R = D // (G * HEAD_DIM)
    per = KV_PER_STEP_FWD
    assert G % per == 0
    qspec, kspec, gspec, sspec, bspec = _attn_specs(R, Lp, per)
    slabs, plan = carry if carry is not None else ([], [])
    nc = len(slabs)
    grid = (G // per, B, Lp // ATT_BLOCK)

    def body(*refs):
        q_ref, k_ref, v_ref, qg_ref, s_ref, b_ref = refs[:6]
        o_ref, bufs = refs[6 + nc], refs[7 + nc:7 + 2 * nc]
        g, b, n = pl.program_id(0), pl.program_id(1), pl.program_id(2)
        if nc:
            @pl.when((g == 0) & (b == 0) & (n == 0))
            def _():
                _run_copies(bufs, plan, refs[-2], refs[-1], start=True, wait=False)

        q_all, outs = q_ref[...], []
        for gi in range(per):
            o = _attn_core(_stack_heads(_heads_of(q_all, gi, R), R), _keys_of(k_ref.at[gi], n), _keys_of(v_ref.at[gi], n),
                           qg_ref[...], s_ref[gi], _row_head(R), b_ref[...])
            outs.append(_unstack_heads(o, R))
        o_ref[...] = jnp.concatenate(outs, axis=1).astype(o_ref.dtype)
        if nc:
            @pl.when((g == grid[0] - 1) & (b == grid[1] - 1) & (n == grid[2] - 1))
            def _():
                _run_copies(bufs, plan, refs[-2], refs[-1], start=False, wait=True)

    sems = [pltpu.SemaphoreType.DMA((_plan_copies(plan),))] * 2 if nc else []
    return _pcall(body, name=name, grid=grid, in_specs=[qspec, kspec, kspec, gspec, sspec, bspec] + [ANY] * nc,
                  out_specs=[qspec] + [ANY] * nc, out_shape=[S(q.shape, bf16)] + [S(s.shape, s.dtype) for s in slabs],
                  input_output_aliases={6 + t: 1 + t for t in range(nc)}, scratch_shapes=sems,
                  compiler_params=_params(("arbitrary",) * 3 if nc else ("parallel",) * 3))(q, kn, v, qg, sinks, _attn_bias(R), *slabs)


def _attn_bwd(name, q, kn, v, qg, sinks, do):
    B, Lp, D = q.shape
    G = kn.shape[0]
    R = D // (G * HEAD_DIM)
    per = KV_PER_STEP_BWD
    qspec, kspec, gspec, sspec, bspec = _attn_specs(R, Lp, per)

    def body(q_ref, k_ref, v_ref, qg_ref, s_ref, b_ref, do_ref, dq_ref, dk_ref, dv_ref, dqg_ref, ds_ref):
        g, b, n = pl.program_id(0), pl.program_id(1), pl.program_id(2)

        @pl.when((g == 0) & (b == 0) & (n == 0))
        def _():
            dqg_ref[...] = jnp.zeros_like(dqg_ref)

        @pl.when((b == 0) & (n == 0))
        def _():
            ds_ref[...] = jnp.zeros_like(ds_ref)

        @pl.when(n == 0)
        def _():
            dk_ref[...] = jnp.zeros_like(dk_ref)
            dv_ref[...] = jnp.zeros_like(dv_ref)

        row_head, bias = _row_head(R), b_ref[...]
        q_all, do_all, dqs = q_ref[...], do_ref[...].astype(f32), []
        band = pl.ds(_band_start(n), 2 * ATT_BLOCK)
        for gi in range(per):
            _, vjp = jax.vjp(lambda q_, k_, v_, a_, s_: _attn_core(q_, k_, v_, a_, s_, row_head, bias),
                             _stack_heads(_heads_of(q_all, gi, R), R), _keys_of(k_ref.at[gi], n).astype(f32),
                             _keys_of(v_ref.at[gi], n).astype(f32), qg_ref[...], s_ref[gi])
            dq, dkk, dvv, dqg, dsk = vjp(_stack_heads(_heads_of(do_all, gi, R), R))
            dqs.append(_unstack_heads(dq, R))
            dqg_ref[...] += dqg
            ds_ref[gi] += dsk
            dk_ref[gi, band, :] += dkk[ATT_BLOCK:]
            dv_ref[gi, band, :] += dvv[ATT_BLOCK:]
            dk_ref[gi, 0:ATT_BLOCK, :] += dkk[:ATT_BLOCK]
            dv_ref[gi, 0:ATT_BLOCK, :] += dvv[:ATT_BLOCK]
        dq_ref[...] = jnp.concatenate(dqs, axis=1)

    return _pcall(body, name=name, grid=(G // per, B, Lp // ATT_BLOCK),
                  in_specs=[qspec, kspec, kspec, gspec, sspec, bspec, qspec],
                  out_specs=[qspec, kspec, kspec, gspec, sspec],
                  out_shape=[S(q.shape, f32), S(kn.shape, f32), S(v.shape, f32), S(qg.shape, f32), S(sinks.shape, f32)],
                  compiler_params=_params(("arbitrary", "arbitrary", "arbitrary")))(q, kn, v, qg, sinks, _attn_bias(R), do)


def _kv_heads(name, kv, kg, tm):
    T, W = kv.shape
    G = W // (2 * HEAD_DIM)

    def body(kv_ref, kg_ref, k_ref, v_ref):
        x = kv_ref[...]
        for g in range(G):
            k_ref[g] = _rms(x[:, g * HEAD_DIM:(g + 1) * HEAD_DIM], kg_ref[...]).astype(bf16)
            v_ref[g] = x[:, (G + g) * HEAD_DIM:(G + g + 1) * HEAD_DIM].astype(bf16)

    hspec = pl.BlockSpec((G, tm, HEAD_DIM), lambda i: (0, i, 0))
    return _pcall(body, name=name, grid=(T // tm,),
                  in_specs=[pl.BlockSpec((tm, W), lambda i: (i, 0)), pl.BlockSpec((1, HEAD_DIM), lambda i: (0, 0))],
                  out_specs=[hspec, hspec], out_shape=[S((G, T, HEAD_DIM), bf16)] * 2,
                  compiler_params=_params(("parallel",)))(kv, kg)


def _kv_heads_bwd(name, kv, kg, dkn, dv, tm):
    T, W = kv.shape
    G = W // (2 * HEAD_DIM)

    def body(kv_ref, kg_ref, dk_ref, dv_ref, o_ref, dkg_ref):
        @pl.when(pl.program_id(0) == 0)
        def _():
            dkg_ref[...] = jnp.zeros_like(dkg_ref)

        x = kv_ref[...]
        pieces = []
        for g in range(G):
            _, vjp = jax.vjp(_rms, x[:, g * HEAD_DIM:(g + 1) * HEAD_DIM], kg_ref[...])
            dk, dkg = vjp(dk_ref[g])
            pieces.append(dk)
            dkg_ref[...] += dkg
        o_ref[...] = jnp.concatenate(pieces + [dv_ref[g] for g in range(G)], axis=1)

    hspec = pl.BlockSpec((G, tm, HEAD_DIM), lambda i: (0, i, 0))
    gspec = pl.BlockSpec((1, HEAD_DIM), lambda i: (0, 0))
    return _pcall(body, name=name, grid=(T // tm,),
                  in_specs=[pl.BlockSpec((tm, W), lambda i: (i, 0)), gspec, hspec, hspec],
                  out_specs=[pl.BlockSpec((tm, W), lambda i: (i, 0)), gspec],
                  out_shape=[S((T, W), f32), S((1, HEAD_DIM), f32)], compiler_params=_params(("arbitrary",)))(kv, kg, dkn, dv)


def _place():
    x, y, c = lax.axis_index("x"), lax.axis_index("y"), lax.axis_index("c")
    chips = [(1 - x, y), (x, 1 - y), (1 - x, 1 - y)]
    return x, y, c, chips, [2 * cx + cy for cx, cy in chips]


def _remote(src, dst, send_sem, recv_sem, to):
    return pltpu.make_async_remote_copy(src_ref=src, dst_ref=dst, send_sem=send_sem, recv_sem=recv_sem,
                                        device_id=to, device_id_type=MESH)


def _into_slot(name, w, l, idx, slots, dtype):
    _, Rr, Cc = w.shape
    tr = _tile(Rr, (512, 256, 128, 64, 32, 16))

    def body(i_ref, x_ref, o_ref):
        o_ref[...] = x_ref[...].astype(o_ref.dtype)

    gs = pltpu.PrefetchScalarGridSpec(
        num_scalar_prefetch=1, grid=(Rr // tr,),
        in_specs=[pl.BlockSpec((None, tr, Cc), lambda i, i_ref: (l, i, 0))],
        out_specs=pl.BlockSpec((None, tr, Cc), lambda i, i_ref: (i_ref[0], i, 0)))
    return _pcall(body, name=name, grid_spec=gs, out_shape=S((slots, Rr, Cc), dtype), compiler_params=_params(("parallel",)))(idx, w)


_PLAN_COPIES = dict(ici=3, fwd=3, xchg=N_CHIPS, scat=3, share=1)


def _plan_copies(plan):
    return sum(_PLAN_COPIES[step[0]] for step in plan)


def _comm_copies(bufs, plan, send, recv):
    x, y, c, chips, qk = _place()
    me_q, sib = 2 * x + y, (x, y, 1 - c)
    starts, lands = [], []

    def add(src, dst, to, land):
        s = len(starts)
        starts.append(_remote(src, dst, send.at[s], recv.at[s], to))
        lands.append(_remote(land, land, send.at[s], recv.at[s], sib))

    for step in plan:
        kind = step[0]
        if kind == "ici":
            mine = bufs[step[1]].at[me_q, c]
            for k in range(3):
                add(mine, mine, (*chips[k], c), bufs[step[1]].at[qk[k], c])
        elif kind == "fwd":
            for k in range(3):
                got = bufs[step[1]].at[qk[k], c]
                add(got, got, sib, bufs[step[1]].at[qk[k], 1 - c])
        elif kind == "xchg":
            for q in range(N_CHIPS):
                add(bufs[step[1]].at[q, 1 - c], bufs[step[2]].at[q], sib, bufs[step[2]].at[q])
        elif kind == "scat":
            for k in range(3):
                add(bufs[step[1]].at[qk[k]], bufs[step[2]].at[k], (*chips[k], c), bufs[step[2]].at[k])
        else:
            mine = bufs[step[1]].at[step[2], c]
            add(mine, mine, sib, bufs[step[1]].at[step[2], 1 - c])
    return starts, lands


def _run_copies(bufs, plan, send, recv, start=True, wait=True):
    starts, lands = _comm_copies(bufs, plan, send, recv)
    if start:
        for cp in starts:
            cp.start()
    if wait:
        for cp in lands:
            cp.wait_recv()
        for cp in starts:
            cp.wait_send()


def _comm_call(name, slabs, phases):
    n = len(slabs)

    def body(*refs):
        bufs, sems = refs[n:2 * n], refs[2 * n:]
        for t, plan in enumerate(phases):
            _run_copies(bufs, plan, sems[2 * t], sems[2 * t + 1])

    sems = [pltpu.SemaphoreType.DMA((_plan_copies(plan),)) for plan in phases for _ in range(2)]
    return _pcall(body, name=name, in_specs=[ANY] * n, out_specs=[ANY] * n, out_shape=[S(s.shape, s.dtype) for s in slabs],
                  input_output_aliases={p: p for p in range(n)}, scratch_shapes=sems)(*slabs)


def _gather_all_devices(block):
    m_per, ncol = block.shape

    def body(x_ref, out_ref, send_sems, recv_sems, local_sem):
        x, y, c, chips, _ = _place()
        me, sib = (x, y, c), (x, y, 1 - c)

        def rows(px, py, pc):
            return out_ref.at[pl.ds((4 * px + 2 * py + pc) * m_per, m_per), :]

        def copy(k, blk, to, src=None):
            return _remote(rows(*blk) if src is None else src, rows(*blk), send_sems.at[k], recv_sems.at[k], to)

        mine = pltpu.make_async_copy(x_ref, rows(*me), local_sem)
        mine.start()
        first = [copy(0, me, sib, src=x_ref)] + [copy(1 + j, me, (*chip, c), src=x_ref) for j, chip in enumerate(chips)]
        for cp in first:
            cp.start()
        passed = [copy(4 + j, (*chip, c), sib) for j, chip in enumerate(chips)]
        for j, chip in enumerate(chips):
            copy(1 + j, (*chip, c), me).wait_recv()
            passed[j].start()
        copy(0, sib, me).wait_recv()
        for j, chip in enumerate(chips):
            copy(4 + j, (*chip, 1 - c), me).wait_recv()
        for cp in first + passed:
            cp.wait_send()
        mine.wait()

    vm = pl.BlockSpec(memory_space=pltpu.VMEM)
    return _pcall(body, name="gather_small_grads", in_specs=[vm], out_specs=vm,
                  out_shape=S((8 * m_per, ncol), block.dtype),
                  scratch_shapes=[pltpu.SemaphoreType.DMA((7,)), pltpu.SemaphoreType.DMA((7,)), pltpu.SemaphoreType.DMA],
                  compiler_params=pltpu.CompilerParams(vmem_limit_bytes=VMEM_LIMIT_BYTES))(block)


def _sum_pair(name, g, r1, c_idx):
    Q, _, Rr, Cc = g.shape
    tr = _tile(Rr, (512, 256, 128))

    def body(c_ref, g_ref, r_ref, o_ref):
        o_ref[...] = (g_ref[...].astype(f32) + r_ref[...].astype(f32)).astype(o_ref.dtype)

    spec = pl.BlockSpec((None, tr, Cc), lambda q, i, c_ref: (q, i, 0))
    gs = pltpu.PrefetchScalarGridSpec(
        num_scalar_prefetch=1, grid=(Q, Rr // tr),
        in_specs=[pl.BlockSpec((None, None, tr, Cc), lambda q, i, c_ref: (q, c_ref[0], i, 0)), spec], out_specs=spec)
    return _pcall(body, name=name, grid_spec=gs, out_shape=S((Q, Rr, Cc), bf16),
                  compiler_params=_params(("parallel", "parallel")))(c_idx, g, r1)


def _sum_owner(name, s, r2, qc_idx, l, nl, into):
    Q, Rr, Cc = s.shape
    tr = _tile(Rr, (512, 256, 128))

    def body(q_ref, s_ref, r_ref, *rest):
        o_ref = rest[-1]
        o_ref[...] = ((s_ref[...].astype(f32) + r_ref[0].astype(f32)) + r_ref[1].astype(f32)) + r_ref[2].astype(f32)

    in_specs = [pl.BlockSpec((None, tr, Cc), lambda i, q_ref: (q_ref[0], i, 0)), pl.BlockSpec((3, tr, Cc), lambda i, q_ref: (0, i, 0))]
    gs = pltpu.PrefetchScalarGridSpec(
        num_scalar_prefetch=1, grid=(Rr // tr,), in_specs=in_specs + ([] if into is None else [ANY]),
        out_specs=pl.BlockSpec((None, None, tr, Cc), lambda i, q_ref: (l, q_ref[1], i, 0)))
    return _pcall(body, name=name, grid_spec=gs, out_shape=S((nl, 2, Rr, Cc), f32),
                  input_output_aliases={} if into is None else {3: 0},
                  compiler_params=_params(("parallel",)))(qc_idx, s, r2, *([] if into is None else [into]))


def _sum_devices(stack):
    n, M, C = stack.shape

    def body(s_ref, o_ref):
        acc = s_ref[0]
        for d in range(1, n):
            acc = acc + s_ref[d]
        o_ref[...] = acc

    return _pcall(body, name="sum_small_grads", out_shape=S((M, C), f32), compiler_params=_params())(stack)


def _adamw(name, w, g, m, v):
    Rr, Cc = w.shape
    tr = _tile(Rr, (256, 128, 64, 32, 16, 8))

    def body(w_ref, g_ref, m_ref, v_ref, go_ref, d_ref, mo_ref, vo_ref):
        g_ = g_ref[...]
        go_ref[...] = g_
        m_ = B1 * m_ref[...] + (1.0 - B1) * g_
        v_ = B2 * v_ref[...] + (1.0 - B2) * jnp.square(g_)
        m_hat = m_ / (1.0 - B1 ** STEP)
        v_hat = v_ / (1.0 - B2 ** STEP)
        d_ref[...] = -LR * (m_hat / (jnp.sqrt(v_hat) + ADAM_EPS) + WD * w_ref[...])
        mo_ref[...] = m_
        vo_ref[...] = v_

    spec = pl.BlockSpec((tr, Cc), lambda i: (i, 0))
    return _pcall(body, name=name, grid=(Rr // tr,), in_specs=[spec] * 4, out_specs=[spec] * 4,
                  out_shape=[S((Rr, Cc), f32)] * 4, compiler_params=_params(("parallel",)))(w, g, m, v)


def _pack(arrs, multiple):
    flat = jnp.concatenate([a.reshape(-1) for a in arrs])
    pad = (-flat.shape[0]) % multiple
    return jnp.pad(flat, (0, pad)).reshape(-1, 128)


def _unpack(slab, shapes):
    flat, out, o = slab.reshape(-1), [], 0
    for shp in shapes:
        n = 1
        for d in shp:
            n *= d
        out.append(flat[o:o + n].reshape(shp))
        o += n
    return out


def kernel(x, meta_tokens, norm_mix, norm_ffn, conv_w_in, conv_b_in, conv_dw, conv_ln_g, conv_ln_b, conv_w_out, conv_b_out, kv_norm, w_kv, k_norm, w_q, q_norm, attn_sinks, w_o, ffn_w_gate, ffn_w_up, ffn_w_down, loss_target, m_meta_tokens, m_norm_mix, m_norm_ffn, m_conv_w_in, m_conv_b_in, m_conv_dw, m_conv_ln_g, m_conv_ln_b, m_conv_w_out, m_conv_b_out, m_kv_norm, m_w_kv, m_k_norm, m_w_q, m_q_norm, m_attn_sinks, m_w_o, m_ffn_w_gate, m_ffn_w_up, m_ffn_w_down, v_meta_tokens, v_norm_mix, v_norm_ffn, v_conv_w_in, v_conv_b_in, v_conv_dw, v_conv_ln_g, v_conv_ln_b, v_conv_w_out, v_conv_b_out, v_kv_norm, v_w_kv, v_k_norm, v_w_q, v_q_norm, v_attn_sinks, v_w_o, v_ffn_w_gate, v_ffn_w_up, v_ffn_w_down):
    Q = N_CHIPS
    B, SEQ, D = x.shape
    L = N_META_ROWS + SEQ
    Lp = -(-L // ATT_BLOCK) * ATT_BLOCK
    T = B * Lp
    NA, NB = conv_w_in.shape[0], w_q.shape[0]
    NL = NA + NB
    Dq = D // Q
    G = N_KV
    R = D // (HEAD_DIM * G)
    KVW = w_kv.shape[1]
    assert NA % 2 == 0 and NB % 2 == 0 and NL % 2 == 0 and (D // Q) % 32 == 0
    tm = _tile(T, (1088, 544, 512, 256, 128))
    tk = _tile(T, (2176, 1088, 544, 512, 256, 128))
    tr = _tile(T, (272, 256, 128))
    my_c = lax.axis_index("c").astype(jnp.int32).reshape(1)
    my_q = (2 * lax.axis_index("x") + lax.axis_index("y")).astype(jnp.int32)
    my_qc = jnp.concatenate([my_q.reshape(1), my_c])

    small_shapes = [meta_tokens.shape, conv_b_in.shape, conv_dw.shape, conv_ln_g.shape, conv_ln_b.shape, conv_b_out.shape]
    small = _pack([meta_tokens, conv_b_in, conv_dw, conv_ln_g, conv_ln_b, conv_b_out], 2048)
    big = [conv_w_in, conv_w_out, w_kv, w_q, w_o, ffn_w_gate, ffn_w_up, ffn_w_down]
    keyed = dict(cin=conv_w_in, cout=conv_w_out, kv=w_kv[None], q=w_q, o=w_o, g=ffn_w_gate, u=ffn_w_up, d=ffn_w_down,
                 small=small[None])
    slabs, where = [], {}
    for key, w3 in keyed.items():
        for l in range(w3.shape[0]):
            s = _into_slot(f"own_{key}{l}", w3, l, my_qc, Q, f32 if key == "small" else bf16)
            where[key, l] = len(slabs)
            slabs.append(s.reshape(Q, 2, s.shape[1] // 2, s.shape[2]))

    def W(key, l):
        s = slabs[where[key, l]]
        return s.reshape(Q, 1, 2 * s.shape[2], s.shape[3])

    def layer_slabs(l):
        j = l - NA
        mix = [("cin", l), ("cout", l)] if l < NA else [("q", j), ("o", j)] + ([("kv", 0)] if j == 0 else [])
        return [where[k] for k in mix], [where["g", l], where["u", l]], [where["d", l]]

    def carry_of(ici, fwd):
        idxs = sorted(set(ici) | set(fwd))
        return idxs, ([slabs[i] for i in idxs], [("ici", idxs.index(i)) for i in ici] + [("fwd", idxs.index(i)) for i in fwd])

    def put_back(idxs, new):
        for i, s in zip(idxs, new):
            slabs[i] = s

    assert NA >= 1
    first = [where["cin", 0], where["small", 0]]
    idxs, (sl, plan) = carry_of(first, first)
    put_back(idxs, _comm_call("gather_first", sl, [[s for s in plan if s[0] == "ici"], [s for s in plan if s[0] == "fwd"]]))
    parts = [_unpack(slabs[where["small", 0]][q], small_shapes) for q in range(Q)]
    meta_f, b_in_f, dw_f, ln_g_f, ln_b_f, b_out_f = [jnp.concatenate([parts[q][i] for q in range(Q)], axis=-1) for i in range(6)]
    dw_pad = jnp.pad(dw_f, ((0, 0), (0, CONV_PAD - CONV_TAPS), (0, 0)))

    h = jnp.concatenate([jnp.broadcast_to(meta_f[None], (B, N_META_ROWS, D)), x, jnp.zeros((B, Lp - L, D), f32)], axis=1).reshape(T, D)
    tgt = jnp.pad(loss_target, ((0, 0), (N_META_ROWS, Lp - L), (0, 0))).reshape(T, D)
    row = lambda a: a.reshape(1, -1)
    seqs = lambda a: a.reshape(B, Lp, a.shape[-1])
    by_seq = lambda a: a.reshape(G, B, Lp, HEAD_DIM)
    saved = []
    sinks3 = attn_sinks.reshape(NB, G, 1, R)
    def carrying(ici, fwd):
        idxs, carry = carry_of(ici, fwd)
        return idxs, (carry if idxs else None)

    normed = []
    for l in range(NL):
        st = {"h_a": h}
        u = normed[0] if normed else _rowwise(f"rms_mix{l}", _rms_fn, [h], [row(norm_mix[l])], [bf16], tr)[0]
        st["u"] = u
        mix_s, (g_s, u_s), d_s = layer_slabs(l)
        idxs, carry = carrying([where["cout", 0], g_s], []) if l == 0 else carrying([], [g_s])
        second = [u_s] + d_s if l == 0 else [u_s]
        if l < NA:
            p, av, ag, *new = _glu_fwd(f"glu{l}", u, W("cin", l), 0, b_in_f[l:l + 1], tm, carry=carry)
            put_back(idxs, new)
            idxs, carry = carrying([u_s] + d_s, [where["cout", 0], g_s]) if l == 0 else carrying([u_s], [])
            cv, *new = _dwconv_fwd(f"dwconv{l}", p.reshape(B, Lp, D), dw_pad[l], carry=carry)
            put_back(idxs, new)
            cv = cv.reshape(T, D)
            s = _rowwise(f"ln_silu{l}", _ln_silu_fn, [cv], [row(ln_g_f[l]), row(ln_b_f[l])], [bf16], tr)[0]
            idxs2, carry2 = carrying([], second)
            h, u2, *new = _proj(f"conv_out{l}", s, W("cout", l), 0, tm, bias=row(b_out_f[l]), resid=h, carry=carry2,
                                norm=row(norm_ffn[l]))
            put_back(idxs2, new)
            st.update(av=av, ag=ag, p=p, cv=cv, s=s)
        else:
            j = l - NA
            if j == 0:
                kvn = normed[1] if len(normed) > 1 else _rowwise("rms_kv", _rms_fn, [h], [row(kv_norm)], [bf16], tr)[0]
                kv = _proj("kv_proj", kvn, W("kv", 0), 0, tm)
                kn, vh = _kv_heads("kv_heads", kv, row(k_norm), tm)
                kn, vh = by_seq(kn), by_seq(vh)
                st.update(kvn=kvn, kv=kv)
            res = _proj(f"q_proj{j}", u, W("q", j), 0, tm, carry=carry)
            q, new = res if carry is not None else (res, [])
            put_back(idxs, new)
            q = seqs(q)
            idxs, carry = carrying([u_s], [])
            o, *new = _attn_fwd(f"attn{j}", q, kn, vh, row(q_norm[j]), sinks3[j], carry=carry)
            put_back(idxs, new)
            o = o.reshape(T, D)
            idxs2, carry2 = carrying([], second)
            h, u2, *new = _proj(f"o_proj{j}", o, W("o", j), 0, tm, resid=h, carry=carry2, norm=row(norm_ffn[l]))
            put_back(idxs2, new)
            st.update(q=q, o=o)
        st["h_b"] = h
        nmix, ngu, nd = layer_slabs(l + 1) if l + 1 < NL else ([], [], [])
        idxs, carry = carrying(nmix + nd, [])
        dgate_of, dup_of, hid, *new = _ffn_up(f"ffn_up{l}", u2, W("g", l), W("u", l), 0, tm, carry=carry)
        put_back(idxs, new)
        idxs, carry = carrying(ngu[:1], nmix + nd)
        gains = [] if l + 1 == NL else [row(norm_mix[l + 1])] + ([row(kv_norm)] if l + 1 == NA else [])
        h, *rest = _ffn_down(f"ffn_down{l}", hid, W("d", l), 0, h, tm, carry=carry, norms=gains)
        normed, new = rest[:len(gains)], rest[len(gains):]
        put_back(idxs, new)
        st.update(u2=u2, dgate_of=dgate_of, dup_of=dup_of, hid=hid)
        saved.append(st)

    dh, dhb, part = _loss_head(h, tgt, Lp, SEQ, tr)
    loss = lax.psum(0.5 / D * jnp.sum(part), ("x", "y", "c"))

    grad_slab = {}
    n_layers = dict(cin=NA, cout=NA, kv=1, q=NB, o=NB, g=NL, u=NL, d=NL)

    def rs_begin(pieces):
        job = []
        for key, lay, g in pieces:
            g4 = g.reshape(Q, 2, g.shape[1] // 2, g.shape[2])
            job.append(dict(key=key, l=lay, g=g4, r1=lax.empty((Q,) + g4.shape[2:], bf16)))
        return job

    def xchg_carry(job):
        n = len(job)
        return [p["g"] for p in job] + [p["r1"] for p in job], [("xchg", t, n + t) for t in range(n)]

    def after_xchg(job, new):
        n = len(job)
        for t, p in enumerate(job):
            p["s"] = _sum_pair(f"sum_pair_{p['key']}{p['l']}", new[t], new[n + t], my_c)
            p["r2"] = lax.empty((3,) + p["s"].shape[1:], bf16)

    def scat_carry(job, keys):
        sel = [p for p in job if p["key"] in keys]
        n = len(sel)
        return sel, ([p["s"] for p in sel] + [p["r2"] for p in sel], [("scat", t, n + t) for t in range(n)])

    def after_scat(sel, new):
        n = len(sel)
        for t, p in enumerate(sel):
            key = p["key"]
            grad_slab[key] = _sum_owner(f"sum_owner_{key}{p['l']}", new[t], new[n + t], my_qc, p["l"], n_layers[key], grad_slab.get(key))

    def share_carry(job):
        return [grad_slab[p["key"]] for p in job], [("share", t, p["l"]) for t, p in enumerate(job)]

    def after_share(job, new):
        for p, s in zip(job, new):
            grad_slab[p["key"]] = s

    def carried(job, make):
        return make(job) if job else None

    def merge(*carries):
        slabs_, plan_ = [], []
        for c in carries:
            if c is not None:
                off = len(slabs_)
                slabs_ += c[0]
                plan_ += [(s[0], s[1] + off, s[2]) if s[0] == "share" else (s[0],) + tuple(i + off for i in s[1:]) for s in c[1]]
        return (slabs_, plan_) if slabs_ else None

    MIXER = ("cin", "cout", "q", "o", "kv")
    g_mix, g_ffn = [None] * NL, [None] * NL
    g_bin, g_dw, g_lng, g_lnb, g_bout = ([None] * NA for _ in range(5))
    g_qn, g_sink = [None] * NB, [None] * NB
    dknp = dvp = None
    zero_row = jnp.zeros((1, D), f32)
    job = []
    early = []
    for l in reversed(range(NL)):
        st = saved[l]
        pieces = []
        dgate, dup, *new = _ffn_dhid(f"ffn_dhid{l}", dhb, W("d", l), 0, st["dgate_of"], st["dup_of"], tm, carry=carried(job, xchg_carry))
        if job:
            after_xchg(job, new)
        sel, carry = scat_carry(job, ("d",)) if job else ([], None)
        dwd, *new = _dw_rows(f"ffn_dwd{l}", st["hid"], dhb, Q, tk, carry=carry)
        after_scat(sel, new)
        sel, carry = scat_carry(job, ("g", "u")) if job else ([], None)
        dwg, dwu, *new = _dw_cols(f"ffn_dwgu{l}", st["u2"], dgate, Q, tk, Q, carry=carry, dyc2=dup)
        after_scat(sel, new)
        pieces += [("d", l, dwd), ("g", l, dwg), ("u", l, dwu)]
        if l == 0 and l < NA:
            early, pieces = rs_begin(pieces), []
        sel, carry = scat_carry(job, MIXER) if job else ([], None)
        du2, *new = _ffn_du(f"ffn_du{l}", dgate, dup, W("g", l), W("u", l), 0, tm, carry=merge(carry, carried(early, xchg_carry)))
        after_scat(sel, new[:2 * len(sel)])
        if early:
            after_xchg(early, new[2 * len(sel):])
        dh, g_ffn[l], dhb = _rowwise_vjp(f"rms_ffn_bwd{l}", _rms_res_fn, [st["h_b"]], [row(norm_ffn[l])], [du2, dh], [f32], tr,
                                         also_bf16=True)
        if l < NA:
            esel, ecarry = scat_carry(early, ("d",)) if early else ([], None)
            ds, *new = _proj_dx(f"conv_out_dx{l}", dhb, W("cout", l), 0, tm, carry=merge(carried(job, share_carry), ecarry))
            after_share(job, new[:len(job)])
            after_scat(esel, new[len(job):])
            pieces.append(("cout", l, _dw_rows(f"conv_out_dw{l}", st["s"], dhb, Q, tk)[0]))
            dcv, g_lng[l], g_lnb[l] = _rowwise_vjp(f"ln_silu_bwd{l}", _ln_silu_fn, [st["cv"]], [row(ln_g_f[l]), row(ln_b_f[l])], [ds], [f32], tr)
            dp, g_dw[l] = _dwconv_bwd(f"dwconv_bwd{l}", dcv.reshape(B, Lp, D), st["p"].reshape(B, Lp, D), dw_pad[l])
            dav, dag, dbv, dbg = _rowwise_vjp(f"glu_bwd{l}", _glu_fn, [st["av"], st["ag"]], [zero_row, zero_row], [dp.reshape(T, D)], [bf16, bf16], tr)
            g_bin[l] = jnp.concatenate([dbv, dbg], axis=1)
            esel, ecarry = scat_carry(early, ("g",)) if early else ([], None)
            dwin, *new = _dw_cols(f"glu_dwv{l}", st["u"], dav, Q // 2, tk, Q, carry=ecarry)
            after_scat(esel, new)
            esel, ecarry = scat_carry(early, ("u",)) if early else ([], None)
            dwin, *new = _dw_cols(f"glu_dwg{l}", st["u"], dag, Q // 2, tk, Q, q_off=Q // 2, into=dwin, carry=ecarry)
            after_scat(esel, new)
            pieces.append(("cin", l, dwin))
            du = _glu_du(f"glu_du{l}", dav, dag, W("cin", l), 0, tm)
            dh, g_mix[l], g_bout[l], dhb = _rowwise_vjp(f"rms_mix_bwd{l}", _rms_res_bias_fn, [st["h_a"]], [row(norm_mix[l]), zero_row],
                                                        [du, dh], [f32], tr, also_bf16=True)
        else:
            j = l - NA
            do, *new = _proj_dx(f"o_proj_dx{j}", dhb, W("o", j), 0, tm, out_dtype=bf16, carry=carried(job, share_carry))
            after_share(job, new)
            pieces.append(("o", j, _dw_rows(f"o_proj_dw{j}", st["o"], dhb, Q, tk)[0]))
            dq, dk1, dv1, g_qn[j], g_sink[j] = _attn_bwd(f"attn_bwd{j}", st["q"], kn, vh, row(q_norm[j]), sinks3[j], seqs(do))
            dknp, dvp = (dk1, dv1) if dknp is None else (dknp + dk1, dvp + dv1)
            dq = dq.reshape(T, D)
            pieces.append(("q", j, _dw_rows(f"q_proj_dw{j}", st["u"], dq, Q, tk)[0]))
            du = _proj_dx(f"q_proj_dx{j}", dq, W("q", j), 0, tm)[0]
            dh, g_mix[l], dhb = _rowwise_vjp(f"rms_mix_bwd{l}", _rms_res_fn, [st["h_a"]], [row(norm_mix[l])], [du, dh], [f32], tr,
                                             also_bf16=True)
            if j == 0:
                dkv, g_kn = _kv_heads_bwd("kv_heads_bwd", st["kv"], row(k_norm), dknp.reshape(G, T, HEAD_DIM), dvp.reshape(G, T, HEAD_DIM), tm)
                pieces.append(("kv", 0, _dw_rows("kv_proj_dw", st["kvn"], dkv, Q, tk)[0]))
                dkvn = _proj_dx("kv_proj_dx", dkv, W("kv", 0), 0, tm)[0]
                dh, g_kvn, dhb = _rowwise_vjp("rms_kv_bwd", _rms_res_fn, [st["h_a"]], [row(kv_norm)], [dkvn, dh], [f32], tr,
                                              also_bf16=True)
        job = rs_begin(pieces)
    dh3 = dh.reshape(B, Lp, D)
    grad_x = _take_rows("grad_x", dh3, N_META_ROWS, SEQ)
    g_meta = jnp.sum(dh3[:, :N_META_ROWS], axis=0)

    sl, plan = xchg_carry(job)
    after_xchg(job, _comm_call("rs_exchange", sl, [plan]))
    sel, (sl, plan) = scat_carry(job, tuple(n_layers))
    after_scat(sel, _comm_call("rs_scatter", sl, [plan]))
    sl, plan = share_carry(job + early)
    after_share(job + early, _comm_call("rs_share", sl, [plan]))

    names = ["conv_w_in", "conv_w_out", "w_kv", "w_q", "w_o", "ffn_w_gate", "ffn_w_up", "ffn_w_down"]
    ws = dict(zip(names, big))
    ms = dict(zip(names, [m_conv_w_in, m_conv_w_out, m_w_kv, m_w_q, m_w_o, m_ffn_w_gate, m_ffn_w_up, m_ffn_w_down]))
    vs = dict(zip(names, [v_conv_w_in, v_conv_w_out, v_w_kv, v_w_q, v_w_o, v_ffn_w_gate, v_ffn_w_up, v_ffn_w_down]))
    out_g, out_d, out_m, out_v = {}, {}, {}, {}
    for nm, key in zip(names, ("cin", "cout", "kv", "q", "o", "g", "u", "d")):
        w, gsh = ws[nm], grad_slab[key]
        flat = lambda a: a.reshape(-1, w.shape[-1])
        g2, d2, m2, v2 = _adamw(f"adamw_{nm}", flat(w), flat(gsh), flat(ms[nm]), flat(vs[nm]))
        out_g[nm], out_d[nm], out_m[nm], out_v[nm] = (a.reshape(w.shape) for a in (g2, d2, m2, v2))

    small_names = ["norm_mix", "norm_ffn", "kv_norm", "k_norm", "q_norm", "attn_sinks", "meta_tokens", "conv_b_in", "conv_dw", "conv_ln_g", "conv_ln_b", "conv_b_out"]
    small_grads = [jnp.concatenate(g_mix, 0), jnp.concatenate(g_ffn, 0), g_kvn.reshape(-1), g_kn.reshape(-1), jnp.concatenate(g_qn, 0),
                   jnp.stack(g_sink).reshape(NB, G * R), g_meta, jnp.concatenate(g_bin, 0), jnp.stack(g_dw)[:, :CONV_TAPS],
                   jnp.concatenate(g_lng, 0), jnp.concatenate(g_lnb, 0), jnp.concatenate(g_bout, 0)]
    slab = _pack(small_grads, 1024)
    total = _sum_devices(_gather_all_devices(slab).reshape(8, slab.shape[0], 128))
    full_grads = _unpack(total, [g.shape for g in small_grads])
    small_w = dict(zip(small_names, [norm_mix, norm_ffn, kv_norm, k_norm, q_norm, attn_sinks, meta_tokens, conv_b_in, conv_dw, conv_ln_g, conv_ln_b, conv_b_out]))
    small_m = dict(zip(small_names, [m_norm_mix, m_norm_ffn, m_kv_norm, m_k_norm, m_q_norm, m_attn_sinks, m_meta_tokens, m_conv_b_in, m_conv_dw, m_conv_ln_g, m_conv_ln_b, m_conv_b_out]))
    small_v = dict(zip(small_names, [v_norm_mix, v_norm_ffn, v_kv_norm, v_k_norm, v_q_norm, v_attn_sinks, v_meta_tokens, v_conv_b_in, v_conv_dw, v_conv_ln_g, v_conv_ln_b, v_conv_b_out]))
    local_grads = []
    for nm, g in zip(small_names, full_grads):
        w = small_w[nm]
        if g.shape != w.shape:
            wq = w.shape[-1]
            g = lax.dynamic_slice_in_dim(g, my_q * wq, wq, axis=g.ndim - 1)
        local_grads.append(g)
    shapes = [small_w[nm].shape for nm in small_names]
    _, d_s, m_s, v_s = _adamw("adamw_small", _pack([small_w[nm] for nm in small_names], 1024), _pack(local_grads, 1024),
                           _pack([small_m[nm] for nm in small_names], 1024), _pack([small_v[nm] for nm in small_names], 1024))
    for nm, g, d_, m_, v_ in zip(small_names, local_grads, _unpack(d_s, shapes), _unpack(m_s, shapes), _unpack(v_s, shapes)):
        out_g[nm], out_d[nm], out_m[nm], out_v[nm] = g, d_, m_, v_

    order = ["meta_tokens", "norm_mix", "norm_ffn", "conv_w_in", "conv_b_in", "conv_dw", "conv_ln_g", "conv_ln_b", "conv_w_out", "conv_b_out",
             "kv_norm", "w_kv", "k_norm", "w_q", "q_norm", "attn_sinks", "w_o", "ffn_w_gate", "ffn_w_up", "ffn_w_down"]
    return (loss, grad_x, *[out_g[n] for n in order], *[out_d[n] for n in order], *[out_m[n] for n in order], *[out_v[n] for n in order])
```

```python
import functools

import jax
import jax.numpy as jnp
from jax import lax
from jax.experimental import pallas as pl
from jax.experimental.pallas import tpu as pltpu

f32, bf16 = jnp.float32, jnp.bfloat16

N_META_ROWS = 16
ATT_BLOCK = 128
HEAD_DIM = 64
N_KV = 4
CONV_TAPS = 31
CONV_PAD = 32
EPS = 1e-6
MASKED = -1e30
LR, B1, B2, ADAM_EPS, WD, STEP = 0.001, 0.9, 0.999, 1e-08, 0.01, 10
N_CHIPS = 4
VMEM_LIMIT_BYTES = 56 * 1024 * 1024
MESH = pl.DeviceIdType.MESH
ANY = pl.BlockSpec(memory_space=pl.ANY)
S = jax.ShapeDtypeStruct


def _pcall(body, **kw):
    return pl.pallas_call(body, **kw)


def _params(sem=None):
    return pltpu.CompilerParams(dimension_semantics=sem, vmem_limit_bytes=VMEM_LIMIT_BYTES)


def _tile(n, prefs):
    for p in prefs:
        if n % p == 0:
            return p
    return n


_DN = {"nn": (((1,), (0,)), ((), ())), "nt": (((1,), (1,)), ((), ())), "tn": (((0,), (0,)), ((), ()))}


def _matmul(name, mode, grid, a_ops, b_ops, x_ops, outs, terms, acc_shape, n_acc, epilogue, into=None, carry=None):
    na, nb, nx, no = len(a_ops), len(b_ops), len(x_ops), len(outs)
    nk = grid[2]
    slabs, plan = carry if carry is not None else ([], [])
    nc, ncp = len(slabs), _plan_copies(plan)
    n_in = na + nb + nx + (0 if into is None else 1) + nc

    def flat2d(v):
        return v.reshape(-1, v.shape[-1]) if v.ndim == 3 else v

    def dot(a, b):
        return lax.dot_general(a.astype(bf16), b.astype(bf16), _DN[mode], preferred_element_type=f32)

    def dots(a_refs, b_refs):
        parts = [None] * n_acc
        shared = {}

        def lhs(ai):
            if ai not in shared:
                a = flat2d(a_refs[ai][...]).astype(bf16)
                shared[ai] = a.T if mode == "tn" and n_terms_of[ai] > 1 else a
            return shared[ai]

        for ai, bi, ci in terms:
            a_ref, b_ref = a_refs[ai], b_refs[bi]
            if len(a_ref.shape) == 3 and len(b_ref.shape) == 3:
                for c in range(a_ref.shape[0]):
                    d = dot(a_ref[c], b_ref[c])
                    parts[ci] = d if parts[ci] is None else parts[ci] + d
            else:
                dn = _DN["nn"] if mode == "tn" and n_terms_of[ai] > 1 else _DN[mode]
                d = lax.dot_general(lhs(ai), flat2d(b_ref[...]).astype(bf16), dn, preferred_element_type=f32)
                parts[ci] = d if parts[ci] is None else parts[ci] + d
        return parts

    n_terms_of = {ai: sum(1 for t in terms if t[0] == ai) for ai in range(na)}

    def finish(accs, x_refs, o_refs):
        res = epilogue(accs, [x[...] for x in x_refs])
        for o_ref, r in zip(o_refs, res):
            o_ref[...] = r.reshape(o_ref.shape).astype(o_ref.dtype)

    def compute(a_refs, b_refs, x_refs, o_refs, acc_refs):
        if nk == 1:
            finish(dots(a_refs, b_refs), x_refs, o_refs)
            return
        k = pl.program_id(2)

        @pl.when(k == 0)
        def _():
            for acc in acc_refs:
                acc[...] = jnp.zeros_like(acc)

        for acc, d in zip(acc_refs, dots(a_refs, b_refs)):
            acc[...] += d

        @pl.when(k == nk - 1)
        def _():
            finish([acc[...] for acc in acc_refs], x_refs, o_refs)

    def body(*refs):
        a_refs, b_refs = refs[:na], refs[na:na + nb]
        x_refs = refs[na + nb:na + nb + nx]
        o_refs = refs[n_in:n_in + no]
        bufs = refs[n_in + no:n_in + no + nc]
        scratch = refs[n_in + no + nc:]
        if not nc:
            compute(a_refs, b_refs, x_refs, o_refs, scratch)
            return
        acc_refs, (send, recv) = scratch[:-2], scratch[-2:]
        i, j, k = pl.program_id(0), pl.program_id(1), pl.program_id(2)

        @pl.when((i == 0) & (j == 0) & (k == 0))
        def _():
            _run_copies(bufs, plan, send, recv, start=True, wait=False)

        compute(a_refs, b_refs, x_refs, o_refs, acc_refs)

        @pl.when((i == grid[0] - 1) & (j == grid[1] - 1) & (k == nk - 1))
        def _():
            _run_copies(bufs, plan, send, recv, start=False, wait=True)

    ops = list(a_ops) + list(b_ops) + list(x_ops)
    aliases = {}
    if into is not None:
        ops.append((into, ANY))
        aliases = {len(ops) - 1: 0}
    for t, s in enumerate(slabs):
        ops.append((s, ANY))
        aliases[len(ops) - 1] = no + t
    outs = list(outs) + [(S(s.shape, s.dtype), ANY) for s in slabs]
    scratch = [pltpu.VMEM(acc_shape, f32)] * (n_acc if nk > 1 else 0)
    if nc:
        scratch += [pltpu.SemaphoreType.DMA((ncp,)), pltpu.SemaphoreType.DMA((ncp,))]
    sem = ("arbitrary",) * 3 if nc else ("parallel", "parallel", "arbitrary")
    res = _pcall(body, name=name, grid=grid, in_specs=[s for _, s in ops], out_specs=[s for _, s in outs],
                 out_shape=[s for s, _ in outs], scratch_shapes=scratch, input_output_aliases=aliases,
                 compiler_params=_params(sem))(*[a for a, _ in ops])
    return res


def _first(accs, xs):
    return (accs[0],)


def _proj(name, a, w, l, tm, bias=None, resid=None, out_dtype=f32, carry=None, norm=None):
    T = a.shape[0]
    Q, _, Kq, N = w.shape
    tn = N if norm is not None else _tile(N, (512, 256, 128))
    x_ops = []
    if bias is not None:
        x_ops.append((bias, pl.BlockSpec((1, tn), lambda i, j, k: (0, j))))
    if resid is not None:
        x_ops.append((resid, pl.BlockSpec((tm, tn), lambda i, j, k: (i, j))))
    n_add = len(x_ops)
    ospec = pl.BlockSpec((tm, tn), lambda i, j, k: (i, j))
    outs = [(S((T, N), out_dtype), ospec)]
    if norm is not None:
        x_ops.append((norm, pl.BlockSpec((1, tn), lambda i, j, k: (0, j))))
        outs.append((S((T, N), bf16), ospec))

    def epi(accs, xs):
        y = accs[0]
        for x in xs[:n_add]:
            y = y + x
        return (y,) if norm is None else (y, _rms(y, xs[n_add]))

    res = _matmul(name, "nn", (T // tm, N // tn, 1),
                  [(a, pl.BlockSpec((tm, Q * Kq), lambda i, j, k: (i, 0)))],
                  [(w, pl.BlockSpec((Q, None, Kq, tn), lambda i, j, k: (0, l, 0, j)))],
                  x_ops, outs, [(0, 0, 0)], (tm, tn), 1, epi, carry=carry)
    if norm is not None:
        return res
    return res[0] if carry is None else (res[0], res[1:])


def _proj_dx(name, dy, w, l, tm, out_dtype=f32, carry=None):
    T, N = dy.shape
    Q, _, Kq, _ = w.shape
    return _matmul(name, "nt", (T // tm, 1, 1),
                   [(dy, pl.BlockSpec((tm, N), lambda i, j, k: (i, 0)))],
                   [(w, pl.BlockSpec((Q, None, Kq, N), lambda i, j, k: (0, l, 0, 0)))],
                   [], [(S((T, Q * Kq), out_dtype), pl.BlockSpec((tm, Q * Kq), lambda i, j, k: (i, 0)))],
                   [(0, 0, 0)], (tm, Q * Kq), 1, _first, carry=carry)


def _dw_rows(name, a, dy, Q, tk, carry=None):
    T, N = dy.shape
    tn = _tile(N, (512, 256, 128))
    if a.ndim == 2:
        K = a.shape[1]
        a_op, ni, acc_rows = (a, pl.BlockSpec((tk, K), lambda i, j, k: (k, 0))), 1, K
        osh, ospec = S((Q, K // Q, N), bf16), pl.BlockSpec((Q, K // Q, tn), lambda i, j, k: (0, 0, j))
    else:
        Kq = a.shape[2]
        a_op, ni, acc_rows = (a, pl.BlockSpec((None, tk, Kq), lambda i, j, k: (i, k, 0))), Q, Kq
        osh, ospec = S((Q, Kq, N), bf16), pl.BlockSpec((None, Kq, tn), lambda i, j, k: (i, 0, j))
    return _matmul(name, "tn", (ni, N // tn, T // tk), [a_op], [(dy, pl.BlockSpec((tk, tn), lambda i, j, k: (k, j)))],
                   [], [(osh, ospec)], [(0, 0, 0)], (acc_rows, tn), 1, _first, carry=carry)


def _dw_cols(name, a, dyc, Qc, tk, Q, q_off=0, into=None, carry=None, dyc2=None):
    T, K = a.shape
    tkin = _tile(K, (512, 256, 128))
    if dyc.ndim == 3:
        Nq = dyc.shape[2]
        bspec = pl.BlockSpec((None, tk, Nq), lambda i, j, k: (j, k, 0))
    else:
        Nq = dyc.shape[1] // Qc
        bspec = pl.BlockSpec((tk, Nq), lambda i, j, k: (k, j))
    b_ops = [(dyc, bspec)] + ([] if dyc2 is None else [(dyc2, bspec)])
    ospec = pl.BlockSpec((None, tkin, Nq), lambda i, j, k: (j + q_off, i, 0))
    n = len(b_ops)
    return _matmul(name, "tn", (K // tkin, Qc, T // tk),
                   [(a, pl.BlockSpec((tk, tkin), lambda i, j, k: (k, i)))], b_ops, [],
                   [(S((Q, K, Nq), bf16), ospec)] * n, [(0, t, t) for t in range(n)], (tkin, Nq), n,
                   lambda accs, xs: tuple(accs), into=into, carry=carry)


def _glu_fwd(name, u, w, l, b_in, tm, carry=None):
    T, D = u.shape
    Q, _, _, Cq = w.shape
    H = Q // 2

    def epi(accs, xs):
        av, ag = accs[0] + xs[0], accs[1] + xs[1]
        return av * jax.nn.sigmoid(ag), av, ag

    wspec = lambda off: pl.BlockSpec((None, None, D, Cq), lambda i, j, k: (j + off, l, 0, 0))
    bspec = lambda off: pl.BlockSpec((1, Cq), lambda i, j, k: (0, j + off))
    ospec = pl.BlockSpec((tm, Cq), lambda i, j, k: (i, j))
    return _matmul(name, "nn", (T // tm, H, 1),
                   [(u, pl.BlockSpec((tm, D), lambda i, j, k: (i, 0)))],
                   [(w, wspec(0)), (w, wspec(H))], [(b_in, bspec(0)), (b_in, bspec(H))],
                   [(S((T, H * Cq), f32), ospec), (S((T, H * Cq), bf16), ospec), (S((T, H * Cq), bf16), ospec)],
                   [(0, 0, 0), (0, 1, 1)], (tm, Cq), 2, epi, carry=carry)


def _glu_du(name, dav, dag, w, l, tm):
    T = dav.shape[0]
    Q, _, D, Cq = w.shape
    H = Q // 2
    tn = _tile(D, (512, 256, 128))
    aspec = pl.BlockSpec((tm, Cq), lambda i, j, k: (i, k))
    wspec = lambda off: pl.BlockSpec((None, None, tn, Cq), lambda i, j, k: (k + off, l, j, 0))
    return _matmul(name, "nt", (T // tm, D // tn, H), [(dav, aspec), (dag, aspec)],
                   [(w, wspec(0)), (w, wspec(H))], [],
                   [(S((T, D), f32), pl.BlockSpec((tm, tn), lambda i, j, k: (i, j)))],
                   [(0, 0, 0), (1, 1, 0)], (tm, tn), 1, _first)[0]


def _ffn_up(name, u, wg, wu, l, tm, carry=None):
    T, D = u.shape
    Q, _, _, Fq = wg.shape

    def epi(accs, xs):
        g, up = accs
        sg = jax.nn.sigmoid(g)
        silu = g * sg
        return up * (sg * (1.0 + g * (1.0 - sg))), silu, silu * up

    wspec = pl.BlockSpec((None, None, D, Fq), lambda i, j, k: (j, l, 0, 0))
    ospec = pl.BlockSpec((None, tm, Fq), lambda i, j, k: (j, i, 0))
    osh = S((Q, T, Fq), bf16)
    return _matmul(name, "nn", (T // tm, Q, 1), [(u, pl.BlockSpec((tm, D), lambda i, j, k: (i, 0)))],
                   [(wg, wspec), (wu, wspec)], [], [(osh, ospec)] * 3, [(0, 0, 0), (0, 1, 1)], (tm, Fq), 2, epi, carry=carry)


def _ffn_down(name, hid, wd, l, resid, tm, carry=None, norms=()):
    Q, T, Fq = hid.shape
    D = wd.shape[3]
    if norms:
        tn, tm = D, (tm // 2 if tm % 32 == 0 and tm > 512 else tm)
    else:
        tn = _tile(D, (512, 256, 128))
    ospec = pl.BlockSpec((tm, tn), lambda i, j, k: (i, j))
    gspec = pl.BlockSpec((1, tn), lambda i, j, k: (0, j))

    def epi(accs, xs):
        h = accs[0] + xs[0]
        return (h,) + tuple(_rms(h, g) for g in xs[1:])

    return _matmul(name, "nn", (T // tm, D // tn, 1),
                   [(hid, pl.BlockSpec((Q, tm, Fq), lambda i, j, k: (0, i, 0)))],
                   [(wd, pl.BlockSpec((Q, None, Fq, tn), lambda i, j, k: (0, l, 0, j)))],
                   [(resid, ospec)] + [(g, gspec) for g in norms],
                   [(S((T, D), f32), ospec)] + [(S((T, D), bf16), ospec)] * len(norms),
                   [(0, 0, 0)], (tm, tn), 1, epi, carry=carry)


def _ffn_dhid(name, dy, wd, l, dgate_of, dup_of, tm, carry=None):
    T, D = dy.shape
    Q, _, Fq, _ = wd.shape

    def epi(accs, xs):
        return accs[0] * xs[0].astype(f32), accs[0] * xs[1].astype(f32)

    cspec = pl.BlockSpec((None, tm, Fq), lambda i, j, k: (j, i, 0))
    osh = S((Q, T, Fq), bf16)
    return _matmul(name, "nt", (T // tm, Q, 1), [(dy, pl.BlockSpec((tm, D), lambda i, j, k: (i, 0)))],
                   [(wd, pl.BlockSpec((None, None, Fq, D), lambda i, j, k: (j, l, 0, 0)))],
                   [(dgate_of, cspec), (dup_of, cspec)], [(osh, cspec)] * 2, [(0, 0, 0)], (tm, Fq), 1, epi, carry=carry)


def _ffn_du(name, dgate, dup, wg, wu, l, tm, carry=None):
    Q, T, Fq = dgate.shape
    D = wg.shape[2]
    tn = _tile(D, (512, 256, 128))
    tm = tm // 2 if tm % 32 == 0 and tm > 512 else tm
    aspec = pl.BlockSpec((Q, tm, Fq), lambda i, j, k: (0, i, 0))
    wspec = pl.BlockSpec((Q, None, tn, Fq), lambda i, j, k: (0, l, j, 0))
    return _matmul(name, "nt", (T // tm, D // tn, 1), [(dgate, aspec), (dup, aspec)], [(wg, wspec), (wu, wspec)], [],
                   [(S((T, D), f32), pl.BlockSpec((tm, tn), lambda i, j, k: (i, j)))],
                   [(0, 0, 0), (1, 1, 0)], (tm, tn), 1, _first, carry=carry)


def _rowwise(name, fn, rows, params, out_dtypes, tm):
    nr, npar = len(rows), len(params)
    T = rows[0].shape[0]
    shp = jax.eval_shape(fn, *[S((tm, r.shape[1]), f32) for r in rows], *[S(p.shape, f32) for p in params])

    def body(*refs):
        r = [x[...].astype(f32) for x in refs[:nr]]
        p = [x[...] for x in refs[nr:nr + npar]]
        for o_ref, o in zip(refs[nr + npar:], fn(*r, *p)):
            o_ref[...] = o.astype(o_ref.dtype)

    row_spec = lambda w: pl.BlockSpec((tm, w), lambda i: (i, 0))
    par_spec = lambda p: pl.BlockSpec(p.shape, lambda i: (0, 0))
    return _pcall(body, name=name, grid=(T // tm,),
                  in_specs=[row_spec(r.shape[1]) for r in rows] + [par_spec(p) for p in params],
                  out_specs=[row_spec(s.shape[1]) for s in shp],
                  out_shape=[S((T, s.shape[1]), dt) for s, dt in zip(shp, out_dtypes)],
                  compiler_params=_params(("parallel",)))(*rows, *params)


def _rowwise_vjp(name, fn, rows, params, cots, drow_dtypes, tm, also_bf16=False):
    nr, npar, nc = len(rows), len(params), len(cots)
    T = rows[0].shape[0]

    def body(*refs):
        r = [x[...].astype(f32) for x in refs[:nr]]
        p = [x[...] for x in refs[nr:nr + npar]]
        c = tuple(x[...].astype(f32) for x in refs[nr + npar:nr + npar + nc])
        o_refs = refs[nr + npar + nc:]
        _, vjp = jax.vjp(fn, *r, *p)
        grads = vjp(c)
        for o_ref, g in zip(o_refs[:nr], grads[:nr]):
            o_ref[...] = g.astype(o_ref.dtype)
        if also_bf16:
            o_refs[nr + npar][...] = grads[0].astype(bf16)

        @pl.when(pl.program_id(0) == 0)
        def _():
            for o_ref in o_refs[nr:nr + npar]:
                o_ref[...] = jnp.zeros_like(o_ref)

        for o_ref, g in zip(o_refs[nr:nr + npar], grads[nr:]):
            o_ref[...] += g

    row_spec = lambda w: pl.BlockSpec((tm, w), lambda i: (i, 0))
    par_spec = lambda p: pl.BlockSpec(p.shape, lambda i: (0, 0))
    extra = [rows[0]] if also_bf16 else []
    return _pcall(body, name=name, grid=(T // tm,),
                  in_specs=[row_spec(r.shape[1]) for r in rows] + [par_spec(p) for p in params] + [row_spec(c.shape[1]) for c in cots],
                  out_specs=[row_spec(r.shape[1]) for r in rows] + [par_spec(p) for p in params] + [row_spec(r.shape[1]) for r in extra],
                  out_shape=[S(r.shape, dt) for r, dt in zip(rows, drow_dtypes)] + [S(p.shape, f32) for p in params]
                  + [S(r.shape, bf16) for r in extra],
                  compiler_params=_params(("arbitrary",)))(*rows, *params, *cots)


def _rms(h, g):
    return h * lax.rsqrt(jnp.mean(h * h, axis=-1, keepdims=True) + EPS) * g


def _rms_fn(h, g):
    return (_rms(h, g),)


def _rms_res_fn(h, g):
    return _rms(h, g), h


def _rms_res_bias_fn(h, g, b0):
    return _rms(h, g), h + b0


def _ln_silu_fn(c, g, b):
    mu = jnp.mean(c, axis=-1, keepdims=True)
    var = jnp.mean(jnp.square(c - mu), axis=-1, keepdims=True)
    y = (c - mu) * lax.rsqrt(var + EPS) * g + b
    return (y * jax.nn.sigmoid(y),)


def _glu_fn(av, ag, bv, bg):
    return ((av + bv) * jax.nn.sigmoid(ag + bg),)


def _take_rows(name, x, start, size):
    B, L, D = x.shape
    dc = _tile(D, (256, 128))

    def body(x_ref, o_ref):
        o_ref[...] = x_ref[start:start + size, :]

    return _pcall(body, name=name, grid=(B, D // dc), in_specs=[pl.BlockSpec((None, L, dc), lambda b, j: (b, 0, j))],
                  out_specs=pl.BlockSpec((None, size, dc), lambda b, j: (b, 0, j)), out_shape=S((B, size, D), x.dtype),
                  compiler_params=_params(("parallel", "parallel")))(x)


def _loss_head(h, tgt, Lp, n_real, tm):
    T, D = h.shape

    def body(h_ref, t_ref, dy_ref, dyb_ref, part_ref):
        i = pl.program_id(0)
        pos = (i * tm + lax.broadcasted_iota(jnp.int32, (tm, 1), 0)) % Lp
        real = (pos >= N_META_ROWS) & (pos < N_META_ROWS + n_real)
        err = jnp.where(real, h_ref[...] - t_ref[...], 0.0)
        dy_ref[...] = err * (1.0 / D)
        dyb_ref[...] = (err * (1.0 / D)).astype(bf16)

        @pl.when(i == 0)
        def _():
            part_ref[...] = jnp.zeros_like(part_ref)

        part_ref[...] += jnp.sum(err * err, axis=0, keepdims=True)

    spec = pl.BlockSpec((tm, D), lambda i: (i, 0))
    return _pcall(body, name="loss_head", grid=(T // tm,), in_specs=[spec, spec],
                  out_specs=[spec, spec, pl.BlockSpec((1, D), lambda i: (0, 0))],
                  out_shape=[S((T, D), f32), S((T, D), bf16), S((1, D), f32)], compiler_params=_params(("arbitrary",)))(h, tgt)


CONV_OFF = CONV_PAD - (CONV_TAPS - 1)
WIN_ROWS = ATT_BLOCK + CONV_PAD


def _phases(ph_ref, win):
    n = win.shape[0]
    for b in range(1, 8):
        ph_ref[b - 1, 0:n - 8, :] = win[b:n - 8 + b]

    def tap(o):
        a = (o // 8) * 8
        return win[a:a + ATT_BLOCK] if o % 8 == 0 else ph_ref[o % 8 - 1, a:a + ATT_BLOCK, :]

    return tap


def _phase_scratch(width):
    return pltpu.VMEM((7, WIN_ROWS - 8, width), f32)


def _dwconv_fwd(name, p, dw, carry=None):
    B, Lp, D = p.shape
    dc = _tile(D, (256, 128))
    slabs, plan = carry if carry is not None else ([], [])
    nc = len(slabs)

    def body(*refs):
        p_ref, w_ref = refs[:2]
        o_ref, bufs, ph_ref = refs[2 + nc], refs[3 + nc:3 + 2 * nc], refs[3 + 2 * nc]
        first = (pl.program_id(0) == 0) & (pl.program_id(1) == 0)
        last = (pl.program_id(0) == B - 1) & (pl.program_id(1) == D // dc - 1)
        if nc:
            @pl.when(first)
            def _():
                _run_copies(bufs, plan, refs[-2], refs[-1], start=True, wait=False)

        def tile(win, base):
            tap = _phases(ph_ref, win)
            acc = jnp.zeros((ATT_BLOCK, dc), f32)
            for k in range(CONV_TAPS):
                acc = acc + tap(CONV_OFF + k) * w_ref[k:k + 1, :]
            o_ref[pl.ds(base, ATT_BLOCK), :] = acc

        tile(jnp.concatenate([jnp.zeros((CONV_PAD, dc), f32), p_ref[0:ATT_BLOCK, :]], axis=0), 0)

        def step(r, carry):
            base = pl.multiple_of(r * ATT_BLOCK, ATT_BLOCK)
            tile(p_ref[pl.ds(pl.multiple_of(base - CONV_PAD, CONV_PAD), WIN_ROWS), :], base)
            return carry

        lax.fori_loop(1, Lp // ATT_BLOCK, step, 0)
        if nc:
            @pl.when(last)
            def _():
                _run_copies(bufs, plan, refs[-2], refs[-1], start=False, wait=True)

    seq = pl.BlockSpec((None, Lp, dc), lambda b, j: (b, 0, j))
    sems = [pltpu.SemaphoreType.DMA((_plan_copies(plan),))] * 2 if nc else []
    return _pcall(body, name=name, grid=(B, D // dc),
                  in_specs=[seq, pl.BlockSpec((CONV_PAD, dc), lambda b, j: (0, j))] + [ANY] * nc,
                  out_specs=[seq] + [ANY] * nc, out_shape=[S((B, Lp, D), f32)] + [S(s.shape, s.dtype) for s in slabs],
                  input_output_aliases={2 + t: 1 + t for t in range(nc)}, scratch_shapes=[_phase_scratch(dc)] + sems,
                  compiler_params=_params(("arbitrary", "arbitrary") if nc else ("parallel", "parallel")))(p, dw, *slabs)


def _dwconv_bwd(name, dcv, p, dw):
    B, Lp, D = p.shape
    dcw = _tile(D, (256, 128))
    nblk = Lp // ATT_BLOCK
    assert nblk >= 2
    zeros = lambda: jnp.zeros((CONV_PAD, dcw), f32)

    def body(dc_ref, p_ref, w_ref, dp_ref, ddw_ref, dph_ref, pph_ref, part_ref):
        @pl.when(pl.program_id(1) == 0)
        def _():
            ddw_ref[...] = jnp.zeros_like(ddw_ref)

        part_ref[...] = jnp.zeros_like(part_ref)

        def tile(dwin, pwin, base):
            dtap, ptap = _phases(dph_ref, dwin), _phases(pph_ref, pwin)
            dtile = dwin[0:ATT_BLOCK]
            acc = jnp.zeros((ATT_BLOCK, dcw), f32)
            for k in range(CONV_TAPS):
                acc = acc + dtap(CONV_TAPS - 1 - k) * w_ref[k:k + 1, :]
                part_ref[k] += jnp.sum((dtile * ptap(CONV_OFF + k)).reshape(ATT_BLOCK // 8, 8, dcw), axis=0)
            dp_ref[pl.ds(base, ATT_BLOCK), :] = acc

        tile(dc_ref[0:WIN_ROWS, :], jnp.concatenate([zeros(), p_ref[0:ATT_BLOCK, :]], axis=0), 0)

        def step(r, carry):
            base = pl.multiple_of(r * ATT_BLOCK, ATT_BLOCK)
            tile(dc_ref[pl.ds(base, WIN_ROWS), :], p_ref[pl.ds(pl.multiple_of(base - CONV_PAD, CONV_PAD), WIN_ROWS), :], base)
            return carry

        lax.fori_loop(1, nblk - 1, step, 0)
        last = Lp - ATT_BLOCK
        tile(jnp.concatenate([dc_ref[last:Lp, :], zeros()], axis=0), p_ref[last - CONV_PAD:Lp, :], last)
        ddw_ref[...] += jnp.sum(part_ref[...], axis=1)

    seq = pl.BlockSpec((None, Lp, dcw), lambda j, b: (b, 0, j))
    wsp = pl.BlockSpec((CONV_PAD, dcw), lambda j, b: (0, j))
    return _pcall(body, name=name, grid=(D // dcw, B), in_specs=[seq, seq, wsp], out_specs=[seq, wsp],
                  out_shape=[S((B, Lp, D), f32), S((CONV_PAD, D), f32)],
                  scratch_shapes=[_phase_scratch(dcw)] * 2 + [pltpu.VMEM((CONV_PAD, 8, dcw), f32)],
                  compiler_params=_params(("parallel", "arbitrary")))(dcv, p, dw)


def _band_start(n):
    return pl.multiple_of(jnp.maximum(n - 1, 0) * ATT_BLOCK, ATT_BLOCK)


def _attn_bias(R):
    row = jnp.arange(R * ATT_BLOCK)[:, None] % ATT_BLOCK
    col = jnp.arange(3 * ATT_BLOCK)[None, :]
    out = []
    for n in range(3):
        qpos = n * ATT_BLOCK + row
        meta_ok = (col < N_META_ROWS) & (col <= qpos)
        band_pos = max(n - 1, 0) * ATT_BLOCK + (col - ATT_BLOCK)
        diff = qpos - band_pos
        band_ok = (col >= ATT_BLOCK) & (diff >= 0) & (diff < ATT_BLOCK) & (band_pos >= N_META_ROWS)
        out.append(jnp.where(meta_ok | band_ok, 0.0, MASKED))
    return jnp.stack(out).astype(f32)


def _row_head(R):
    row = lax.broadcasted_iota(jnp.int32, (R * ATT_BLOCK, R), 0)
    lo = lax.broadcasted_iota(jnp.int32, (R * ATT_BLOCK, R), 1) * ATT_BLOCK
    return ((row >= lo) & (row < lo + ATT_BLOCK)).astype(f32)


def _attn_core(q, kn, vv, qg, sink_vec, row_head, bias):
    qn = _rms(q, qg)
    s = lax.dot_general(qn.astype(bf16), kn.astype(bf16), _DN["nt"], preferred_element_type=f32) * (HEAD_DIM ** -0.5)
    s = s + bias
    sink = jnp.sum(row_head * sink_vec, axis=-1, keepdims=True)
    m = lax.stop_gradient(jnp.maximum(jnp.max(s, axis=-1, keepdims=True), sink))
    p = jnp.exp(s - m)
    denom = jnp.sum(p, axis=-1, keepdims=True) + jnp.exp(sink - m)
    return jnp.dot((p / denom).astype(bf16), vv.astype(bf16), preferred_element_type=f32)


def _keys_of(ref, n):
    return jnp.concatenate([ref[0:ATT_BLOCK, :], ref[pl.ds(_band_start(n), 2 * ATT_BLOCK), :]], axis=0)


def _stack_heads(x, R):
    return x if R == 1 else jnp.concatenate([x[:, r * HEAD_DIM:(r + 1) * HEAD_DIM] for r in range(R)], axis=0)


def _unstack_heads(x, R):
    return x if R == 1 else jnp.concatenate([x[r * ATT_BLOCK:(r + 1) * ATT_BLOCK] for r in range(R)], axis=1)


KV_PER_STEP_FWD = 4
KV_PER_STEP_BWD = 1


def _attn_specs(R, Lp, per):
    qspec = pl.BlockSpec((None, ATT_BLOCK, per * R * HEAD_DIM), lambda g, b, n: (b, n, g))
    kspec = pl.BlockSpec((per, None, Lp, HEAD_DIM), lambda g, b, n: (g, b, 0, 0))
    gspec = pl.BlockSpec((1, HEAD_DIM), lambda g, b, n: (0, 0))
    sspec = pl.BlockSpec((per, 1, R), lambda g, b, n: (g, 0, 0))
    bspec = pl.BlockSpec((None, R * ATT_BLOCK, 3 * ATT_BLOCK), lambda g, b, n: (jnp.minimum(n, 2), 0, 0))
    return qspec, kspec, gspec, sspec, bspec


def _heads_of(x, gi, R):
    return x[:, gi * R * HEAD_DIM:(gi + 1) * R * HEAD_DIM]


def _attn_fwd(name, q, kn, v, qg, sinks, carry=None):
    B, Lp, D = q.shape
    G = kn.shape[0]
    R = D // (G * HEAD_DIM)
    per = KV_PER_STEP_FWD
    assert G % per == 0
    qspec, kspec, gspec, sspec, bspec = _attn_specs(R, Lp, per)
    slabs, plan = carry if carry is not None else ([], [])
    nc = len(slabs)
    grid = (G // per, B, Lp // ATT_BLOCK)

    def body(*refs):
        q_ref, k_ref, v_ref, qg_ref, s_ref, b_ref = refs[:6]
        o_ref, bufs = refs[6 + nc], refs[7 + nc:7 + 2 * nc]
        g, b, n = pl.program_id(0), pl.program_id(1), pl.program_id(2)
        if nc:
            @pl.when((g == 0) & (b == 0) & (n == 0))
            def _():
                _run_copies(bufs, plan, refs[-2], refs[-1], start=True, wait=False)

        q_all, outs = q_ref[...], []
        for gi in range(per):
            o = _attn_core(_stack_heads(_heads_of(q_all, gi, R), R), _keys_of(k_ref.at[gi], n), _keys_of(v_ref.at[gi], n),
                           qg_ref[...], s_ref[gi], _row_head(R), b_ref[...])
            outs.append(_unstack_heads(o, R))
        o_ref[...] = jnp.concatenate(outs, axis=1).astype(o_ref.dtype)
        if nc:
            @pl.when((g == grid[0] - 1) & (b == grid[1] - 1) & (n == grid[2] - 1))
            def _():
                _run_copies(bufs, plan, refs[-2], refs[-1], start=False, wait=True)

    sems = [pltpu.SemaphoreType.DMA((_plan_copies(plan),))] * 2 if nc else []
    return _pcall(body, name=name, grid=grid, in_specs=[qspec, kspec, kspec, gspec, sspec, bspec] + [ANY] * nc,
                  out_specs=[qspec] + [ANY] * nc, out_shape=[S(q.shape, bf16)] + [S(s.shape, s.dtype) for s in slabs],
                  input_output_aliases={6 + t: 1 + t for t in range(nc)}, scratch_shapes=sems,
                  compiler_params=_params(("arbitrary",) * 3 if nc else ("parallel",) * 3))(q, kn, v, qg, sinks, _attn_bias(R), *slabs)


def _attn_bwd(name, q, kn, v, qg, sinks, do):
    B, Lp, D = q.shape
    G = kn.shape[0]
    R = D // (G * HEAD_DIM)
    per = KV_PER_STEP_BWD
    qspec, kspec, gspec, sspec, bspec = _attn_specs(R, Lp, per)

    def body(q_ref, k_ref, v_ref, qg_ref, s_ref, b_ref, do_ref, dq_ref, dk_ref, dv_ref, dqg_ref, ds_ref):
        g, b, n = pl.program_id(0), pl.program_id(1), pl.program_id(2)

        @pl.when((g == 0) & (b == 0) & (n == 0))
        def _():
            dqg_ref[...] = jnp.zeros_like(dqg_ref)

        @pl.when((b == 0) & (n == 0))
        def _():
            ds_ref[...] = jnp.zeros_like(ds_ref)

        @pl.when(n == 0)
        def _():
            dk_ref[...] = jnp.zeros_like(dk_ref)
            dv_ref[...] = jnp.zeros_like(dv_ref)

        row_head, bias = _row_head(R), b_ref[...]
        q_all, do_all, dqs = q_ref[...], do_ref[...].astype(f32), []
        band = pl.ds(_band_start(n), 2 * ATT_BLOCK)
        for gi in range(per):
            _, vjp = jax.vjp(lambda q_, k_, v_, a_, s_: _attn_core(q_, k_, v_, a_, s_, row_head, bias),
                             _stack_heads(_heads_of(q_all, gi, R), R), _keys_of(k_ref.at[gi], n).astype(f32),
                             _keys_of(v_ref.at[gi], n).astype(f32), qg_ref[...], s_ref[gi])
            dq, dkk, dvv, dqg, dsk = vjp(_stack_heads(_heads_of(do_all, gi, R), R))
            dqs.append(_unstack_heads(dq, R))
            dqg_ref[...] += dqg
            ds_ref[gi] += dsk
            dk_ref[gi, band, :] += dkk[ATT_BLOCK:]
            dv_ref[gi, band, :] += dvv[ATT_BLOCK:]
            dk_ref[gi, 0:ATT_BLOCK, :] += dkk[:ATT_BLOCK]
            dv_ref[gi, 0:ATT_BLOCK, :] += dvv[:ATT_BLOCK]
        dq_ref[...] = jnp.concatenate(dqs, axis=1)

    return _pcall(body, name=name, grid=(G // per, B, Lp // ATT_BLOCK),
                  in_specs=[qspec, kspec, kspec, gspec, sspec, bspec, qspec],
                  out_specs=[qspec, kspec, kspec, gspec, sspec],
                  out_shape=[S(q.shape, f32), S(kn.shape, f32), S(v.shape, f32), S(qg.shape, f32), S(sinks.shape, f32)],
                  compiler_params=_params(("arbitrary", "arbitrary", "arbitrary")))(q, kn, v, qg, sinks, _attn_bias(R), do)


def _kv_heads(name, kv, kg, tm):
    T, W = kv.shape
    G = W // (2 * HEAD_DIM)

    def body(kv_ref, kg_ref, k_ref, v_ref):
        x = kv_ref[...]
        for g in range(G):
            k_ref[g] = _rms(x[:, g * HEAD_DIM:(g + 1) * HEAD_DIM], kg_ref[...]).astype(bf16)
            v_ref[g] = x[:, (G + g) * HEAD_DIM:(G + g + 1) * HEAD_DIM].astype(bf16)

    hspec = pl.BlockSpec((G, tm, HEAD_DIM), lambda i: (0, i, 0))
    return _pcall(body, name=name, grid=(T // tm,),
                  in_specs=[pl.BlockSpec((tm, W), lambda i: (i, 0)), pl.BlockSpec((1, HEAD_DIM), lambda i: (0, 0))],
                  out_specs=[hspec, hspec], out_shape=[S((G, T, HEAD_DIM), bf16)] * 2,
                  compiler_params=_params(("parallel",)))(kv, kg)


def _kv_heads_bwd(name, kv, kg, dkn, dv, tm):
    T, W = kv.shape
    G = W // (2 * HEAD_DIM)

    def body(kv_ref, kg_ref, dk_ref, dv_ref, o_ref, dkg_ref):
        @pl.when(pl.program_id(0) == 0)
        def _():
            dkg_ref[...] = jnp.zeros_like(dkg_ref)

        x = kv_ref[...]
        pieces = []
        for g in range(G):
            _, vjp = jax.vjp(_rms, x[:, g * HEAD_DIM:(g + 1) * HEAD_DIM], kg_ref[...])
            dk, dkg = vjp(dk_ref[g])
            pieces.append(dk)
            dkg_ref[...] += dkg
        o_ref[...] = jnp.concatenate(pieces + [dv_ref[g] for g in range(G)], axis=1)

    hspec = pl.BlockSpec((G, tm, HEAD_DIM), lambda i: (0, i, 0))
    gspec = pl.BlockSpec((1, HEAD_DIM), lambda i: (0, 0))
    return _pcall(body, name=name, grid=(T // tm,),
                  in_specs=[pl.BlockSpec((tm, W), lambda i: (i, 0)), gspec, hspec, hspec],
                  out_specs=[pl.BlockSpec((tm, W), lambda i: (i, 0)), gspec],
                  out_shape=[S((T, W), f32), S((1, HEAD_DIM), f32)], compiler_params=_params(("arbitrary",)))(kv, kg, dkn, dv)


def _place():
    x, y, c = lax.axis_index("x"), lax.axis_index("y"), lax.axis_index("c")
    chips = [(1 - x, y), (x, 1 - y), (1 - x, 1 - y)]
    return x, y, c, chips, [2 * cx + cy for cx, cy in chips]


def _remote(src, dst, send_sem, recv_sem, to):
    return pltpu.make_async_remote_copy(src_ref=src, dst_ref=dst, send_sem=send_sem, recv_sem=recv_sem,
                                        device_id=to, device_id_type=MESH)


def _into_slot(name, w, l, idx, slots, dtype):
    _, Rr, Cc = w.shape
    tr = _tile(Rr, (512, 256, 128, 64, 32, 16))

    def body(i_ref, x_ref, o_ref):
        o_ref[...] = x_ref[...].astype(o_ref.dtype)

    gs = pltpu.PrefetchScalarGridSpec(
        num_scalar_prefetch=1, grid=(Rr // tr,),
        in_specs=[pl.BlockSpec((None, tr, Cc), lambda i, i_ref: (l, i, 0))],
        out_specs=pl.BlockSpec((None, tr, Cc), lambda i, i_ref: (i_ref[0], i, 0)))
    return _pcall(body, name=name, grid_spec=gs, out_shape=S((slots, Rr, Cc), dtype), compiler_params=_params(("parallel",)))(idx, w)


_PLAN_COPIES = dict(ici=3, fwd=3, xchg=N_CHIPS, scat=3, share=1)


def _plan_copies(plan):
    return sum(_PLAN_COPIES[step[0]] for step in plan)


def _comm_copies(bufs, plan, send, recv):
    x, y, c, chips, qk = _place()
    me_q, sib = 2 * x + y, (x, y, 1 - c)
    starts, lands = [], []

    def add(src, dst, to, land):
        s = len(starts)
        starts.append(_remote(src, dst, send.at[s], recv.at[s], to))
        lands.append(_remote(land, land, send.at[s], recv.at[s], sib))

    for step in plan:
        kind = step[0]
        if kind == "ici":
            mine = bufs[step[1]].at[me_q, c]
            for k in range(3):
                add(mine, mine, (*chips[k], c), bufs[step[1]].at[qk[k], c])
        elif kind == "fwd":
            for k in range(3):
                got = bufs[step[1]].at[qk[k], c]
                add(got, got, sib, bufs[step[1]].at[qk[k], 1 - c])
        elif kind == "xchg":
            for q in range(N_CHIPS):
                add(bufs[step[1]].at[q, 1 - c], bufs[step[2]].at[q], sib, bufs[step[2]].at[q])
        elif kind == "scat":
            for k in range(3):
                add(bufs[step[1]].at[qk[k]], bufs[step[2]].at[k], (*chips[k], c), bufs[step[2]].at[k])
        else:
            mine = bufs[step[1]].at[step[2], c]
            add(mine, mine, sib, bufs[step[1]].at[step[2], 1 - c])
    return starts, lands


def _run_copies(bufs, plan, send, recv, start=True, wait=True):
    starts, lands = _comm_copies(bufs, plan, send, recv)
    if start:
        for cp in starts:
            cp.start()
    if wait:
        for cp in lands:
            cp.wait_recv()
        for cp in starts:
            cp.wait_send()


def _comm_call(name, slabs, phases):
    n = len(slabs)

    def body(*refs):
        bufs, sems = refs[n:2 * n], refs[2 * n:]
        for t, plan in enumerate(phases):
            _run_copies(bufs, plan, sems[2 * t], sems[2 * t + 1])

    sems = [pltpu.SemaphoreType.DMA((_plan_copies(plan),)) for plan in phases for _ in range(2)]
    return _pcall(body, name=name, in_specs=[ANY] * n, out_specs=[ANY] * n, out_shape=[S(s.shape, s.dtype) for s in slabs],
                  input_output_aliases={p: p for p in range(n)}, scratch_shapes=sems)(*slabs)


def _gather_all_devices(block):
    m_per, ncol = block.shape

    def body(x_ref, out_ref, send_sems, recv_sems, local_sem):
        x, y, c, chips, _ = _place()
        me, sib = (x, y, c), (x, y, 1 - c)

        def rows(px, py, pc):
            return out_ref.at[pl.ds((4 * px + 2 * py + pc) * m_per, m_per), :]

        def copy(k, blk, to, src=None):
            return _remote(rows(*blk) if src is None else src, rows(*blk), send_sems.at[k], recv_sems.at[k], to)

        mine = pltpu.make_async_copy(x_ref, rows(*me), local_sem)
        mine.start()
        first = [copy(0, me, sib, src=x_ref)] + [copy(1 + j, me, (*chip, c), src=x_ref) for j, chip in enumerate(chips)]
        for cp in first:
            cp.start()
        passed = [copy(4 + j, (*chip, c), sib) for j, chip in enumerate(chips)]
        for j, chip in enumerate(chips):
            copy(1 + j, (*chip, c), me).wait_recv()
            passed[j].start()
        copy(0, sib, me).wait_recv()
        for j, chip in enumerate(chips):
            copy(4 + j, (*chip, 1 - c), me).wait_recv()
        for cp in first + passed:
            cp.wait_send()
        mine.wait()

    vm = pl.BlockSpec(memory_space=pltpu.VMEM)
    return _pcall(body, name="gather_small_grads", in_specs=[vm], out_specs=vm,
                  out_shape=S((8 * m_per, ncol), block.dtype),
                  scratch_shapes=[pltpu.SemaphoreType.DMA((7,)), pltpu.SemaphoreType.DMA((7,)), pltpu.SemaphoreType.DMA],
                  compiler_params=pltpu.CompilerParams(vmem_limit_bytes=VMEM_LIMIT_BYTES))(block)


def _sum_pair(name, g, r1, c_idx):
    Q, _, Rr, Cc = g.shape
    tr = _tile(Rr, (512, 256, 128))

    def body(c_ref, g_ref, r_ref, o_ref):
        o_ref[...] = (g_ref[...].astype(f32) + r_ref[...].astype(f32)).astype(o_ref.dtype)

    spec = pl.BlockSpec((None, tr, Cc), lambda q, i, c_ref: (q, i, 0))
    gs = pltpu.PrefetchScalarGridSpec(
        num_scalar_prefetch=1, grid=(Q, Rr // tr),
        in_specs=[pl.BlockSpec((None, None, tr, Cc), lambda q, i, c_ref: (q, c_ref[0], i, 0)), spec], out_specs=spec)
    return _pcall(body, name=name, grid_spec=gs, out_shape=S((Q, Rr, Cc), bf16),
                  compiler_params=_params(("parallel", "parallel")))(c_idx, g, r1)


def _sum_owner(name, s, r2, qc_idx, l, nl, into):
    Q, Rr, Cc = s.shape
    tr = _tile(Rr, (512, 256, 128))

    def body(q_ref, s_ref, r_ref, *rest):
        o_ref = rest[-1]
        o_ref[...] = ((s_ref[...].astype(f32) + r_ref[0].astype(f32)) + r_ref[1].astype(f32)) + r_ref[2].astype(f32)

    in_specs = [pl.BlockSpec((None, tr, Cc), lambda i, q_ref: (q_ref[0], i, 0)), pl.BlockSpec((3, tr, Cc), lambda i, q_ref: (0, i, 0))]
    gs = pltpu.PrefetchScalarGridSpec(
        num_scalar_prefetch=1, grid=(Rr // tr,), in_specs=in_specs + ([] if into is None else [ANY]),
        out_specs=pl.BlockSpec((None, None, tr, Cc), lambda i, q_ref: (l, q_ref[1], i, 0)))
    return _pcall(body, name=name, grid_spec=gs, out_shape=S((nl, 2, Rr, Cc), f32),
                  input_output_aliases={} if into is None else {3: 0},
                  compiler_params=_params(("parallel",)))(qc_idx, s, r2, *([] if into is None else [into]))


def _sum_devices(stack):
    n, M, C = stack.shape

    def body(s_ref, o_ref):
        acc = s_ref[0]
        for d in range(1, n):
            acc = acc + s_ref[d]
        o_ref[...] = acc

    return _pcall(body, name="sum_small_grads", out_shape=S((M, C), f32), compiler_params=_params())(stack)


def _adamw(name, w, g, m, v):
    Rr, Cc = w.shape
    tr = _tile(Rr, (256, 128, 64, 32, 16, 8))

    def body(w_ref, g_ref, m_ref, v_ref, go_ref, d_ref, mo_ref, vo_ref):
        g_ = g_ref[...]
        go_ref[...] = g_
        m_ = B1 * m_ref[...] + (1.0 - B1) * g_
        v_ = B2 * v_ref[...] + (1.0 - B2) * jnp.square(g_)
        m_hat = m_ / (1.0 - B1 ** STEP)
        v_hat = v_ / (1.0 - B2 ** STEP)
        d_ref[...] = -LR * (m_hat / (jnp.sqrt(v_hat) + ADAM_EPS) + WD * w_ref[...])
        mo_ref[...] = m_
        vo_ref[...] = v_

    spec = pl.BlockSpec((tr, Cc), lambda i: (i, 0))
    return _pcall(body, name=name, grid=(Rr // tr,), in_specs=[spec] * 4, out_specs=[spec] * 4,
                  out_shape=[S((Rr, Cc), f32)] * 4, compiler_params=_params(("parallel",)))(w, g, m, v)


def _pack(arrs, multiple):
    flat = jnp.concatenate([a.reshape(-1) for a in arrs])
    pad = (-flat.shape[0]) % multiple
    return jnp.pad(flat, (0, pad)).reshape(-1, 128)


def _unpack(slab, shapes):
    flat, out, o = slab.reshape(-1), [], 0
    for shp in shapes:
        n = 1
        for d in shp:
            n *= d
        out.append(flat[o:o + n].reshape(shp))
        o += n
    return out


def kernel(x, meta_tokens, norm_mix, norm_ffn, conv_w_in, conv_b_in, conv_dw, conv_ln_g, conv_ln_b, conv_w_out, conv_b_out, kv_norm, w_kv, k_norm, w_q, q_norm, attn_sinks, w_o, ffn_w_gate, ffn_w_up, ffn_w_down, loss_target, m_meta_tokens, m_norm_mix, m_norm_ffn, m_conv_w_in, m_conv_b_in, m_conv_dw, m_conv_ln_g, m_conv_ln_b, m_conv_w_out, m_conv_b_out, m_kv_norm, m_w_kv, m_k_norm, m_w_q, m_q_norm, m_attn_sinks, m_w_o, m_ffn_w_gate, m_ffn_w_up, m_ffn_w_down, v_meta_tokens, v_norm_mix, v_norm_ffn, v_conv_w_in, v_conv_b_in, v_conv_dw, v_conv_ln_g, v_conv_ln_b, v_conv_w_out, v_conv_b_out, v_kv_norm, v_w_kv, v_k_norm, v_w_q, v_q_norm, v_attn_sinks, v_w_o, v_ffn_w_gate, v_ffn_w_up, v_ffn_w_down):
    Q = N_CHIPS
    B, SEQ, D = x.shape
    L = N_META_ROWS + SEQ
    Lp = -(-L // ATT_BLOCK) * ATT_BLOCK
    T = B * Lp
    NA, NB = conv_w_in.shape[0], w_q.shape[0]
    NL = NA + NB
    Dq = D // Q
    G = N_KV
    R = D // (HEAD_DIM * G)
    KVW = w_kv.shape[1]
    assert NA % 2 == 0 and NB % 2 == 0 and NL % 2 == 0 and (D // Q) % 32 == 0
    tm = _tile(T, (1088, 544, 512, 256, 128))
    tk = _tile(T, (2176, 1088, 544, 512, 256, 128))
    tr = _tile(T, (272, 256, 128))
    my_c = lax.axis_index("c").astype(jnp.int32).reshape(1)
    my_q = (2 * lax.axis_index("x") + lax.axis_index("y")).astype(jnp.int32)
    my_qc = jnp.concatenate([my_q.reshape(1), my_c])

    small_shapes = [meta_tokens.shape, conv_b_in.shape, conv_dw.shape, conv_ln_g.shape, conv_ln_b.shape, conv_b_out.shape]
    small = _pack([meta_tokens, conv_b_in, conv_dw, conv_ln_g, conv_ln_b, conv_b_out], 2048)
    big = [conv_w_in, conv_w_out, w_kv, w_q, w_o, ffn_w_gate, ffn_w_up, ffn_w_down]
    keyed = dict(cin=conv_w_in, cout=conv_w_out, kv=w_kv[None], q=w_q, o=w_o, g=ffn_w_gate, u=ffn_w_up, d=ffn_w_down,
                 small=small[None])
    slabs, where = [], {}
    for key, w3 in keyed.items():
        for l in range(w3.shape[0]):
            s = _into_slot(f"own_{key}{l}", w3, l, my_qc, Q, f32 if key == "small" else bf16)
            where[key, l] = len(slabs)
            slabs.append(s.reshape(Q, 2, s.shape[1] // 2, s.shape[2]))

    def W(key, l):
        s = slabs[where[key, l]]
        return s.reshape(Q, 1, 2 * s.shape[2], s.shape[3])

    def layer_slabs(l):
        j = l - NA
        mix = [("cin", l), ("cout", l)] if l < NA else [("q", j), ("o", j)] + ([("kv", 0)] if j == 0 else [])
        return [where[k] for k in mix], [where["g", l], where["u", l]], [where["d", l]]

    def carry_of(ici, fwd):
        idxs = sorted(set(ici) | set(fwd))
        return idxs, ([slabs[i] for i in idxs], [("ici", idxs.index(i)) for i in ici] + [("fwd", idxs.index(i)) for i in fwd])

    def put_back(idxs, new):
        for i, s in zip(idxs, new):
            slabs[i] = s

    assert NA >= 1
    first = [where["cin", 0], where["small", 0]]
    idxs, (sl, plan) = carry_of(first, first)
    put_back(idxs, _comm_call("gather_first", sl, [[s for s in plan if s[0] == "ici"], [s for s in plan if s[0] == "fwd"]]))
    parts = [_unpack(slabs[where["small", 0]][q], small_shapes) for q in range(Q)]
    meta_f, b_in_f, dw_f, ln_g_f, ln_b_f, b_out_f = [jnp.concatenate([parts[q][i] for q in range(Q)], axis=-1) for i in range(6)]
    dw_pad = jnp.pad(dw_f, ((0, 0), (0, CONV_PAD - CONV_TAPS), (0, 0)))

    h = jnp.concatenate([jnp.broadcast_to(meta_f[None], (B, N_META_ROWS, D)), x, jnp.zeros((B, Lp - L, D), f32)], axis=1).reshape(T, D)
    tgt = jnp.pad(loss_target, ((0, 0), (N_META_ROWS, Lp - L), (0, 0))).reshape(T, D)
    row = lambda a: a.reshape(1, -1)
    seqs = lambda a: a.reshape(B, Lp, a.shape[-1])
    by_seq = lambda a: a.reshape(G, B, Lp, HEAD_DIM)
    saved = []
    sinks3 = attn_sinks.reshape(NB, G, 1, R)
    def carrying(ici, fwd):
        idxs, carry = carry_of(ici, fwd)
        return idxs, (carry if idxs else None)

    normed = []
    for l in range(NL):
        st = {"h_a": h}
        u = normed[0] if normed else _rowwise(f"rms_mix{l}", _rms_fn, [h], [row(norm_mix[l])], [bf16], tr)[0]
        st["u"] = u
        mix_s, (g_s, u_s), d_s = layer_slabs(l)
        idxs, carry = carrying([where["cout", 0], g_s], []) if l == 0 else carrying([], [g_s])
        second = [u_s] + d_s if l == 0 else [u_s]
        if l < NA:
            p, av, ag, *new = _glu_fwd(f"glu{l}", u, W("cin", l), 0, b_in_f[l:l + 1], tm, carry=carry)
            put_back(idxs, new)
            idxs, carry = carrying([u_s] + d_s, [where["cout", 0], g_s]) if l == 0 else carrying([u_s], [])
            cv, *new = _dwconv_fwd(f"dwconv{l}", p.reshape(B, Lp, D), dw_pad[l], carry=carry)
            put_back(idxs, new)
            cv = cv.reshape(T, D)
            s = _rowwise(f"ln_silu{l}", _ln_silu_fn, [cv], [row(ln_g_f[l]), row(ln_b_f[l])], [bf16], tr)[0]
            idxs2, carry2 = carrying([], second)
            h, u2, *new = _proj(f"conv_out{l}", s, W("cout", l), 0, tm, bias=row(b_out_f[l]), resid=h, carry=carry2,
                                norm=row(norm_ffn[l]))
            put_back(idxs2, new)
            st.update(av=av, ag=ag, p=p, cv=cv, s=s)
        else:
            j = l - NA
            if j == 0:
                kvn = normed[1] if len(normed) > 1 else _rowwise("rms_kv", _rms_fn, [h], [row(kv_norm)], [bf16], tr)[0]
                kv = _proj("kv_proj", kvn, W("kv", 0), 0, tm)
                kn, vh = _kv_heads("kv_heads", kv, row(k_norm), tm)
                kn, vh = by_seq(kn), by_seq(vh)
                st.update(kvn=kvn, kv=kv)
            res = _proj(f"q_proj{j}", u, W("q", j), 0, tm, carry=carry)
            q, new = res if carry is not None else (res, [])
            put_back(idxs, new)
            q = seqs(q)
            idxs, carry = carrying([u_s], [])
            o, *new = _attn_fwd(f"attn{j}", q, kn, vh, row(q_norm[j]), sinks3[j], carry=carry)
            put_back(idxs, new)
            o = o.reshape(T, D)
            idxs2, carry2 = carrying([], second)
            h, u2, *new = _proj(f"o_proj{j}", o, W("o", j), 0, tm, resid=h, carry=carry2, norm=row(norm_ffn[l]))
            put_back(idxs2, new)
            st.update(q=q, o=o)
        st["h_b"] = h
        nmix, ngu, nd = layer_slabs(l + 1) if l + 1 < NL else ([], [], [])
        idxs, carry = carrying(nmix + nd, [])
        dgate_of, dup_of, hid, *new = _ffn_up(f"ffn_up{l}", u2, W("g", l), W("u", l), 0, tm, carry=carry)
        put_back(idxs, new)
        idxs, carry = carrying(ngu[:1], nmix + nd)
        gains = [] if l + 1 == NL else [row(norm_mix[l + 1])] + ([row(kv_norm)] if l + 1 == NA else [])
        h, *rest = _ffn_down(f"ffn_down{l}", hid, W("d", l), 0, h, tm, carry=carry, norms=gains)
        normed, new = rest[:len(gains)], rest[len(gains):]
        put_back(idxs, new)
        st.update(u2=u2, dgate_of=dgate_of, dup_of=dup_of, hid=hid)
        saved.append(st)

    dh, dhb, part = _loss_head(h, tgt, Lp, SEQ, tr)
    loss = lax.psum(0.5 / D * jnp.sum(part), ("x", "y", "c"))

    grad_slab = {}
    n_layers = dict(cin=NA, cout=NA, kv=1, q=NB, o=NB, g=NL, u=NL, d=NL)

    def rs_begin(pieces):
        job = []
        for key, lay, g in pieces:
            g4 = g.reshape(Q, 2, g.shape[1] // 2, g.shape[2])
            job.append(dict(key=key, l=lay, g=g4, r1=lax.empty((Q,) + g4.shape[2:], bf16)))
        return job

    def xchg_carry(job):
        n = len(job)
        return [p["g"] for p in job] + [p["r1"] for p in job], [("xchg", t, n + t) for t in range(n)]

    def after_xchg(job, new):
        n = len(job)
        for t, p in enumerate(job):
            p["s"] = _sum_pair(f"sum_pair_{p['key']}{p['l']}", new[t], new[n + t], my_c)
            p["r2"] = lax.empty((3,) + p["s"].shape[1:], bf16)

    def scat_carry(job, keys):
        sel = [p for p in job if p["key"] in keys]
        n = len(sel)
        return sel, ([p["s"] for p in sel] + [p["r2"] for p in sel], [("scat", t, n + t) for t in range(n)])

    def after_scat(sel, new):
        n = len(sel)
        for t, p in enumerate(sel):
            key = p["key"]
            grad_slab[key] = _sum_owner(f"sum_owner_{key}{p['l']}", new[t], new[n + t], my_qc, p["l"], n_layers[key], grad_slab.get(key))

    def share_carry(job):
        return [grad_slab[p["key"]] for p in job], [("share", t, p["l"]) for t, p in enumerate(job)]

    def after_share(job, new):
        for p, s in zip(job, new):
            grad_slab[p["key"]] = s

    def carried(job, make):
        return make(job) if job else None

    def merge(*carries):
        slabs_, plan_ = [], []
        for c in carries:
            if c is not None:
                off = len(slabs_)
                slabs_ += c[0]
                plan_ += [(s[0], s[1] + off, s[2]) if s[0] == "share" else (s[0],) + tuple(i + off for i in s[1:]) for s in c[1]]
        return (slabs_, plan_) if slabs_ else None

    MIXER = ("cin", "cout", "q", "o", "kv")
    g_mix, g_ffn = [None] * NL, [None] * NL
    g_bin, g_dw, g_lng, g_lnb, g_bout = ([None] * NA for _ in range(5))
    g_qn, g_sink = [None] * NB, [None] * NB
    dknp = dvp = None
    zero_row = jnp.zeros((1, D), f32)
    job = []
    early = []
    for l in reversed(range(NL)):
        st = saved[l]
        pieces = []
        dgate, dup, *new = _ffn_dhid(f"ffn_dhid{l}", dhb, W("d", l), 0, st["dgate_of"], st["dup_of"], tm, carry=carried(job, xchg_carry))
        if job:
            after_xchg(job, new)
        sel, carry = scat_carry(job, ("d",)) if job else ([], None)
        dwd, *new = _dw_rows(f"ffn_dwd{l}", st["hid"], dhb, Q, tk, carry=carry)
        after_scat(sel, new)
        sel, carry = scat_carry(job, ("g", "u")) if job else ([], None)
        dwg, dwu, *new = _dw_cols(f"ffn_dwgu{l}", st["u2"], dgate, Q, tk, Q, carry=carry, dyc2=dup)
        after_scat(sel, new)
        pieces += [("d", l, dwd), ("g", l, dwg), ("u", l, dwu)]
        if l == 0 and l < NA:
            early, pieces = rs_begin(pieces), []
        sel, carry = scat_carry(job, MIXER) if job else ([], None)
        du2, *new = _ffn_du(f"ffn_du{l}", dgate, dup, W("g", l), W("u", l), 0, tm, carry=merge(carry, carried(early, xchg_carry)))
        after_scat(sel, new[:2 * len(sel)])
        if early:
            after_xchg(early, new[2 * len(sel):])
        dh, g_ffn[l], dhb = _rowwise_vjp(f"rms_ffn_bwd{l}", _rms_res_fn, [st["h_b"]], [row(norm_ffn[l])], [du2, dh], [f32], tr,
                                         also_bf16=True)
        if l < NA:
            esel, ecarry = scat_carry(early, ("d",)) if early else ([], None)
            ds, *new = _proj_dx(f"conv_out_dx{l}", dhb, W("cout", l), 0, tm, carry=merge(carried(job, share_carry), ecarry))
            after_share(job, new[:len(job)])
            after_scat(esel, new[len(job):])
            pieces.append(("cout", l, _dw_rows(f"conv_out_dw{l}", st["s"], dhb, Q, tk)[0]))
            dcv, g_lng[l], g_lnb[l] = _rowwise_vjp(f"ln_silu_bwd{l}", _ln_silu_fn, [st["cv"]], [row(ln_g_f[l]), row(ln_b_f[l])], [ds], [f32], tr)
            dp, g_dw[l] = _dwconv_bwd(f"dwconv_bwd{l}", dcv.reshape(B, Lp, D), st["p"].reshape(B, Lp, D), dw_pad[l])
            dav, dag, dbv, dbg = _rowwise_vjp(f"glu_bwd{l}", _glu_fn, [st["av"], st["ag"]], [zero_row, zero_row], [dp.reshape(T, D)], [bf16, bf16], tr)
            g_bin[l] = jnp.concatenate([dbv, dbg], axis=1)
            esel, ecarry = scat_carry(early, ("g",)) if early else ([], None)
            dwin, *new = _dw_cols(f"glu_dwv{l}", st["u"], dav, Q // 2, tk, Q, carry=ecarry)
            after_scat(esel, new)
            esel, ecarry = scat_carry(early, ("u",)) if early else ([], None)
            dwin, *new = _dw_cols(f"glu_dwg{l}", st["u"], dag, Q // 2, tk, Q, q_off=Q // 2, into=dwin, carry=ecarry)
            after_scat(esel, new)
            pieces.append(("cin", l, dwin))
            du = _glu_du(f"glu_du{l}", dav, dag, W("cin", l), 0, tm)
            dh, g_mix[l], g_bout[l], dhb = _rowwise_vjp(f"rms_mix_bwd{l}", _rms_res_bias_fn, [st["h_a"]], [row(norm_mix[l]), zero_row],
                                                        [du, dh], [f32], tr, also_bf16=True)
        else:
            j = l - NA
            do, *new = _proj_dx(f"o_proj_dx{j}", dhb, W("o", j), 0, tm, out_dtype=bf16, carry=carried(job, share_carry))
            after_share(job, new)
            pieces.append(("o", j, _dw_rows(f"o_proj_dw{j}", st["o"], dhb, Q, tk)[0]))
            dq, dk1, dv1, g_qn[j], g_sink[j] = _attn_bwd(f"attn_bwd{j}", st["q"], kn, vh, row(q_norm[j]), sinks3[j], seqs(do))
            dknp, dvp = (dk1, dv1) if dknp is None else (dknp + dk1, dvp + dv1)
            dq = dq.reshape(T, D)
            pieces.append(("q", j, _dw_rows(f"q_proj_dw{j}", st["u"], dq, Q, tk)[0]))
            du = _proj_dx(f"q_proj_dx{j}", dq, W("q", j), 0, tm)[0]
            dh, g_mix[l], dhb = _rowwise_vjp(f"rms_mix_bwd{l}", _rms_res_fn, [st["h_a"]], [row(norm_mix[l])], [du, dh], [f32], tr,
                                             also_bf16=True)
            if j == 0:
                dkv, g_kn = _kv_heads_bwd("kv_heads_bwd", st["kv"], row(k_norm), dknp.reshape(G, T, HEAD_DIM), dvp.reshape(G, T, HEAD_DIM), tm)
                pieces.append(("kv", 0, _dw_rows("kv_proj_dw", st["kvn"], dkv, Q, tk)[0]))
                dkvn = _proj_dx("kv_proj_dx", dkv, W("kv", 0), 0, tm)[0]
                dh, g_kvn, dhb = _rowwise_vjp("rms_kv_bwd", _rms_res_fn, [st["h_a"]], [row(kv_norm)], [dkvn, dh], [f32], tr,
                                              also_bf16=True)
        job = rs_begin(pieces)
    dh3 = dh.reshape(B, Lp, D)
    grad_x = _take_rows("grad_x", dh3, N_META_ROWS, SEQ)
    g_meta = jnp.sum(dh3[:, :N_META_ROWS], axis=0)

    sl, plan = xchg_carry(job)
    after_xchg(job, _comm_call("rs_exchange", sl, [plan]))
    sel, (sl, plan) = scat_carry(job, tuple(n_layers))
    after_scat(sel, _comm_call("rs_scatter", sl, [plan]))
    sl, plan = share_carry(job + early)
    after_share(job + early, _comm_call("rs_share", sl, [plan]))

    names = ["conv_w_in", "conv_w_out", "w_kv", "w_q", "w_o", "ffn_w_gate", "ffn_w_up", "ffn_w_down"]
    ws = dict(zip(names, big))
    ms = dict(zip(names, [m_conv_w_in, m_conv_w_out, m_w_kv, m_w_q, m_w_o, m_ffn_w_gate, m_ffn_w_up, m_ffn_w_down]))
    vs = dict(zip(names, [v_conv_w_in, v_conv_w_out, v_w_kv, v_w_q, v_w_o, v_ffn_w_gate, v_ffn_w_up, v_ffn_w_down]))
    out_g, out_d, out_m, out_v = {}, {}, {}, {}
    for nm, key in zip(names, ("cin", "cout", "kv", "q", "o", "g", "u", "d")):
        w, gsh = ws[nm], grad_slab[key]
        flat = lambda a: a.reshape(-1, w.shape[-1])
        g2, d2, m2, v2 = _adamw(f"adamw_{nm}", flat(w), flat(gsh), flat(ms[nm]), flat(vs[nm]))
        out_g[nm], out_d[nm], out_m[nm], out_v[nm] = (a.reshape(w.shape) for a in (g2, d2, m2, v2))

    small_names = ["norm_mix", "norm_ffn", "kv_norm", "k_norm", "q_norm", "attn_sinks", "meta_tokens", "conv_b_in", "conv_dw", "conv_ln_g", "conv_ln_b", "conv_b_out"]
    small_grads = [jnp.concatenate(g_mix, 0), jnp.concatenate(g_ffn, 0), g_kvn.reshape(-1), g_kn.reshape(-1), jnp.concatenate(g_qn, 0),
                   jnp.stack(g_sink).reshape(NB, G * R), g_meta, jnp.concatenate(g_bin, 0), jnp.stack(g_dw)[:, :CONV_TAPS],
                   jnp.concatenate(g_lng, 0), jnp.concatenate(g_lnb, 0), jnp.concatenate(g_bout, 0)]
    slab = _pack(small_grads, 1024)
    total = _sum_devices(_gather_all_devices(slab).reshape(8, slab.shape[0], 128))
    full_grads = _unpack(total, [g.shape for g in small_grads])
    small_w = dict(zip(small_names, [norm_mix, norm_ffn, kv_norm, k_norm, q_norm, attn_sinks, meta_tokens, conv_b_in, conv_dw, conv_ln_g, conv_ln_b, conv_b_out]))
    small_m = dict(zip(small_names, [m_norm_mix, m_norm_ffn, m_kv_norm, m_k_norm, m_q_norm, m_attn_sinks, m_meta_tokens, m_conv_b_in, m_conv_dw, m_conv_ln_g, m_conv_ln_b, m_conv_b_out]))
    small_v = dict(zip(small_names, [v_norm_mix, v_norm_ffn, v_kv_norm, v_k_norm, v_q_norm, v_attn_sinks, v_meta_tokens, v_conv_b_in, v_conv_dw, v_conv_ln_g, v_conv_ln_b, v_conv_b_out]))
    local_grads = []
    for nm, g in zip(small_names, full_grads):
        w = small_w[nm]
        if g.shape != w.shape:
            wq = w.shape[-1]
            g = lax.dynamic_slice_in_dim(g, my_q * wq, wq, axis=g.ndim - 1)
        local_grads.append(g)
    shapes = [small_w[nm].shape for nm in small_names]
    _, d_s, m_s, v_s = _adamw("adamw_small", _pack([small_w[nm] for nm in small_names], 1024), _pack(local_grads, 1024),
                           _pack([small_m[nm] for nm in small_names], 1024), _pack([small_v[nm] for nm in small_names], 1024))
    for nm, g, d_, m_, v_ in zip(small_names, local_grads, _unpack(d_s, shapes), _unpack(m_s, shapes), _unpack(v_s, shapes)):
        out_g[nm], out_d[nm], out_m[nm], out_v[nm] = g, d_, m_, v_

    order = ["meta_tokens", "norm_mix", "norm_ffn", "conv_w_in", "conv_b_in", "conv_dw", "conv_ln_g", "conv_ln_b", "conv_w_out", "conv_b_out",
             "kv_norm", "w_kv", "k_norm", "w_q", "q_norm", "attn_sinks", "w_o", "ffn_w_gate", "ffn_w_up", "ffn_w_down"]
    return (loss, grad_x, *[out_g[n] for n in order], *[out_d[n] for n in order], *[out_m[n] for n in order], *[out_v[n] for n in order])
```
